```python
import jax, jax.numpy as jnp
from jax import lax
import numpy as np

D_MODEL = 4096
BATCH = 8
SEQ = 4096
DEPTH = 2

CHUNK = 64
Q_BLOCK = 128
HEAD_DIM = 128
N_HEADS = D_MODEL // HEAD_DIM
N_HEADS_A = N_HEADS // 2
N_HEADS_B = N_HEADS - N_HEADS_A
N_HEADS_C = N_HEADS
WIDTH_A = N_HEADS_A * HEAD_DIM
WIDTH_B = N_HEADS_B * HEAD_DIM
WIDTH_C = N_HEADS_C * HEAD_DIM
LEFT_CHUNKS = 8
BAND_CHUNKS = LEFT_CHUNKS + 1
REL_CLIP = 128
N_REL = 2 * REL_CLIP + 1
RMS_EPS = 1e-6
N_EVEN = (DEPTH + 1) // 2
N_ODD = DEPTH // 2
IN_EVEN = 4 * WIDTH_A + 4 * WIDTH_B + N_HEADS_A
IN_ODD = 4 * WIDTH_C

kernel_name = "hybrid_fox_chunkrel_stickbreak_sandwich"


def rms_norm(x, g):
    xf = x.astype(jnp.float32)
    y = xf * lax.rsqrt(jnp.mean(xf * xf, axis=-1, keepdims=True) + RMS_EPS)
    return (y * g.astype(jnp.float32)).astype(x.dtype)


def split_cols(proj, widths):
    outs, off = [], 0
    for w in widths:
        outs.append(proj[..., off:off + w])
        off += w
    return outs


def heads(t, n_heads):
    b, s, _ = t.shape
    return t.reshape(b, s, n_heads, HEAD_DIM)


def forgetting_attention(q, k, v, log_f):
    b, s_len, h, dh = q.shape
    scale = dh ** -0.5
    c = jnp.transpose(jnp.cumsum(log_f, axis=1), (0, 2, 1))
    outs = []
    for i in range(s_len // Q_BLOCK):
        q0, q1 = i * Q_BLOCK, (i + 1) * Q_BLOCK
        logits = jnp.einsum('bqhd,bkhd->bhqk', q[:, q0:q1], k[:, :q1],
                            preferred_element_type=jnp.float32) * scale
        decay = c[:, :, q0:q1, None] - c[:, :, None, :q1]
        tq = jnp.arange(q0, q1)[:, None]
        tk = jnp.arange(q1)[None, :]
        logits = jnp.where(tk <= tq, logits + decay, -jnp.inf)
        p = jax.nn.softmax(logits, axis=-1).astype(v.dtype)
        outs.append(jnp.einsum('bhqk,bkhd->bqhd', p, v[:, :q1]))
    return jnp.concatenate(outs, axis=1)


def chunked_relpos_attention(q, k, v, rel_bias):
    b, s_len, h, dh = q.shape
    scale = dh ** -0.5
    nc = s_len // CHUNK
    band = BAND_CHUNKS * CHUNK
    qc = q.reshape(b, nc, CHUNK, h, dh)
    pad = ((0, 0), (LEFT_CHUNKS, 0), (0, 0), (0, 0), (0, 0))
    kc = jnp.pad(k.reshape(b, nc, CHUNK, h, dh), pad)
    vc = jnp.pad(v.reshape(b, nc, CHUNK, h, dh), pad)
    logits = jnp.concatenate(
        [jnp.einsum('bnqhd,bnkhd->bhnqk', qc, kc[:, j:j + nc],
                    preferred_element_type=jnp.float32) for j in range(BAND_CHUNKS)],
        axis=-1) * scale
    qi = jnp.arange(CHUNK)[:, None]
    km = jnp.arange(band)[None, :]
    rel = jnp.clip(LEFT_CHUNKS * CHUNK + qi - km, -REL_CLIP, REL_CLIP) + REL_CLIP
    bias = rel_bias.astype(jnp.float32)[:, rel]
    src_chunk = jnp.arange(nc)[:, None] - LEFT_CHUNKS + km // CHUNK
    valid = (src_chunk >= 0)[None, None, :, None, :]
    logits = jnp.where(valid, logits + bias[None, :, None], -jnp.inf)
    p = jax.nn.softmax(logits, axis=-1).astype(v.dtype)
    out = jnp.einsum('bhnqk,bnkhd->bnqhd', p[..., :CHUNK], vc[:, 0:nc])
    for j in range(1, BAND_CHUNKS):
        out = out + jnp.einsum('bhnqk,bnkhd->bnqhd',
                               p[..., j * CHUNK:(j + 1) * CHUNK], vc[:, j:j + nc])
    return out.reshape(b, s_len, h, dh)


def stick_breaking_attention(q, k, v):
    b, s_len, h, dh = q.shape
    scale = dh ** -0.5
    outs = []
    for i in range(s_len // Q_BLOCK):
        q0, q1 = i * Q_BLOCK, (i + 1) * Q_BLOCK
        z = jnp.einsum('bqhd,bkhd->bhqk', q[:, q0:q1], k[:, :q1],
                       preferred_element_type=jnp.float32) * scale
        tq = jnp.arange(q0, q1)[:, None]
        tk = jnp.arange(q1)[None, :]
        causal = tk < tq
        log_beta = jax.nn.log_sigmoid(z)
        log_one_minus = jnp.where(causal, jax.nn.log_sigmoid(-z), 0.0)
        tail = lax.cumsum(log_one_minus, axis=3, reverse=True) - log_one_minus
        a = jnp.where(causal, jnp.exp(log_beta + tail), 0.0).astype(v.dtype)
        outs.append(jnp.einsum('bhqk,bkhd->bqhd', a, v[:, :q1]))
    return jnp.concatenate(outs, axis=1)


def even_mixer(h, w_in, b_f, rel_bias, w_out):
    b, s_len, _ = h.shape
    proj = jnp.einsum('bsd,de->bse', h, w_in)
    aq, ak, av, ag, bq, bk, bv, bg, af = split_cols(
        proj, [WIDTH_A] * 4 + [WIDTH_B] * 4 + [N_HEADS_A])
    log_f = jax.nn.log_sigmoid((af + b_f).astype(jnp.float32))
    oa = forgetting_attention(heads(aq, N_HEADS_A), heads(ak, N_HEADS_A),
                              heads(av, N_HEADS_A), log_f).reshape(b, s_len, WIDTH_A)
    ob = chunked_relpos_attention(heads(bq, N_HEADS_B), heads(bk, N_HEADS_B),
                                  heads(bv, N_HEADS_B), rel_bias).reshape(b, s_len, WIDTH_B)
    mixed = jnp.concatenate([oa * jax.nn.silu(ag), ob * jax.nn.silu(bg)], axis=-1)
    return jnp.einsum('bse,ed->bsd', mixed, w_out)


def odd_mixer(h, w_in, w_out):
    b, s_len, _ = h.shape
    proj = jnp.einsum('bsd,de->bse', h, w_in)
    cq, ck, cv, cg = split_cols(proj, [WIDTH_C] * 4)
    oc = stick_breaking_attention(heads(cq, N_HEADS_C), heads(ck, N_HEADS_C),
                                  heads(cv, N_HEADS_C)).reshape(b, s_len, WIDTH_C)
    return jnp.einsum('bse,ed->bsd', oc * jax.nn.silu(cg), w_out)


def _fwd_setup_inputs(seed: int = 0) -> dict:
    key = jax.random.key(seed)
    ks = jax.random.split(key, 10)
    fan = D_MODEL ** -0.5
    x = jax.random.normal(ks[0], (BATCH, SEQ, D_MODEL), jnp.float32)
    norm_pre = 1.0 + 0.05 * jax.random.normal(ks[1], (DEPTH, D_MODEL), jnp.float32)
    norm_post = 1.0 + 0.05 * jax.random.normal(ks[2], (DEPTH, D_MODEL), jnp.float32)
    w_in_even = jax.random.normal(ks[3], (N_EVEN, D_MODEL, IN_EVEN), jnp.float32) * fan
    b_f_even = jax.random.uniform(ks[4], (N_EVEN, N_HEADS_A), jnp.float32, 1.0, 4.0)
    rel_bias_even = 0.1 * jax.random.normal(ks[5], (N_EVEN, N_HEADS_B, N_REL), jnp.float32)
    w_out_even = jax.random.normal(ks[6], (N_EVEN, WIDTH_A + WIDTH_B, D_MODEL), jnp.float32) * (WIDTH_A + WIDTH_B) ** -0.5
    w_in_odd = jax.random.normal(ks[7], (N_ODD, D_MODEL, IN_ODD), jnp.float32) * fan
    w_out_odd = jax.random.normal(ks[8], (N_ODD, WIDTH_C, D_MODEL), jnp.float32) * WIDTH_C ** -0.5
    return {"x": x, "norm_pre": norm_pre, "norm_post": norm_post,
            "w_in_even": w_in_even, "b_f_even": b_f_even,
            "rel_bias_even": rel_bias_even, "w_out_even": w_out_even,
            "w_in_odd": w_in_odd, "w_out_odd": w_out_odd}


def _fwd_reference(x, norm_pre, norm_post, w_in_even, b_f_even, rel_bias_even, w_out_even, w_in_odd, w_out_odd):
    for layer in range(DEPTH):
        h = rms_norm(x, norm_pre[layer])
        if layer % 2 == 0:
            e = layer // 2
            y = even_mixer(h, w_in_even[e], b_f_even[e], rel_bias_even[e], w_out_even[e])
        else:
            o = layer // 2
            y = odd_mixer(h, w_in_odd[o], w_out_odd[o])
        x = x + rms_norm(y, norm_post[layer])
    return x


import jax as _jax
import jax.numpy as _jnp

TWIN_FORMAT = 'train_step'
FWD_PARAMS = ['x', 'norm_pre', 'norm_post', 'w_in_even', 'b_f_even', 'rel_bias_even', 'w_out_even', 'w_in_odd', 'w_out_odd']
TWIN_WEIGHTS = ['norm_pre', 'norm_post', 'w_in_even', 'b_f_even', 'rel_bias_even', 'w_out_even', 'w_in_odd', 'w_out_odd']
TWIN_DIFF_INPUT = 'x'
TWIN_INPUTS = ['x', 'norm_pre', 'norm_post', 'w_in_even', 'b_f_even', 'rel_bias_even', 'w_out_even', 'w_in_odd', 'w_out_odd', 'loss_target', 'm_norm_pre', 'm_norm_post', 'm_w_in_even', 'm_b_f_even', 'm_rel_bias_even', 'm_w_out_even', 'm_w_in_odd', 'm_w_out_odd', 'v_norm_pre', 'v_norm_post', 'v_w_in_even', 'v_b_f_even', 'v_rel_bias_even', 'v_w_out_even', 'v_w_in_odd', 'v_w_out_odd']
TWIN_OUTPUTS = ['loss', 'grad_x', 'grad_norm_pre', 'grad_norm_post', 'grad_w_in_even', 'grad_b_f_even', 'grad_rel_bias_even', 'grad_w_out_even', 'grad_w_in_odd', 'grad_w_out_odd', 'delta_norm_pre', 'delta_norm_post', 'delta_w_in_even', 'delta_b_f_even', 'delta_rel_bias_even', 'delta_w_out_even', 'delta_w_in_odd', 'delta_w_out_odd', 'new_m_norm_pre', 'new_m_norm_post', 'new_m_w_in_even', 'new_m_b_f_even', 'new_m_rel_bias_even', 'new_m_w_out_even', 'new_m_w_in_odd', 'new_m_w_out_odd', 'new_v_norm_pre', 'new_v_norm_post', 'new_v_w_in_even', 'new_v_b_f_even', 'new_v_rel_bias_even', 'new_v_w_out_even', 'new_v_w_in_odd', 'new_v_w_out_odd']
TWIN_LEAF_KINDS = {'loss': 'loss', 'grad_x': 'grad_x', 'grad_norm_pre': 'grad_w', 'grad_norm_post': 'grad_w', 'grad_w_in_even': 'grad_w', 'grad_b_f_even': 'grad_w', 'grad_rel_bias_even': 'grad_w', 'grad_w_out_even': 'grad_w', 'grad_w_in_odd': 'grad_w', 'grad_w_out_odd': 'grad_w', 'delta_norm_pre': 'delta_w', 'delta_norm_post': 'delta_w', 'delta_w_in_even': 'delta_w', 'delta_b_f_even': 'delta_w', 'delta_rel_bias_even': 'delta_w', 'delta_w_out_even': 'delta_w', 'delta_w_in_odd': 'delta_w', 'delta_w_out_odd': 'delta_w', 'new_m_norm_pre': 'new_m', 'new_m_norm_post': 'new_m', 'new_m_w_in_even': 'new_m', 'new_m_b_f_even': 'new_m', 'new_m_rel_bias_even': 'new_m', 'new_m_w_out_even': 'new_m', 'new_m_w_in_odd': 'new_m', 'new_m_w_out_odd': 'new_m', 'new_v_norm_pre': 'new_v', 'new_v_norm_post': 'new_v', 'new_v_w_in_even': 'new_v', 'new_v_b_f_even': 'new_v', 'new_v_rel_bias_even': 'new_v', 'new_v_w_out_even': 'new_v', 'new_v_w_in_odd': 'new_v', 'new_v_w_out_odd': 'new_v'}


def _forward(args):
    return _fwd_reference(*[args[k] for k in FWD_PARAMS])


def _output_shape():
    out = _jax.eval_shape(lambda: _forward(_fwd_setup_inputs(0)))
    return out.shape, out.dtype

N_MICROBATCH = 1
ADAM_LR = 0.001
ADAM_B1 = 0.9
ADAM_B2 = 0.999
ADAM_EPS = 1e-08
ADAM_WD = 0.01
ADAM_STEP = 10
PER_EXAMPLE_BATCH_AXIS = {'x': 0, 'loss_target': 0}
SHARED_INPUTS = []
_WEIGHT_DTYPES = {'norm_pre': _jnp.float32, 'norm_post': _jnp.float32, 'w_in_even': _jnp.float32, 'b_f_even': _jnp.float32, 'rel_bias_even': _jnp.float32, 'w_out_even': _jnp.float32, 'w_in_odd': _jnp.float32, 'w_out_odd': _jnp.float32}
MOMENT_SCALE = {'norm_pre': 1.714748e-01, 'norm_post': 8.004769e+00, 'w_in_even': 1.052212e-01, 'b_f_even': 9.042854e-01, 'rel_bias_even': 3.596657e-02, 'w_out_even': 1.168753e-01, 'w_in_odd': 6.073503e-02, 'w_out_odd': 7.739913e-02}


def _to_microbatches(a, axis):
    t = _jnp.moveaxis(a, axis, 0)
    t = t.reshape((N_MICROBATCH, t.shape[0] // N_MICROBATCH) + t.shape[1:])
    return _jnp.moveaxis(t, 1, axis + 1)


def setup_inputs(seed: int = 0) -> dict:
    inp = _fwd_setup_inputs(seed)
    key = _jax.random.fold_in(_jax.random.key(seed), 7919)
    shape, _ = _output_shape()
    out = dict(inp)
    out["loss_target"] = _jax.random.normal(_jax.random.fold_in(key, 0), shape, _jnp.float32)
    for i, name in enumerate(TWIN_WEIGHTS):
        w = inp[name].astype(_jnp.float32)
        if MOMENT_SCALE is None:
            s = _jnp.sqrt(_jnp.mean(_jnp.square(w)) + 1e-30)
        else:
            s = MOMENT_SCALE[name]
        km, kv = _jax.random.split(_jax.random.fold_in(key, i + 1))
        out[name] = w
        out["m_" + name] = s * _jax.random.normal(km, w.shape, _jnp.float32)
        out["v_" + name] = (s * s) * _jax.random.uniform(kv, w.shape, _jnp.float32, 0.5, 1.5)
    if N_MICROBATCH > 1:
        for name, axis in PER_EXAMPLE_BATCH_AXIS.items():
            out[name] = _to_microbatches(out[name], axis)
    return {'x': out['x'], 'norm_pre': out['norm_pre'], 'norm_post': out['norm_post'], 'w_in_even': out['w_in_even'], 'b_f_even': out['b_f_even'], 'rel_bias_even': out['rel_bias_even'], 'w_out_even': out['w_out_even'], 'w_in_odd': out['w_in_odd'], 'w_out_odd': out['w_out_odd'], 'loss_target': out['loss_target'], 'm_norm_pre': out['m_norm_pre'], 'm_norm_post': out['m_norm_post'], 'm_w_in_even': out['m_w_in_even'], 'm_b_f_even': out['m_b_f_even'], 'm_rel_bias_even': out['m_rel_bias_even'], 'm_w_out_even': out['m_w_out_even'], 'm_w_in_odd': out['m_w_in_odd'], 'm_w_out_odd': out['m_w_out_odd'], 'v_norm_pre': out['v_norm_pre'], 'v_norm_post': out['v_norm_post'], 'v_w_in_even': out['v_w_in_even'], 'v_b_f_even': out['v_b_f_even'], 'v_rel_bias_even': out['v_rel_bias_even'], 'v_w_out_even': out['v_w_out_even'], 'v_w_in_odd': out['v_w_in_odd'], 'v_w_out_odd': out['v_w_out_odd']}


def _loss(weights, diff, rest, loss_target):
    with _jax.named_scope("forward"):
        args = {**rest, TWIN_DIFF_INPUT: diff, **{k: w.astype(_WEIGHT_DTYPES[k]) for k, w in weights.items()}}
        y = _forward(args)
    with _jax.named_scope("loss_head"):
        err = _jnp.square(y.astype(_jnp.float32) - loss_target)
        return 0.5 * _jnp.sum(_jnp.mean(err, axis=-1)) if err.ndim else 0.5 * err


def _adamw(w, g, m, v):
    m = ADAM_B1 * m + (1.0 - ADAM_B1) * g
    v = ADAM_B2 * v + (1.0 - ADAM_B2) * _jnp.square(g)
    m_hat = m / (1.0 - ADAM_B1 ** ADAM_STEP)
    v_hat = v / (1.0 - ADAM_B2 ** ADAM_STEP)
    delta = -ADAM_LR * (m_hat / (_jnp.sqrt(v_hat) + ADAM_EPS) + ADAM_WD * w)
    return delta, m, v


def reference(x, norm_pre, norm_post, w_in_even, b_f_even, rel_bias_even, w_out_even, w_in_odd, w_out_odd, loss_target, m_norm_pre, m_norm_post, m_w_in_even, m_b_f_even, m_rel_bias_even, m_w_out_even, m_w_in_odd, m_w_out_odd, v_norm_pre, v_norm_post, v_w_in_even, v_b_f_even, v_rel_bias_even, v_w_out_even, v_w_in_odd, v_w_out_odd):
    given = dict(x=x, norm_pre=norm_pre, norm_post=norm_post, w_in_even=w_in_even, b_f_even=b_f_even, rel_bias_even=rel_bias_even, w_out_even=w_out_even, w_in_odd=w_in_odd, w_out_odd=w_out_odd, loss_target=loss_target, m_norm_pre=m_norm_pre, m_norm_post=m_norm_post, m_w_in_even=m_w_in_even, m_b_f_even=m_b_f_even, m_rel_bias_even=m_rel_bias_even, m_w_out_even=m_w_out_even, m_w_in_odd=m_w_in_odd, m_w_out_odd=m_w_out_odd, v_norm_pre=v_norm_pre, v_norm_post=v_norm_post, v_w_in_even=v_w_in_even, v_b_f_even=v_b_f_even, v_rel_bias_even=v_rel_bias_even, v_w_out_even=v_w_out_even, v_w_in_odd=v_w_in_odd, v_w_out_odd=v_w_out_odd)
    weights = {n: given[n] for n in TWIN_WEIGHTS}
    shared = {n: given[n] for n in SHARED_INPUTS}
    per_example = {n: given[n] for n in ['x']}
    grad_fn = _jax.value_and_grad(_loss, argnums=(0, 1))

    def one_microbatch(ex, loss_target):
        ex = dict(ex)
        diff = ex.pop(TWIN_DIFF_INPUT)
        return grad_fn(weights, diff, {**shared, **ex}, loss_target)

    if N_MICROBATCH == 1:
        loss, (grad_w, grad_x) = one_microbatch(per_example, given["loss_target"])
    else:
        def body(carry, xs):
            loss_sum, grad_sum = carry
            l_k, (gw_k, gx_k) = one_microbatch(xs[0], xs[1])
            with _jax.named_scope("update"):
                return (loss_sum + l_k, _jax.tree.map(_jnp.add, grad_sum, gw_k)), gx_k

        init = (_jnp.zeros((), _jnp.float32), _jax.tree.map(_jnp.zeros_like, weights))
        (loss, grad_w), grad_x = _jax.lax.scan(body, init, (per_example, given["loss_target"]))
    with _jax.named_scope("update"):
        delta_w, new_m, new_v = {}, {}, {}
        for n in TWIN_WEIGHTS:
            delta_w[n], new_m[n], new_v[n] = _adamw(weights[n], grad_w[n], given["m_" + n], given["v_" + n])
    return (loss, grad_x, *[grad_w[n] for n in TWIN_WEIGHTS], *[delta_w[n] for n in TWIN_WEIGHTS],
            *[new_m[n] for n in TWIN_WEIGHTS], *[new_v[n] for n in TWIN_WEIGHTS])
```

```python
import functools

import numpy as np
import jax
import jax.numpy as jnp
from jax import lax
from jax.experimental import pallas as pl
from jax.experimental.pallas import tpu as pltpu

F32 = jnp.float32
BF16 = jnp.bfloat16

HEAD_DIM = 128
CHUNK = 64
LEFT_CHUNKS = 8
REL_CLIP = 128
N_REL = 2 * REL_CLIP + 1
RMS_EPS = 1e-6
SCALE = HEAD_DIM ** -0.5

ADAM_LR = 0.001
ADAM_B1 = 0.9
ADAM_B2 = 0.999
ADAM_EPS = 1e-08
ADAM_WD = 0.01
ADAM_STEP = 10

N_DEV = 8
V7X_VMEM_LIMIT_BYTES = 56 * 1024 * 1024
NEG = -1e30

NT = (((1,), (1,)), ((), ()))
TN = (((0,), (0,)), ((), ()))
NN = (((1,), (0,)), ((), ()))

CQ = 256
BAND_TILES = 3
TOEP = 2 * CQ
assert (BAND_TILES - 1) * CQ == LEFT_CHUNKS * CHUNK


def _params(sem):
    return pltpu.CompilerParams(dimension_semantics=sem, vmem_limit_bytes=V7X_VMEM_LIMIT_BYTES)


def _split3(x):
    hi = x.astype(BF16)
    r1 = x - hi.astype(F32)
    mid = r1.astype(BF16)
    lo = (r1 - mid.astype(F32)).astype(BF16)
    return hi, mid, lo


def _split2(x):
    hi = x.astype(BF16)
    lo = (x - hi.astype(F32)).astype(BF16)
    return hi, lo


def _sigmoid(g):
    return 1.0 / (1.0 + jnp.exp(-g))


def _tile(n, cap, *offsets):
    if n <= 128:
        return n
    t = (min(cap, n) // 128) * 128
    while n % t or any(o % t for o in offsets):
        t -= 128
    return t


def _matmul(a, b, *, mode, m, n, k, out_dtype, name, a_off=(0, 0), b_off=(0, 0), tm=1024, tn=1024, tk=512):
    a_m, a_k = (a_off if mode in ("nn", "nt") else a_off[::-1])
    b_k, b_n = (b_off if mode in ("nn", "tn") else b_off[::-1])
    tm, tn, tk = _tile(m, tm, a_m), _tile(n, tn, b_n), _tile(k, tk, a_k, b_k)
    nk = k // tk
    if mode in ("nn", "nt"):
        ao = (a_off[0] // tm, a_off[1] // tk)
        a_spec = pl.BlockSpec((tm, tk), lambda i, j, l: (i + ao[0], l + ao[1]))
    else:
        ao = (a_off[0] // tk, a_off[1] // tm)
        a_spec = pl.BlockSpec((tk, tm), lambda i, j, l: (l + ao[0], i + ao[1]))
    if mode in ("nn", "tn"):
        bo = (b_off[0] // tk, b_off[1] // tn)
        b_spec = pl.BlockSpec((tk, tn), lambda i, j, l: (l + bo[0], j + bo[1]))
    else:
        bo = (b_off[0] // tn, b_off[1] // tk)
        b_spec = pl.BlockSpec((tn, tk), lambda i, j, l: (j + bo[0], l + bo[1]))
    dn = {"nn": NN, "nt": NT, "tn": TN}[mode]

    def body(a_ref, b_ref, o_ref, acc_ref):
        @pl.when(pl.program_id(2) == 0)
        def _():
            acc_ref[...] = jnp.zeros_like(acc_ref)

        acc_ref[...] += lax.dot_general(a_ref[...], b_ref[...], dn, preferred_element_type=F32)

        @pl.when(pl.program_id(2) == nk - 1)
        def _():
            o_ref[...] = acc_ref[...].astype(out_dtype)

    return pl.pallas_call(
        body,
        name=name,
        grid=(m // tm, n // tn, nk),
        in_specs=[a_spec, b_spec],
        out_specs=pl.BlockSpec((tm, tn), lambda i, j, l: (i, j)),
        out_shape=jax.ShapeDtypeStruct((m, n), out_dtype),
        scratch_shapes=[pltpu.VMEM((tm, tn), F32)],
        compiler_params=_params(("parallel", "parallel", "arbitrary")),
    )(a, b)


ROWS = 128


def _rms_fwd(x, gain, *, name):
    s, d = x.shape

    def body(x_ref, g_ref, h_ref):
        xf = x_ref[...]
        r = lax.rsqrt(jnp.mean(xf * xf, axis=-1, keepdims=True) + RMS_EPS)
        h_ref[...] = ((xf * r) * g_ref[...]).astype(BF16)

    return pl.pallas_call(
        body, name=name, grid=(s // ROWS,),
        in_specs=[pl.BlockSpec((ROWS, d), lambda i: (i, 0)), pl.BlockSpec((1, d), lambda i: (0, 0))],
        out_specs=pl.BlockSpec((ROWS, d), lambda i: (i, 0)),
        out_shape=jax.ShapeDtypeStruct((s, d), BF16),
        compiler_params=_params(("parallel",)),
    )(x, gain)


def _post_fwd(x, y, gain, *, name):
    s, d = x.shape

    def body(x_ref, y_ref, g_ref, o_ref):
        yf = y_ref[...]
        r = lax.rsqrt(jnp.mean(yf * yf, axis=-1, keepdims=True) + RMS_EPS)
        o_ref[...] = x_ref[...] + (yf * r) * g_ref[...]

    return pl.pallas_call(
        body, name=name, grid=(s // ROWS,),
        in_specs=[pl.BlockSpec((ROWS, d), lambda i: (i, 0)), pl.BlockSpec((ROWS, d), lambda i: (i, 0)),
                  pl.BlockSpec((1, d), lambda i: (0, 0))],
        out_specs=pl.BlockSpec((ROWS, d), lambda i: (i, 0)),
        out_shape=jax.ShapeDtypeStruct((s, d), F32),
        compiler_params=_params(("parallel",)),
    )(x, y, gain)


def _loss_head(xo, target, *, name):
    s, d = xo.shape
    inv_d = 1.0 / d

    def body(x_ref, t_ref, loss_ref, dx_ref):
        @pl.when(pl.program_id(0) == 0)
        def _():
            loss_ref[...] = jnp.zeros_like(loss_ref)

        e = x_ref[...] - t_ref[...]
        dx_ref[...] = e * inv_d
        loss_ref[...] += 0.5 * jnp.sum(jnp.mean(e * e, axis=-1, keepdims=True), axis=0, keepdims=True)

    return pl.pallas_call(
        body, name=name, grid=(s // ROWS,),
        in_specs=[pl.BlockSpec((ROWS, d), lambda i: (i, 0)), pl.BlockSpec((ROWS, d), lambda i: (i, 0))],
        out_specs=[pl.BlockSpec((1, 1), lambda i: (0, 0)), pl.BlockSpec((ROWS, d), lambda i: (i, 0))],
        out_shape=[jax.ShapeDtypeStruct((1, 1), F32), jax.ShapeDtypeStruct((s, d), F32)],
        compiler_params=_params(("arbitrary",)),
    )(xo, target)


def _post_bwd(dxo, y, gain, *, name):
    s, d = y.shape

    def body(dx_ref, y_ref, g_ref, dy_ref, dg_ref):
        @pl.when(pl.program_id(0) == 0)
        def _():
            dg_ref[...] = jnp.zeros_like(dg_ref)

        yf = y_ref[...]
        dxo_ = dx_ref[...]
        r = lax.rsqrt(jnp.mean(yf * yf, axis=-1, keepdims=True) + RMS_EPS)
        nrm = yf * r
        dg_ref[...] += jnp.sum(dxo_ * nrm, axis=0, keepdims=True)
        dn = dxo_ * g_ref[...]
        dy_ref[...] = (r * (dn - nrm * jnp.mean(dn * nrm, axis=-1, keepdims=True))).astype(BF16)

    return pl.pallas_call(
        body, name=name, grid=(s // ROWS,),
        in_specs=[pl.BlockSpec((ROWS, d), lambda i: (i, 0)), pl.BlockSpec((ROWS, d), lambda i: (i, 0)),
                  pl.BlockSpec((1, d), lambda i: (0, 0))],
        out_specs=[pl.BlockSpec((ROWS, d), lambda i: (i, 0)), pl.BlockSpec((1, d), lambda i: (0, 0))],
        out_shape=[jax.ShapeDtypeStruct((s, d), BF16), jax.ShapeDtypeStruct((1, d), F32)],
        compiler_params=_params(("arbitrary",)),
    )(dxo, y, gain)


def _pre_bwd(dhs, x, gain, dres, *, name):
    s, d = x.shape
    n_dh = len(dhs)

    def body(*refs):
        dh_refs = refs[:n_dh]
        x_ref, g_ref, dr_ref, dx_ref, dg_ref = refs[n_dh:]

        @pl.when(pl.program_id(0) == 0)
        def _():
            dg_ref[...] = jnp.zeros_like(dg_ref)

        xf = x_ref[...]
        dh_ = dh_refs[0][...]
        for extra in dh_refs[1:]:
            dh_ = dh_ + extra[...]
        r = lax.rsqrt(jnp.mean(xf * xf, axis=-1, keepdims=True) + RMS_EPS)
        nrm = xf * r
        dg_ref[...] += jnp.sum(dh_ * nrm, axis=0, keepdims=True)
        dn = dh_ * g_ref[...]
        dx_ref[...] = dr_ref[...] + r * (dn - nrm * jnp.mean(dn * nrm, axis=-1, keepdims=True))

    return pl.pallas_call(
        body, name=name, grid=(s // ROWS,),
        in_specs=[pl.BlockSpec((ROWS, d), lambda i: (i, 0))] * (n_dh + 1)
        + [pl.BlockSpec((1, d), lambda i: (0, 0)), pl.BlockSpec((ROWS, d), lambda i: (i, 0))],
        out_specs=[pl.BlockSpec((ROWS, d), lambda i: (i, 0)), pl.BlockSpec((1, d), lambda i: (0, 0))],
        out_shape=[jax.ShapeDtypeStruct((s, d), F32), jax.ShapeDtypeStruct((1, d), F32)],
        compiler_params=_params(("arbitrary",)),
    )(*dhs, x, gain, dres)


GB = 256


def _gate_fwd(af, bias, *, name):
    s, w = af.shape

    def body(af_ref, b_ref, c_ref, carry_ref):
        @pl.when(pl.program_id(0) == 0)
        def _():
            carry_ref[...] = jnp.zeros_like(carry_ref)

        z = af_ref[...] + b_ref[...]
        lf = jnp.minimum(z, 0.0) - jnp.log(1.0 + jnp.exp(-jnp.abs(z)))
        r_i = lax.broadcasted_iota(jnp.int32, (GB, GB), 0)
        c_i = lax.broadcasted_iota(jnp.int32, (GB, GB), 1)
        tri = (c_i <= r_i).astype(BF16)
        hi, mid, lo = _split3(lf)
        pre = (jnp.dot(tri, hi, preferred_element_type=F32) + jnp.dot(tri, mid, preferred_element_type=F32)
               + jnp.dot(tri, lo, preferred_element_type=F32))
        c_ref[...] = pre + carry_ref[...]
        carry_ref[...] += jnp.sum(lf, axis=0, keepdims=True)

    return pl.pallas_call(
        body, name=name, grid=(s // GB,),
        in_specs=[pl.BlockSpec((GB, w), lambda i: (i, 0)), pl.BlockSpec((1, w), lambda i: (0, 0))],
        out_specs=pl.BlockSpec((GB, w), lambda i: (i, 0)),
        out_shape=jax.ShapeDtypeStruct((s, w), F32),
        scratch_shapes=[pltpu.VMEM((1, w), F32)],
        compiler_params=_params(("arbitrary",)),
    )(af, bias)


def _gate_bwd(dc, af, bias, *, name):
    s, w = af.shape
    nb = s // GB

    def body(dc_ref, af_ref, b_ref, daf_ref, db_ref, carry_ref):
        @pl.when(pl.program_id(0) == 0)
        def _():
            carry_ref[...] = jnp.zeros_like(carry_ref)
            db_ref[...] = jnp.zeros_like(db_ref)

        dcb = dc_ref[...]
        r_i = lax.broadcasted_iota(jnp.int32, (GB, GB), 0)
        c_i = lax.broadcasted_iota(jnp.int32, (GB, GB), 1)
        tri = (c_i >= r_i).astype(BF16)
        hi, mid, lo = _split3(dcb)
        suf = (jnp.dot(tri, hi, preferred_element_type=F32) + jnp.dot(tri, mid, preferred_element_type=F32)
               + jnp.dot(tri, lo, preferred_element_type=F32)) + carry_ref[...]
        carry_ref[...] += jnp.sum(dcb, axis=0, keepdims=True)
        z = af_ref[...] + b_ref[...]
        daf = suf * _sigmoid(-z)
        daf_ref[...] = daf.astype(BF16)
        db_ref[...] += jnp.sum(daf, axis=0, keepdims=True)

    return pl.pallas_call(
        body, name=name, grid=(nb,),
        in_specs=[pl.BlockSpec((GB, w), lambda i: (nb - 1 - i, 0)), pl.BlockSpec((GB, w), lambda i: (nb - 1 - i, 0)),
                  pl.BlockSpec((1, w), lambda i: (0, 0))],
        out_specs=[pl.BlockSpec((GB, w), lambda i: (nb - 1 - i, 0)), pl.BlockSpec((1, w), lambda i: (0, 0))],
        out_shape=[jax.ShapeDtypeStruct((s, w), BF16), jax.ShapeDtypeStruct((1, w), F32)],
        scratch_shapes=[pltpu.VMEM((1, w), F32)],
        compiler_params=_params(("arbitrary",)),
    )(dc, af, bias)


def _rel_index_rows():
    w = np.arange(TOEP)
    wp = np.where(w < CQ, w, w - TOEP)
    return np.stack([np.clip(LEFT_CHUNKS * CHUNK - CQ * j - wp, -REL_CLIP, REL_CLIP) + REL_CLIP
                     for j in range(BAND_TILES)]).astype(np.int32)


def _skew_rows(xw, sign):
    row = lax.broadcasted_iota(jnp.int32, xw.shape, 0)
    for b in range(CQ.bit_length() - 1):
        amt = (1 << b) if sign > 0 else TOEP - (1 << b)
        xw = jnp.where(((row >> b) & 1) == 1, pltpu.roll(xw, amt, 1), xw)
    return xw


REL_PAD = 384


def _rel_onehot():
    return jnp.asarray(_rel_index_rows()[:, :, None] == np.arange(REL_PAD)[None, None, :], BF16)


def _fill_bias_tiles(rel_ref, oh_ref, bias_scr):
    parts = _split3(jnp.broadcast_to(rel_ref[...], (8, REL_PAD)))
    for j in range(BAND_TILES):
        row = sum(lax.dot_general(p, oh_ref[j], NT, preferred_element_type=F32) for p in parts)[0:1]
        bias_scr[j] = _skew_rows(jnp.broadcast_to(row, (CQ, TOEP)), +1)[:, :CQ]


def _fox_scores(s, q0, k0, cq, cr):
    bq, bk = s.shape
    tq = q0 + lax.broadcasted_iota(jnp.int32, (bq, bk), 0)
    tk = k0 + lax.broadcasted_iota(jnp.int32, (bq, bk), 1)
    return jnp.where(tk <= tq, s + (cq - cr), NEG)


def _chunk_scores(s, q0, k0, bias):
    bq, bk = s.shape
    qc = (q0 + lax.broadcasted_iota(jnp.int32, (bq, bk), 0)) >> 6
    kc = (k0 + lax.broadcasted_iota(jnp.int32, (bq, bk), 1)) >> 6
    return jnp.where((kc <= qc) & (kc >= qc - LEFT_CHUNKS), s + bias, NEG)


def _softmax_fwd(qkv, gate, aux, *, mode, width, name):
    s = qkv.shape[0]
    nh = width // HEAD_DIM
    bq = bk = CQ
    nq = s // bq

    def body(q_ref, k_ref, v_ref, g_ref, *rest):
        if mode == "fox":
            cc_ref, cr_ref, mixed_ref, o_ref, lse_ref = rest
        else:
            rel_ref, oh_ref, mixed_ref, o_ref, lse_ref, bias_scr = rest
        qi = pl.program_id(1)
        q0 = qi * bq
        q = q_ref[...]

        if mode == "chunk":
            @pl.when(qi == 0)
            def _():
                _fill_bias_tiles(rel_ref, oh_ref, bias_scr)

        if mode == "fox":
            cq = cc_ref[...]
            lo, hi = 0, qi + 1
        else:
            lo, hi = jnp.maximum(qi - (BAND_TILES - 1), 0), qi + 1

        def step(ki, carry):
            m, l, acc = carry
            k0 = pl.multiple_of(ki * bk, bk)
            kt = k_ref[pl.ds(k0, bk), :]
            vt = v_ref[pl.ds(k0, bk), :]
            sc = lax.dot_general(q, kt, NT, preferred_element_type=F32) * SCALE
            if mode == "fox":
                sc = _fox_scores(sc, q0, k0, cq, cr_ref[ki])
            else:
                sc = _chunk_scores(sc, q0, k0, bias_scr[ki - qi + (BAND_TILES - 1)])
            m_new = jnp.maximum(m, jnp.max(sc, axis=-1, keepdims=True))
            p = jnp.exp(sc - m_new)
            alpha = jnp.exp(m - m_new)
            l = alpha * l + jnp.sum(p, axis=-1, keepdims=True)
            acc = alpha * acc + jnp.dot(p.astype(BF16), vt, preferred_element_type=F32)
            return m_new, l, acc

        init = (jnp.full((bq, 1), NEG, F32), jnp.zeros((bq, 1), F32), jnp.zeros((bq, HEAD_DIM), F32))
        m, l, acc = lax.fori_loop(lo, hi, step, init)
        o = acc / l
        g = g_ref[...]
        o_ref[...] = o
        mixed_ref[...] = (o * (g * _sigmoid(g))).astype(BF16)
        lse_ref[...] = m + jnp.log(l)

    head_col = lambda off: pl.BlockSpec((s, HEAD_DIM), lambda h, i: (0, off + h))
    in_specs = [pl.BlockSpec((bq, HEAD_DIM), lambda h, i: (i, h)), head_col(nh), head_col(2 * nh),
                pl.BlockSpec((bq, HEAD_DIM), lambda h, i: (i, h))]
    scratch = []
    if mode == "fox":
        in_specs += [pl.BlockSpec((None, bq, 1), lambda h, i: (h, i, 0)),
                     pl.BlockSpec((None, nq, 1, bk), lambda h, i: (h, 0, 0, 0))]
    else:
        in_specs += [pl.BlockSpec((None, 1, REL_PAD), lambda h, i: (h, 0, 0)),
                     pl.BlockSpec((BAND_TILES, TOEP, REL_PAD), lambda h, i: (0, 0, 0))]
        scratch = [pltpu.VMEM((BAND_TILES, CQ, CQ), F32)]
    return pl.pallas_call(
        body, name=name, grid=(nh, nq), in_specs=in_specs,
        out_specs=[pl.BlockSpec((bq, HEAD_DIM), lambda h, i: (i, h)), pl.BlockSpec((bq, HEAD_DIM), lambda h, i: (i, h)),
                   pl.BlockSpec((None, bq, 1), lambda h, i: (h, i, 0))],
        out_shape=[jax.ShapeDtypeStruct((s, width), BF16), jax.ShapeDtypeStruct((s, width), F32),
                   jax.ShapeDtypeStruct((nh, s, 1), F32)],
        scratch_shapes=scratch,
        compiler_params=_params(("parallel", "arbitrary")),
    )(qkv, qkv, qkv, gate, *aux)


def _softmax_bwd(qkv, gate, o, dmixed, dm_off, lse, aux, *, mode, width, name):
    s = qkv.shape[0]
    nh = width // HEAD_DIM
    bq = bk = CQ
    nq = s // bq
    dmo = dm_off // HEAD_DIM
    rel_pad = REL_PAD

    def body(q_ref, k_ref, v_ref, g_ref, o_ref, dm_ref, lse_ref, *rest):
        if mode == "fox":
            cc_ref, cr_ref, dq_ref, dk_ref, dv_ref, dg_ref, dc_ref, dcq_ref, dk_scr, dv_scr, dc_scr = rest
        else:
            rel_ref, oh_ref, dq_ref, dk_ref, dv_ref, dg_ref, drel_ref, dk_scr, dv_scr, bias_scr, db_scr = rest
        qi = pl.program_id(1)
        q0 = qi * bq

        @pl.when(qi == 0)
        def _():
            dk_scr[...] = jnp.zeros_like(dk_scr)
            dv_scr[...] = jnp.zeros_like(dv_scr)
            if mode == "fox":
                dc_scr[...] = jnp.zeros_like(dc_scr)
            else:
                db_scr[...] = jnp.zeros_like(db_scr)
                _fill_bias_tiles(rel_ref, oh_ref, bias_scr)

        g = g_ref[...]
        of = o_ref[...]
        dm = dm_ref[...]
        sig = _sigmoid(g)
        do = dm * (g * sig)
        dg_ref[...] = (dm * of * (sig * (1.0 + g * (1.0 - sig)))).astype(BF16)
        delta = jnp.sum(do * of, axis=-1, keepdims=True)
        do_b = do.astype(BF16)
        q = q_ref[...]
        lse_q = lse_ref[...]
        if mode == "fox":
            cq = cc_ref[...]
            lo, hi = 0, qi + 1
        else:
            lo, hi = jnp.maximum(qi - (BAND_TILES - 1), 0), qi + 1

        def step(ki, carry):
            dq, rsum = carry
            k0 = pl.multiple_of(ki * bk, bk)
            kt = k_ref[pl.ds(k0, bk), :]
            vt = v_ref[pl.ds(k0, bk), :]
            sc = lax.dot_general(q, kt, NT, preferred_element_type=F32) * SCALE
            if mode == "fox":
                sc = _fox_scores(sc, q0, k0, cq, cr_ref[ki])
            else:
                sc = _chunk_scores(sc, q0, k0, bias_scr[ki - qi + (BAND_TILES - 1)])
            p = jnp.exp(sc - lse_q)
            dp = lax.dot_general(do_b, vt, NT, preferred_element_type=F32)
            ds = p * (dp - delta)
            if mode == "fox":
                dc_scr[ki] += -jnp.sum(ds, axis=0, keepdims=True)
                rsum = rsum + jnp.sum(ds, axis=-1, keepdims=True)
            else:
                db_scr[ki - qi + (BAND_TILES - 1)] += ds
            ds_b = (ds * SCALE).astype(BF16)
            dk_scr[pl.ds(k0, bk), :] += lax.dot_general(ds_b, q, TN, preferred_element_type=F32)
            dv_scr[pl.ds(k0, bk), :] += lax.dot_general(p.astype(BF16), do_b, TN, preferred_element_type=F32)
            return dq + jnp.dot(ds_b, kt, preferred_element_type=F32), rsum

        dq, rsum = lax.fori_loop(lo, hi, step, (jnp.zeros((bq, HEAD_DIM), F32), jnp.zeros((bq, 1), F32)))
        dq_ref[...] = dq.astype(BF16)
        if mode == "fox":
            dcq_ref[...] = rsum

        @pl.when(qi == nq - 1)
        def _():
            dk_ref[...] = dk_scr[...].astype(BF16)
            dv_ref[...] = dv_scr[...].astype(BF16)
            if mode == "fox":
                dc_ref[...] = dc_scr[...]
            else:
                tot = jnp.zeros((8, rel_pad), F32)
                for j in range(BAND_TILES):
                    wide = jnp.concatenate([db_scr[j], jnp.zeros((CQ, TOEP - CQ), F32)], axis=1)
                    diag = jnp.sum(_skew_rows(wide, -1), axis=0, keepdims=True)
                    for part in _split3(jnp.broadcast_to(diag, (8, TOEP))):
                        tot = tot + jnp.dot(part, oh_ref[j], preferred_element_type=F32)
                drel_ref[...] = tot[0:1, :]

    head_col = lambda off: pl.BlockSpec((s, HEAD_DIM), lambda h, i: (0, off + h))
    qblk = lambda off: pl.BlockSpec((bq, HEAD_DIM), lambda h, i: (i, off + h))
    in_specs = [qblk(0), head_col(nh), head_col(2 * nh), qblk(0), qblk(0), qblk(dmo),
                pl.BlockSpec((None, bq, 1), lambda h, i: (h, i, 0))]
    out_specs = [qblk(0), head_col(0), head_col(0), qblk(0)]
    out_shape = [jax.ShapeDtypeStruct((s, width), BF16)] * 4
    scratch = [pltpu.VMEM((s, HEAD_DIM), F32), pltpu.VMEM((s, HEAD_DIM), F32)]
    if mode == "fox":
        in_specs += [pl.BlockSpec((None, bq, 1), lambda h, i: (h, i, 0)),
                     pl.BlockSpec((None, nq, 1, bk), lambda h, i: (h, 0, 0, 0))]
        out_specs += [pl.BlockSpec((None, nq, 1, bk), lambda h, i: (h, 0, 0, 0)),
                      pl.BlockSpec((None, bq, 1), lambda h, i: (h, i, 0))]
        out_shape += [jax.ShapeDtypeStruct((nh, nq, 1, bk), F32), jax.ShapeDtypeStruct((nh, s, 1), F32)]
        scratch += [pltpu.VMEM((nq, 1, bk), F32)]
    else:
        in_specs += [pl.BlockSpec((None, 1, rel_pad), lambda h, i: (h, 0, 0)),
                     pl.BlockSpec((BAND_TILES, TOEP, rel_pad), lambda h, i: (0, 0, 0))]
        out_specs += [pl.BlockSpec((None, 1, rel_pad), lambda h, i: (h, 0, 0))]
        out_shape += [jax.ShapeDtypeStruct((nh, 1, rel_pad), F32)]
        scratch += [pltpu.VMEM((BAND_TILES, CQ, CQ), F32), pltpu.VMEM((BAND_TILES, CQ, CQ), F32)]
    return pl.pallas_call(
        body, name=name, grid=(nh, nq), in_specs=in_specs, out_specs=out_specs, out_shape=out_shape,
        scratch_shapes=scratch, compiler_params=_params(("parallel", "arbitrary")),
    )(qkv, qkv, qkv, gate, o, dmixed, lse, *aux)


SBQ = 256
SBK = 128


def _suffix_excl(x, tri):
    hi, lo = _split2(x)
    return jnp.dot(hi, tri, preferred_element_type=F32) + jnp.dot(lo, tri, preferred_element_type=F32)


def _sb_tile(q, kt, q0, k0):
    z = lax.dot_general(q, kt, NT, preferred_element_type=F32) * SCALE
    tq = q0 + lax.broadcasted_iota(jnp.int32, z.shape, 0)
    tk = k0 + lax.broadcasted_iota(jnp.int32, z.shape, 1)
    causal = tk < tq
    e = jnp.exp(-jnp.abs(z))
    sp = jnp.log(1.0 + e)
    lb = jnp.minimum(z, 0.0) - sp
    lom = jnp.where(causal, jnp.minimum(-z, 0.0) - sp, 0.0)
    return z, causal, lom, lb, e


def _sb_fwd(qkv, gate, *, width, name):
    s = qkv.shape[0]
    nh = width // HEAD_DIM
    nq = s // SBQ
    ratio = SBQ // SBK

    def body(q_ref, k_ref, v_ref, g_ref, mixed_ref, o_ref, ltot_ref):
        qi = pl.program_id(1)
        q0 = qi * SBQ
        q = q_ref[...]
        tri = (lax.broadcasted_iota(jnp.int32, (SBK, SBK), 0) > lax.broadcasted_iota(jnp.int32, (SBK, SBK), 1)).astype(BF16)
        nkb = (qi + 1) * ratio

        def step(t, carry):
            run, acc = carry
            k0 = pl.multiple_of((nkb - 1 - t) * SBK, SBK)
            kt = k_ref[pl.ds(k0, SBK), :]
            vt = v_ref[pl.ds(k0, SBK), :]
            _, causal, lom, lb, _ = _sb_tile(q, kt, q0, k0)
            tail = _suffix_excl(lom, tri) + run
            a = jnp.where(causal, jnp.exp(lb + tail), 0.0)
            acc = acc + jnp.dot(a.astype(BF16), vt, preferred_element_type=F32)
            return run + jnp.sum(lom, axis=-1, keepdims=True), acc

        ltot, o = lax.fori_loop(0, nkb, step, (jnp.zeros((SBQ, 1), F32), jnp.zeros((SBQ, HEAD_DIM), F32)))
        g = g_ref[...]
        o_ref[...] = o
        mixed_ref[...] = (o * (g * _sigmoid(g))).astype(BF16)
        ltot_ref[...] = ltot

    head_col = lambda off: pl.BlockSpec((s, HEAD_DIM), lambda h, i: (0, off + h))
    qblk = pl.BlockSpec((SBQ, HEAD_DIM), lambda h, i: (i, h))
    stat = pl.BlockSpec((None, SBQ, 1), lambda h, i: (h, i, 0))
    return pl.pallas_call(
        body, name=name, grid=(nh, nq), in_specs=[qblk, head_col(nh), head_col(2 * nh), qblk],
        out_specs=[qblk, qblk, stat],
        out_shape=[jax.ShapeDtypeStruct((s, width), BF16), jax.ShapeDtypeStruct((s, width), F32),
                   jax.ShapeDtypeStruct((nh, s, 1), F32)],
        compiler_params=_params(("parallel", "arbitrary")),
    )(qkv, qkv, qkv, gate)


def _sb_bwd(qkv, gate, o, ltot, dmixed, *, width, name):
    s = qkv.shape[0]
    nh = width // HEAD_DIM
    nq = s // SBQ
    ratio = SBQ // SBK

    def body(q_ref, k_ref, v_ref, g_ref, o_ref, lt_ref, dm_ref, dq_ref, dk_ref, dv_ref, dg_ref, dk_scr, dv_scr):
        qi = pl.program_id(1)
        q0 = qi * SBQ

        @pl.when(qi == 0)
        def _():
            dk_scr[...] = jnp.zeros_like(dk_scr)
            dv_scr[...] = jnp.zeros_like(dv_scr)

        g = g_ref[...]
        of = o_ref[...]
        dm = dm_ref[...]
        sig = _sigmoid(g)
        do = dm * (g * sig)
        dg_ref[...] = (dm * of * (sig * (1.0 + g * (1.0 - sig)))).astype(BF16)
        do_b = do.astype(BF16)
        q = q_ref[...]
        r_i = lax.broadcasted_iota(jnp.int32, (SBK, SBK), 0)
        c_i = lax.broadcasted_iota(jnp.int32, (SBK, SBK), 1)
        tri = (r_i > c_i).astype(BF16)
        tri_pre = (r_i < c_i).astype(BF16)
        nkb = (qi + 1) * ratio

        def step(t, carry):
            rem, gpre, dq = carry
            k0 = pl.multiple_of(t * SBK, SBK)
            kt = k_ref[pl.ds(k0, SBK), :]
            vt = v_ref[pl.ds(k0, SBK), :]
            z, causal, lom, lb, e = _sb_tile(q, kt, q0, k0)
            rem = rem - jnp.sum(lom, axis=-1, keepdims=True)
            tail = _suffix_excl(lom, tri) + rem
            a = jnp.where(causal, jnp.exp(lb + tail), 0.0)
            da = lax.dot_general(do_b, vt, NT, preferred_element_type=F32)
            gg = da * a
            pre = _suffix_excl(gg, tri_pre) + gpre
            rr = 1.0 / (1.0 + e)
            beta = jnp.where(z >= 0.0, rr, e * rr)
            one_m = jnp.where(z >= 0.0, e * rr, rr)
            dz = jnp.where(causal, gg * one_m - pre * beta, 0.0)
            dz_b = (dz * SCALE).astype(BF16)
            dk_scr[pl.ds(k0, SBK), :] += lax.dot_general(dz_b, q, TN, preferred_element_type=F32)
            dv_scr[pl.ds(k0, SBK), :] += lax.dot_general(a.astype(BF16), do_b, TN, preferred_element_type=F32)
            dq = dq + jnp.dot(dz_b, kt, preferred_element_type=F32)
            return rem, gpre + jnp.sum(gg, axis=-1, keepdims=True), dq

        init = (lt_ref[...], jnp.zeros((SBQ, 1), F32), jnp.zeros((SBQ, HEAD_DIM), F32))
        _, _, dq = lax.fori_loop(0, nkb, step, init)
        dq_ref[...] = dq.astype(BF16)

        @pl.when(qi == nq - 1)
        def _():
            dk_ref[...] = dk_scr[...].astype(BF16)
            dv_ref[...] = dv_scr[...].astype(BF16)

    head_col = lambda off: pl.BlockSpec((s, HEAD_DIM), lambda h, i: (0, off + h))
    qblk = pl.BlockSpec((SBQ, HEAD_DIM), lambda h, i: (i, h))
    return pl.pallas_call(
        body, name=name, grid=(nh, nq),
        in_specs=[qblk, head_col(nh), head_col(2 * nh), qblk, qblk,
                  pl.BlockSpec((None, SBQ, 1), lambda h, i: (h, i, 0)), qblk],
        out_specs=[qblk, head_col(0), head_col(0), qblk],
        out_shape=[jax.ShapeDtypeStruct((s, width), BF16)] * 4,
        scratch_shapes=[pltpu.VMEM((s, HEAD_DIM), F32), pltpu.VMEM((s, HEAD_DIM), F32)],
        compiler_params=_params(("parallel", "arbitrary")),
    )(qkv, qkv, qkv, gate, o, ltot, dmixed)


HBM = pl.BlockSpec(memory_space=pl.ANY)
MESH = pl.DeviceIdType.MESH


def _all_gather(shard, *, name):
    r, c_ = shard.shape

    def body(x_ref, out_ref, send_sems, recv_sems, local_sem):
        x, y, c = lax.axis_index("x"), lax.axis_index("y"), lax.axis_index("c")
        me, sibling = (x, y, c), (x, y, 1 - c)
        chips = [(1 - x, y), (x, 1 - y), (1 - x, 1 - y)]

        def slot(px, py, pc):
            return out_ref.at[4 * px + 2 * py + pc]

        def copy(k, block, to, src=None):
            return pltpu.make_async_remote_copy(
                src_ref=slot(*block) if src is None else src, dst_ref=slot(*block),
                send_sem=send_sems.at[k], recv_sem=recv_sems.at[k], device_id=to, device_id_type=MESH)

        mine = pltpu.make_async_copy(x_ref, slot(*me), local_sem)
        mine.start()
        first = [copy(0, me, sibling, src=x_ref)]
        first += [copy(1 + j, me, (*chip, c), src=x_ref) for j, chip in enumerate(chips)]
        for cp in first:
            cp.start()
        passed = [copy(4 + j, (*chip, c), sibling) for j, chip in enumerate(chips)]
        for j, chip in enumerate(chips):
            copy(1 + j, (*chip, c), me).wait_recv()
            passed[j].start()
        copy(0, sibling, me).wait_recv()
        for j, chip in enumerate(chips):
            copy(4 + j, (*chip, 1 - c), me).wait_recv()
        for cp in first + passed:
            cp.wait_send()
        mine.wait()

    return pl.pallas_call(
        body, name=name, in_specs=[HBM], out_specs=HBM,
        out_shape=jax.ShapeDtypeStruct((N_DEV, r, c_), shard.dtype),
        scratch_shapes=[pltpu.SemaphoreType.DMA((7,)), pltpu.SemaphoreType.DMA((7,)), pltpu.SemaphoreType.DMA],
    )(shard)


def _exchange(parts, *, name, space=HBM):
    _, r, c_ = parts.shape

    def body(p_ref, out_ref, send_sems, recv_sems, local_sem):
        x, y, c = lax.axis_index("x"), lax.axis_index("y"), lax.axis_index("c")
        me = 4 * x + 2 * y + c
        mine = pltpu.make_async_copy(p_ref.at[me], out_ref.at[me], local_sem)
        mine.start()
        copies = []
        for k in range(1, N_DEV):
            px, py, pc = x ^ ((k >> 2) & 1), y ^ ((k >> 1) & 1), c ^ (k & 1)
            peer = 4 * px + 2 * py + pc
            copies.append(pltpu.make_async_remote_copy(
                src_ref=p_ref.at[peer], dst_ref=out_ref.at[me],
                send_sem=send_sems.at[k - 1], recv_sem=recv_sems.at[k - 1],
                device_id=(px, py, pc), device_id_type=MESH))
        for cp in copies:
            cp.start()
        for cp in copies:
            cp.wait_recv()
        for cp in copies:
            cp.wait_send()
        mine.wait()

    return pl.pallas_call(
        body, name=name, in_specs=[space], out_specs=space,
        out_shape=jax.ShapeDtypeStruct(parts.shape, parts.dtype),
        scratch_shapes=[pltpu.SemaphoreType.DMA((7,)), pltpu.SemaphoreType.DMA((7,)), pltpu.SemaphoreType.DMA],
    )(parts)


def _adamw(parts, w, m, v, *, name, rows):
    r, c_ = w.shape
    rows = min(rows, r)
    assert r % rows == 0
    c1 = 1.0 / (1.0 - ADAM_B1 ** ADAM_STEP)
    c2 = 1.0 / (1.0 - ADAM_B2 ** ADAM_STEP)

    def body(p_ref, w_ref, m_ref, v_ref, g_ref, d_ref, nm_ref, nv_ref):
        g = p_ref[0].astype(F32)
        for i in range(1, N_DEV):
            g = g + p_ref[i].astype(F32)
        nm = ADAM_B1 * m_ref[...] + (1.0 - ADAM_B1) * g
        nv = ADAM_B2 * v_ref[...] + (1.0 - ADAM_B2) * (g * g)
        g_ref[...] = g
        nm_ref[...] = nm
        nv_ref[...] = nv
        d_ref[...] = -ADAM_LR * ((nm * c1) / (jnp.sqrt(nv * c2) + ADAM_EPS) + ADAM_WD * w_ref[...])

    blk = pl.BlockSpec((rows, c_), lambda i: (i, 0))
    return pl.pallas_call(
        body, name=name, grid=(r // rows,),
        in_specs=[pl.BlockSpec((N_DEV, rows, c_), lambda i: (0, i, 0)), blk, blk, blk],
        out_specs=[blk] * 4, out_shape=[jax.ShapeDtypeStruct((r, c_), F32)] * 4,
        compiler_params=_params(("parallel",)),
    )(parts, w, m, v)


def _pad_cols(a, n):
    return jnp.pad(a, ((0, 0), (0, n - a.shape[1])))


def _fox_dc(shares, nh, s):
    key_side, query_side = shares
    return _pad_cols(jnp.transpose(key_side.reshape(nh, s) + query_side.reshape(nh, s)), 128)


def _local_step(x, target, norm_pre, norm_post, w_in_e, w_f, b_f, rel_bias, w_out_e, w_in_o, w_out_o):
    s, d = x.shape
    wa = d // 2
    nha = wa // HEAD_DIM
    nq = s // CQ

    h0 = _rms_fwd(x, norm_pre[0:1], name="rms_pre0")
    proj = lambda w, off, n, dt, nm: _matmul(h0, w, mode="nn", m=s, n=n, k=d, out_dtype=dt, name=nm, b_off=(0, off))
    qkv_a = proj(w_in_e, 0, 3 * wa, BF16, "proj_qkv_a")
    g_a = proj(w_in_e, 3 * wa, wa, F32, "proj_gate_a")
    qkv_b = proj(w_in_e, 4 * wa, 3 * wa, BF16, "proj_qkv_b")
    g_b = proj(w_in_e, 7 * wa, wa, F32, "proj_gate_b")
    af = _matmul(h0, w_f, mode="nn", m=s, n=128, k=d, out_dtype=F32, name="proj_forget")
    bias128 = _pad_cols(b_f, 128)
    cum = _gate_fwd(af, bias128, name="forget_cumsum")
    c_t = jnp.transpose(cum[:, :nha])
    c_col = c_t.reshape(nha, s, 1)
    c_row = c_t.reshape(nha, nq, 1, CQ)
    mixed_a, o_a, lse_a = _softmax_fwd(qkv_a, g_a, (c_col, c_row), mode="fox", width=wa, name="fox_fwd")
    rel_aux = (_pad_cols(rel_bias, REL_PAD).reshape(nha, 1, REL_PAD), _rel_onehot())
    mixed_b, o_b, lse_b = _softmax_fwd(qkv_b, g_b, rel_aux, mode="chunk", width=wa, name="chunk_fwd")
    mixed0 = jnp.concatenate([mixed_a, mixed_b], axis=1)
    y0 = _matmul(mixed0, w_out_e, mode="nn", m=s, n=d, k=d, out_dtype=F32, name="out_proj0")
    x1 = _post_fwd(x, y0, norm_post[0:1], name="post0")

    h1 = _rms_fwd(x1, norm_pre[1:2], name="rms_pre1")
    qkv_c = _matmul(h1, w_in_o, mode="nn", m=s, n=3 * d, k=d, out_dtype=BF16, name="proj_qkv_c")
    g_c = _matmul(h1, w_in_o, mode="nn", m=s, n=d, k=d, out_dtype=F32, name="proj_gate_c", b_off=(0, 3 * d))
    mixed1, o_c, ltot_c = _sb_fwd(qkv_c, g_c, width=d, name="sb_fwd")
    y1 = _matmul(mixed1, w_out_o, mode="nn", m=s, n=d, k=d, out_dtype=F32, name="out_proj1")
    x2 = _post_fwd(x1, y1, norm_post[1:2], name="post1")

    loss, dx2 = _loss_head(x2, target, name="loss_head")

    dy1, dgpost1 = _post_bwd(dx2, y1, norm_post[1:2], name="post_bwd1")
    dmixed1 = _matmul(dy1, w_out_o, mode="nt", m=s, n=d, k=d, out_dtype=F32, name="dmixed1")
    dw_out_o = _matmul(mixed1, dy1, mode="tn", m=d, n=d, k=s, out_dtype=BF16, name="dw_out1")
    dq_c, dk_c, dv_c, dg_c = _sb_bwd(qkv_c, g_c, o_c, ltot_c, dmixed1, width=d, name="sb_bwd")
    dproj1 = jnp.concatenate([dq_c, dk_c, dv_c, dg_c], axis=1)
    dh1 = _matmul(dproj1, w_in_o, mode="nt", m=s, n=d, k=4 * d, out_dtype=F32, name="dh1")
    dw_in_o = _matmul(h1, dproj1, mode="tn", m=d, n=4 * d, k=s, out_dtype=BF16, name="dw_in1")
    dx1, dgpre1 = _pre_bwd((dh1,), x1, norm_pre[1:2], dx2, name="pre_bwd1")

    dy0, dgpost0 = _post_bwd(dx1, y0, norm_post[0:1], name="post_bwd0")
    dmixed0 = _matmul(dy0, w_out_e, mode="nt", m=s, n=d, k=d, out_dtype=F32, name="dmixed0")
    dw_out_e = _matmul(mixed0, dy0, mode="tn", m=d, n=d, k=s, out_dtype=BF16, name="dw_out0")
    dq_a, dk_a, dv_a, dg_a, *dc_shares = _softmax_bwd(qkv_a, g_a, o_a, dmixed0, 0, lse_a, (c_col, c_row),
                                                      mode="fox", width=wa, name="fox_bwd")
    dq_b, dk_b, dv_b, dg_b, drel = _softmax_bwd(qkv_b, g_b, o_b, dmixed0, wa, lse_b, rel_aux,
                                                mode="chunk", width=wa, name="chunk_bwd")
    dc = _fox_dc(dc_shares, nha, s)
    daf, dbf = _gate_bwd(dc, af, bias128, name="forget_bwd")
    dproj0 = jnp.concatenate([dq_a, dk_a, dv_a, dg_a, dq_b, dk_b, dv_b, dg_b], axis=1)
    dh0 = _matmul(dproj0, w_in_e, mode="nt", m=s, n=d, k=8 * wa, out_dtype=F32, name="dh0_main")
    dh0f = _matmul(daf, w_f, mode="nt", m=s, n=d, k=128, out_dtype=F32, name="dh0_forget")
    dw_in_e = _matmul(h0, dproj0, mode="tn", m=d, n=8 * wa, k=s, out_dtype=BF16, name="dw_in0")
    dw_f = _matmul(h0, daf, mode="tn", m=d, n=128, k=s, out_dtype=BF16, name="dw_forget")
    dx0, dgpre0 = _pre_bwd((dh0, dh0f), x, norm_pre[0:1], dx1, name="pre_bwd0")

    small = dict(
        norm_pre=jnp.concatenate([dgpre0, dgpre1], axis=0),
        norm_post=jnp.concatenate([dgpost0, dgpost1], axis=0),
        b_f=dbf[:, :nha], rel_bias=drel[:, 0, :N_REL])
    big = dict(w_in_e=dw_in_e, w_f=dw_f[:, :nha], w_out_e=dw_out_e, w_in_o=dw_in_o, w_out_o=dw_out_o)
    return loss, dx0, big, small


def _pack_small(norm_pre, norm_post, b_f, rel_bias):
    flat = jnp.concatenate([norm_pre.reshape(-1), norm_post.reshape(-1), b_f.reshape(-1), rel_bias.reshape(-1)])
    n = flat.shape[0]
    rows = -(-n // 128)
    rows = -(-rows // 8) * 8
    return jnp.pad(flat, (0, rows * 128 - n)).reshape(rows, 128)


def _unpack_small(slab, shapes):
    flat = slab.reshape(-1)
    out, off = [], 0
    for shp in shapes:
        n = int(np.prod(shp))
        out.append(flat[off:off + n].reshape(shp))
        off += n
    return out


def kernel(x, norm_pre, norm_post, w_in_even, b_f_even, rel_bias_even, w_out_even, w_in_odd, w_out_odd, loss_target, m_norm_pre, m_norm_post, m_w_in_even, m_b_f_even, m_rel_bias_even, m_w_out_even, m_w_in_odd, m_w_out_odd, v_norm_pre, v_norm_post, v_w_in_even, v_b_f_even, v_rel_bias_even, v_w_out_even, v_w_in_odd, v_w_out_odd):
    _, s, d = x.shape
    wa = d // 2
    nha = wa // HEAD_DIM
    in_e = w_in_even.shape[2] * N_DEV
    sh_e = w_in_even.shape[2]

    ag = lambda w, nm: _all_gather(w.astype(BF16), name=nm)
    w_in_e_all = jnp.transpose(ag(w_in_even[0], "ag_w_in_even"), (1, 0, 2)).reshape(d, in_e)
    w_out_e_all = ag(w_out_even[0], "ag_w_out_even").reshape(d, d)
    w_in_o_all = jnp.transpose(ag(w_in_odd[0], "ag_w_in_odd"), (1, 0, 2)).reshape(d, 4 * d)
    w_out_o_all = ag(w_out_odd[0], "ag_w_out_odd").reshape(d, d)
    w_main = w_in_e_all[:, :8 * wa]
    w_f = _pad_cols(w_in_e_all[:, 8 * wa:], 128)

    loss, dx, big, small = _local_step(x[0], loss_target[0], norm_pre, norm_post, w_main, w_f, b_f_even,
                                       rel_bias_even[0], w_out_e_all, w_in_o_all, w_out_o_all)

    dw_e = jnp.concatenate([big["w_in_e"], big["w_f"]], axis=1)
    parts_in_e = jnp.transpose(dw_e.reshape(d, N_DEV, sh_e), (1, 0, 2))
    parts_in_o = jnp.transpose(big["w_in_o"].reshape(d, N_DEV, 4 * d // N_DEV), (1, 0, 2))
    parts_out_e = big["w_out_e"].reshape(N_DEV, d // N_DEV, d)
    parts_out_o = big["w_out_o"].reshape(N_DEV, d // N_DEV, d)
    upd = {}
    upd["w_in_even"] = _adamw(_exchange(parts_in_e, name="rs_w_in_even"), w_in_even[0], m_w_in_even[0], v_w_in_even[0],
                              name="adamw_w_in_even", rows=128)
    upd["w_out_even"] = _adamw(_exchange(parts_out_e, name="rs_w_out_even"), w_out_even[0], m_w_out_even[0], v_w_out_even[0],
                               name="adamw_w_out_even", rows=64)
    upd["w_in_odd"] = _adamw(_exchange(parts_in_o, name="rs_w_in_odd"), w_in_odd[0], m_w_in_odd[0], v_w_in_odd[0],
                             name="adamw_w_in_odd", rows=128)
    upd["w_out_odd"] = _adamw(_exchange(parts_out_o, name="rs_w_out_odd"), w_out_odd[0], m_w_out_odd[0], v_w_out_odd[0],
                              name="adamw_w_out_odd", rows=64)

    shapes = [norm_pre.shape, norm_post.shape, b_f_even.shape, rel_bias_even.shape]
    g_slab = _pack_small(small["norm_pre"], small["norm_post"], small["b_f"], small["rel_bias"])
    parts_small = _exchange(jnp.broadcast_to(g_slab[None], (N_DEV,) + g_slab.shape), name="ar_small")
    sm = _adamw(parts_small, _pack_small(norm_pre, norm_post, b_f_even, rel_bias_even),
                _pack_small(m_norm_pre, m_norm_post, m_b_f_even, m_rel_bias_even),
                _pack_small(v_norm_pre, v_norm_post, v_b_f_even, v_rel_bias_even), name="adamw_small", rows=g_slab.shape[0])
    sm = [_unpack_small(a, shapes) for a in sm]

    total = lax.psum(loss[0, 0], ("x", "y", "c"))

    def leaves(kind):
        return (sm[kind][0], sm[kind][1], upd["w_in_even"][kind][None], sm[kind][2], sm[kind][3],
                upd["w_out_even"][kind][None], upd["w_in_odd"][kind][None], upd["w_out_odd"][kind][None])

    return (total, dx[None], *leaves(0), *leaves(1), *leaves(2), *leaves(3))
```

```python
import functools

import numpy as np
import jax
import jax.numpy as jnp
from jax import lax
from jax.experimental import pallas as pl
from jax.experimental.pallas import tpu as pltpu

F32 = jnp.float32
BF16 = jnp.bfloat16

HEAD_DIM = 128
CHUNK = 64
LEFT_CHUNKS = 8
REL_CLIP = 128
N_REL = 2 * REL_CLIP + 1
RMS_EPS = 1e-6
SCALE = HEAD_DIM ** -0.5

ADAM_LR = 0.001
ADAM_B1 = 0.9
ADAM_B2 = 0.999
ADAM_EPS = 1e-08
ADAM_WD = 0.01
ADAM_STEP = 10

N_DEV = 8
V7X_VMEM_LIMIT_BYTES = 56 * 1024 * 1024
NEG = -1e30

NT = (((1,), (1,)), ((), ()))
TN = (((0,), (0,)), ((), ()))
NN = (((1,), (0,)), ((), ()))

CQ = 256
BAND_TILES = 3
TOEP = 2 * CQ
assert (BAND_TILES - 1) * CQ == LEFT_CHUNKS * CHUNK


def _params(sem):
    return pltpu.CompilerParams(dimension_semantics=sem, vmem_limit_bytes=V7X_VMEM_LIMIT_BYTES)


def _split3(x):
    hi = x.astype(BF16)
    r1 = x - hi.astype(F32)
    mid = r1.astype(BF16)
    lo = (r1 - mid.astype(F32)).astype(BF16)
    return hi, mid, lo


def _split2(x):
    hi = x.astype(BF16)
    lo = (x - hi.astype(F32)).astype(BF16)
    return hi, lo


def _sigmoid(g):
    return 1.0 / (1.0 + jnp.exp(-g))


def _tile(n, cap, *offsets):
    if n <= 128:
        return n
    t = (min(cap, n) // 128) * 128
    while n % t or any(o % t for o in offsets):
        t -= 128
    return t


def _matmul(a, b, *, mode, m, n, k, out_dtype, name, a_off=(0, 0), b_off=(0, 0), tm=1024, tn=1024, tk=1024):
    a_m, a_k = (a_off if mode in ("nn", "nt") else a_off[::-1])
    b_k, b_n = (b_off if mode in ("nn", "tn") else b_off[::-1])
    tm, tn, tk = _tile(m, tm, a_m), _tile(n, tn, b_n), _tile(k, tk, a_k, b_k)
    nk = k // tk
    if mode in ("nn", "nt"):
        ao = (a_off[0] // tm, a_off[1] // tk)
        a_spec = pl.BlockSpec((tm, tk), lambda i, j, l: (i + ao[0], l + ao[1]))
    else:
        ao = (a_off[0] // tk, a_off[1] // tm)
        a_spec = pl.BlockSpec((tk, tm), lambda i, j, l: (l + ao[0], i + ao[1]))
    if mode in ("nn", "tn"):
        bo = (b_off[0] // tk, b_off[1] // tn)
        b_spec = pl.BlockSpec((tk, tn), lambda i, j, l: (l + bo[0], j + bo[1]))
    else:
        bo = (b_off[0] // tn, b_off[1] // tk)
        b_spec = pl.BlockSpec((tn, tk), lambda i, j, l: (j + bo[0], l + bo[1]))
    dn = {"nn": NN, "nt": NT, "tn": TN}[mode]

    def body(a_ref, b_ref, o_ref, acc_ref):
        @pl.when(pl.program_id(2) == 0)
        def _():
            acc_ref[...] = jnp.zeros_like(acc_ref)

        acc_ref[...] += lax.dot_general(a_ref[...], b_ref[...], dn, preferred_element_type=F32)

        @pl.when(pl.program_id(2) == nk - 1)
        def _():
            o_ref[...] = acc_ref[...].astype(out_dtype)

    return pl.pallas_call(
        body,
        name=name,
        grid=(m // tm, n // tn, nk),
        in_specs=[a_spec, b_spec],
        out_specs=pl.BlockSpec((tm, tn), lambda i, j, l: (i, j)),
        out_shape=jax.ShapeDtypeStruct((m, n), out_dtype),
        scratch_shapes=[pltpu.VMEM((tm, tn), F32)],
        compiler_params=_params(("parallel", "parallel", "arbitrary")),
    )(a, b)


ROWS = 128


def _rms_fwd(x, gain, *, name):
    s, d = x.shape

    def body(x_ref, g_ref, h_ref):
        xf = x_ref[...]
        r = lax.rsqrt(jnp.mean(xf * xf, axis=-1, keepdims=True) + RMS_EPS)
        h_ref[...] = ((xf * r) * g_ref[...]).astype(BF16)

    return pl.pallas_call(
        body, name=name, grid=(s // ROWS,),
        in_specs=[pl.BlockSpec((ROWS, d), lambda i: (i, 0)), pl.BlockSpec((1, d), lambda i: (0, 0))],
        out_specs=pl.BlockSpec((ROWS, d), lambda i: (i, 0)),
        out_shape=jax.ShapeDtypeStruct((s, d), BF16),
        compiler_params=_params(("parallel",)),
    )(x, gain)


def _post_fwd(x, y, gain, *, name):
    s, d = x.shape

    def body(x_ref, y_ref, g_ref, o_ref):
        yf = y_ref[...]
        r = lax.rsqrt(jnp.mean(yf * yf, axis=-1, keepdims=True) + RMS_EPS)
        o_ref[...] = x_ref[...] + (yf * r) * g_ref[...]

    return pl.pallas_call(
        body, name=name, grid=(s // ROWS,),
        in_specs=[pl.BlockSpec((ROWS, d), lambda i: (i, 0)), pl.BlockSpec((ROWS, d), lambda i: (i, 0)),
                  pl.BlockSpec((1, d), lambda i: (0, 0))],
        out_specs=pl.BlockSpec((ROWS, d), lambda i: (i, 0)),
        out_shape=jax.ShapeDtypeStruct((s, d), F32),
        compiler_params=_params(("parallel",)),
    )(x, y, gain)


def _loss_head(xo, target, *, name):
    s, d = xo.shape
    inv_d = 1.0 / d

    def body(x_ref, t_ref, loss_ref, dx_ref):
        @pl.when(pl.program_id(0) == 0)
        def _():
            loss_ref[...] = jnp.zeros_like(loss_ref)

        e = x_ref[...] - t_ref[...]
        dx_ref[...] = e * inv_d
        loss_ref[...] += 0.5 * jnp.sum(jnp.mean(e * e, axis=-1, keepdims=True), axis=0, keepdims=True)

    return pl.pallas_call(
        body, name=name, grid=(s // ROWS,),
        in_specs=[pl.BlockSpec((ROWS, d), lambda i: (i, 0)), pl.BlockSpec((ROWS, d), lambda i: (i, 0))],
        out_specs=[pl.BlockSpec((1, 1), lambda i: (0, 0)), pl.BlockSpec((ROWS, d), lambda i: (i, 0))],
        out_shape=[jax.ShapeDtypeStruct((1, 1), F32), jax.ShapeDtypeStruct((s, d), F32)],
        compiler_params=_params(("arbitrary",)),
    )(xo, target)


def _post_bwd(dxo, y, gain, *, name):
    s, d = y.shape

    def body(dx_ref, y_ref, g_ref, dy_ref, dg_ref):
        @pl.when(pl.program_id(0) == 0)
        def _():
            dg_ref[...] = jnp.zeros_like(dg_ref)

        yf = y_ref[...]
        dxo_ = dx_ref[...]
        r = lax.rsqrt(jnp.mean(yf * yf, axis=-1, keepdims=True) + RMS_EPS)
        nrm = yf * r
        dg_ref[...] += jnp.sum(dxo_ * nrm, axis=0, keepdims=True)
        dn = dxo_ * g_ref[...]
        dy_ref[...] = (r * (dn - nrm * jnp.mean(dn * nrm, axis=-1, keepdims=True))).astype(BF16)

    return pl.pallas_call(
        body, name=name, grid=(s // ROWS,),
        in_specs=[pl.BlockSpec((ROWS, d), lambda i: (i, 0)), pl.BlockSpec((ROWS, d), lambda i: (i, 0)),
                  pl.BlockSpec((1, d), lambda i: (0, 0))],
        out_specs=[pl.BlockSpec((ROWS, d), lambda i: (i, 0)), pl.BlockSpec((1, d), lambda i: (0, 0))],
        out_shape=[jax.ShapeDtypeStruct((s, d), BF16), jax.ShapeDtypeStruct((1, d), F32)],
        compiler_params=_params(("arbitrary",)),
    )(dxo, y, gain)


def _pre_bwd(dhs, x, gain, dres, *, name):
    s, d = x.shape
    n_dh = len(dhs)

    def body(*refs):
        dh_refs = refs[:n_dh]
        x_ref, g_ref, dr_ref, dx_ref, dg_ref = refs[n_dh:]

        @pl.when(pl.program_id(0) == 0)
        def _():
            dg_ref[...] = jnp.zeros_like(dg_ref)

        xf = x_ref[...]
        dh_ = dh_refs[0][...]
        for extra in dh_refs[1:]:
            dh_ = dh_ + extra[...]
        r = lax.rsqrt(jnp.mean(xf * xf, axis=-1, keepdims=True) + RMS_EPS)
        nrm = xf * r
        dg_ref[...] += jnp.sum(dh_ * nrm, axis=0, keepdims=True)
        dn = dh_ * g_ref[...]
        dx_ref[...] = dr_ref[...] + r * (dn - nrm * jnp.mean(dn * nrm, axis=-1, keepdims=True))

    return pl.pallas_call(
        body, name=name, grid=(s // ROWS,),
        in_specs=[pl.BlockSpec((ROWS, d), lambda i: (i, 0))] * (n_dh + 1)
        + [pl.BlockSpec((1, d), lambda i: (0, 0)), pl.BlockSpec((ROWS, d), lambda i: (i, 0))],
        out_specs=[pl.BlockSpec((ROWS, d), lambda i: (i, 0)), pl.BlockSpec((1, d), lambda i: (0, 0))],
        out_shape=[jax.ShapeDtypeStruct((s, d), F32), jax.ShapeDtypeStruct((1, d), F32)],
        compiler_params=_params(("arbitrary",)),
    )(*dhs, x, gain, dres)


GB = 256


def _gate_fwd(af, bias, *, name):
    s, w = af.shape

    def body(af_ref, b_ref, c_ref, carry_ref):
        @pl.when(pl.program_id(0) == 0)
        def _():
            carry_ref[...] = jnp.zeros_like(carry_ref)

        z = af_ref[...] + b_ref[...]
        lf = jnp.minimum(z, 0.0) - jnp.log(1.0 + jnp.exp(-jnp.abs(z)))
        r_i = lax.broadcasted_iota(jnp.int32, (GB, GB), 0)
        c_i = lax.broadcasted_iota(jnp.int32, (GB, GB), 1)
        tri = (c_i <= r_i).astype(BF16)
        hi, mid, lo = _split3(lf)
        pre = (jnp.dot(tri, hi, preferred_element_type=F32) + jnp.dot(tri, mid, preferred_element_type=F32)
               + jnp.dot(tri, lo, preferred_element_type=F32))
        c_ref[...] = pre + carry_ref[...]
        carry_ref[...] += jnp.sum(lf, axis=0, keepdims=True)

    return pl.pallas_call(
        body, name=name, grid=(s // GB,),
        in_specs=[pl.BlockSpec((GB, w), lambda i: (i, 0)), pl.BlockSpec((1, w), lambda i: (0, 0))],
        out_specs=pl.BlockSpec((GB, w), lambda i: (i, 0)),
        out_shape=jax.ShapeDtypeStruct((s, w), F32),
        scratch_shapes=[pltpu.VMEM((1, w), F32)],
        compiler_params=_params(("arbitrary",)),
    )(af, bias)


def _gate_bwd(dc, af, bias, *, name):
    s, w = af.shape
    nb = s // GB

    def body(dc_ref, af_ref, b_ref, daf_ref, db_ref, carry_ref):
        @pl.when(pl.program_id(0) == 0)
        def _():
            carry_ref[...] = jnp.zeros_like(carry_ref)
            db_ref[...] = jnp.zeros_like(db_ref)

        dcb = dc_ref[...]
        r_i = lax.broadcasted_iota(jnp.int32, (GB, GB), 0)
        c_i = lax.broadcasted_iota(jnp.int32, (GB, GB), 1)
        tri = (c_i >= r_i).astype(BF16)
        hi, mid, lo = _split3(dcb)
        suf = (jnp.dot(tri, hi, preferred_element_type=F32) + jnp.dot(tri, mid, preferred_element_type=F32)
               + jnp.dot(tri, lo, preferred_element_type=F32)) + carry_ref[...]
        carry_ref[...] += jnp.sum(dcb, axis=0, keepdims=True)
        z = af_ref[...] + b_ref[...]
        daf = suf * _sigmoid(-z)
        daf_ref[...] = daf.astype(BF16)
        db_ref[...] += jnp.sum(daf, axis=0, keepdims=True)

    return pl.pallas_call(
        body, name=name, grid=(nb,),
        in_specs=[pl.BlockSpec((GB, w), lambda i: (nb - 1 - i, 0)), pl.BlockSpec((GB, w), lambda i: (nb - 1 - i, 0)),
                  pl.BlockSpec((1, w), lambda i: (0, 0))],
        out_specs=[pl.BlockSpec((GB, w), lambda i: (nb - 1 - i, 0)), pl.BlockSpec((1, w), lambda i: (0, 0))],
        out_shape=[jax.ShapeDtypeStruct((s, w), BF16), jax.ShapeDtypeStruct((1, w), F32)],
        scratch_shapes=[pltpu.VMEM((1, w), F32)],
        compiler_params=_params(("arbitrary",)),
    )(dc, af, bias)


def _rel_index_rows():
    w = np.arange(TOEP)
    wp = np.where(w < CQ, w, w - TOEP)
    return np.stack([np.clip(LEFT_CHUNKS * CHUNK - CQ * j - wp, -REL_CLIP, REL_CLIP) + REL_CLIP
                     for j in range(BAND_TILES)]).astype(np.int32)


def _skew_rows(xw, sign):
    row = lax.broadcasted_iota(jnp.int32, xw.shape, 0)
    for b in range(CQ.bit_length() - 1):
        amt = (1 << b) if sign > 0 else TOEP - (1 << b)
        xw = jnp.where(((row >> b) & 1) == 1, pltpu.roll(xw, amt, 1), xw)
    return xw


REL_PAD = 384


def _rel_onehot():
    return jnp.asarray(_rel_index_rows()[:, :, None] == np.arange(REL_PAD)[None, None, :], BF16)


def _fill_bias_tiles(rel_ref, oh_ref, bias_scr):
    parts = _split3(jnp.broadcast_to(rel_ref[...], (8, REL_PAD)))
    for j in range(BAND_TILES):
        row = sum(lax.dot_general(p, oh_ref[j], NT, preferred_element_type=F32) for p in parts)[0:1]
        bias_scr[j] = _skew_rows(jnp.broadcast_to(row, (CQ, TOEP)), +1)[:, :CQ]


def _fox_scores(s, cq, cr, diagonal):
    s = s + (cq - cr)
    if not diagonal:
        return s
    bq, bk = s.shape
    return jnp.where(lax.broadcasted_iota(jnp.int32, (bq, bk), 1) <= lax.broadcasted_iota(jnp.int32, (bq, bk), 0), s, NEG)


def _chunk_scores(s, q0, k0, bias):
    bq, bk = s.shape
    qc = (q0 + lax.broadcasted_iota(jnp.int32, (bq, bk), 0)) >> 6
    kc = (k0 + lax.broadcasted_iota(jnp.int32, (bq, bk), 1)) >> 6
    return jnp.where((kc <= qc) & (kc >= qc - LEFT_CHUNKS), s + bias, NEG)


def _softmax_fwd(qkv, gate, aux, *, mode, width, name):
    s = qkv.shape[0]
    nh = width // HEAD_DIM
    bq = bk = CQ
    nq = s // bq

    def body(q_ref, k_ref, v_ref, g_ref, *rest):
        if mode == "fox":
            cc_ref, cr_ref, mixed_ref, o_ref, lse_ref = rest
        else:
            rel_ref, oh_ref, mixed_ref, o_ref, lse_ref, bias_scr = rest
        qi = pl.program_id(1)
        q0 = qi * bq
        q = q_ref[...]

        if mode == "chunk":
            @pl.when(qi == 0)
            def _():
                _fill_bias_tiles(rel_ref, oh_ref, bias_scr)

        if mode == "fox":
            cq = cc_ref[...]
        else:
            lo, hi = jnp.maximum(qi - (BAND_TILES - 1), 0), qi + 1

        def step(ki, carry, diagonal=False):
            m, l, acc = carry
            k0 = pl.multiple_of(ki * bk, bk)
            kt = k_ref[pl.ds(k0, bk), :]
            vt = v_ref[pl.ds(k0, bk), :]
            sc = lax.dot_general(q, kt, NT, preferred_element_type=F32) * SCALE
            if mode == "fox":
                sc = _fox_scores(sc, cq, cr_ref[ki], diagonal)
            else:
                sc = _chunk_scores(sc, q0, k0, bias_scr[ki - qi + (BAND_TILES - 1)])
            m_new = jnp.maximum(m, jnp.max(sc, axis=-1, keepdims=True))
            p = jnp.exp(sc - m_new)
            alpha = jnp.exp(m - m_new)
            l = alpha * l + jnp.sum(p, axis=-1, keepdims=True)
            acc = alpha * acc + jnp.dot(p.astype(BF16), vt, preferred_element_type=F32)
            return m_new, l, acc

        init = (jnp.full((bq, 1), NEG, F32), jnp.zeros((bq, 1), F32), jnp.zeros((bq, HEAD_DIM), F32))
        if mode == "fox":
            m, l, acc = step(qi, lax.fori_loop(0, qi, step, init), diagonal=True)
        else:
            m, l, acc = lax.fori_loop(lo, hi, step, init)
        o = acc / l
        g = g_ref[...]
        o_ref[...] = o
        mixed_ref[...] = (o * (g * _sigmoid(g))).astype(BF16)
        lse_ref[...] = m + jnp.log(l)

    head_col = lambda off: pl.BlockSpec((s, HEAD_DIM), lambda h, i: (0, off + h))
    in_specs = [pl.BlockSpec((bq, HEAD_DIM), lambda h, i: (i, h)), head_col(nh), head_col(2 * nh),
                pl.BlockSpec((bq, HEAD_DIM), lambda h, i: (i, h))]
    scratch = []
    if mode == "fox":
        in_specs += [pl.BlockSpec((None, bq, 1), lambda h, i: (h, i, 0)),
                     pl.BlockSpec((None, nq, 1, bk), lambda h, i: (h, 0, 0, 0))]
    else:
        in_specs += [pl.BlockSpec((None, 1, REL_PAD), lambda h, i: (h, 0, 0)),
                     pl.BlockSpec((BAND_TILES, TOEP, REL_PAD), lambda h, i: (0, 0, 0))]
        scratch = [pltpu.VMEM((BAND_TILES, CQ, CQ), F32)]
    return pl.pallas_call(
        body, name=name, grid=(nh, nq), in_specs=in_specs,
        out_specs=[pl.BlockSpec((bq, HEAD_DIM), lambda h, i: (i, h)), pl.BlockSpec((bq, HEAD_DIM), lambda h, i: (i, h)),
                   pl.BlockSpec((None, bq, 1), lambda h, i: (h, i, 0))],
        out_shape=[jax.ShapeDtypeStruct((s, width), BF16), jax.ShapeDtypeStruct((s, width), F32),
                   jax.ShapeDtypeStruct((nh, s, 1), F32)],
        scratch_shapes=scratch,
        compiler_params=_params(("parallel", "arbitrary")),
    )(qkv, qkv, qkv, gate, *aux)


def _softmax_bwd(qkv, gate, o, dmixed, dm_off, lse, aux, *, mode, width, name):
    s = qkv.shape[0]
    nh = width // HEAD_DIM
    bq = bk = CQ
    nq = s // bq
    dmo = dm_off // HEAD_DIM
    rel_pad = REL_PAD

    def body(q_ref, k_ref, v_ref, g_ref, o_ref, dm_ref, lse_ref, *rest):
        if mode == "fox":
            cc_ref, cr_ref, dq_ref, dk_ref, dv_ref, dg_ref, dc_ref, dcq_ref, dk_scr, dv_scr, dc_scr = rest
        else:
            rel_ref, oh_ref, dq_ref, dk_ref, dv_ref, dg_ref, drel_ref, dk_scr, dv_scr, bias_scr, db_scr = rest
        qi = pl.program_id(1)
        q0 = qi * bq

        @pl.when(qi == 0)
        def _():
            dk_scr[...] = jnp.zeros_like(dk_scr)
            dv_scr[...] = jnp.zeros_like(dv_scr)
            if mode == "fox":
                dc_scr[...] = jnp.zeros_like(dc_scr)
            else:
                db_scr[...] = jnp.zeros_like(db_scr)
                _fill_bias_tiles(rel_ref, oh_ref, bias_scr)

        g = g_ref[...]
        of = o_ref[...]
        dm = dm_ref[...]
        sig = _sigmoid(g)
        do = dm * (g * sig)
        dg_ref[...] = (dm * of * (sig * (1.0 + g * (1.0 - sig)))).astype(BF16)
        delta = jnp.sum(do * of, axis=-1, keepdims=True)
        do_b = do.astype(BF16)
        q = q_ref[...]
        lse_q = lse_ref[...]
        if mode == "fox":
            cq = cc_ref[...]
        else:
            lo, hi = jnp.maximum(qi - (BAND_TILES - 1), 0), qi + 1

        def step(ki, carry, diagonal=False):
            dq, rsum = carry
            k0 = pl.multiple_of(ki * bk, bk)
            kt = k_ref[pl.ds(k0, bk), :]
            vt = v_ref[pl.ds(k0, bk), :]
            sc = lax.dot_general(q, kt, NT, preferred_element_type=F32) * SCALE
            if mode == "fox":
                sc = _fox_scores(sc, cq, cr_ref[ki], diagonal)
            else:
                sc = _chunk_scores(sc, q0, k0, bias_scr[ki - qi + (BAND_TILES - 1)])
            p = jnp.exp(sc - lse_q)
            dp = lax.dot_general(do_b, vt, NT, preferred_element_type=F32)
            ds = p * (dp - delta)
            if mode == "fox":
                dc_scr[ki] += -jnp.sum(ds, axis=0, keepdims=True)
                rsum = rsum + jnp.sum(ds, axis=-1, keepdims=True)
            else:
                db_scr[ki - qi + (BAND_TILES - 1)] += ds
            ds_b = (ds * SCALE).astype(BF16)
            dk_scr[pl.ds(k0, bk), :] += lax.dot_general(ds_b, q, TN, preferred_element_type=F32)
            dv_scr[pl.ds(k0, bk), :] += lax.dot_general(p.astype(BF16), do_b, TN, preferred_element_type=F32)
            return dq + jnp.dot(ds_b, kt, preferred_element_type=F32), rsum

        init = (jnp.zeros((bq, HEAD_DIM), F32), jnp.zeros((bq, 1), F32))
        if mode == "fox":
            dq, rsum = step(qi, lax.fori_loop(0, qi, step, init), diagonal=True)
        else:
            dq, rsum = lax.fori_loop(lo, hi, step, init)
        dq_ref[...] = dq.astype(BF16)
        if mode == "fox":
            dcq_ref[...] = rsum

        @pl.when(qi == nq - 1)
        def _():
            dk_ref[...] = dk_scr[...].astype(BF16)
            dv_ref[...] = dv_scr[...].astype(BF16)
            if mode == "fox":
                dc_ref[...] = dc_scr[...]
            else:
                tot = jnp.zeros((8, rel_pad), F32)
                for j in range(BAND_TILES):
                    wide = jnp.concatenate([db_scr[j], jnp.zeros((CQ, TOEP - CQ), F32)], axis=1)
                    diag = jnp.sum(_skew_rows(wide, -1), axis=0, keepdims=True)
                    for part in _split3(jnp.broadcast_to(diag, (8, TOEP))):
                        tot = tot + jnp.dot(part, oh_ref[j], preferred_element_type=F32)
                drel_ref[...] = tot[0:1, :]

    head_col = lambda off: pl.BlockSpec((s, HEAD_DIM), lambda h, i: (0, off + h))
    qblk = lambda off: pl.BlockSpec((bq, HEAD_DIM), lambda h, i: (i, off + h))
    in_specs = [qblk(0), head_col(nh), head_col(2 * nh), qblk(0), qblk(0), qblk(dmo),
                pl.BlockSpec((None, bq, 1), lambda h, i: (h, i, 0))]
    out_specs = [qblk(0), head_col(0), head_col(0), qblk(0)]
    out_shape = [jax.ShapeDtypeStruct((s, width), BF16)] * 4
    scratch = [pltpu.VMEM((s, HEAD_DIM), F32), pltpu.VMEM((s, HEAD_DIM), F32)]
    if mode == "fox":
        in_specs += [pl.BlockSpec((None, bq, 1), lambda h, i: (h, i, 0)),
                     pl.BlockSpec((None, nq, 1, bk), lambda h, i: (h, 0, 0, 0))]
        out_specs += [pl.BlockSpec((None, nq, 1, bk), lambda h, i: (h, 0, 0, 0)),
                      pl.BlockSpec((None, bq, 1), lambda h, i: (h, i, 0))]
        out_shape += [jax.ShapeDtypeStruct((nh, nq, 1, bk), F32), jax.ShapeDtypeStruct((nh, s, 1), F32)]
        scratch += [pltpu.VMEM((nq, 1, bk), F32)]
    else:
        in_specs += [pl.BlockSpec((None, 1, rel_pad), lambda h, i: (h, 0, 0)),
                     pl.BlockSpec((BAND_TILES, TOEP, rel_pad), lambda h, i: (0, 0, 0))]
        out_specs += [pl.BlockSpec((None, 1, rel_pad), lambda h, i: (h, 0, 0))]
        out_shape += [jax.ShapeDtypeStruct((nh, 1, rel_pad), F32)]
        scratch += [pltpu.VMEM((BAND_TILES, CQ, CQ), F32), pltpu.VMEM((BAND_TILES, CQ, CQ), F32)]
    return pl.pallas_call(
        body, name=name, grid=(nh, nq), in_specs=in_specs, out_specs=out_specs, out_shape=out_shape,
        scratch_shapes=scratch, compiler_params=_params(("parallel", "arbitrary")),
    )(qkv, qkv, qkv, gate, o, dmixed, lse, *aux)


SBK = 256
SBQ = 2 * SBK
SB_TILES_PER_TRIP = 2


def _suffix_excl(x, tri):
    r = x.shape[0]
    both = jnp.dot(jnp.concatenate(_split2(x), axis=0), tri, preferred_element_type=F32)
    return both[:r] + both[r:]


def _sb_logits(qh, kt, diag):
    z = lax.dot_general(qh, kt, NT, preferred_element_type=F32) * SCALE
    lom = jnp.minimum(-z, 0.0) - jnp.log(1.0 + jnp.exp(-jnp.abs(z)))
    if diag is not None:
        lom = jnp.where(diag, lom, 0.0)
    return z, lom


def _sb_fwd_tile(qh, kt, vt, tri, diag, run, acc):
    z, lom = _sb_logits(qh, kt, diag)
    a = jnp.exp(lom + z + (_suffix_excl(lom, tri) + run))
    if diag is not None:
        a = jnp.where(diag, a, 0.0)
    acc = acc + jnp.dot(a.astype(BF16), vt, preferred_element_type=F32)
    return run + jnp.sum(lom, axis=-1, keepdims=True), acc


def _sb_fwd(qkv, gate, *, width, name):
    s = qkv.shape[0]
    nh = width // HEAD_DIM
    nq = s // SBQ
    ratio = SBQ // SBK

    def body(q_ref, k_ref, v_ref, g_ref, mixed_ref, o_ref, ltot_ref):
        qi = pl.program_id(1)
        r_i = lax.broadcasted_iota(jnp.int32, (SBK, SBK), 0)
        c_i = lax.broadcasted_iota(jnp.int32, (SBK, SBK), 1)
        tri = (r_i > c_i).astype(BF16)
        diag = c_i < r_i
        q_a = q_ref[0:SBK, :]
        q_b = q_ref[SBK:SBQ, :]

        def kv(tile):
            k0 = pl.multiple_of(tile * SBK, SBK)
            return k_ref[pl.ds(k0, SBK), :], v_ref[pl.ds(k0, SBK), :]

        zero = (jnp.zeros((SBK, 1), F32), jnp.zeros((SBK, HEAD_DIM), F32))
        kt, vt = kv(2 * qi + 1)
        run_b, acc_b = _sb_fwd_tile(q_b, kt, vt, tri, diag, *zero)
        kt, vt = kv(2 * qi)
        run_b, acc_b = _sb_fwd_tile(q_b, kt, vt, tri, None, run_b, acc_b)
        run_a, acc_a = _sb_fwd_tile(q_a, kt, vt, tri, diag, *zero)

        def step(t, carry):
            run_a, acc_a, run_b, acc_b = carry
            for u in range(SB_TILES_PER_TRIP):
                kt, vt = kv(2 * qi - 1 - (SB_TILES_PER_TRIP * t + u))
                run_a, acc_a = _sb_fwd_tile(q_a, kt, vt, tri, None, run_a, acc_a)
                run_b, acc_b = _sb_fwd_tile(q_b, kt, vt, tri, None, run_b, acc_b)
            return run_a, acc_a, run_b, acc_b

        run_a, acc_a, run_b, acc_b = lax.fori_loop(0, (2 // SB_TILES_PER_TRIP) * qi, step, (run_a, acc_a, run_b, acc_b))
        ltot = jnp.concatenate([run_a, run_b], axis=0)
        o = jnp.concatenate([acc_a, acc_b], axis=0)
        g = g_ref[...]
        o_ref[...] = o
        mixed_ref[...] = (o * (g * _sigmoid(g))).astype(BF16)
        ltot_ref[...] = ltot

    head_col = lambda off: pl.BlockSpec((s, HEAD_DIM), lambda h, i: (0, off + h))
    qblk = pl.BlockSpec((SBQ, HEAD_DIM), lambda h, i: (i, h))
    stat = pl.BlockSpec((None, SBQ, 1), lambda h, i: (h, i, 0))
    return pl.pallas_call(
        body, name=name, grid=(nh, nq), in_specs=[qblk, head_col(nh), head_col(2 * nh), qblk],
        out_specs=[qblk, qblk, stat],
        out_shape=[jax.ShapeDtypeStruct((s, width), BF16), jax.ShapeDtypeStruct((s, width), F32),
                   jax.ShapeDtypeStruct((nh, s, 1), F32)],
        compiler_params=_params(("parallel", "arbitrary")),
    )(qkv, qkv, qkv, gate)


def _sb_bwd(qkv, gate, o, ltot, dmixed, *, width, name):
    s = qkv.shape[0]
    nh = width // HEAD_DIM
    nq = s // SBQ
    ratio = SBQ // SBK

    def body(q_ref, k_ref, v_ref, g_ref, o_ref, lt_ref, dm_ref, dq_ref, dk_ref, dv_ref, dg_ref, dk_scr, dv_scr):
        qi = pl.program_id(1)
        q0 = qi * SBQ

        @pl.when(qi == 0)
        def _():
            dk_scr[...] = jnp.zeros_like(dk_scr)
            dv_scr[...] = jnp.zeros_like(dv_scr)

        g = g_ref[...]
        of = o_ref[...]
        dm = dm_ref[...]
        sig = _sigmoid(g)
        do = dm * (g * sig)
        dg_ref[...] = (dm * of * (sig * (1.0 + g * (1.0 - sig)))).astype(BF16)
        do_b = do.astype(BF16)
        q = q_ref[...]
        r_i = lax.broadcasted_iota(jnp.int32, (SBK, SBK), 0)
        c_i = lax.broadcasted_iota(jnp.int32, (SBK, SBK), 1)
        tri = (r_i > c_i).astype(BF16)
        tri_pre = (r_i < c_i).astype(BF16)
        diag = c_i < r_i
        q_a, q_b = q[0:SBK], q[SBK:SBQ]
        do_a, do_b2 = do_b[0:SBK], do_b[SBK:SBQ]
        lt = lt_ref[...]

        def tile(qh, doh, kt, vt, dg_, carry):
            rem, gpre, dq = carry
            z, lom = _sb_logits(qh, kt, dg_)
            rem = rem - jnp.sum(lom, axis=-1, keepdims=True)
            a = jnp.exp(lom + z + (_suffix_excl(lom, tri) + rem))
            if dg_ is not None:
                a = jnp.where(dg_, a, 0.0)
            gg = lax.dot_general(doh, vt, NT, preferred_element_type=F32) * a
            pre = _suffix_excl(gg, tri_pre) + gpre
            dz = gg * jnp.exp(lom) - pre * jnp.exp(lom + z)
            if dg_ is not None:
                dz = jnp.where(dg_, dz, 0.0)
            dz_b = (dz * SCALE).astype(BF16)
            dq = dq + jnp.dot(dz_b, kt, preferred_element_type=F32)
            return (rem, gpre + jnp.sum(gg, axis=-1, keepdims=True), dq), dz_b, a.astype(BF16)

        def kv(t):
            k0 = pl.multiple_of(t * SBK, SBK)
            return k0, k_ref[pl.ds(k0, SBK), :], v_ref[pl.ds(k0, SBK), :]

        def both(t, ca, cb, dg_a):
            k0, kt, vt = kv(t)
            ca, dz_a, a_a = tile(q_a, do_a, kt, vt, dg_a, ca)
            cb, dz_b_, a_b = tile(q_b, do_b2, kt, vt, None, cb)
            dk_scr[pl.ds(k0, SBK), :] += lax.dot_general(jnp.concatenate([dz_a, dz_b_], axis=0), q, TN,
                                                         preferred_element_type=F32)
            dv_scr[pl.ds(k0, SBK), :] += lax.dot_general(jnp.concatenate([a_a, a_b], axis=0), do_b, TN,
                                                         preferred_element_type=F32)
            return ca, cb

        def step(t, carry):
            ca, cb = carry
            for u in range(SB_TILES_PER_TRIP):
                ca, cb = both(SB_TILES_PER_TRIP * t + u, ca, cb, None)
            return ca, cb

        zero = (jnp.zeros((SBK, 1), F32), jnp.zeros((SBK, HEAD_DIM), F32))
        ca, cb = lax.fori_loop(0, (2 // SB_TILES_PER_TRIP) * qi, step, ((lt[0:SBK], *zero), (lt[SBK:SBQ], *zero)))
        ca, cb = both(2 * qi, ca, cb, diag)
        k0, kt, vt = kv(2 * qi + 1)
        cb, dz_b_, a_b = tile(q_b, do_b2, kt, vt, diag, cb)
        dk_scr[pl.ds(k0, SBK), :] += lax.dot_general(dz_b_, q_b, TN, preferred_element_type=F32)
        dv_scr[pl.ds(k0, SBK), :] += lax.dot_general(a_b, do_b2, TN, preferred_element_type=F32)
        dq_ref[...] = jnp.concatenate([ca[2], cb[2]], axis=0).astype(BF16)

        @pl.when(qi == nq - 1)
        def _():
            dk_ref[...] = dk_scr[...].astype(BF16)
            dv_ref[...] = dv_scr[...].astype(BF16)

    head_col = lambda off: pl.BlockSpec((s, HEAD_DIM), lambda h, i: (0, off + h))
    qblk = pl.BlockSpec((SBQ, HEAD_DIM), lambda h, i: (i, h))
    return pl.pallas_call(
        body, name=name, grid=(nh, nq),
        in_specs=[qblk, head_col(nh), head_col(2 * nh), qblk, qblk,
                  pl.BlockSpec((None, SBQ, 1), lambda h, i: (h, i, 0)), qblk],
        out_specs=[qblk, head_col(0), head_col(0), qblk],
        out_shape=[jax.ShapeDtypeStruct((s, width), BF16)] * 4,
        scratch_shapes=[pltpu.VMEM((s, HEAD_DIM), F32), pltpu.VMEM((s, HEAD_DIM), F32)],
        compiler_params=_params(("parallel", "arbitrary")),
    )(qkv, qkv, qkv, gate, o, ltot, dmixed)


HBM = pl.BlockSpec(memory_space=pl.ANY)
MESH = pl.DeviceIdType.MESH


def _all_gather(shard, *, name):
    r, c_ = shard.shape

    def body(x_ref, out_ref, send_sems, recv_sems, local_sem):
        x, y, c = lax.axis_index("x"), lax.axis_index("y"), lax.axis_index("c")
        me, sibling = (x, y, c), (x, y, 1 - c)
        chips = [(1 - x, y), (x, 1 - y), (1 - x, 1 - y)]

        def slot(px, py, pc):
            return out_ref.at[4 * px + 2 * py + pc]

        def copy(k, block, to, src=None):
            return pltpu.make_async_remote_copy(
                src_ref=slot(*block) if src is None else src, dst_ref=slot(*block),
                send_sem=send_sems.at[k], recv_sem=recv_sems.at[k], device_id=to, device_id_type=MESH)

        mine = pltpu.make_async_copy(x_ref, slot(*me), local_sem)
        mine.start()
        first = [copy(0, me, sibling, src=x_ref)]
        first += [copy(1 + j, me, (*chip, c), src=x_ref) for j, chip in enumerate(chips)]
        for cp in first:
            cp.start()
        passed = [copy(4 + j, (*chip, c), sibling) for j, chip in enumerate(chips)]
        for j, chip in enumerate(chips):
            copy(1 + j, (*chip, c), me).wait_recv()
            passed[j].start()
        copy(0, sibling, me).wait_recv()
        for j, chip in enumerate(chips):
            copy(4 + j, (*chip, 1 - c), me).wait_recv()
        for cp in first + passed:
            cp.wait_send()
        mine.wait()

    return pl.pallas_call(
        body, name=name, in_specs=[HBM], out_specs=HBM,
        out_shape=jax.ShapeDtypeStruct((N_DEV, r, c_), shard.dtype),
        scratch_shapes=[pltpu.SemaphoreType.DMA((7,)), pltpu.SemaphoreType.DMA((7,)), pltpu.SemaphoreType.DMA],
    )(shard)


def _exchange(parts, *, name, space=HBM):
    _, r, c_ = parts.shape

    def body(p_ref, out_ref, send_sems, recv_sems, local_sem):
        x, y, c = lax.axis_index("x"), lax.axis_index("y"), lax.axis_index("c")
        me = 4 * x + 2 * y + c
        mine = pltpu.make_async_copy(p_ref.at[me], out_ref.at[me], local_sem)
        mine.start()
        copies = []
        for k in range(1, N_DEV):
            px, py, pc = x ^ ((k >> 2) & 1), y ^ ((k >> 1) & 1), c ^ (k & 1)
            peer = 4 * px + 2 * py + pc
            copies.append(pltpu.make_async_remote_copy(
                src_ref=p_ref.at[peer], dst_ref=out_ref.at[me],
                send_sem=send_sems.at[k - 1], recv_sem=recv_sems.at[k - 1],
                device_id=(px, py, pc), device_id_type=MESH))
        for cp in copies:
            cp.start()
        for cp in copies:
            cp.wait_recv()
        for cp in copies:
            cp.wait_send()
        mine.wait()

    return pl.pallas_call(
        body, name=name, in_specs=[space], out_specs=space,
        out_shape=jax.ShapeDtypeStruct(parts.shape, parts.dtype),
        scratch_shapes=[pltpu.SemaphoreType.DMA((7,)), pltpu.SemaphoreType.DMA((7,)), pltpu.SemaphoreType.DMA],
    )(parts)


def _adamw(parts, w, m, v, *, name, rows):
    r, c_ = w.shape
    rows = min(rows, r)
    assert r % rows == 0
    c1 = 1.0 / (1.0 - ADAM_B1 ** ADAM_STEP)
    c2 = 1.0 / (1.0 - ADAM_B2 ** ADAM_STEP)

    def body(p_ref, w_ref, m_ref, v_ref, g_ref, d_ref, nm_ref, nv_ref):
        g = p_ref[0].astype(F32)
        for i in range(1, N_DEV):
            g = g + p_ref[i].astype(F32)
        nm = ADAM_B1 * m_ref[...] + (1.0 - ADAM_B1) * g
        nv = ADAM_B2 * v_ref[...] + (1.0 - ADAM_B2) * (g * g)
        g_ref[...] = g
        nm_ref[...] = nm
        nv_ref[...] = nv
        d_ref[...] = -ADAM_LR * ((nm * c1) / (jnp.sqrt(nv * c2) + ADAM_EPS) + ADAM_WD * w_ref[...])

    blk = pl.BlockSpec((rows, c_), lambda i: (i, 0))
    return pl.pallas_call(
        body, name=name, grid=(r // rows,),
        in_specs=[pl.BlockSpec((N_DEV, rows, c_), lambda i: (0, i, 0)), blk, blk, blk],
        out_specs=[blk] * 4, out_shape=[jax.ShapeDtypeStruct((r, c_), F32)] * 4,
        compiler_params=_params(("parallel",)),
    )(parts, w, m, v)


def _pad_cols(a, n):
    return jnp.pad(a, ((0, 0), (0, n - a.shape[1])))


def _fox_dc(shares, nh, s):
    key_side, query_side = shares
    return _pad_cols(jnp.transpose(key_side.reshape(nh, s) + query_side.reshape(nh, s)), 128)


def _local_step(x, target, norm_pre, norm_post, w_in_e, w_f, b_f, rel_bias, w_out_e, w_in_o, w_out_o):
    s, d = x.shape
    wa = d // 2
    nha = wa // HEAD_DIM
    nq = s // CQ

    h0 = _rms_fwd(x, norm_pre[0:1], name="rms_pre0")
    proj = lambda w, off, n, dt, nm: _matmul(h0, w, mode="nn", m=s, n=n, k=d, out_dtype=dt, name=nm, b_off=(0, off))
    qkv_a = proj(w_in_e, 0, 3 * wa, BF16, "proj_qkv_a")
    g_a = proj(w_in_e, 3 * wa, wa, F32, "proj_gate_a")
    qkv_b = proj(w_in_e, 4 * wa, 3 * wa, BF16, "proj_qkv_b")
    g_b = proj(w_in_e, 7 * wa, wa, F32, "proj_gate_b")
    af = _matmul(h0, w_f, mode="nn", m=s, n=128, k=d, out_dtype=F32, name="proj_forget")
    bias128 = _pad_cols(b_f, 128)
    cum = _gate_fwd(af, bias128, name="forget_cumsum")
    c_t = jnp.transpose(cum[:, :nha])
    c_col = c_t.reshape(nha, s, 1)
    c_row = c_t.reshape(nha, nq, 1, CQ)
    mixed_a, o_a, lse_a = _softmax_fwd(qkv_a, g_a, (c_col, c_row), mode="fox", width=wa, name="fox_fwd")
    rel_aux = (_pad_cols(rel_bias, REL_PAD).reshape(nha, 1, REL_PAD), _rel_onehot())
    mixed_b, o_b, lse_b = _softmax_fwd(qkv_b, g_b, rel_aux, mode="chunk", width=wa, name="chunk_fwd")
    mixed0 = jnp.concatenate([mixed_a, mixed_b], axis=1)
    y0 = _matmul(mixed0, w_out_e, mode="nn", m=s, n=d, k=d, out_dtype=F32, name="out_proj0")
    x1 = _post_fwd(x, y0, norm_post[0:1], name="post0")

    h1 = _rms_fwd(x1, norm_pre[1:2], name="rms_pre1")
    qkv_c = _matmul(h1, w_in_o, mode="nn", m=s, n=3 * d, k=d, out_dtype=BF16, name="proj_qkv_c")
    g_c = _matmul(h1, w_in_o, mode="nn", m=s, n=d, k=d, out_dtype=F32, name="proj_gate_c", b_off=(0, 3 * d))
    mixed1, o_c, ltot_c = _sb_fwd(qkv_c, g_c, width=d, name="sb_fwd")
    y1 = _matmul(mixed1, w_out_o, mode="nn", m=s, n=d, k=d, out_dtype=F32, name="out_proj1")
    x2 = _post_fwd(x1, y1, norm_post[1:2], name="post1")

    loss, dx2 = _loss_head(x2, target, name="loss_head")

    dy1, dgpost1 = _post_bwd(dx2, y1, norm_post[1:2], name="post_bwd1")
    dmixed1 = _matmul(dy1, w_out_o, mode="nt", m=s, n=d, k=d, out_dtype=F32, name="dmixed1")
    dw_out_o = _matmul(mixed1, dy1, mode="tn", m=d, n=d, k=s, out_dtype=BF16, name="dw_out1")
    dq_c, dk_c, dv_c, dg_c = _sb_bwd(qkv_c, g_c, o_c, ltot_c, dmixed1, width=d, name="sb_bwd")
    dproj1 = jnp.concatenate([dq_c, dk_c, dv_c, dg_c], axis=1)
    dh1 = _matmul(dproj1, w_in_o, mode="nt", m=s, n=d, k=4 * d, out_dtype=F32, name="dh1")
    dw_in_o = _matmul(h1, dproj1, mode="tn", m=d, n=4 * d, k=s, out_dtype=BF16, name="dw_in1")
    dx1, dgpre1 = _pre_bwd((dh1,), x1, norm_pre[1:2], dx2, name="pre_bwd1")

    dy0, dgpost0 = _post_bwd(dx1, y0, norm_post[0:1], name="post_bwd0")
    dmixed0 = _matmul(dy0, w_out_e, mode="nt", m=s, n=d, k=d, out_dtype=F32, name="dmixed0")
    dw_out_e = _matmul(mixed0, dy0, mode="tn", m=d, n=d, k=s, out_dtype=BF16, name="dw_out0")
    dq_a, dk_a, dv_a, dg_a, *dc_shares = _softmax_bwd(qkv_a, g_a, o_a, dmixed0, 0, lse_a, (c_col, c_row),
                                                      mode="fox", width=wa, name="fox_bwd")
    dq_b, dk_b, dv_b, dg_b, drel = _softmax_bwd(qkv_b, g_b, o_b, dmixed0, wa, lse_b, rel_aux,
                                                mode="chunk", width=wa, name="chunk_bwd")
    dc = _fox_dc(dc_shares, nha, s)
    daf, dbf = _gate_bwd(dc, af, bias128, name="forget_bwd")
    dproj0 = jnp.concatenate([dq_a, dk_a, dv_a, dg_a, dq_b, dk_b, dv_b, dg_b], axis=1)
    dh0 = _matmul(dproj0, w_in_e, mode="nt", m=s, n=d, k=8 * wa, out_dtype=F32, name="dh0_main")
    dh0f = _matmul(daf, w_f, mode="nt", m=s, n=d, k=128, out_dtype=F32, name="dh0_forget")
    dw_in_e = _matmul(h0, dproj0, mode="tn", m=d, n=8 * wa, k=s, out_dtype=BF16, name="dw_in0")
    dw_f = _matmul(h0, daf, mode="tn", m=d, n=128, k=s, out_dtype=BF16, name="dw_forget")
    dx0, dgpre0 = _pre_bwd((dh0, dh0f), x, norm_pre[0:1], dx1, name="pre_bwd0")

    small = dict(
        norm_pre=jnp.concatenate([dgpre0, dgpre1], axis=0),
        norm_post=jnp.concatenate([dgpost0, dgpost1], axis=0),
        b_f=dbf[:, :nha], rel_bias=drel[:, 0, :N_REL])
    big = dict(w_in_e=dw_in_e, w_f=dw_f[:, :nha], w_out_e=dw_out_e, w_in_o=dw_in_o, w_out_o=dw_out_o)
    return loss, dx0, big, small


def _pack_small(norm_pre, norm_post, b_f, rel_bias):
    flat = jnp.concatenate([norm_pre.reshape(-1), norm_post.reshape(-1), b_f.reshape(-1), rel_bias.reshape(-1)])
    n = flat.shape[0]
    rows = -(-n // 128)
    rows = -(-rows // 8) * 8
    return jnp.pad(flat, (0, rows * 128 - n)).reshape(rows, 128)


def _unpack_small(slab, shapes):
    flat = slab.reshape(-1)
    out, off = [], 0
    for shp in shapes:
        n = int(np.prod(shp))
        out.append(flat[off:off + n].reshape(shp))
        off += n
    return out


def kernel(x, norm_pre, norm_post, w_in_even, b_f_even, rel_bias_even, w_out_even, w_in_odd, w_out_odd, loss_target, m_norm_pre, m_norm_post, m_w_in_even, m_b_f_even, m_rel_bias_even, m_w_out_even, m_w_in_odd, m_w_out_odd, v_norm_pre, v_norm_post, v_w_in_even, v_b_f_even, v_rel_bias_even, v_w_out_even, v_w_in_odd, v_w_out_odd):
    _, s, d = x.shape
    wa = d // 2
    nha = wa // HEAD_DIM
    in_e = w_in_even.shape[2] * N_DEV
    sh_e = w_in_even.shape[2]

    ag = lambda w, nm: _all_gather(w.astype(BF16), name=nm)
    w_in_e_all = jnp.transpose(ag(w_in_even[0], "ag_w_in_even"), (1, 0, 2)).reshape(d, in_e)
    w_out_e_all = ag(w_out_even[0], "ag_w_out_even").reshape(d, d)
    w_in_o_all = jnp.transpose(ag(w_in_odd[0], "ag_w_in_odd"), (1, 0, 2)).reshape(d, 4 * d)
    w_out_o_all = ag(w_out_odd[0], "ag_w_out_odd").reshape(d, d)
    w_main = w_in_e_all[:, :8 * wa]
    w_f = _pad_cols(w_in_e_all[:, 8 * wa:], 128)

    loss, dx, big, small = _local_step(x[0], loss_target[0], norm_pre, norm_post, w_main, w_f, b_f_even,
                                       rel_bias_even[0], w_out_e_all, w_in_o_all, w_out_o_all)

    dw_e = jnp.concatenate([big["w_in_e"], big["w_f"]], axis=1)
    parts_in_e = jnp.transpose(dw_e.reshape(d, N_DEV, sh_e), (1, 0, 2))
    parts_in_o = jnp.transpose(big["w_in_o"].reshape(d, N_DEV, 4 * d // N_DEV), (1, 0, 2))
    parts_out_e = big["w_out_e"].reshape(N_DEV, d // N_DEV, d)
    parts_out_o = big["w_out_o"].reshape(N_DEV, d // N_DEV, d)
    upd = {}
    upd["w_in_even"] = _adamw(_exchange(parts_in_e, name="rs_w_in_even"), w_in_even[0], m_w_in_even[0], v_w_in_even[0],
                              name="adamw_w_in_even", rows=128)
    upd["w_out_even"] = _adamw(_exchange(parts_out_e, name="rs_w_out_even"), w_out_even[0], m_w_out_even[0], v_w_out_even[0],
                               name="adamw_w_out_even", rows=64)
    upd["w_in_odd"] = _adamw(_exchange(parts_in_o, name="rs_w_in_odd"), w_in_odd[0], m_w_in_odd[0], v_w_in_odd[0],
                             name="adamw_w_in_odd", rows=128)
    upd["w_out_odd"] = _adamw(_exchange(parts_out_o, name="rs_w_out_odd"), w_out_odd[0], m_w_out_odd[0], v_w_out_odd[0],
                              name="adamw_w_out_odd", rows=64)

    shapes = [norm_pre.shape, norm_post.shape, b_f_even.shape, rel_bias_even.shape]
    g_slab = _pack_small(small["norm_pre"], small["norm_post"], small["b_f"], small["rel_bias"])
    parts_small = _exchange(jnp.broadcast_to(g_slab[None], (N_DEV,) + g_slab.shape), name="ar_small")
    sm = _adamw(parts_small, _pack_small(norm_pre, norm_post, b_f_even, rel_bias_even),
                _pack_small(m_norm_pre, m_norm_post, m_b_f_even, m_rel_bias_even),
                _pack_small(v_norm_pre, v_norm_post, v_b_f_even, v_rel_bias_even), name="adamw_small", rows=g_slab.shape[0])
    sm = [_unpack_small(a, shapes) for a in sm]

    total = lax.psum(loss[0, 0], ("x", "y", "c"))

    def leaves(kind):
        return (sm[kind][0], sm[kind][1], upd["w_in_even"][kind][None], sm[kind][2], sm[kind][3],
                upd["w_out_even"][kind][None], upd["w_in_odd"][kind][None], upd["w_out_odd"][kind][None])

    return (total, dx[None], *leaves(0), *leaves(1), *leaves(2), *leaves(3))
```

```python
import functools

import numpy as np
import jax
import jax.numpy as jnp
from jax import lax
from jax.experimental import pallas as pl
from jax.experimental.pallas import tpu as pltpu

F32 = jnp.float32
BF16 = jnp.bfloat16

HEAD_DIM = 128
CHUNK = 64
LEFT_CHUNKS = 8
REL_CLIP = 128
N_REL = 2 * REL_CLIP + 1
RMS_EPS = 1e-6
SCALE = HEAD_DIM ** -0.5

ADAM_LR = 0.001
ADAM_B1 = 0.9
ADAM_B2 = 0.999
ADAM_EPS = 1e-08
ADAM_WD = 0.01
ADAM_STEP = 10

N_DEV = 8
V7X_VMEM_LIMIT_BYTES = 56 * 1024 * 1024
NEG = -1e30

NT = (((1,), (1,)), ((), ()))
TN = (((0,), (0,)), ((), ()))
NN = (((1,), (0,)), ((), ()))

CQ = 256
BAND_TILES = 3
TOEP = 2 * CQ
assert (BAND_TILES - 1) * CQ == LEFT_CHUNKS * CHUNK


def _params(sem):
    return pltpu.CompilerParams(dimension_semantics=sem, vmem_limit_bytes=V7X_VMEM_LIMIT_BYTES)


def _split3(x):
    hi = x.astype(BF16)
    r1 = x - hi.astype(F32)
    mid = r1.astype(BF16)
    lo = (r1 - mid.astype(F32)).astype(BF16)
    return hi, mid, lo


def _split2(x):
    hi = x.astype(BF16)
    lo = (x - hi.astype(F32)).astype(BF16)
    return hi, lo


def _sigmoid(g):
    return 1.0 / (1.0 + jnp.exp(-g))


def _tile(n, cap, *offsets):
    if n <= 128:
        return n
    t = (min(cap, n) // 128) * 128
    while n % t or any(o % t for o in offsets):
        t -= 128
    return t


def _matmul(a, b, *, mode, m, n, k, out_dtype, name, a_off=(0, 0), b_off=(0, 0), tm=1024, tn=1024, tk=1024,
            ride=None):
    a_m, a_k = (a_off if mode in ("nn", "nt") else a_off[::-1])
    b_k, b_n = (b_off if mode in ("nn", "tn") else b_off[::-1])
    tm, tn, tk = _tile(m, tm, a_m), _tile(n, tn, b_n), _tile(k, tk, a_k, b_k)
    nk = k // tk
    if mode in ("nn", "nt"):
        ao = (a_off[0] // tm, a_off[1] // tk)
        a_spec = pl.BlockSpec((tm, tk), lambda i, j, l: (i + ao[0], l + ao[1]))
    else:
        ao = (a_off[0] // tk, a_off[1] // tm)
        a_spec = pl.BlockSpec((tk, tm), lambda i, j, l: (l + ao[0], i + ao[1]))
    if mode in ("nn", "tn"):
        bo = (b_off[0] // tk, b_off[1] // tn)
        b_spec = pl.BlockSpec((tk, tn), lambda i, j, l: (l + bo[0], j + bo[1]))
    else:
        bo = (b_off[0] // tn, b_off[1] // tk)
        b_spec = pl.BlockSpec((tn, tk), lambda i, j, l: (j + bo[0], l + bo[1]))
    dn = {"nn": NN, "nt": NT, "tn": TN}[mode]

    def body(a_ref, b_ref, o_ref, acc_ref):
        @pl.when(pl.program_id(2) == 0)
        def _():
            acc_ref[...] = jnp.zeros_like(acc_ref)

        acc_ref[...] += lax.dot_general(a_ref[...], b_ref[...], dn, preferred_element_type=F32)

        @pl.when(pl.program_id(2) == nk - 1)
        def _():
            o_ref[...] = acc_ref[...].astype(out_dtype)

    (out,), carried = _call(
        body, name=name, grid=(m // tm, n // tn, nk), in_specs=[a_spec, b_spec],
        out_specs=[pl.BlockSpec((tm, tn), lambda i, j, l: (i, j))], out_shape=[jax.ShapeDtypeStruct((m, n), out_dtype)],
        scratch=[pltpu.VMEM((tm, tn), F32)], sem=("parallel", "parallel", "arbitrary"), args=(a, b), ride=ride)
    return out if ride is None else (out, carried)


ROWS = 128


def _rms_fwd(x, gain, *, name):
    s, d = x.shape

    def body(x_ref, g_ref, h_ref):
        xf = x_ref[...]
        r = lax.rsqrt(jnp.mean(xf * xf, axis=-1, keepdims=True) + RMS_EPS)
        h_ref[...] = ((xf * r) * g_ref[...]).astype(BF16)

    return pl.pallas_call(
        body, name=name, grid=(s // ROWS,),
        in_specs=[pl.BlockSpec((ROWS, d), lambda i: (i, 0)), pl.BlockSpec((1, d), lambda i: (0, 0))],
        out_specs=pl.BlockSpec((ROWS, d), lambda i: (i, 0)),
        out_shape=jax.ShapeDtypeStruct((s, d), BF16),
        compiler_params=_params(("parallel",)),
    )(x, gain)


def _post_fwd(x, y, gain, *, name):
    s, d = x.shape

    def body(x_ref, y_ref, g_ref, o_ref):
        yf = y_ref[...]
        r = lax.rsqrt(jnp.mean(yf * yf, axis=-1, keepdims=True) + RMS_EPS)
        o_ref[...] = x_ref[...] + (yf * r) * g_ref[...]

    return pl.pallas_call(
        body, name=name, grid=(s // ROWS,),
        in_specs=[pl.BlockSpec((ROWS, d), lambda i: (i, 0)), pl.BlockSpec((ROWS, d), lambda i: (i, 0)),
                  pl.BlockSpec((1, d), lambda i: (0, 0))],
        out_specs=pl.BlockSpec((ROWS, d), lambda i: (i, 0)),
        out_shape=jax.ShapeDtypeStruct((s, d), F32),
        compiler_params=_params(("parallel",)),
    )(x, y, gain)


def _loss_head(xo, target, *, name):
    s, d = xo.shape
    inv_d = 1.0 / d

    def body(x_ref, t_ref, loss_ref, dx_ref):
        @pl.when(pl.program_id(0) == 0)
        def _():
            loss_ref[...] = jnp.zeros_like(loss_ref)

        e = x_ref[...] - t_ref[...]
        dx_ref[...] = e * inv_d
        loss_ref[...] += 0.5 * jnp.sum(jnp.mean(e * e, axis=-1, keepdims=True), axis=0, keepdims=True)

    return pl.pallas_call(
        body, name=name, grid=(s // ROWS,),
        in_specs=[pl.BlockSpec((ROWS, d), lambda i: (i, 0)), pl.BlockSpec((ROWS, d), lambda i: (i, 0))],
        out_specs=[pl.BlockSpec((1, 1), lambda i: (0, 0)), pl.BlockSpec((ROWS, d), lambda i: (i, 0))],
        out_shape=[jax.ShapeDtypeStruct((1, 1), F32), jax.ShapeDtypeStruct((s, d), F32)],
        compiler_params=_params(("arbitrary",)),
    )(xo, target)


def _post_bwd(dxo, y, gain, *, name):
    s, d = y.shape

    def body(dx_ref, y_ref, g_ref, dy_ref, dg_ref):
        @pl.when(pl.program_id(0) == 0)
        def _():
            dg_ref[...] = jnp.zeros_like(dg_ref)

        yf = y_ref[...]
        dxo_ = dx_ref[...]
        r = lax.rsqrt(jnp.mean(yf * yf, axis=-1, keepdims=True) + RMS_EPS)
        nrm = yf * r
        dg_ref[...] += jnp.sum(dxo_ * nrm, axis=0, keepdims=True)
        dn = dxo_ * g_ref[...]
        dy_ref[...] = (r * (dn - nrm * jnp.mean(dn * nrm, axis=-1, keepdims=True))).astype(BF16)

    return pl.pallas_call(
        body, name=name, grid=(s // ROWS,),
        in_specs=[pl.BlockSpec((ROWS, d), lambda i: (i, 0)), pl.BlockSpec((ROWS, d), lambda i: (i, 0)),
                  pl.BlockSpec((1, d), lambda i: (0, 0))],
        out_specs=[pl.BlockSpec((ROWS, d), lambda i: (i, 0)), pl.BlockSpec((1, d), lambda i: (0, 0))],
        out_shape=[jax.ShapeDtypeStruct((s, d), BF16), jax.ShapeDtypeStruct((1, d), F32)],
        compiler_params=_params(("arbitrary",)),
    )(dxo, y, gain)


def _pre_bwd(dhs, x, gain, dres, *, name):
    s, d = x.shape
    n_dh = len(dhs)

    def body(*refs):
        dh_refs = refs[:n_dh]
        x_ref, g_ref, dr_ref, dx_ref, dg_ref = refs[n_dh:]

        @pl.when(pl.program_id(0) == 0)
        def _():
            dg_ref[...] = jnp.zeros_like(dg_ref)

        xf = x_ref[...]
        dh_ = dh_refs[0][...]
        for extra in dh_refs[1:]:
            dh_ = dh_ + extra[...]
        r = lax.rsqrt(jnp.mean(xf * xf, axis=-1, keepdims=True) + RMS_EPS)
        nrm = xf * r
        dg_ref[...] += jnp.sum(dh_ * nrm, axis=0, keepdims=True)
        dn = dh_ * g_ref[...]
        dx_ref[...] = dr_ref[...] + r * (dn - nrm * jnp.mean(dn * nrm, axis=-1, keepdims=True))

    return pl.pallas_call(
        body, name=name, grid=(s // ROWS,),
        in_specs=[pl.BlockSpec((ROWS, d), lambda i: (i, 0))] * (n_dh + 1)
        + [pl.BlockSpec((1, d), lambda i: (0, 0)), pl.BlockSpec((ROWS, d), lambda i: (i, 0))],
        out_specs=[pl.BlockSpec((ROWS, d), lambda i: (i, 0)), pl.BlockSpec((1, d), lambda i: (0, 0))],
        out_shape=[jax.ShapeDtypeStruct((s, d), F32), jax.ShapeDtypeStruct((1, d), F32)],
        compiler_params=_params(("arbitrary",)),
    )(*dhs, x, gain, dres)


GB = 256


def _gate_fwd(af, bias, *, name):
    s, w = af.shape

    def body(af_ref, b_ref, c_ref, carry_ref):
        @pl.when(pl.program_id(0) == 0)
        def _():
            carry_ref[...] = jnp.zeros_like(carry_ref)

        z = af_ref[...] + b_ref[...]
        lf = jnp.minimum(z, 0.0) - jnp.log(1.0 + jnp.exp(-jnp.abs(z)))
        r_i = lax.broadcasted_iota(jnp.int32, (GB, GB), 0)
        c_i = lax.broadcasted_iota(jnp.int32, (GB, GB), 1)
        tri = (c_i <= r_i).astype(BF16)
        hi, mid, lo = _split3(lf)
        pre = (jnp.dot(tri, hi, preferred_element_type=F32) + jnp.dot(tri, mid, preferred_element_type=F32)
               + jnp.dot(tri, lo, preferred_element_type=F32))
        c_ref[...] = pre + carry_ref[...]
        carry_ref[...] += jnp.sum(lf, axis=0, keepdims=True)

    return pl.pallas_call(
        body, name=name, grid=(s // GB,),
        in_specs=[pl.BlockSpec((GB, w), lambda i: (i, 0)), pl.BlockSpec((1, w), lambda i: (0, 0))],
        out_specs=pl.BlockSpec((GB, w), lambda i: (i, 0)),
        out_shape=jax.ShapeDtypeStruct((s, w), F32),
        scratch_shapes=[pltpu.VMEM((1, w), F32)],
        compiler_params=_params(("arbitrary",)),
    )(af, bias)


def _gate_bwd(dc, af, bias, *, name):
    s, w = af.shape
    nb = s // GB

    def body(dc_ref, af_ref, b_ref, daf_ref, db_ref, carry_ref):
        @pl.when(pl.program_id(0) == 0)
        def _():
            carry_ref[...] = jnp.zeros_like(carry_ref)
            db_ref[...] = jnp.zeros_like(db_ref)

        dcb = dc_ref[...]
        r_i = lax.broadcasted_iota(jnp.int32, (GB, GB), 0)
        c_i = lax.broadcasted_iota(jnp.int32, (GB, GB), 1)
        tri = (c_i >= r_i).astype(BF16)
        hi, mid, lo = _split3(dcb)
        suf = (jnp.dot(tri, hi, preferred_element_type=F32) + jnp.dot(tri, mid, preferred_element_type=F32)
               + jnp.dot(tri, lo, preferred_element_type=F32)) + carry_ref[...]
        carry_ref[...] += jnp.sum(dcb, axis=0, keepdims=True)
        z = af_ref[...] + b_ref[...]
        daf = suf * _sigmoid(-z)
        daf_ref[...] = daf.astype(BF16)
        db_ref[...] += jnp.sum(daf, axis=0, keepdims=True)

    return pl.pallas_call(
        body, name=name, grid=(nb,),
        in_specs=[pl.BlockSpec((GB, w), lambda i: (nb - 1 - i, 0)), pl.BlockSpec((GB, w), lambda i: (nb - 1 - i, 0)),
                  pl.BlockSpec((1, w), lambda i: (0, 0))],
        out_specs=[pl.BlockSpec((GB, w), lambda i: (nb - 1 - i, 0)), pl.BlockSpec((1, w), lambda i: (0, 0))],
        out_shape=[jax.ShapeDtypeStruct((s, w), BF16), jax.ShapeDtypeStruct((1, w), F32)],
        scratch_shapes=[pltpu.VMEM((1, w), F32)],
        compiler_params=_params(("arbitrary",)),
    )(dc, af, bias)


def _rel_index_rows():
    w = np.arange(TOEP)
    wp = np.where(w < CQ, w, w - TOEP)
    return np.stack([np.clip(LEFT_CHUNKS * CHUNK - CQ * j - wp, -REL_CLIP, REL_CLIP) + REL_CLIP
                     for j in range(BAND_TILES)]).astype(np.int32)


def _skew_rows(xw, sign):
    row = lax.broadcasted_iota(jnp.int32, xw.shape, 0)
    for b in range(CQ.bit_length() - 1):
        amt = (1 << b) if sign > 0 else TOEP - (1 << b)
        xw = jnp.where(((row >> b) & 1) == 1, pltpu.roll(xw, amt, 1), xw)
    return xw


REL_PAD = 384


def _rel_onehot():
    return jnp.asarray(_rel_index_rows()[:, :, None] == np.arange(REL_PAD)[None, None, :], BF16)


def _fill_bias_tiles(rel_ref, oh_ref, bias_scr):
    parts = _split3(jnp.broadcast_to(rel_ref[...], (8, REL_PAD)))
    for j in range(BAND_TILES):
        row = sum(lax.dot_general(p, oh_ref[j], NT, preferred_element_type=F32) for p in parts)[0:1]
        bias_scr[j] = _skew_rows(jnp.broadcast_to(row, (CQ, TOEP)), +1)[:, :CQ]


def _fox_scores(s, cq, cr, diagonal):
    s = s + (cq - cr)
    if not diagonal:
        return s
    bq, bk = s.shape
    return jnp.where(lax.broadcasted_iota(jnp.int32, (bq, bk), 1) <= lax.broadcasted_iota(jnp.int32, (bq, bk), 0), s, NEG)


def _chunk_scores(s, q0, k0, bias):
    bq, bk = s.shape
    qc = (q0 + lax.broadcasted_iota(jnp.int32, (bq, bk), 0)) >> 6
    kc = (k0 + lax.broadcasted_iota(jnp.int32, (bq, bk), 1)) >> 6
    return jnp.where((kc <= qc) & (kc >= qc - LEFT_CHUNKS), s + bias, NEG)


def _softmax_fwd(qkv, gate, aux, *, mode, width, name, ride=None):
    s = qkv.shape[0]
    nh = width // HEAD_DIM
    bq = bk = CQ
    nq = s // bq

    def body(q_ref, k_ref, v_ref, g_ref, *rest):
        if mode == "fox":
            cc_ref, cr_ref, mixed_ref, o_ref, lse_ref = rest
        else:
            rel_ref, oh_ref, mixed_ref, o_ref, lse_ref, bias_scr = rest
        qi = pl.program_id(1)
        q0 = qi * bq
        q = q_ref[...]

        if mode == "chunk":
            @pl.when(qi == 0)
            def _():
                _fill_bias_tiles(rel_ref, oh_ref, bias_scr)

        if mode == "fox":
            cq = cc_ref[...]
        else:
            lo, hi = jnp.maximum(qi - (BAND_TILES - 1), 0), qi + 1

        def step(ki, carry, diagonal=False):
            m, l, acc = carry
            k0 = pl.multiple_of(ki * bk, bk)
            kt = k_ref[pl.ds(k0, bk), :]
            vt = v_ref[pl.ds(k0, bk), :]
            sc = lax.dot_general(q, kt, NT, preferred_element_type=F32) * SCALE
            if mode == "fox":
                sc = _fox_scores(sc, cq, cr_ref[ki], diagonal)
            else:
                sc = _chunk_scores(sc, q0, k0, bias_scr[ki - qi + (BAND_TILES - 1)])
            m_new = jnp.maximum(m, jnp.max(sc, axis=-1, keepdims=True))
            p = jnp.exp(sc - m_new)
            alpha = jnp.exp(m - m_new)
            l = alpha * l + jnp.sum(p, axis=-1, keepdims=True)
            acc = alpha * acc + jnp.dot(p.astype(BF16), vt, preferred_element_type=F32)
            return m_new, l, acc

        init = (jnp.full((bq, 1), NEG, F32), jnp.zeros((bq, 1), F32), jnp.zeros((bq, HEAD_DIM), F32))
        if mode == "fox":
            m, l, acc = step(qi, lax.fori_loop(0, qi, step, init), diagonal=True)
        else:
            m, l, acc = lax.fori_loop(lo, hi, step, init)
        o = acc / l
        g = g_ref[...]
        o_ref[...] = o
        mixed_ref[...] = (o * (g * _sigmoid(g))).astype(BF16)
        lse_ref[...] = m + jnp.log(l)

    head_col = lambda off: pl.BlockSpec((s, HEAD_DIM), lambda h, i: (0, off + h))
    in_specs = [pl.BlockSpec((bq, HEAD_DIM), lambda h, i: (i, h)), head_col(nh), head_col(2 * nh),
                pl.BlockSpec((bq, HEAD_DIM), lambda h, i: (i, h))]
    scratch = []
    if mode == "fox":
        in_specs += [pl.BlockSpec((None, bq, 1), lambda h, i: (h, i, 0)),
                     pl.BlockSpec((None, nq, 1, bk), lambda h, i: (h, 0, 0, 0))]
    else:
        in_specs += [pl.BlockSpec((None, 1, REL_PAD), lambda h, i: (h, 0, 0)),
                     pl.BlockSpec((BAND_TILES, TOEP, REL_PAD), lambda h, i: (0, 0, 0))]
        scratch = [pltpu.VMEM((BAND_TILES, CQ, CQ), F32)]
    outs, carried = _call(
        body, name=name, grid=(nh, nq), in_specs=in_specs,
        out_specs=[pl.BlockSpec((bq, HEAD_DIM), lambda h, i: (i, h)), pl.BlockSpec((bq, HEAD_DIM), lambda h, i: (i, h)),
                   pl.BlockSpec((None, bq, 1), lambda h, i: (h, i, 0))],
        out_shape=[jax.ShapeDtypeStruct((s, width), BF16), jax.ShapeDtypeStruct((s, width), F32),
                   jax.ShapeDtypeStruct((nh, s, 1), F32)],
        scratch=scratch, sem=("parallel", "arbitrary"), args=(qkv, qkv, qkv, gate, *aux), ride=ride)
    return outs if ride is None else (*outs, carried)


def _softmax_bwd(qkv, gate, o, dmixed, dm_off, lse, aux, *, mode, width, name, ride=None):
    s = qkv.shape[0]
    nh = width // HEAD_DIM
    bq = bk = CQ
    nq = s // bq
    dmo = dm_off // HEAD_DIM
    rel_pad = REL_PAD

    def body(q_ref, k_ref, v_ref, g_ref, o_ref, dm_ref, lse_ref, *rest):
        if mode == "fox":
            cc_ref, cr_ref, dq_ref, dk_ref, dv_ref, dg_ref, dc_ref, dcq_ref, dk_scr, dv_scr, dc_scr = rest
        else:
            rel_ref, oh_ref, dq_ref, dk_ref, dv_ref, dg_ref, drel_ref, dk_scr, dv_scr, bias_scr, db_scr = rest
        qi = pl.program_id(1)
        q0 = qi * bq

        @pl.when(qi == 0)
        def _():
            dk_scr[...] = jnp.zeros_like(dk_scr)
            dv_scr[...] = jnp.zeros_like(dv_scr)
            if mode == "fox":
                dc_scr[...] = jnp.zeros_like(dc_scr)
            else:
                db_scr[...] = jnp.zeros_like(db_scr)
                _fill_bias_tiles(rel_ref, oh_ref, bias_scr)

        g = g_ref[...]
        of = o_ref[...]
        dm = dm_ref[...]
        sig = _sigmoid(g)
        do = dm * (g * sig)
        dg_ref[...] = (dm * of * (sig * (1.0 + g * (1.0 - sig)))).astype(BF16)
        delta = jnp.sum(do * of, axis=-1, keepdims=True)
        do_b = do.astype(BF16)
        q = q_ref[...]
        lse_q = lse_ref[...]
        if mode == "fox":
            cq = cc_ref[...]
        else:
            lo, hi = jnp.maximum(qi - (BAND_TILES - 1), 0), qi + 1

        def step(ki, carry, diagonal=False):
            dq, rsum = carry
            k0 = pl.multiple_of(ki * bk, bk)
            kt = k_ref[pl.ds(k0, bk), :]
            vt = v_ref[pl.ds(k0, bk), :]
            sc = lax.dot_general(q, kt, NT, preferred_element_type=F32) * SCALE
            if mode == "fox":
                sc = _fox_scores(sc, cq, cr_ref[ki], diagonal)
            else:
                sc = _chunk_scores(sc, q0, k0, bias_scr[ki - qi + (BAND_TILES - 1)])
            p = jnp.exp(sc - lse_q)
            dp = lax.dot_general(do_b, vt, NT, preferred_element_type=F32)
            ds = p * (dp - delta)
            if mode == "fox":
                dc_scr[ki] += -jnp.sum(ds, axis=0, keepdims=True)
                rsum = rsum + jnp.sum(ds, axis=-1, keepdims=True)
            else:
                db_scr[ki - qi + (BAND_TILES - 1)] += ds
            ds_b = (ds * SCALE).astype(BF16)
            dk_scr[pl.ds(k0, bk), :] += lax.dot_general(ds_b, q, TN, preferred_element_type=F32)
            dv_scr[pl.ds(k0, bk), :] += lax.dot_general(p.astype(BF16), do_b, TN, preferred_element_type=F32)
            return dq + jnp.dot(ds_b, kt, preferred_element_type=F32), rsum

        init = (jnp.zeros((bq, HEAD_DIM), F32), jnp.zeros((bq, 1), F32))
        if mode == "fox":
            dq, rsum = step(qi, lax.fori_loop(0, qi, step, init), diagonal=True)
        else:
            dq, rsum = lax.fori_loop(lo, hi, step, init)
        dq_ref[...] = dq.astype(BF16)
        if mode == "fox":
            dcq_ref[...] = rsum

        @pl.when(qi == nq - 1)
        def _():
            dk_ref[...] = dk_scr[...].astype(BF16)
            dv_ref[...] = dv_scr[...].astype(BF16)
            if mode == "fox":
                dc_ref[...] = dc_scr[...]
            else:
                tot = jnp.zeros((8, rel_pad), F32)
                for j in range(BAND_TILES):
                    wide = jnp.concatenate([db_scr[j], jnp.zeros((CQ, TOEP - CQ), F32)], axis=1)
                    diag = jnp.sum(_skew_rows(wide, -1), axis=0, keepdims=True)
                    for part in _split3(jnp.broadcast_to(diag, (8, TOEP))):
                        tot = tot + jnp.dot(part, oh_ref[j], preferred_element_type=F32)
                drel_ref[...] = tot[0:1, :]

    head_col = lambda off: pl.BlockSpec((s, HEAD_DIM), lambda h, i: (0, off + h))
    qblk = lambda off: pl.BlockSpec((bq, HEAD_DIM), lambda h, i: (i, off + h))
    in_specs = [qblk(0), head_col(nh), head_col(2 * nh), qblk(0), qblk(0), qblk(dmo),
                pl.BlockSpec((None, bq, 1), lambda h, i: (h, i, 0))]
    out_specs = [qblk(0), head_col(0), head_col(0), qblk(0)]
    out_shape = [jax.ShapeDtypeStruct((s, width), BF16)] * 4
    scratch = [pltpu.VMEM((s, HEAD_DIM), F32), pltpu.VMEM((s, HEAD_DIM), F32)]
    if mode == "fox":
        in_specs += [pl.BlockSpec((None, bq, 1), lambda h, i: (h, i, 0)),
                     pl.BlockSpec((None, nq, 1, bk), lambda h, i: (h, 0, 0, 0))]
        out_specs += [pl.BlockSpec((None, nq, 1, bk), lambda h, i: (h, 0, 0, 0)),
                      pl.BlockSpec((None, bq, 1), lambda h, i: (h, i, 0))]
        out_shape += [jax.ShapeDtypeStruct((nh, nq, 1, bk), F32), jax.ShapeDtypeStruct((nh, s, 1), F32)]
        scratch += [pltpu.VMEM((nq, 1, bk), F32)]
    else:
        in_specs += [pl.BlockSpec((None, 1, rel_pad), lambda h, i: (h, 0, 0)),
                     pl.BlockSpec((BAND_TILES, TOEP, rel_pad), lambda h, i: (0, 0, 0))]
        out_specs += [pl.BlockSpec((None, 1, rel_pad), lambda h, i: (h, 0, 0))]
        out_shape += [jax.ShapeDtypeStruct((nh, 1, rel_pad), F32)]
        scratch += [pltpu.VMEM((BAND_TILES, CQ, CQ), F32), pltpu.VMEM((BAND_TILES, CQ, CQ), F32)]
    outs, carried = _call(
        body, name=name, grid=(nh, nq), in_specs=in_specs, out_specs=out_specs, out_shape=out_shape, scratch=scratch,
        sem=("parallel", "arbitrary"), args=(qkv, qkv, qkv, gate, o, dmixed, lse, *aux), ride=ride)
    return outs if ride is None else (*outs, carried)


SBK = 256
SBQ = 2 * SBK
SB_TILES_PER_TRIP = 2


def _suffix_excl(x, tri):
    r = x.shape[0]
    both = jnp.dot(jnp.concatenate(_split2(x), axis=0), tri, preferred_element_type=F32)
    return both[:r] + both[r:]


def _sb_logits(qh, kt, diag):
    z = lax.dot_general(qh, kt, NT, preferred_element_type=F32) * SCALE
    lom = jnp.minimum(-z, 0.0) - jnp.log(1.0 + jnp.exp(-jnp.abs(z)))
    if diag is not None:
        lom = jnp.where(diag, lom, 0.0)
    return z, lom


def _sb_fwd_tile(qh, kt, vt, tri, diag, run, acc):
    z, lom = _sb_logits(qh, kt, diag)
    a = jnp.exp(lom + z + (_suffix_excl(lom, tri) + run))
    if diag is not None:
        a = jnp.where(diag, a, 0.0)
    acc = acc + jnp.dot(a.astype(BF16), vt, preferred_element_type=F32)
    return run + jnp.sum(lom, axis=-1, keepdims=True), acc


def _sb_fwd(qkv, gate, *, width, name):
    s = qkv.shape[0]
    nh = width // HEAD_DIM
    nq = s // SBQ
    ratio = SBQ // SBK

    def body(q_ref, k_ref, v_ref, g_ref, mixed_ref, o_ref, ltot_ref):
        qi = pl.program_id(1)
        r_i = lax.broadcasted_iota(jnp.int32, (SBK, SBK), 0)
        c_i = lax.broadcasted_iota(jnp.int32, (SBK, SBK), 1)
        tri = (r_i > c_i).astype(BF16)
        diag = c_i < r_i
        q_a = q_ref[0:SBK, :]
        q_b = q_ref[SBK:SBQ, :]

        def kv(tile):
            k0 = pl.multiple_of(tile * SBK, SBK)
            return k_ref[pl.ds(k0, SBK), :], v_ref[pl.ds(k0, SBK), :]

        zero = (jnp.zeros((SBK, 1), F32), jnp.zeros((SBK, HEAD_DIM), F32))
        kt, vt = kv(2 * qi + 1)
        run_b, acc_b = _sb_fwd_tile(q_b, kt, vt, tri, diag, *zero)
        kt, vt = kv(2 * qi)
        run_b, acc_b = _sb_fwd_tile(q_b, kt, vt, tri, None, run_b, acc_b)
        run_a, acc_a = _sb_fwd_tile(q_a, kt, vt, tri, diag, *zero)

        def step(t, carry):
            run_a, acc_a, run_b, acc_b = carry
            for u in range(SB_TILES_PER_TRIP):
                kt, vt = kv(2 * qi - 1 - (SB_TILES_PER_TRIP * t + u))
                run_a, acc_a = _sb_fwd_tile(q_a, kt, vt, tri, None, run_a, acc_a)
                run_b, acc_b = _sb_fwd_tile(q_b, kt, vt, tri, None, run_b, acc_b)
            return run_a, acc_a, run_b, acc_b

        run_a, acc_a, run_b, acc_b = lax.fori_loop(0, (2 // SB_TILES_PER_TRIP) * qi, step, (run_a, acc_a, run_b, acc_b))
        ltot = jnp.concatenate([run_a, run_b], axis=0)
        o = jnp.concatenate([acc_a, acc_b], axis=0)
        g = g_ref[...]
        o_ref[...] = o
        mixed_ref[...] = (o * (g * _sigmoid(g))).astype(BF16)
        ltot_ref[...] = ltot

    head_col = lambda off: pl.BlockSpec((s, HEAD_DIM), lambda h, i: (0, off + h))
    qblk = pl.BlockSpec((SBQ, HEAD_DIM), lambda h, i: (i, h))
    stat = pl.BlockSpec((None, SBQ, 1), lambda h, i: (h, i, 0))
    return pl.pallas_call(
        body, name=name, grid=(nh, nq), in_specs=[qblk, head_col(nh), head_col(2 * nh), qblk],
        out_specs=[qblk, qblk, stat],
        out_shape=[jax.ShapeDtypeStruct((s, width), BF16), jax.ShapeDtypeStruct((s, width), F32),
                   jax.ShapeDtypeStruct((nh, s, 1), F32)],
        compiler_params=_params(("parallel", "arbitrary")),
    )(qkv, qkv, qkv, gate)


def _sb_bwd(qkv, gate, o, ltot, dmixed, *, width, name):
    s = qkv.shape[0]
    nh = width // HEAD_DIM
    nq = s // SBQ
    ratio = SBQ // SBK

    def body(q_ref, k_ref, v_ref, g_ref, o_ref, lt_ref, dm_ref, dq_ref, dk_ref, dv_ref, dg_ref, dk_scr, dv_scr):
        qi = pl.program_id(1)
        q0 = qi * SBQ

        @pl.when(qi == 0)
        def _():
            dk_scr[...] = jnp.zeros_like(dk_scr)
            dv_scr[...] = jnp.zeros_like(dv_scr)

        g = g_ref[...]
        of = o_ref[...]
        dm = dm_ref[...]
        sig = _sigmoid(g)
        do = dm * (g * sig)
        dg_ref[...] = (dm * of * (sig * (1.0 + g * (1.0 - sig)))).astype(BF16)
        do_b = do.astype(BF16)
        q = q_ref[...]
        r_i = lax.broadcasted_iota(jnp.int32, (SBK, SBK), 0)
        c_i = lax.broadcasted_iota(jnp.int32, (SBK, SBK), 1)
        tri = (r_i > c_i).astype(BF16)
        tri_pre = (r_i < c_i).astype(BF16)
        diag = c_i < r_i
        q_a, q_b = q[0:SBK], q[SBK:SBQ]
        do_a, do_b2 = do_b[0:SBK], do_b[SBK:SBQ]
        lt = lt_ref[...]

        def tile(qh, doh, kt, vt, dg_, carry):
            rem, gpre, dq = carry
            z, lom = _sb_logits(qh, kt, dg_)
            rem = rem - jnp.sum(lom, axis=-1, keepdims=True)
            a = jnp.exp(lom + z + (_suffix_excl(lom, tri) + rem))
            if dg_ is not None:
                a = jnp.where(dg_, a, 0.0)
            gg = lax.dot_general(doh, vt, NT, preferred_element_type=F32) * a
            pre = _suffix_excl(gg, tri_pre) + gpre
            dz = gg * jnp.exp(lom) - pre * jnp.exp(lom + z)
            if dg_ is not None:
                dz = jnp.where(dg_, dz, 0.0)
            dz_b = (dz * SCALE).astype(BF16)
            dq = dq + jnp.dot(dz_b, kt, preferred_element_type=F32)
            return (rem, gpre + jnp.sum(gg, axis=-1, keepdims=True), dq), dz_b, a.astype(BF16)

        def kv(t):
            k0 = pl.multiple_of(t * SBK, SBK)
            return k0, k_ref[pl.ds(k0, SBK), :], v_ref[pl.ds(k0, SBK), :]

        def both(t, ca, cb, dg_a):
            k0, kt, vt = kv(t)
            ca, dz_a, a_a = tile(q_a, do_a, kt, vt, dg_a, ca)
            cb, dz_b_, a_b = tile(q_b, do_b2, kt, vt, None, cb)
            dk_scr[pl.ds(k0, SBK), :] += lax.dot_general(jnp.concatenate([dz_a, dz_b_], axis=0), q, TN,
                                                         preferred_element_type=F32)
            dv_scr[pl.ds(k0, SBK), :] += lax.dot_general(jnp.concatenate([a_a, a_b], axis=0), do_b, TN,
                                                         preferred_element_type=F32)
            return ca, cb

        def step(t, carry):
            ca, cb = carry
            for u in range(SB_TILES_PER_TRIP):
                ca, cb = both(SB_TILES_PER_TRIP * t + u, ca, cb, None)
            return ca, cb

        zero = (jnp.zeros((SBK, 1), F32), jnp.zeros((SBK, HEAD_DIM), F32))
        ca, cb = lax.fori_loop(0, (2 // SB_TILES_PER_TRIP) * qi, step, ((lt[0:SBK], *zero), (lt[SBK:SBQ], *zero)))
        ca, cb = both(2 * qi, ca, cb, diag)
        k0, kt, vt = kv(2 * qi + 1)
        cb, dz_b_, a_b = tile(q_b, do_b2, kt, vt, diag, cb)
        dk_scr[pl.ds(k0, SBK), :] += lax.dot_general(dz_b_, q_b, TN, preferred_element_type=F32)
        dv_scr[pl.ds(k0, SBK), :] += lax.dot_general(a_b, do_b2, TN, preferred_element_type=F32)
        dq_ref[...] = jnp.concatenate([ca[2], cb[2]], axis=0).astype(BF16)

        @pl.when(qi == nq - 1)
        def _():
            dk_ref[...] = dk_scr[...].astype(BF16)
            dv_ref[...] = dv_scr[...].astype(BF16)

    head_col = lambda off: pl.BlockSpec((s, HEAD_DIM), lambda h, i: (0, off + h))
    qblk = pl.BlockSpec((SBQ, HEAD_DIM), lambda h, i: (i, h))
    return pl.pallas_call(
        body, name=name, grid=(nh, nq),
        in_specs=[qblk, head_col(nh), head_col(2 * nh), qblk, qblk,
                  pl.BlockSpec((None, SBQ, 1), lambda h, i: (h, i, 0)), qblk],
        out_specs=[qblk, head_col(0), head_col(0), qblk],
        out_shape=[jax.ShapeDtypeStruct((s, width), BF16)] * 4,
        scratch_shapes=[pltpu.VMEM((s, HEAD_DIM), F32), pltpu.VMEM((s, HEAD_DIM), F32)],
        compiler_params=_params(("parallel", "arbitrary")),
    )(qkv, qkv, qkv, gate, o, ltot, dmixed)


HBM = pl.BlockSpec(memory_space=pl.ANY)
MESH = pl.DeviceIdType.MESH


def _all_gather(shard, *, name):
    r, c_ = shard.shape

    def body(x_ref, out_ref, send_sems, recv_sems, local_sem):
        x, y, c = lax.axis_index("x"), lax.axis_index("y"), lax.axis_index("c")
        me, sibling = (x, y, c), (x, y, 1 - c)
        chips = [(1 - x, y), (x, 1 - y), (1 - x, 1 - y)]

        def slot(px, py, pc):
            return out_ref.at[4 * px + 2 * py + pc]

        def copy(k, block, to, src=None):
            return pltpu.make_async_remote_copy(
                src_ref=slot(*block) if src is None else src, dst_ref=slot(*block),
                send_sem=send_sems.at[k], recv_sem=recv_sems.at[k], device_id=to, device_id_type=MESH)

        mine = pltpu.make_async_copy(x_ref, slot(*me), local_sem)
        mine.start()
        first = [copy(0, me, sibling, src=x_ref)]
        first += [copy(1 + j, me, (*chip, c), src=x_ref) for j, chip in enumerate(chips)]
        for cp in first:
            cp.start()
        passed = [copy(4 + j, (*chip, c), sibling) for j, chip in enumerate(chips)]
        for j, chip in enumerate(chips):
            copy(1 + j, (*chip, c), me).wait_recv()
            passed[j].start()
        copy(0, sibling, me).wait_recv()
        for j, chip in enumerate(chips):
            copy(4 + j, (*chip, 1 - c), me).wait_recv()
        for cp in first + passed:
            cp.wait_send()
        mine.wait()

    return pl.pallas_call(
        body, name=name, in_specs=[HBM], out_specs=HBM,
        out_shape=jax.ShapeDtypeStruct((N_DEV, r, c_), shard.dtype),
        scratch_shapes=[pltpu.SemaphoreType.DMA((7,)), pltpu.SemaphoreType.DMA((7,)), pltpu.SemaphoreType.DMA],
    )(shard)


class _Ride:
    def __init__(self, src, *, gather):
        self.src, self.gather = src, gather
        self.out_shape = jax.ShapeDtypeStruct((N_DEV, *(src.shape if gather else src.shape[1:])), src.dtype)
        self.scratch = [pltpu.SemaphoreType.DMA((7,)), pltpu.SemaphoreType.DMA((7,)), pltpu.SemaphoreType.DMA]

    def _copies(self, src_ref, out_ref, send_sems, recv_sems, local_sem):
        x, y, c = lax.axis_index("x"), lax.axis_index("y"), lax.axis_index("c")
        me = 4 * x + 2 * y + c
        pick = (lambda j: src_ref) if self.gather else (lambda j: src_ref.at[j])
        mine = pltpu.make_async_copy(pick(me), out_ref.at[me], local_sem)
        copies = []
        for k in range(1, N_DEV):
            px, py, pc = x ^ ((k >> 2) & 1), y ^ ((k >> 1) & 1), c ^ (k & 1)
            copies.append(pltpu.make_async_remote_copy(
                src_ref=pick(4 * px + 2 * py + pc), dst_ref=out_ref.at[me],
                send_sem=send_sems.at[k - 1], recv_sem=recv_sems.at[k - 1],
                device_id=(px, py, pc), device_id_type=MESH))
        return mine, copies

    def start(self, *refs):
        mine, copies = self._copies(*refs)
        mine.start()
        for cp in copies:
            cp.start()

    def finish(self, *refs):
        mine, copies = self._copies(*refs)
        for cp in copies:
            cp.wait_recv()
        for cp in copies:
            cp.wait_send()
        mine.wait()


def _exchange(src, *, gather, name):
    ride = _Ride(src, gather=gather)

    def body(*refs):
        ride.start(*refs)
        ride.finish(*refs)

    return pl.pallas_call(body, name=name, in_specs=[HBM], out_specs=HBM, out_shape=ride.out_shape,
                          scratch_shapes=ride.scratch)(src)


def _call(body, *, name, grid, in_specs, out_specs, out_shape, scratch, sem, args, ride=None):
    if ride is None:
        outs = pl.pallas_call(body, name=name, grid=grid, in_specs=in_specs, out_specs=out_specs, out_shape=out_shape,
                              scratch_shapes=scratch, compiler_params=_params(sem))(*args)
        return list(outs), None
    n_in, n_out = len(in_specs), len(out_specs)

    def carrying(*refs):
        ins, src_ref = refs[:n_in], refs[n_in]
        outs, dst_ref = refs[n_in + 1:n_in + 1 + n_out], refs[n_in + 1 + n_out]
        rest = refs[n_in + 2 + n_out:]
        own, sems = rest[:len(rest) - 3], rest[len(rest) - 3:]
        ids = [pl.program_id(a) for a in range(len(grid))]
        first = functools.reduce(jnp.logical_and, [i == 0 for i in ids])
        last = functools.reduce(jnp.logical_and, [i == n - 1 for i, n in zip(ids, grid)])

        @pl.when(first)
        def _():
            ride.start(src_ref, dst_ref, *sems)

        body(*ins, *outs, *own)

        @pl.when(last)
        def _():
            ride.finish(src_ref, dst_ref, *sems)

    outs = pl.pallas_call(
        carrying, name=name, grid=grid, in_specs=[*in_specs, HBM], out_specs=[*out_specs, HBM],
        out_shape=[*out_shape, ride.out_shape], scratch_shapes=[*scratch, *ride.scratch],
        compiler_params=_params(("arbitrary",) * len(grid)))(*args, ride.src)
    return list(outs[:-1]), outs[-1]


def _adamw(parts, w, m, v, *, name, rows):
    r, c_ = w.shape
    rows = min(rows, r)
    assert r % rows == 0
    c1 = 1.0 / (1.0 - ADAM_B1 ** ADAM_STEP)
    c2 = 1.0 / (1.0 - ADAM_B2 ** ADAM_STEP)

    def body(p_ref, w_ref, m_ref, v_ref, g_ref, d_ref, nm_ref, nv_ref):
        g = p_ref[0].astype(F32)
        for i in range(1, N_DEV):
            g = g + p_ref[i].astype(F32)
        nm = ADAM_B1 * m_ref[...] + (1.0 - ADAM_B1) * g
        nv = ADAM_B2 * v_ref[...] + (1.0 - ADAM_B2) * (g * g)
        g_ref[...] = g
        nm_ref[...] = nm
        nv_ref[...] = nv
        d_ref[...] = -ADAM_LR * ((nm * c1) / (jnp.sqrt(nv * c2) + ADAM_EPS) + ADAM_WD * w_ref[...])

    blk = pl.BlockSpec((rows, c_), lambda i: (i, 0))
    return pl.pallas_call(
        body, name=name, grid=(r // rows,),
        in_specs=[pl.BlockSpec((N_DEV, rows, c_), lambda i: (0, i, 0)), blk, blk, blk],
        out_specs=[blk] * 4, out_shape=[jax.ShapeDtypeStruct((r, c_), F32)] * 4,
        compiler_params=_params(("parallel",)),
    )(parts, w, m, v)


def _pad_cols(a, n):
    return jnp.pad(a, ((0, 0), (0, n - a.shape[1])))


def _fox_dc(shares, nh, s):
    key_side, query_side = shares
    return _pad_cols(jnp.transpose(key_side.reshape(nh, s) + query_side.reshape(nh, s)), 128)


def _local_step(x, target, norm_pre, norm_post, w_in_e, w_f, b_f, rel_bias, sh_out_e, sh_in_o, sh_out_o):
    s, d = x.shape
    wa = d // 2
    nha = wa // HEAD_DIM
    nq = s // CQ
    gather = lambda shard: _Ride(shard, gather=True)
    scatter = lambda parts: _Ride(parts, gather=False)

    h0 = _rms_fwd(x, norm_pre[0:1], name="rms_pre0")
    proj = lambda w, off, n, dt, nm, **kw: _matmul(h0, w, mode="nn", m=s, n=n, k=d, out_dtype=dt, name=nm,
                                                   b_off=(0, off), **kw)
    qkv_a, w_out_e = proj(w_in_e, 0, 3 * wa, BF16, "proj_qkv_a", ride=gather(sh_out_e))
    w_out_e = w_out_e.reshape(d, d)
    g_a = proj(w_in_e, 3 * wa, wa, F32, "proj_gate_a")
    qkv_b = proj(w_in_e, 4 * wa, 3 * wa, BF16, "proj_qkv_b")
    g_b = proj(w_in_e, 7 * wa, wa, F32, "proj_gate_b")
    af = _matmul(h0, w_f, mode="nn", m=s, n=128, k=d, out_dtype=F32, name="proj_forget")
    bias128 = _pad_cols(b_f, 128)
    cum = _gate_fwd(af, bias128, name="forget_cumsum")
    c_t = jnp.transpose(cum[:, :nha])
    c_col = c_t.reshape(nha, s, 1)
    c_row = c_t.reshape(nha, nq, 1, CQ)
    mixed_a, o_a, lse_a, w_in_o = _softmax_fwd(qkv_a, g_a, (c_col, c_row), mode="fox", width=wa, name="fox_fwd",
                                               ride=gather(sh_in_o))
    w_in_o = jnp.transpose(w_in_o, (1, 0, 2)).reshape(d, 4 * d)
    rel_aux = (_pad_cols(rel_bias, REL_PAD).reshape(nha, 1, REL_PAD), _rel_onehot())
    mixed_b, o_b, lse_b, w_out_o = _softmax_fwd(qkv_b, g_b, rel_aux, mode="chunk", width=wa, name="chunk_fwd",
                                                ride=gather(sh_out_o))
    w_out_o = w_out_o.reshape(d, d)
    mixed0 = jnp.concatenate([mixed_a, mixed_b], axis=1)
    y0 = _matmul(mixed0, w_out_e, mode="nn", m=s, n=d, k=d, out_dtype=F32, name="out_proj0")
    x1 = _post_fwd(x, y0, norm_post[0:1], name="post0")

    h1 = _rms_fwd(x1, norm_pre[1:2], name="rms_pre1")
    qkv_c = _matmul(h1, w_in_o, mode="nn", m=s, n=3 * d, k=d, out_dtype=BF16, name="proj_qkv_c")
    g_c = _matmul(h1, w_in_o, mode="nn", m=s, n=d, k=d, out_dtype=F32, name="proj_gate_c", b_off=(0, 3 * d))
    mixed1, o_c, ltot_c = _sb_fwd(qkv_c, g_c, width=d, name="sb_fwd")
    y1 = _matmul(mixed1, w_out_o, mode="nn", m=s, n=d, k=d, out_dtype=F32, name="out_proj1")
    x2 = _post_fwd(x1, y1, norm_post[1:2], name="post1")

    loss, dx2 = _loss_head(x2, target, name="loss_head")

    dy1, dgpost1 = _post_bwd(dx2, y1, norm_post[1:2], name="post_bwd1")
    dmixed1 = _matmul(dy1, w_out_o, mode="nt", m=s, n=d, k=d, out_dtype=F32, name="dmixed1")
    dw_out_o = _matmul(mixed1, dy1, mode="tn", m=d, n=d, k=s, out_dtype=BF16, name="dw_out1")
    dq_c, dk_c, dv_c, dg_c = _sb_bwd(qkv_c, g_c, o_c, ltot_c, dmixed1, width=d, name="sb_bwd")
    dproj1 = jnp.concatenate([dq_c, dk_c, dv_c, dg_c], axis=1)
    dh1 = _matmul(dproj1, w_in_o, mode="nt", m=s, n=d, k=4 * d, out_dtype=F32, name="dh1")
    dw_in_o = _matmul(h1, dproj1, mode="tn", m=d, n=4 * d, k=s, out_dtype=BF16, name="dw_in1")
    dx1, dgpre1 = _pre_bwd((dh1,), x1, norm_pre[1:2], dx2, name="pre_bwd1")

    dy0, dgpost0 = _post_bwd(dx1, y0, norm_post[0:1], name="post_bwd0")
    dmixed0 = _matmul(dy0, w_out_e, mode="nt", m=s, n=d, k=d, out_dtype=F32, name="dmixed0")
    dw_out_e = _matmul(mixed0, dy0, mode="tn", m=d, n=d, k=s, out_dtype=BF16, name="dw_out0")
    parts_in_o = jnp.transpose(dw_in_o.reshape(d, N_DEV, 4 * d // N_DEV), (1, 0, 2))
    dq_a, dk_a, dv_a, dg_a, *dc_shares, got_in_o = _softmax_bwd(
        qkv_a, g_a, o_a, dmixed0, 0, lse_a, (c_col, c_row), mode="fox", width=wa, name="fox_bwd", ride=scatter(parts_in_o))
    dq_b, dk_b, dv_b, dg_b, drel, got_out_o = _softmax_bwd(
        qkv_b, g_b, o_b, dmixed0, wa, lse_b, rel_aux, mode="chunk", width=wa, name="chunk_bwd",
        ride=scatter(dw_out_o.reshape(N_DEV, d // N_DEV, d)))
    dc = _fox_dc(dc_shares, nha, s)
    daf, dbf = _gate_bwd(dc, af, bias128, name="forget_bwd")
    dproj0 = jnp.concatenate([dq_a, dk_a, dv_a, dg_a, dq_b, dk_b, dv_b, dg_b], axis=1)
    dh0, got_out_e = _matmul(dproj0, w_in_e, mode="nt", m=s, n=d, k=8 * wa, out_dtype=F32, name="dh0_main",
                             ride=scatter(dw_out_e.reshape(N_DEV, d // N_DEV, d)))
    dh0f = _matmul(daf, w_f, mode="nt", m=s, n=d, k=128, out_dtype=F32, name="dh0_forget")
    dw_in_e = _matmul(h0, dproj0, mode="tn", m=d, n=8 * wa, k=s, out_dtype=BF16, name="dw_in0")
    dw_f = _matmul(h0, daf, mode="tn", m=d, n=128, k=s, out_dtype=BF16, name="dw_forget")
    dx0, dgpre0 = _pre_bwd((dh0, dh0f), x, norm_pre[0:1], dx1, name="pre_bwd0")

    small = dict(
        norm_pre=jnp.concatenate([dgpre0, dgpre1], axis=0),
        norm_post=jnp.concatenate([dgpost0, dgpost1], axis=0),
        b_f=dbf[:, :nha], rel_bias=drel[:, 0, :N_REL])
    got = dict(w_out_even=got_out_e, w_in_odd=got_in_o, w_out_odd=got_out_o)
    return loss, dx0, got, jnp.concatenate([dw_in_e, dw_f[:, :nha]], axis=1), small


def _pack_small(norm_pre, norm_post, b_f, rel_bias):
    flat = jnp.concatenate([norm_pre.reshape(-1), norm_post.reshape(-1), b_f.reshape(-1), rel_bias.reshape(-1)])
    n = flat.shape[0]
    rows = -(-n // 128)
    rows = -(-rows // 8) * 8
    return jnp.pad(flat, (0, rows * 128 - n)).reshape(rows, 128)


def _unpack_small(slab, shapes):
    flat = slab.reshape(-1)
    out, off = [], 0
    for shp in shapes:
        n = int(np.prod(shp))
        out.append(flat[off:off + n].reshape(shp))
        off += n
    return out


def kernel(x, norm_pre, norm_post, w_in_even, b_f_even, rel_bias_even, w_out_even, w_in_odd, w_out_odd, loss_target, m_norm_pre, m_norm_post, m_w_in_even, m_b_f_even, m_rel_bias_even, m_w_out_even, m_w_in_odd, m_w_out_odd, v_norm_pre, v_norm_post, v_w_in_even, v_b_f_even, v_rel_bias_even, v_w_out_even, v_w_in_odd, v_w_out_odd):
    _, s, d = x.shape
    wa = d // 2
    nha = wa // HEAD_DIM
    in_e = w_in_even.shape[2] * N_DEV
    sh_e = w_in_even.shape[2]

    w_in_e_all = jnp.transpose(_all_gather(w_in_even[0].astype(BF16), name="ag_w_in_even"), (1, 0, 2)).reshape(d, in_e)
    w_main = w_in_e_all[:, :8 * wa]
    w_f = _pad_cols(w_in_e_all[:, 8 * wa:], 128)

    loss, dx, got, dw_e, small = _local_step(
        x[0], loss_target[0], norm_pre, norm_post, w_main, w_f, b_f_even, rel_bias_even[0],
        w_out_even[0].astype(BF16), w_in_odd[0].astype(BF16), w_out_odd[0].astype(BF16))

    parts_in_e = jnp.transpose(dw_e.reshape(d, N_DEV, sh_e), (1, 0, 2))
    got["w_in_even"] = _exchange(parts_in_e, gather=False, name="rs_w_in_even")
    upd = {}
    upd["w_in_even"] = _adamw(got["w_in_even"], w_in_even[0], m_w_in_even[0], v_w_in_even[0],
                              name="adamw_w_in_even", rows=128)
    upd["w_out_even"] = _adamw(got["w_out_even"], w_out_even[0], m_w_out_even[0], v_w_out_even[0],
                               name="adamw_w_out_even", rows=64)
    upd["w_in_odd"] = _adamw(got["w_in_odd"], w_in_odd[0], m_w_in_odd[0], v_w_in_odd[0],
                             name="adamw_w_in_odd", rows=128)
    upd["w_out_odd"] = _adamw(got["w_out_odd"], w_out_odd[0], m_w_out_odd[0], v_w_out_odd[0],
                              name="adamw_w_out_odd", rows=64)

    shapes = [norm_pre.shape, norm_post.shape, b_f_even.shape, rel_bias_even.shape]
    g_slab = _pack_small(small["norm_pre"], small["norm_post"], small["b_f"], small["rel_bias"])
    parts_small = _exchange(g_slab, gather=True, name="ar_small")
    sm = _adamw(parts_small, _pack_small(norm_pre, norm_post, b_f_even, rel_bias_even),
                _pack_small(m_norm_pre, m_norm_post, m_b_f_even, m_rel_bias_even),
                _pack_small(v_norm_pre, v_norm_post, v_b_f_even, v_rel_bias_even), name="adamw_small", rows=g_slab.shape[0])
    sm = [_unpack_small(a, shapes) for a in sm]

    total = lax.psum(loss[0, 0], ("x", "y", "c"))

    def leaves(kind):
        return (sm[kind][0], sm[kind][1], upd["w_in_even"][kind][None], sm[kind][2], sm[kind][3],
                upd["w_out_even"][kind][None], upd["w_in_odd"][kind][None], upd["w_out_odd"][kind][None])

    return (total, dx[None], *leaves(0), *leaves(1), *leaves(2), *leaves(3))
```

```python
import functools

import numpy as np
import jax
import jax.numpy as jnp
from jax import lax
from jax.experimental import pallas as pl
from jax.experimental.pallas import tpu as pltpu

F32 = jnp.float32
BF16 = jnp.bfloat16

HEAD_DIM = 128
CHUNK = 64
LEFT_CHUNKS = 8
REL_CLIP = 128
N_REL = 2 * REL_CLIP + 1
RMS_EPS = 1e-6
SCALE = HEAD_DIM ** -0.5

ADAM_LR = 0.001
ADAM_B1 = 0.9
ADAM_B2 = 0.999
ADAM_EPS = 1e-08
ADAM_WD = 0.01
ADAM_STEP = 10

N_DEV = 8
V7X_VMEM_LIMIT_BYTES = 56 * 1024 * 1024
NEG = -1e30

NT = (((1,), (1,)), ((), ()))
TN = (((0,), (0,)), ((), ()))
NN = (((1,), (0,)), ((), ()))

CQ = 256
BAND_TILES = 3
TOEP = 2 * CQ
assert (BAND_TILES - 1) * CQ == LEFT_CHUNKS * CHUNK


def _params(sem):
    return pltpu.CompilerParams(dimension_semantics=sem, vmem_limit_bytes=V7X_VMEM_LIMIT_BYTES)


def _split3(x):
    hi = x.astype(BF16)
    r1 = x - hi.astype(F32)
    mid = r1.astype(BF16)
    lo = (r1 - mid.astype(F32)).astype(BF16)
    return hi, mid, lo


def _split2(x):
    hi = x.astype(BF16)
    lo = (x - hi.astype(F32)).astype(BF16)
    return hi, lo


def _sigmoid(g):
    return 1.0 / (1.0 + jnp.exp(-g))


def _tile(n, cap, *offsets):
    if n <= 128:
        return n
    t = (min(cap, n) // 128) * 128
    while n % t or any(o % t for o in offsets):
        t -= 128
    return t


def _matmul(a, b, *, mode, m, n, k, out_dtype, name, a_off=(0, 0), b_off=(0, 0), tm=1024, tn=1024, tk=1024,
            ride=None):
    a_m, a_k = (a_off if mode in ("nn", "nt") else a_off[::-1])
    b_k, b_n = (b_off if mode in ("nn", "tn") else b_off[::-1])
    tm, tn, tk = _tile(m, tm, a_m), _tile(n, tn, b_n), _tile(k, tk, a_k, b_k)
    nk = k // tk
    if mode in ("nn", "nt"):
        ao = (a_off[0] // tm, a_off[1] // tk)
        a_spec = pl.BlockSpec((tm, tk), lambda i, j, l: (i + ao[0], l + ao[1]))
    else:
        ao = (a_off[0] // tk, a_off[1] // tm)
        a_spec = pl.BlockSpec((tk, tm), lambda i, j, l: (l + ao[0], i + ao[1]))
    if mode in ("nn", "tn"):
        bo = (b_off[0] // tk, b_off[1] // tn)
        b_spec = pl.BlockSpec((tk, tn), lambda i, j, l: (l + bo[0], j + bo[1]))
    else:
        bo = (b_off[0] // tn, b_off[1] // tk)
        b_spec = pl.BlockSpec((tn, tk), lambda i, j, l: (j + bo[0], l + bo[1]))
    dn = {"nn": NN, "nt": NT, "tn": TN}[mode]

    def body(a_ref, b_ref, o_ref, acc_ref):
        @pl.when(pl.program_id(2) == 0)
        def _():
            acc_ref[...] = jnp.zeros_like(acc_ref)

        acc_ref[...] += lax.dot_general(a_ref[...], b_ref[...], dn, preferred_element_type=F32)

        @pl.when(pl.program_id(2) == nk - 1)
        def _():
            o_ref[...] = acc_ref[...].astype(out_dtype)

    (out,), carried = _call(
        body, name=name, grid=(m // tm, n // tn, nk), in_specs=[a_spec, b_spec],
        out_specs=[pl.BlockSpec((tm, tn), lambda i, j, l: (i, j))], out_shape=[jax.ShapeDtypeStruct((m, n), out_dtype)],
        scratch=[pltpu.VMEM((tm, tn), F32)], sem=("parallel", "parallel", "arbitrary"), args=(a, b), ride=ride)
    return out if ride is None else (out, carried)


ROWS = 128


def _rms_fwd(x, gain, *, name):
    s, d = x.shape

    def body(x_ref, g_ref, h_ref):
        xf = x_ref[...]
        r = lax.rsqrt(jnp.mean(xf * xf, axis=-1, keepdims=True) + RMS_EPS)
        h_ref[...] = ((xf * r) * g_ref[...]).astype(BF16)

    return pl.pallas_call(
        body, name=name, grid=(s // ROWS,),
        in_specs=[pl.BlockSpec((ROWS, d), lambda i: (i, 0)), pl.BlockSpec((1, d), lambda i: (0, 0))],
        out_specs=pl.BlockSpec((ROWS, d), lambda i: (i, 0)),
        out_shape=jax.ShapeDtypeStruct((s, d), BF16),
        compiler_params=_params(("parallel",)),
    )(x, gain)


def _post_fwd(x, y, gain, *, name):
    s, d = x.shape

    def body(x_ref, y_ref, g_ref, o_ref):
        yf = y_ref[...]
        r = lax.rsqrt(jnp.mean(yf * yf, axis=-1, keepdims=True) + RMS_EPS)
        o_ref[...] = x_ref[...] + (yf * r) * g_ref[...]

    return pl.pallas_call(
        body, name=name, grid=(s // ROWS,),
        in_specs=[pl.BlockSpec((ROWS, d), lambda i: (i, 0)), pl.BlockSpec((ROWS, d), lambda i: (i, 0)),
                  pl.BlockSpec((1, d), lambda i: (0, 0))],
        out_specs=pl.BlockSpec((ROWS, d), lambda i: (i, 0)),
        out_shape=jax.ShapeDtypeStruct((s, d), F32),
        compiler_params=_params(("parallel",)),
    )(x, y, gain)


def _loss_head(xo, target, *, name):
    s, d = xo.shape
    inv_d = 1.0 / d

    def body(x_ref, t_ref, loss_ref, dx_ref):
        @pl.when(pl.program_id(0) == 0)
        def _():
            loss_ref[...] = jnp.zeros_like(loss_ref)

        e = x_ref[...] - t_ref[...]
        dx_ref[...] = e * inv_d
        loss_ref[...] += 0.5 * jnp.sum(jnp.mean(e * e, axis=-1, keepdims=True), axis=0, keepdims=True)

    return pl.pallas_call(
        body, name=name, grid=(s // ROWS,),
        in_specs=[pl.BlockSpec((ROWS, d), lambda i: (i, 0)), pl.BlockSpec((ROWS, d), lambda i: (i, 0))],
        out_specs=[pl.BlockSpec((1, 1), lambda i: (0, 0)), pl.BlockSpec((ROWS, d), lambda i: (i, 0))],
        out_shape=[jax.ShapeDtypeStruct((1, 1), F32), jax.ShapeDtypeStruct((s, d), F32)],
        compiler_params=_params(("arbitrary",)),
    )(xo, target)


def _post_bwd(dxo, y, gain, *, name):
    s, d = y.shape

    def body(dx_ref, y_ref, g_ref, dy_ref, dg_ref):
        @pl.when(pl.program_id(0) == 0)
        def _():
            dg_ref[...] = jnp.zeros_like(dg_ref)

        yf = y_ref[...]
        dxo_ = dx_ref[...]
        r = lax.rsqrt(jnp.mean(yf * yf, axis=-1, keepdims=True) + RMS_EPS)
        nrm = yf * r
        dg_ref[...] += jnp.sum(dxo_ * nrm, axis=0, keepdims=True)
        dn = dxo_ * g_ref[...]
        dy_ref[...] = (r * (dn - nrm * jnp.mean(dn * nrm, axis=-1, keepdims=True))).astype(BF16)

    return pl.pallas_call(
        body, name=name, grid=(s // ROWS,),
        in_specs=[pl.BlockSpec((ROWS, d), lambda i: (i, 0)), pl.BlockSpec((ROWS, d), lambda i: (i, 0)),
                  pl.BlockSpec((1, d), lambda i: (0, 0))],
        out_specs=[pl.BlockSpec((ROWS, d), lambda i: (i, 0)), pl.BlockSpec((1, d), lambda i: (0, 0))],
        out_shape=[jax.ShapeDtypeStruct((s, d), BF16), jax.ShapeDtypeStruct((1, d), F32)],
        compiler_params=_params(("arbitrary",)),
    )(dxo, y, gain)


def _pre_bwd(dhs, x, gain, dres, *, name):
    s, d = x.shape
    n_dh = len(dhs)

    def body(*refs):
        dh_refs = refs[:n_dh]
        x_ref, g_ref, dr_ref, dx_ref, dg_ref = refs[n_dh:]

        @pl.when(pl.program_id(0) == 0)
        def _():
            dg_ref[...] = jnp.zeros_like(dg_ref)

        xf = x_ref[...]
        dh_ = dh_refs[0][...]
        for extra in dh_refs[1:]:
            dh_ = dh_ + extra[...]
        r = lax.rsqrt(jnp.mean(xf * xf, axis=-1, keepdims=True) + RMS_EPS)
        nrm = xf * r
        dg_ref[...] += jnp.sum(dh_ * nrm, axis=0, keepdims=True)
        dn = dh_ * g_ref[...]
        dx_ref[...] = dr_ref[...] + r * (dn - nrm * jnp.mean(dn * nrm, axis=-1, keepdims=True))

    return pl.pallas_call(
        body, name=name, grid=(s // ROWS,),
        in_specs=[pl.BlockSpec((ROWS, d), lambda i: (i, 0))] * (n_dh + 1)
        + [pl.BlockSpec((1, d), lambda i: (0, 0)), pl.BlockSpec((ROWS, d), lambda i: (i, 0))],
        out_specs=[pl.BlockSpec((ROWS, d), lambda i: (i, 0)), pl.BlockSpec((1, d), lambda i: (0, 0))],
        out_shape=[jax.ShapeDtypeStruct((s, d), F32), jax.ShapeDtypeStruct((1, d), F32)],
        compiler_params=_params(("arbitrary",)),
    )(*dhs, x, gain, dres)


GB = 256


def _gate_fwd(af, bias, *, name):
    s, w = af.shape

    def body(af_ref, b_ref, c_ref, carry_ref):
        @pl.when(pl.program_id(0) == 0)
        def _():
            carry_ref[...] = jnp.zeros_like(carry_ref)

        z = af_ref[...] + b_ref[...]
        lf = jnp.minimum(z, 0.0) - jnp.log(1.0 + jnp.exp(-jnp.abs(z)))
        r_i = lax.broadcasted_iota(jnp.int32, (GB, GB), 0)
        c_i = lax.broadcasted_iota(jnp.int32, (GB, GB), 1)
        tri = (c_i <= r_i).astype(BF16)
        hi, mid, lo = _split3(lf)
        pre = (jnp.dot(tri, hi, preferred_element_type=F32) + jnp.dot(tri, mid, preferred_element_type=F32)
               + jnp.dot(tri, lo, preferred_element_type=F32))
        c_ref[...] = pre + carry_ref[...]
        carry_ref[...] += jnp.sum(lf, axis=0, keepdims=True)

    return pl.pallas_call(
        body, name=name, grid=(s // GB,),
        in_specs=[pl.BlockSpec((GB, w), lambda i: (i, 0)), pl.BlockSpec((1, w), lambda i: (0, 0))],
        out_specs=pl.BlockSpec((GB, w), lambda i: (i, 0)),
        out_shape=jax.ShapeDtypeStruct((s, w), F32),
        scratch_shapes=[pltpu.VMEM((1, w), F32)],
        compiler_params=_params(("arbitrary",)),
    )(af, bias)


def _gate_bwd(dc, af, bias, *, name):
    s, w = af.shape
    nb = s // GB

    def body(dc_ref, af_ref, b_ref, daf_ref, db_ref, carry_ref):
        @pl.when(pl.program_id(0) == 0)
        def _():
            carry_ref[...] = jnp.zeros_like(carry_ref)
            db_ref[...] = jnp.zeros_like(db_ref)

        dcb = dc_ref[...]
        r_i = lax.broadcasted_iota(jnp.int32, (GB, GB), 0)
        c_i = lax.broadcasted_iota(jnp.int32, (GB, GB), 1)
        tri = (c_i >= r_i).astype(BF16)
        hi, mid, lo = _split3(dcb)
        suf = (jnp.dot(tri, hi, preferred_element_type=F32) + jnp.dot(tri, mid, preferred_element_type=F32)
               + jnp.dot(tri, lo, preferred_element_type=F32)) + carry_ref[...]
        carry_ref[...] += jnp.sum(dcb, axis=0, keepdims=True)
        z = af_ref[...] + b_ref[...]
        daf = suf * _sigmoid(-z)
        daf_ref[...] = daf.astype(BF16)
        db_ref[...] += jnp.sum(daf, axis=0, keepdims=True)

    return pl.pallas_call(
        body, name=name, grid=(nb,),
        in_specs=[pl.BlockSpec((GB, w), lambda i: (nb - 1 - i, 0)), pl.BlockSpec((GB, w), lambda i: (nb - 1 - i, 0)),
                  pl.BlockSpec((1, w), lambda i: (0, 0))],
        out_specs=[pl.BlockSpec((GB, w), lambda i: (nb - 1 - i, 0)), pl.BlockSpec((1, w), lambda i: (0, 0))],
        out_shape=[jax.ShapeDtypeStruct((s, w), BF16), jax.ShapeDtypeStruct((1, w), F32)],
        scratch_shapes=[pltpu.VMEM((1, w), F32)],
        compiler_params=_params(("arbitrary",)),
    )(dc, af, bias)


def _rel_index_rows():
    w = np.arange(TOEP)
    wp = np.where(w < CQ, w, w - TOEP)
    return np.stack([np.clip(LEFT_CHUNKS * CHUNK - CQ * j - wp, -REL_CLIP, REL_CLIP) + REL_CLIP
                     for j in range(BAND_TILES)]).astype(np.int32)


def _skew_rows(xw, sign):
    row = lax.broadcasted_iota(jnp.int32, xw.shape, 0)
    for b in range(CQ.bit_length() - 1):
        amt = (1 << b) if sign > 0 else TOEP - (1 << b)
        xw = jnp.where(((row >> b) & 1) == 1, pltpu.roll(xw, amt, 1), xw)
    return xw


REL_PAD = 384


def _rel_onehot():
    return jnp.asarray(_rel_index_rows()[:, :, None] == np.arange(REL_PAD)[None, None, :], BF16)


def _fill_bias_tiles(rel_ref, oh_ref, bias_scr):
    parts = _split3(jnp.broadcast_to(rel_ref[...], (8, REL_PAD)))
    for j in range(BAND_TILES):
        row = sum(lax.dot_general(p, oh_ref[j], NT, preferred_element_type=F32) for p in parts)[0:1]
        bias_scr[j] = _skew_rows(jnp.broadcast_to(row, (CQ, TOEP)), +1)[:, :CQ]


def _fox_scores(s, cq, cr, diagonal):
    s = s + (cq - cr)
    if not diagonal:
        return s
    bq, bk = s.shape
    return jnp.where(lax.broadcasted_iota(jnp.int32, (bq, bk), 1) <= lax.broadcasted_iota(jnp.int32, (bq, bk), 0), s, NEG)


def _chunk_scores(s, q0, k0, bias):
    bq, bk = s.shape
    qc = (q0 + lax.broadcasted_iota(jnp.int32, (bq, bk), 0)) >> 6
    kc = (k0 + lax.broadcasted_iota(jnp.int32, (bq, bk), 1)) >> 6
    return jnp.where((kc <= qc) & (kc >= qc - LEFT_CHUNKS), s + bias, NEG)


def _softmax_fwd(qkv, gate, aux, *, mode, width, name, ride=None):
    s = qkv.shape[0]
    nh = width // HEAD_DIM
    bq = bk = CQ
    nq = s // bq

    def body(q_ref, k_ref, v_ref, g_ref, *rest):
        if mode == "fox":
            cc_ref, cr_ref, mixed_ref, o_ref, lse_ref = rest
        else:
            rel_ref, oh_ref, mixed_ref, o_ref, lse_ref, bias_scr = rest
        qi = pl.program_id(1)
        q0 = qi * bq
        q = q_ref[...]

        if mode == "chunk":
            @pl.when(qi == 0)
            def _():
                _fill_bias_tiles(rel_ref, oh_ref, bias_scr)

        if mode == "fox":
            cq = cc_ref[...]
        else:
            lo, hi = jnp.maximum(qi - (BAND_TILES - 1), 0), qi + 1

        def step(ki, carry, diagonal=False):
            m, l, acc = carry
            k0 = pl.multiple_of(ki * bk, bk)
            kt = k_ref[pl.ds(k0, bk), :]
            vt = v_ref[pl.ds(k0, bk), :]
            sc = lax.dot_general(q, kt, NT, preferred_element_type=F32) * SCALE
            if mode == "fox":
                sc = _fox_scores(sc, cq, cr_ref[ki], diagonal)
            else:
                sc = _chunk_scores(sc, q0, k0, bias_scr[ki - qi + (BAND_TILES - 1)])
            m_new = jnp.maximum(m, jnp.max(sc, axis=-1, keepdims=True))
            p = jnp.exp(sc - m_new)
            alpha = jnp.exp(m - m_new)
            l = alpha * l + jnp.sum(p, axis=-1, keepdims=True)
            acc = alpha * acc + jnp.dot(p.astype(BF16), vt, preferred_element_type=F32)
            return m_new, l, acc

        init = (jnp.full((bq, 1), NEG, F32), jnp.zeros((bq, 1), F32), jnp.zeros((bq, HEAD_DIM), F32))
        if mode == "fox":
            m, l, acc = step(qi, lax.fori_loop(0, qi, step, init), diagonal=True)
        else:
            m, l, acc = lax.fori_loop(lo, hi, step, init)
        o = acc / l
        g = g_ref[...]
        o_ref[...] = o
        mixed_ref[...] = (o * (g * _sigmoid(g))).astype(BF16)
        lse_ref[...] = m + jnp.log(l)

    head_col = lambda off: pl.BlockSpec((s, HEAD_DIM), lambda h, i: (0, off + h))
    in_specs = [pl.BlockSpec((bq, HEAD_DIM), lambda h, i: (i, h)), head_col(nh), head_col(2 * nh),
                pl.BlockSpec((bq, HEAD_DIM), lambda h, i: (i, h))]
    scratch = []
    if mode == "fox":
        in_specs += [pl.BlockSpec((None, bq, 1), lambda h, i: (h, i, 0)),
                     pl.BlockSpec((None, nq, 1, bk), lambda h, i: (h, 0, 0, 0))]
    else:
        in_specs += [pl.BlockSpec((None, 1, REL_PAD), lambda h, i: (h, 0, 0)),
                     pl.BlockSpec((BAND_TILES, TOEP, REL_PAD), lambda h, i: (0, 0, 0))]
        scratch = [pltpu.VMEM((BAND_TILES, CQ, CQ), F32)]
    outs, carried = _call(
        body, name=name, grid=(nh, nq), in_specs=in_specs,
        out_specs=[pl.BlockSpec((bq, HEAD_DIM), lambda h, i: (i, h)), pl.BlockSpec((bq, HEAD_DIM), lambda h, i: (i, h)),
                   pl.BlockSpec((None, bq, 1), lambda h, i: (h, i, 0))],
        out_shape=[jax.ShapeDtypeStruct((s, width), BF16), jax.ShapeDtypeStruct((s, width), F32),
                   jax.ShapeDtypeStruct((nh, s, 1), F32)],
        scratch=scratch, sem=("parallel", "arbitrary"), args=(qkv, qkv, qkv, gate, *aux), ride=ride)
    return outs if ride is None else (*outs, carried)


def _softmax_bwd(qkv, gate, o, dmixed, dm_off, lse, aux, *, mode, width, name, ride=None):
    s = qkv.shape[0]
    nh = width // HEAD_DIM
    bq = bk = CQ
    nq = s // bq
    dmo = dm_off // HEAD_DIM
    rel_pad = REL_PAD

    def body(q_ref, k_ref, v_ref, g_ref, o_ref, dm_ref, lse_ref, *rest):
        if mode == "fox":
            cc_ref, cr_ref, dq_ref, dk_ref, dv_ref, dg_ref, dc_ref, dcq_ref, dk_scr, dv_scr, dc_scr = rest
        else:
            rel_ref, oh_ref, dq_ref, dk_ref, dv_ref, dg_ref, drel_ref, dk_scr, dv_scr, bias_scr, db_scr = rest
        qi = pl.program_id(1)
        q0 = qi * bq

        @pl.when(qi == 0)
        def _():
            dk_scr[...] = jnp.zeros_like(dk_scr)
            dv_scr[...] = jnp.zeros_like(dv_scr)
            if mode == "fox":
                dc_scr[...] = jnp.zeros_like(dc_scr)
            else:
                db_scr[...] = jnp.zeros_like(db_scr)
                _fill_bias_tiles(rel_ref, oh_ref, bias_scr)

        g = g_ref[...]
        of = o_ref[...]
        dm = dm_ref[...]
        sig = _sigmoid(g)
        do = dm * (g * sig)
        dg_ref[...] = (dm * of * (sig * (1.0 + g * (1.0 - sig)))).astype(BF16)
        delta = jnp.sum(do * of, axis=-1, keepdims=True)
        do_b = do.astype(BF16)
        q = q_ref[...]
        lse_q = lse_ref[...]
        if mode == "fox":
            cq = cc_ref[...]
        else:
            lo, hi = jnp.maximum(qi - (BAND_TILES - 1), 0), qi + 1

        def step(ki, carry, diagonal=False):
            dq, rsum = carry
            k0 = pl.multiple_of(ki * bk, bk)
            kt = k_ref[pl.ds(k0, bk), :]
            vt = v_ref[pl.ds(k0, bk), :]
            sc = lax.dot_general(q, kt, NT, preferred_element_type=F32) * SCALE
            if mode == "fox":
                sc = _fox_scores(sc, cq, cr_ref[ki], diagonal)
            else:
                sc = _chunk_scores(sc, q0, k0, bias_scr[ki - qi + (BAND_TILES - 1)])
            p = jnp.exp(sc - lse_q)
            dp = lax.dot_general(do_b, vt, NT, preferred_element_type=F32)
            ds = p * (dp - delta)
            if mode == "fox":
                dc_scr[ki] += -jnp.sum(ds, axis=0, keepdims=True)
                rsum = rsum + jnp.sum(ds, axis=-1, keepdims=True)
            else:
                db_scr[ki - qi + (BAND_TILES - 1)] += ds
            ds_b = (ds * SCALE).astype(BF16)
            dk_scr[pl.ds(k0, bk), :] += lax.dot_general(ds_b, q, TN, preferred_element_type=F32)
            dv_scr[pl.ds(k0, bk), :] += lax.dot_general(p.astype(BF16), do_b, TN, preferred_element_type=F32)
            return dq + jnp.dot(ds_b, kt, preferred_element_type=F32), rsum

        init = (jnp.zeros((bq, HEAD_DIM), F32), jnp.zeros((bq, 1), F32))
        if mode == "fox":
            dq, rsum = step(qi, lax.fori_loop(0, qi, step, init), diagonal=True)
        else:
            dq, rsum = lax.fori_loop(lo, hi, step, init)
        dq_ref[...] = dq.astype(BF16)
        if mode == "fox":
            dcq_ref[...] = rsum

        @pl.when(qi == nq - 1)
        def _():
            dk_ref[...] = dk_scr[...].astype(BF16)
            dv_ref[...] = dv_scr[...].astype(BF16)
            if mode == "fox":
                dc_ref[...] = dc_scr[...]
            else:
                tot = jnp.zeros((8, rel_pad), F32)
                for j in range(BAND_TILES):
                    wide = jnp.concatenate([db_scr[j], jnp.zeros((CQ, TOEP - CQ), F32)], axis=1)
                    diag = jnp.sum(_skew_rows(wide, -1), axis=0, keepdims=True)
                    for part in _split3(jnp.broadcast_to(diag, (8, TOEP))):
                        tot = tot + jnp.dot(part, oh_ref[j], preferred_element_type=F32)
                drel_ref[...] = tot[0:1, :]

    head_col = lambda off: pl.BlockSpec((s, HEAD_DIM), lambda h, i: (0, off + h))
    qblk = lambda off: pl.BlockSpec((bq, HEAD_DIM), lambda h, i: (i, off + h))
    in_specs = [qblk(0), head_col(nh), head_col(2 * nh), qblk(0), qblk(0), qblk(dmo),
                pl.BlockSpec((None, bq, 1), lambda h, i: (h, i, 0))]
    out_specs = [qblk(0), head_col(0), head_col(0), qblk(0)]
    out_shape = [jax.ShapeDtypeStruct((s, width), BF16)] * 4
    scratch = [pltpu.VMEM((s, HEAD_DIM), F32), pltpu.VMEM((s, HEAD_DIM), F32)]
    if mode == "fox":
        in_specs += [pl.BlockSpec((None, bq, 1), lambda h, i: (h, i, 0)),
                     pl.BlockSpec((None, nq, 1, bk), lambda h, i: (h, 0, 0, 0))]
        out_specs += [pl.BlockSpec((None, nq, 1, bk), lambda h, i: (h, 0, 0, 0)),
                      pl.BlockSpec((None, bq, 1), lambda h, i: (h, i, 0))]
        out_shape += [jax.ShapeDtypeStruct((nh, nq, 1, bk), F32), jax.ShapeDtypeStruct((nh, s, 1), F32)]
        scratch += [pltpu.VMEM((nq, 1, bk), F32)]
    else:
        in_specs += [pl.BlockSpec((None, 1, rel_pad), lambda h, i: (h, 0, 0)),
                     pl.BlockSpec((BAND_TILES, TOEP, rel_pad), lambda h, i: (0, 0, 0))]
        out_specs += [pl.BlockSpec((None, 1, rel_pad), lambda h, i: (h, 0, 0))]
        out_shape += [jax.ShapeDtypeStruct((nh, 1, rel_pad), F32)]
        scratch += [pltpu.VMEM((BAND_TILES, CQ, CQ), F32), pltpu.VMEM((BAND_TILES, CQ, CQ), F32)]
    outs, carried = _call(
        body, name=name, grid=(nh, nq), in_specs=in_specs, out_specs=out_specs, out_shape=out_shape, scratch=scratch,
        sem=("parallel", "arbitrary"), args=(qkv, qkv, qkv, gate, o, dmixed, lse, *aux), ride=ride)
    return outs if ride is None else (*outs, carried)


SBK = 256
SBQ = 2 * SBK
SB_TILES_PER_TRIP = 1
SB_DEAD = -110.0


def _suffix_excl(x, tri):
    r = x.shape[0]
    both = jnp.dot(jnp.concatenate(_split2(x), axis=0), tri, preferred_element_type=F32)
    return both[:r] + both[r:]


def _sb_logits(qh, kt, diag):
    z = lax.dot_general(qh, kt, NT, preferred_element_type=F32) * SCALE
    lom = jnp.minimum(-z, 0.0) - jnp.log(1.0 + jnp.exp(-jnp.abs(z)))
    if diag is not None:
        lom = jnp.where(diag, lom, 0.0)
    return z, lom


def _sb_fwd_tile(qh, kt, vt, tri, diag, run, acc):
    z, lom = _sb_logits(qh, kt, diag)
    a = jnp.exp(lom + z + (_suffix_excl(lom, tri) + run))
    if diag is not None:
        a = jnp.where(diag, a, 0.0)
    acc = acc + jnp.dot(a.astype(BF16), vt, preferred_element_type=F32)
    return run + jnp.sum(lom, axis=-1, keepdims=True), acc


def _sb_alive(run_a, run_b):
    return (jnp.max(jnp.maximum(run_a, run_b)) > SB_DEAD).astype(jnp.int32)


def _sb_fwd(qkv, gate, *, width, name):
    s = qkv.shape[0]
    nh = width // HEAD_DIM
    nq = s // SBQ

    def body(q_ref, k_ref, v_ref, g_ref, mixed_ref, o_ref):
        qi = pl.program_id(1)
        r_i = lax.broadcasted_iota(jnp.int32, (SBK, SBK), 0)
        c_i = lax.broadcasted_iota(jnp.int32, (SBK, SBK), 1)
        tri = (r_i > c_i).astype(BF16)
        diag = c_i < r_i
        q_a = q_ref[0:SBK, :]
        q_b = q_ref[SBK:SBQ, :]

        def kv(tile):
            k0 = pl.multiple_of(tile * SBK, SBK)
            return k_ref[pl.ds(k0, SBK), :], v_ref[pl.ds(k0, SBK), :]

        zero = (jnp.zeros((SBK, 1), F32), jnp.zeros((SBK, HEAD_DIM), F32))
        kt, vt = kv(2 * qi + 1)
        run_b, acc_b = _sb_fwd_tile(q_b, kt, vt, tri, diag, *zero)
        kt, vt = kv(2 * qi)
        run_b, acc_b = _sb_fwd_tile(q_b, kt, vt, tri, None, run_b, acc_b)
        run_a, acc_a = _sb_fwd_tile(q_a, kt, vt, tri, diag, *zero)

        trips = (2 // SB_TILES_PER_TRIP) * qi

        def step(carry):
            t, _, run_a, acc_a, run_b, acc_b = carry
            for u in range(SB_TILES_PER_TRIP):
                kt, vt = kv(2 * qi - 1 - (SB_TILES_PER_TRIP * t + u))
                run_a, acc_a = _sb_fwd_tile(q_a, kt, vt, tri, None, run_a, acc_a)
                run_b, acc_b = _sb_fwd_tile(q_b, kt, vt, tri, None, run_b, acc_b)
            return t + 1, _sb_alive(run_a, run_b), run_a, acc_a, run_b, acc_b

        _, _, _, acc_a, _, acc_b = lax.while_loop(
            lambda c: jnp.logical_and(c[0] < trips, c[1] > 0), step,
            (jnp.int32(0), _sb_alive(run_a, run_b), run_a, acc_a, run_b, acc_b))
        o = jnp.concatenate([acc_a, acc_b], axis=0)
        g = g_ref[...]
        o_ref[...] = o
        mixed_ref[...] = (o * (g * _sigmoid(g))).astype(BF16)

    head_col = lambda off: pl.BlockSpec((s, HEAD_DIM), lambda h, i: (0, off + h))
    qblk = pl.BlockSpec((SBQ, HEAD_DIM), lambda h, i: (i, h))
    return pl.pallas_call(
        body, name=name, grid=(nh, nq), in_specs=[qblk, head_col(nh), head_col(2 * nh), qblk],
        out_specs=[qblk, qblk],
        out_shape=[jax.ShapeDtypeStruct((s, width), BF16), jax.ShapeDtypeStruct((s, width), F32)],
        compiler_params=_params(("parallel", "arbitrary")),
    )(qkv, qkv, qkv, gate)


def _sb_bwd(qkv, gate, o, dmixed, *, width, name):
    s = qkv.shape[0]
    nh = width // HEAD_DIM
    nq = s // SBQ

    def body(q_ref, k_ref, v_ref, g_ref, o_ref, dm_ref, dq_ref, dk_ref, dv_ref, dg_ref, dk_scr, dv_scr):
        qi = pl.program_id(1)

        @pl.when(qi == 0)
        def _():
            dk_scr[...] = jnp.zeros_like(dk_scr)
            dv_scr[...] = jnp.zeros_like(dv_scr)

        g = g_ref[...]
        of = o_ref[...]
        dm = dm_ref[...]
        sig = _sigmoid(g)
        do = dm * (g * sig)
        dg_ref[...] = (dm * of * (sig * (1.0 + g * (1.0 - sig)))).astype(BF16)
        do_b = do.astype(BF16)
        q = q_ref[...]
        r_i = lax.broadcasted_iota(jnp.int32, (SBK, SBK), 0)
        c_i = lax.broadcasted_iota(jnp.int32, (SBK, SBK), 1)
        tri = (r_i > c_i).astype(BF16)
        tri_pre = (r_i < c_i).astype(BF16)
        diag = c_i < r_i
        q_a, q_b = q[0:SBK], q[SBK:SBQ]
        do_a, do_b2 = do_b[0:SBK], do_b[SBK:SBQ]

        def kv(t):
            k0 = pl.multiple_of(t * SBK, SBK)
            return k0, k_ref[pl.ds(k0, SBK), :], v_ref[pl.ds(k0, SBK), :]

        def mass(qh, tile_, dg_):
            return jnp.sum(_sb_logits(qh, kv(tile_)[1], dg_)[1], axis=-1, keepdims=True)

        run_b = mass(q_b, 2 * qi + 1, diag) + mass(q_b, 2 * qi, None)
        run_a = mass(q_a, 2 * qi, diag)
        trips = (2 // SB_TILES_PER_TRIP) * qi

        def scout(carry):
            t, _, run_a, run_b = carry
            for u in range(SB_TILES_PER_TRIP):
                tile_ = 2 * qi - 1 - (SB_TILES_PER_TRIP * t + u)
                run_a = run_a + mass(q_a, tile_, None)
                run_b = run_b + mass(q_b, tile_, None)
            return t + 1, _sb_alive(run_a, run_b), run_a, run_b

        walked, _, run_a, run_b = lax.while_loop(
            lambda c: jnp.logical_and(c[0] < trips, c[1] > 0), scout,
            (jnp.int32(0), _sb_alive(run_a, run_b), run_a, run_b))

        def tile(qh, doh, kt, vt, dg_, carry):
            rem, gpre, dq = carry
            z, lom = _sb_logits(qh, kt, dg_)
            rem = rem - jnp.sum(lom, axis=-1, keepdims=True)
            a = jnp.exp(lom + z + (_suffix_excl(lom, tri) + rem))
            if dg_ is not None:
                a = jnp.where(dg_, a, 0.0)
            gg = lax.dot_general(doh, vt, NT, preferred_element_type=F32) * a
            pre = _suffix_excl(gg, tri_pre) + gpre
            dz = gg * jnp.exp(lom) - pre * jnp.exp(lom + z)
            if dg_ is not None:
                dz = jnp.where(dg_, dz, 0.0)
            dz_b = (dz * SCALE).astype(BF16)
            dq = dq + jnp.dot(dz_b, kt, preferred_element_type=F32)
            return (rem, gpre + jnp.sum(gg, axis=-1, keepdims=True), dq), dz_b, a.astype(BF16)

        def both(t, ca, cb, dg_a):
            k0, kt, vt = kv(t)
            ca, dz_a, a_a = tile(q_a, do_a, kt, vt, dg_a, ca)
            cb, dz_b_, a_b = tile(q_b, do_b2, kt, vt, None, cb)
            dk_scr[pl.ds(k0, SBK), :] += lax.dot_general(jnp.concatenate([dz_a, dz_b_], axis=0), q, TN,
                                                         preferred_element_type=F32)
            dv_scr[pl.ds(k0, SBK), :] += lax.dot_general(jnp.concatenate([a_a, a_b], axis=0), do_b, TN,
                                                         preferred_element_type=F32)
            return ca, cb

        def step(t, carry):
            ca, cb = carry
            for u in range(SB_TILES_PER_TRIP):
                ca, cb = both(SB_TILES_PER_TRIP * t + u, ca, cb, None)
            return ca, cb

        zero = (jnp.zeros((SBK, 1), F32), jnp.zeros((SBK, HEAD_DIM), F32))
        ca, cb = lax.fori_loop(trips - walked, trips, step, ((run_a, *zero), (run_b, *zero)))
        ca, cb = both(2 * qi, ca, cb, diag)
        k0, kt, vt = kv(2 * qi + 1)
        cb, dz_b_, a_b = tile(q_b, do_b2, kt, vt, diag, cb)
        dk_scr[pl.ds(k0, SBK), :] += lax.dot_general(dz_b_, q_b, TN, preferred_element_type=F32)
        dv_scr[pl.ds(k0, SBK), :] += lax.dot_general(a_b, do_b2, TN, preferred_element_type=F32)
        dq_ref[...] = jnp.concatenate([ca[2], cb[2]], axis=0).astype(BF16)

        @pl.when(qi == nq - 1)
        def _():
            dk_ref[...] = dk_scr[...].astype(BF16)
            dv_ref[...] = dv_scr[...].astype(BF16)

    head_col = lambda off: pl.BlockSpec((s, HEAD_DIM), lambda h, i: (0, off + h))
    qblk = pl.BlockSpec((SBQ, HEAD_DIM), lambda h, i: (i, h))
    return pl.pallas_call(
        body, name=name, grid=(nh, nq),
        in_specs=[qblk, head_col(nh), head_col(2 * nh), qblk, qblk, qblk],
        out_specs=[qblk, head_col(0), head_col(0), qblk],
        out_shape=[jax.ShapeDtypeStruct((s, width), BF16)] * 4,
        scratch_shapes=[pltpu.VMEM((s, HEAD_DIM), F32), pltpu.VMEM((s, HEAD_DIM), F32)],
        compiler_params=_params(("parallel", "arbitrary")),
    )(qkv, qkv, qkv, gate, o, dmixed)


HBM = pl.BlockSpec(memory_space=pl.ANY)
MESH = pl.DeviceIdType.MESH


def _all_gather(shard, *, name):
    r, c_ = shard.shape

    def body(x_ref, out_ref, send_sems, recv_sems, local_sem):
        x, y, c = lax.axis_index("x"), lax.axis_index("y"), lax.axis_index("c")
        me, sibling = (x, y, c), (x, y, 1 - c)
        chips = [(1 - x, y), (x, 1 - y), (1 - x, 1 - y)]

        def slot(px, py, pc):
            return out_ref.at[4 * px + 2 * py + pc]

        def copy(k, block, to, src=None):
            return pltpu.make_async_remote_copy(
                src_ref=slot(*block) if src is None else src, dst_ref=slot(*block),
                send_sem=send_sems.at[k], recv_sem=recv_sems.at[k], device_id=to, device_id_type=MESH)

        mine = pltpu.make_async_copy(x_ref, slot(*me), local_sem)
        mine.start()
        first = [copy(0, me, sibling, src=x_ref)]
        first += [copy(1 + j, me, (*chip, c), src=x_ref) for j, chip in enumerate(chips)]
        for cp in first:
            cp.start()
        passed = [copy(4 + j, (*chip, c), sibling) for j, chip in enumerate(chips)]
        for j, chip in enumerate(chips):
            copy(1 + j, (*chip, c), me).wait_recv()
            passed[j].start()
        copy(0, sibling, me).wait_recv()
        for j, chip in enumerate(chips):
            copy(4 + j, (*chip, 1 - c), me).wait_recv()
        for cp in first + passed:
            cp.wait_send()
        mine.wait()

    return pl.pallas_call(
        body, name=name, in_specs=[HBM], out_specs=HBM,
        out_shape=jax.ShapeDtypeStruct((N_DEV, r, c_), shard.dtype),
        scratch_shapes=[pltpu.SemaphoreType.DMA((7,)), pltpu.SemaphoreType.DMA((7,)), pltpu.SemaphoreType.DMA],
    )(shard)


class _Ride:
    def __init__(self, src, *, gather):
        self.src, self.gather = src, gather
        self.out_shape = jax.ShapeDtypeStruct((N_DEV, *(src.shape if gather else src.shape[1:])), src.dtype)
        self.scratch = [pltpu.SemaphoreType.DMA((7,)), pltpu.SemaphoreType.DMA((7,)), pltpu.SemaphoreType.DMA]

    def _copies(self, src_ref, out_ref, send_sems, recv_sems, local_sem):
        x, y, c = lax.axis_index("x"), lax.axis_index("y"), lax.axis_index("c")
        me = 4 * x + 2 * y + c
        pick = (lambda j: src_ref) if self.gather else (lambda j: src_ref.at[j])
        mine = pltpu.make_async_copy(pick(me), out_ref.at[me], local_sem)
        copies = []
        for k in range(1, N_DEV):
            px, py, pc = x ^ ((k >> 2) & 1), y ^ ((k >> 1) & 1), c ^ (k & 1)
            copies.append(pltpu.make_async_remote_copy(
                src_ref=pick(4 * px + 2 * py + pc), dst_ref=out_ref.at[me],
                send_sem=send_sems.at[k - 1], recv_sem=recv_sems.at[k - 1],
                device_id=(px, py, pc), device_id_type=MESH))
        return mine, copies

    def start(self, *refs):
        mine, copies = self._copies(*refs)
        mine.start()
        for cp in copies:
            cp.start()

    def finish(self, *refs):
        mine, copies = self._copies(*refs)
        for cp in copies:
            cp.wait_recv()
        for cp in copies:
            cp.wait_send()
        mine.wait()


def _exchange(src, *, gather, name):
    ride = _Ride(src, gather=gather)

    def body(*refs):
        ride.start(*refs)
        ride.finish(*refs)

    return pl.pallas_call(body, name=name, in_specs=[HBM], out_specs=HBM, out_shape=ride.out_shape,
                          scratch_shapes=ride.scratch)(src)


def _call(body, *, name, grid, in_specs, out_specs, out_shape, scratch, sem, args, ride=None):
    if ride is None:
        outs = pl.pallas_call(body, name=name, grid=grid, in_specs=in_specs, out_specs=out_specs, out_shape=out_shape,
                              scratch_shapes=scratch, compiler_params=_params(sem))(*args)
        return list(outs), None
    n_in, n_out = len(in_specs), len(out_specs)

    def carrying(*refs):
        ins, src_ref = refs[:n_in], refs[n_in]
        outs, dst_ref = refs[n_in + 1:n_in + 1 + n_out], refs[n_in + 1 + n_out]
        rest = refs[n_in + 2 + n_out:]
        own, sems = rest[:len(rest) - 3], rest[len(rest) - 3:]
        ids = [pl.program_id(a) for a in range(len(grid))]
        first = functools.reduce(jnp.logical_and, [i == 0 for i in ids])
        last = functools.reduce(jnp.logical_and, [i == n - 1 for i, n in zip(ids, grid)])

        @pl.when(first)
        def _():
            ride.start(src_ref, dst_ref, *sems)

        body(*ins, *outs, *own)

        @pl.when(last)
        def _():
            ride.finish(src_ref, dst_ref, *sems)

    outs = pl.pallas_call(
        carrying, name=name, grid=grid, in_specs=[*in_specs, HBM], out_specs=[*out_specs, HBM],
        out_shape=[*out_shape, ride.out_shape], scratch_shapes=[*scratch, *ride.scratch],
        compiler_params=_params(("arbitrary",) * len(grid)))(*args, ride.src)
    return list(outs[:-1]), outs[-1]


def _adamw(parts, w, m, v, *, name, rows):
    r, c_ = w.shape
    rows = min(rows, r)
    assert r % rows == 0
    c1 = 1.0 / (1.0 - ADAM_B1 ** ADAM_STEP)
    c2 = 1.0 / (1.0 - ADAM_B2 ** ADAM_STEP)

    def body(p_ref, w_ref, m_ref, v_ref, g_ref, d_ref, nm_ref, nv_ref):
        g = p_ref[0].astype(F32)
        for i in range(1, N_DEV):
            g = g + p_ref[i].astype(F32)
        nm = ADAM_B1 * m_ref[...] + (1.0 - ADAM_B1) * g
        nv = ADAM_B2 * v_ref[...] + (1.0 - ADAM_B2) * (g * g)
        g_ref[...] = g
        nm_ref[...] = nm
        nv_ref[...] = nv
        d_ref[...] = -ADAM_LR * ((nm * c1) / (jnp.sqrt(nv * c2) + ADAM_EPS) + ADAM_WD * w_ref[...])

    blk = pl.BlockSpec((rows, c_), lambda i: (i, 0))
    return pl.pallas_call(
        body, name=name, grid=(r // rows,),
        in_specs=[pl.BlockSpec((N_DEV, rows, c_), lambda i: (0, i, 0)), blk, blk, blk],
        out_specs=[blk] * 4, out_shape=[jax.ShapeDtypeStruct((r, c_), F32)] * 4,
        compiler_params=_params(("parallel",)),
    )(parts, w, m, v)


def _pad_cols(a, n):
    return jnp.pad(a, ((0, 0), (0, n - a.shape[1])))


def _fox_dc(shares, nh, s):
    key_side, query_side = shares
    return _pad_cols(jnp.transpose(key_side.reshape(nh, s) + query_side.reshape(nh, s)), 128)


def _local_step(x, target, norm_pre, norm_post, w_in_e, w_f, b_f, rel_bias, sh_out_e, sh_in_o, sh_out_o):
    s, d = x.shape
    wa = d // 2
    nha = wa // HEAD_DIM
    nq = s // CQ
    gather = lambda shard: _Ride(shard, gather=True)
    scatter = lambda parts: _Ride(parts, gather=False)

    h0 = _rms_fwd(x, norm_pre[0:1], name="rms_pre0")
    proj = lambda w, off, n, dt, nm, **kw: _matmul(h0, w, mode="nn", m=s, n=n, k=d, out_dtype=dt, name=nm,
                                                   b_off=(0, off), **kw)
    qkv_a, w_out_e = proj(w_in_e, 0, 3 * wa, BF16, "proj_qkv_a", ride=gather(sh_out_e))
    w_out_e = w_out_e.reshape(d, d)
    g_a = proj(w_in_e, 3 * wa, wa, F32, "proj_gate_a")
    qkv_b = proj(w_in_e, 4 * wa, 3 * wa, BF16, "proj_qkv_b")
    g_b = proj(w_in_e, 7 * wa, wa, F32, "proj_gate_b")
    af = _matmul(h0, w_f, mode="nn", m=s, n=128, k=d, out_dtype=F32, name="proj_forget")
    bias128 = _pad_cols(b_f, 128)
    cum = _gate_fwd(af, bias128, name="forget_cumsum")
    c_t = jnp.transpose(cum[:, :nha])
    c_col = c_t.reshape(nha, s, 1)
    c_row = c_t.reshape(nha, nq, 1, CQ)
    mixed_a, o_a, lse_a, w_in_o = _softmax_fwd(qkv_a, g_a, (c_col, c_row), mode="fox", width=wa, name="fox_fwd",
                                               ride=gather(sh_in_o))
    w_in_o = jnp.transpose(w_in_o, (1, 0, 2)).reshape(d, 4 * d)
    rel_aux = (_pad_cols(rel_bias, REL_PAD).reshape(nha, 1, REL_PAD), _rel_onehot())
    mixed_b, o_b, lse_b, w_out_o = _softmax_fwd(qkv_b, g_b, rel_aux, mode="chunk", width=wa, name="chunk_fwd",
                                                ride=gather(sh_out_o))
    w_out_o = w_out_o.reshape(d, d)
    mixed0 = jnp.concatenate([mixed_a, mixed_b], axis=1)
    y0 = _matmul(mixed0, w_out_e, mode="nn", m=s, n=d, k=d, out_dtype=F32, name="out_proj0")
    x1 = _post_fwd(x, y0, norm_post[0:1], name="post0")

    h1 = _rms_fwd(x1, norm_pre[1:2], name="rms_pre1")
    qkv_c = _matmul(h1, w_in_o, mode="nn", m=s, n=3 * d, k=d, out_dtype=BF16, name="proj_qkv_c")
    g_c = _matmul(h1, w_in_o, mode="nn", m=s, n=d, k=d, out_dtype=F32, name="proj_gate_c", b_off=(0, 3 * d))
    mixed1, o_c = _sb_fwd(qkv_c, g_c, width=d, name="sb_fwd")
    y1 = _matmul(mixed1, w_out_o, mode="nn", m=s, n=d, k=d, out_dtype=F32, name="out_proj1")
    x2 = _post_fwd(x1, y1, norm_post[1:2], name="post1")

    loss, dx2 = _loss_head(x2, target, name="loss_head")

    dy1, dgpost1 = _post_bwd(dx2, y1, norm_post[1:2], name="post_bwd1")
    dmixed1 = _matmul(dy1, w_out_o, mode="nt", m=s, n=d, k=d, out_dtype=F32, name="dmixed1")
    dw_out_o = _matmul(mixed1, dy1, mode="tn", m=d, n=d, k=s, out_dtype=BF16, name="dw_out1")
    dq_c, dk_c, dv_c, dg_c = _sb_bwd(qkv_c, g_c, o_c, dmixed1, width=d, name="sb_bwd")
    dproj1 = jnp.concatenate([dq_c, dk_c, dv_c, dg_c], axis=1)
    dh1 = _matmul(dproj1, w_in_o, mode="nt", m=s, n=d, k=4 * d, out_dtype=F32, name="dh1")
    dw_in_o = _matmul(h1, dproj1, mode="tn", m=d, n=4 * d, k=s, out_dtype=BF16, name="dw_in1")
    dx1, dgpre1 = _pre_bwd((dh1,), x1, norm_pre[1:2], dx2, name="pre_bwd1")

    dy0, dgpost0 = _post_bwd(dx1, y0, norm_post[0:1], name="post_bwd0")
    dmixed0 = _matmul(dy0, w_out_e, mode="nt", m=s, n=d, k=d, out_dtype=F32, name="dmixed0")
    dw_out_e = _matmul(mixed0, dy0, mode="tn", m=d, n=d, k=s, out_dtype=BF16, name="dw_out0")
    parts_in_o = jnp.transpose(dw_in_o.reshape(d, N_DEV, 4 * d // N_DEV), (1, 0, 2))
    dq_a, dk_a, dv_a, dg_a, *dc_shares, got_in_o = _softmax_bwd(
        qkv_a, g_a, o_a, dmixed0, 0, lse_a, (c_col, c_row), mode="fox", width=wa, name="fox_bwd", ride=scatter(parts_in_o))
    dq_b, dk_b, dv_b, dg_b, drel, got_out_o = _softmax_bwd(
        qkv_b, g_b, o_b, dmixed0, wa, lse_b, rel_aux, mode="chunk", width=wa, name="chunk_bwd",
        ride=scatter(dw_out_o.reshape(N_DEV, d // N_DEV, d)))
    dc = _fox_dc(dc_shares, nha, s)
    daf, dbf = _gate_bwd(dc, af, bias128, name="forget_bwd")
    dproj0 = jnp.concatenate([dq_a, dk_a, dv_a, dg_a, dq_b, dk_b, dv_b, dg_b], axis=1)
    dh0, got_out_e = _matmul(dproj0, w_in_e, mode="nt", m=s, n=d, k=8 * wa, out_dtype=F32, name="dh0_main",
                             ride=scatter(dw_out_e.reshape(N_DEV, d // N_DEV, d)))
    dh0f = _matmul(daf, w_f, mode="nt", m=s, n=d, k=128, out_dtype=F32, name="dh0_forget")
    dw_in_e = _matmul(h0, dproj0, mode="tn", m=d, n=8 * wa, k=s, out_dtype=BF16, name="dw_in0")
    dw_f = _matmul(h0, daf, mode="tn", m=d, n=128, k=s, out_dtype=BF16, name="dw_forget")
    dx0, dgpre0 = _pre_bwd((dh0, dh0f), x, norm_pre[0:1], dx1, name="pre_bwd0")

    small = dict(
        norm_pre=jnp.concatenate([dgpre0, dgpre1], axis=0),
        norm_post=jnp.concatenate([dgpost0, dgpost1], axis=0),
        b_f=dbf[:, :nha], rel_bias=drel[:, 0, :N_REL])
    got = dict(w_out_even=got_out_e, w_in_odd=got_in_o, w_out_odd=got_out_o)
    return loss, dx0, got, jnp.concatenate([dw_in_e, dw_f[:, :nha]], axis=1), small


def _pack_small(norm_pre, norm_post, b_f, rel_bias):
    flat = jnp.concatenate([norm_pre.reshape(-1), norm_post.reshape(-1), b_f.reshape(-1), rel_bias.reshape(-1)])
    n = flat.shape[0]
    rows = -(-n // 128)
    rows = -(-rows // 8) * 8
    return jnp.pad(flat, (0, rows * 128 - n)).reshape(rows, 128)


def _unpack_small(slab, shapes):
    flat = slab.reshape(-1)
    out, off = [], 0
    for shp in shapes:
        n = int(np.prod(shp))
        out.append(flat[off:off + n].reshape(shp))
        off += n
    return out


def kernel(x, norm_pre, norm_post, w_in_even, b_f_even, rel_bias_even, w_out_even, w_in_odd, w_out_odd, loss_target, m_norm_pre, m_norm_post, m_w_in_even, m_b_f_even, m_rel_bias_even, m_w_out_even, m_w_in_odd, m_w_out_odd, v_norm_pre, v_norm_post, v_w_in_even, v_b_f_even, v_rel_bias_even, v_w_out_even, v_w_in_odd, v_w_out_odd):
    _, s, d = x.shape
    wa = d // 2
    nha = wa // HEAD_DIM
    in_e = w_in_even.shape[2] * N_DEV
    sh_e = w_in_even.shape[2]

    w_in_e_all = jnp.transpose(_all_gather(w_in_even[0].astype(BF16), name="ag_w_in_even"), (1, 0, 2)).reshape(d, in_e)
    w_main = w_in_e_all[:, :8 * wa]
    w_f = _pad_cols(w_in_e_all[:, 8 * wa:], 128)

    loss, dx, got, dw_e, small = _local_step(
        x[0], loss_target[0], norm_pre, norm_post, w_main, w_f, b_f_even, rel_bias_even[0],
        w_out_even[0].astype(BF16), w_in_odd[0].astype(BF16), w_out_odd[0].astype(BF16))

    parts_in_e = jnp.transpose(dw_e.reshape(d, N_DEV, sh_e), (1, 0, 2))
    got["w_in_even"] = _exchange(parts_in_e, gather=False, name="rs_w_in_even")
    upd = {}
    upd["w_in_even"] = _adamw(got["w_in_even"], w_in_even[0], m_w_in_even[0], v_w_in_even[0],
                              name="adamw_w_in_even", rows=128)
    upd["w_out_even"] = _adamw(got["w_out_even"], w_out_even[0], m_w_out_even[0], v_w_out_even[0],
                               name="adamw_w_out_even", rows=64)
    upd["w_in_odd"] = _adamw(got["w_in_odd"], w_in_odd[0], m_w_in_odd[0], v_w_in_odd[0],
                             name="adamw_w_in_odd", rows=128)
    upd["w_out_odd"] = _adamw(got["w_out_odd"], w_out_odd[0], m_w_out_odd[0], v_w_out_odd[0],
                              name="adamw_w_out_odd", rows=64)

    shapes = [norm_pre.shape, norm_post.shape, b_f_even.shape, rel_bias_even.shape]
    g_slab = _pack_small(small["norm_pre"], small["norm_post"], small["b_f"], small["rel_bias"])
    parts_small = _exchange(g_slab, gather=True, name="ar_small")
    sm = _adamw(parts_small, _pack_small(norm_pre, norm_post, b_f_even, rel_bias_even),
                _pack_small(m_norm_pre, m_norm_post, m_b_f_even, m_rel_bias_even),
                _pack_small(v_norm_pre, v_norm_post, v_b_f_even, v_rel_bias_even), name="adamw_small", rows=g_slab.shape[0])
    sm = [_unpack_small(a, shapes) for a in sm]

    total = lax.psum(loss[0, 0], ("x", "y", "c"))

    def leaves(kind):
        return (sm[kind][0], sm[kind][1], upd["w_in_even"][kind][None], sm[kind][2], sm[kind][3],
                upd["w_out_even"][kind][None], upd["w_in_odd"][kind][None], upd["w_out_odd"][kind][None])

    return (total, dx[None], *leaves(0), *leaves(1), *leaves(2), *leaves(3))
```

```python
import functools

import numpy as np
import jax
import jax.numpy as jnp
from jax import lax
from jax.experimental import pallas as pl
from jax.experimental.pallas import tpu as pltpu

F32 = jnp.float32
BF16 = jnp.bfloat16

HEAD_DIM = 128
CHUNK = 64
LEFT_CHUNKS = 8
REL_CLIP = 128
N_REL = 2 * REL_CLIP + 1
RMS_EPS = 1e-6
SCALE = HEAD_DIM ** -0.5

ADAM_LR = 0.001
ADAM_B1 = 0.9
ADAM_B2 = 0.999
ADAM_EPS = 1e-08
ADAM_WD = 0.01
ADAM_STEP = 10

N_DEV = 8
V7X_VMEM_LIMIT_BYTES = 56 * 1024 * 1024
NEG = -1e30

NT = (((1,), (1,)), ((), ()))
TN = (((0,), (0,)), ((), ()))
NN = (((1,), (0,)), ((), ()))

CQ = 256
BAND_TILES = 3
TOEP = 2 * CQ
assert (BAND_TILES - 1) * CQ == LEFT_CHUNKS * CHUNK


def _params(sem):
    return pltpu.CompilerParams(dimension_semantics=sem, vmem_limit_bytes=V7X_VMEM_LIMIT_BYTES)


def _split3(x):
    hi = x.astype(BF16)
    r1 = x - hi.astype(F32)
    mid = r1.astype(BF16)
    lo = (r1 - mid.astype(F32)).astype(BF16)
    return hi, mid, lo


def _split2(x):
    hi = x.astype(BF16)
    lo = (x - hi.astype(F32)).astype(BF16)
    return hi, lo


def _sigmoid(g):
    return 1.0 / (1.0 + jnp.exp(-g))


def _tile(n, cap, *offsets):
    if n <= 128:
        return n
    t = (min(cap, n) // 128) * 128
    while n % t or any(o % t for o in offsets):
        t -= 128
    return t


def _matmul(a, b, *, mode, m, n, k, out_dtype, name, a_off=(0, 0), b_off=(0, 0), tm=1024, tn=1024, tk=1024,
            ride=None):
    a_m, a_k = (a_off if mode in ("nn", "nt") else a_off[::-1])
    b_k, b_n = (b_off if mode in ("nn", "tn") else b_off[::-1])
    tm, tn, tk = _tile(m, tm, a_m), _tile(n, tn, b_n), _tile(k, tk, a_k, b_k)
    nk = k // tk
    if mode in ("nn", "nt"):
        ao = (a_off[0] // tm, a_off[1] // tk)
        a_spec = pl.BlockSpec((tm, tk), lambda i, j, l: (i + ao[0], l + ao[1]))
    else:
        ao = (a_off[0] // tk, a_off[1] // tm)
        a_spec = pl.BlockSpec((tk, tm), lambda i, j, l: (l + ao[0], i + ao[1]))
    if mode in ("nn", "tn"):
        bo = (b_off[0] // tk, b_off[1] // tn)
        b_spec = pl.BlockSpec((tk, tn), lambda i, j, l: (l + bo[0], j + bo[1]))
    else:
        bo = (b_off[0] // tn, b_off[1] // tk)
        b_spec = pl.BlockSpec((tn, tk), lambda i, j, l: (j + bo[0], l + bo[1]))
    dn = {"nn": NN, "nt": NT, "tn": TN}[mode]

    def body(a_ref, b_ref, o_ref, acc_ref):
        @pl.when(pl.program_id(2) == 0)
        def _():
            acc_ref[...] = jnp.zeros_like(acc_ref)

        acc_ref[...] += lax.dot_general(a_ref[...], b_ref[...], dn, preferred_element_type=F32)

        @pl.when(pl.program_id(2) == nk - 1)
        def _():
            o_ref[...] = acc_ref[...].astype(out_dtype)

    (out,), carried = _call(
        body, name=name, grid=(m // tm, n // tn, nk), in_specs=[a_spec, b_spec],
        out_specs=[pl.BlockSpec((tm, tn), lambda i, j, l: (i, j))], out_shape=[jax.ShapeDtypeStruct((m, n), out_dtype)],
        scratch=[pltpu.VMEM((tm, tn), F32)], sem=("parallel", "parallel", "arbitrary"), args=(a, b), ride=ride)
    return out if ride is None else (out, carried)


ROWS = 128


def _rms_fwd(x, gain, *, name):
    s, d = x.shape

    def body(x_ref, g_ref, h_ref):
        xf = x_ref[...]
        r = lax.rsqrt(jnp.mean(xf * xf, axis=-1, keepdims=True) + RMS_EPS)
        h_ref[...] = ((xf * r) * g_ref[...]).astype(BF16)

    return pl.pallas_call(
        body, name=name, grid=(s // ROWS,),
        in_specs=[pl.BlockSpec((ROWS, d), lambda i: (i, 0)), pl.BlockSpec((1, d), lambda i: (0, 0))],
        out_specs=pl.BlockSpec((ROWS, d), lambda i: (i, 0)),
        out_shape=jax.ShapeDtypeStruct((s, d), BF16),
        compiler_params=_params(("parallel",)),
    )(x, gain)


def _post_fwd(x, y, gain, *, name):
    s, d = x.shape

    def body(x_ref, y_ref, g_ref, o_ref):
        yf = y_ref[...]
        r = lax.rsqrt(jnp.mean(yf * yf, axis=-1, keepdims=True) + RMS_EPS)
        o_ref[...] = x_ref[...] + (yf * r) * g_ref[...]

    return pl.pallas_call(
        body, name=name, grid=(s // ROWS,),
        in_specs=[pl.BlockSpec((ROWS, d), lambda i: (i, 0)), pl.BlockSpec((ROWS, d), lambda i: (i, 0)),
                  pl.BlockSpec((1, d), lambda i: (0, 0))],
        out_specs=pl.BlockSpec((ROWS, d), lambda i: (i, 0)),
        out_shape=jax.ShapeDtypeStruct((s, d), F32),
        compiler_params=_params(("parallel",)),
    )(x, y, gain)


def _loss_head(xo, target, *, name):
    s, d = xo.shape
    inv_d = 1.0 / d

    def body(x_ref, t_ref, loss_ref, dx_ref):
        @pl.when(pl.program_id(0) == 0)
        def _():
            loss_ref[...] = jnp.zeros_like(loss_ref)

        e = x_ref[...] - t_ref[...]
        dx_ref[...] = e * inv_d
        loss_ref[...] += 0.5 * jnp.sum(jnp.mean(e * e, axis=-1, keepdims=True), axis=0, keepdims=True)

    return pl.pallas_call(
        body, name=name, grid=(s // ROWS,),
        in_specs=[pl.BlockSpec((ROWS, d), lambda i: (i, 0)), pl.BlockSpec((ROWS, d), lambda i: (i, 0))],
        out_specs=[pl.BlockSpec((1, 1), lambda i: (0, 0)), pl.BlockSpec((ROWS, d), lambda i: (i, 0))],
        out_shape=[jax.ShapeDtypeStruct((1, 1), F32), jax.ShapeDtypeStruct((s, d), F32)],
        compiler_params=_params(("arbitrary",)),
    )(xo, target)


def _post_bwd(dxo, y, gain, *, name):
    s, d = y.shape

    def body(dx_ref, y_ref, g_ref, dy_ref, dg_ref):
        @pl.when(pl.program_id(0) == 0)
        def _():
            dg_ref[...] = jnp.zeros_like(dg_ref)

        yf = y_ref[...]
        dxo_ = dx_ref[...]
        r = lax.rsqrt(jnp.mean(yf * yf, axis=-1, keepdims=True) + RMS_EPS)
        nrm = yf * r
        dg_ref[...] += jnp.sum(dxo_ * nrm, axis=0, keepdims=True)
        dn = dxo_ * g_ref[...]
        dy_ref[...] = (r * (dn - nrm * jnp.mean(dn * nrm, axis=-1, keepdims=True))).astype(BF16)

    return pl.pallas_call(
        body, name=name, grid=(s // ROWS,),
        in_specs=[pl.BlockSpec((ROWS, d), lambda i: (i, 0)), pl.BlockSpec((ROWS, d), lambda i: (i, 0)),
                  pl.BlockSpec((1, d), lambda i: (0, 0))],
        out_specs=[pl.BlockSpec((ROWS, d), lambda i: (i, 0)), pl.BlockSpec((1, d), lambda i: (0, 0))],
        out_shape=[jax.ShapeDtypeStruct((s, d), BF16), jax.ShapeDtypeStruct((1, d), F32)],
        compiler_params=_params(("arbitrary",)),
    )(dxo, y, gain)


def _pre_bwd(dhs, x, gain, dres, *, name):
    s, d = x.shape
    n_dh = len(dhs)

    def body(*refs):
        dh_refs = refs[:n_dh]
        x_ref, g_ref, dr_ref, dx_ref, dg_ref = refs[n_dh:]

        @pl.when(pl.program_id(0) == 0)
        def _():
            dg_ref[...] = jnp.zeros_like(dg_ref)

        xf = x_ref[...]
        dh_ = dh_refs[0][...]
        for extra in dh_refs[1:]:
            dh_ = dh_ + extra[...]
        r = lax.rsqrt(jnp.mean(xf * xf, axis=-1, keepdims=True) + RMS_EPS)
        nrm = xf * r
        dg_ref[...] += jnp.sum(dh_ * nrm, axis=0, keepdims=True)
        dn = dh_ * g_ref[...]
        dx_ref[...] = dr_ref[...] + r * (dn - nrm * jnp.mean(dn * nrm, axis=-1, keepdims=True))

    return pl.pallas_call(
        body, name=name, grid=(s // ROWS,),
        in_specs=[pl.BlockSpec((ROWS, d), lambda i: (i, 0))] * (n_dh + 1)
        + [pl.BlockSpec((1, d), lambda i: (0, 0)), pl.BlockSpec((ROWS, d), lambda i: (i, 0))],
        out_specs=[pl.BlockSpec((ROWS, d), lambda i: (i, 0)), pl.BlockSpec((1, d), lambda i: (0, 0))],
        out_shape=[jax.ShapeDtypeStruct((s, d), F32), jax.ShapeDtypeStruct((1, d), F32)],
        compiler_params=_params(("arbitrary",)),
    )(*dhs, x, gain, dres)


GB = 256


def _gate_fwd(af, bias, *, name):
    s, w = af.shape

    def body(af_ref, b_ref, c_ref, carry_ref):
        @pl.when(pl.program_id(0) == 0)
        def _():
            carry_ref[...] = jnp.zeros_like(carry_ref)

        z = af_ref[...] + b_ref[...]
        lf = jnp.minimum(z, 0.0) - jnp.log(1.0 + jnp.exp(-jnp.abs(z)))
        r_i = lax.broadcasted_iota(jnp.int32, (GB, GB), 0)
        c_i = lax.broadcasted_iota(jnp.int32, (GB, GB), 1)
        tri = (c_i <= r_i).astype(BF16)
        hi, mid, lo = _split3(lf)
        pre = (jnp.dot(tri, hi, preferred_element_type=F32) + jnp.dot(tri, mid, preferred_element_type=F32)
               + jnp.dot(tri, lo, preferred_element_type=F32))
        c_ref[...] = pre + carry_ref[...]
        carry_ref[...] += jnp.sum(lf, axis=0, keepdims=True)

    return pl.pallas_call(
        body, name=name, grid=(s // GB,),
        in_specs=[pl.BlockSpec((GB, w), lambda i: (i, 0)), pl.BlockSpec((1, w), lambda i: (0, 0))],
        out_specs=pl.BlockSpec((GB, w), lambda i: (i, 0)),
        out_shape=jax.ShapeDtypeStruct((s, w), F32),
        scratch_shapes=[pltpu.VMEM((1, w), F32)],
        compiler_params=_params(("arbitrary",)),
    )(af, bias)


def _gate_bwd(dc, af, bias, *, name):
    s, w = af.shape
    nb = s // GB

    def body(dc_ref, af_ref, b_ref, daf_ref, db_ref, carry_ref):
        @pl.when(pl.program_id(0) == 0)
        def _():
            carry_ref[...] = jnp.zeros_like(carry_ref)
            db_ref[...] = jnp.zeros_like(db_ref)

        dcb = dc_ref[...]
        r_i = lax.broadcasted_iota(jnp.int32, (GB, GB), 0)
        c_i = lax.broadcasted_iota(jnp.int32, (GB, GB), 1)
        tri = (c_i >= r_i).astype(BF16)
        hi, mid, lo = _split3(dcb)
        suf = (jnp.dot(tri, hi, preferred_element_type=F32) + jnp.dot(tri, mid, preferred_element_type=F32)
               + jnp.dot(tri, lo, preferred_element_type=F32)) + carry_ref[...]
        carry_ref[...] += jnp.sum(dcb, axis=0, keepdims=True)
        z = af_ref[...] + b_ref[...]
        daf = suf * _sigmoid(-z)
        daf_ref[...] = daf.astype(BF16)
        db_ref[...] += jnp.sum(daf, axis=0, keepdims=True)

    return pl.pallas_call(
        body, name=name, grid=(nb,),
        in_specs=[pl.BlockSpec((GB, w), lambda i: (nb - 1 - i, 0)), pl.BlockSpec((GB, w), lambda i: (nb - 1 - i, 0)),
                  pl.BlockSpec((1, w), lambda i: (0, 0))],
        out_specs=[pl.BlockSpec((GB, w), lambda i: (nb - 1 - i, 0)), pl.BlockSpec((1, w), lambda i: (0, 0))],
        out_shape=[jax.ShapeDtypeStruct((s, w), BF16), jax.ShapeDtypeStruct((1, w), F32)],
        scratch_shapes=[pltpu.VMEM((1, w), F32)],
        compiler_params=_params(("arbitrary",)),
    )(dc, af, bias)


def _rel_index_rows():
    w = np.arange(TOEP)
    wp = np.where(w < CQ, w, w - TOEP)
    return np.stack([np.clip(LEFT_CHUNKS * CHUNK - CQ * j - wp, -REL_CLIP, REL_CLIP) + REL_CLIP
                     for j in range(BAND_TILES)]).astype(np.int32)


def _skew_rows(xw, sign):
    row = lax.broadcasted_iota(jnp.int32, xw.shape, 0)
    for b in range(CQ.bit_length() - 1):
        amt = (1 << b) if sign > 0 else TOEP - (1 << b)
        xw = jnp.where(((row >> b) & 1) == 1, pltpu.roll(xw, amt, 1), xw)
    return xw


REL_PAD = 384


def _rel_onehot():
    return jnp.asarray(_rel_index_rows()[:, :, None] == np.arange(REL_PAD)[None, None, :], BF16)


def _fill_bias_tiles(rel_ref, oh_ref, bias_scr):
    parts = _split3(jnp.broadcast_to(rel_ref[...], (8, REL_PAD)))
    for j in range(BAND_TILES):
        row = sum(lax.dot_general(p, oh_ref[j], NT, preferred_element_type=F32) for p in parts)[0:1]
        bias_scr[j] = _skew_rows(jnp.broadcast_to(row, (CQ, TOEP)), +1)[:, :CQ]


def _fox_scores(s, cq, cr, diagonal):
    s = s + (cq - cr)
    if not diagonal:
        return s
    bq, bk = s.shape
    return jnp.where(lax.broadcasted_iota(jnp.int32, (bq, bk), 1) <= lax.broadcasted_iota(jnp.int32, (bq, bk), 0), s, NEG)


def _chunk_scores(s, q0, k0, bias):
    bq, bk = s.shape
    qc = (q0 + lax.broadcasted_iota(jnp.int32, (bq, bk), 0)) >> 6
    kc = (k0 + lax.broadcasted_iota(jnp.int32, (bq, bk), 1)) >> 6
    return jnp.where((kc <= qc) & (kc >= qc - LEFT_CHUNKS), s + bias, NEG)


def _softmax_fwd(qkv, gate, aux, *, mode, width, name, ride=None):
    s = qkv.shape[0]
    nh = width // HEAD_DIM
    bq = bk = CQ
    nq = s // bq

    def body(q_ref, k_ref, v_ref, g_ref, *rest):
        if mode == "fox":
            cc_ref, cr_ref, mixed_ref, o_ref, lse_ref = rest
        else:
            rel_ref, oh_ref, mixed_ref, o_ref, lse_ref, bias_scr = rest
        qi = pl.program_id(1)
        q0 = qi * bq
        q = q_ref[...]

        if mode == "chunk":
            @pl.when(qi == 0)
            def _():
                _fill_bias_tiles(rel_ref, oh_ref, bias_scr)

        if mode == "fox":
            cq = cc_ref[...]
        else:
            lo, hi = jnp.maximum(qi - (BAND_TILES - 1), 0), qi + 1

        def step(ki, carry, diagonal=False):
            m, l, acc = carry
            k0 = pl.multiple_of(ki * bk, bk)
            kt = k_ref[pl.ds(k0, bk), :]
            vt = v_ref[pl.ds(k0, bk), :]
            sc = lax.dot_general(q, kt, NT, preferred_element_type=F32) * SCALE
            if mode == "fox":
                sc = _fox_scores(sc, cq, cr_ref[ki], diagonal)
            else:
                sc = _chunk_scores(sc, q0, k0, bias_scr[ki - qi + (BAND_TILES - 1)])
            m_new = jnp.maximum(m, jnp.max(sc, axis=-1, keepdims=True))
            p = jnp.exp(sc - m_new)
            alpha = jnp.exp(m - m_new)
            l = alpha * l + jnp.sum(p, axis=-1, keepdims=True)
            acc = alpha * acc + jnp.dot(p.astype(BF16), vt, preferred_element_type=F32)
            return m_new, l, acc

        init = (jnp.full((bq, 1), NEG, F32), jnp.zeros((bq, 1), F32), jnp.zeros((bq, HEAD_DIM), F32))
        if mode == "fox":
            m, l, acc = step(qi, lax.fori_loop(0, qi, step, init), diagonal=True)
        else:
            m, l, acc = lax.fori_loop(lo, hi, step, init)
        o = acc / l
        g = g_ref[...]
        o_ref[...] = o
        mixed_ref[...] = (o * (g * _sigmoid(g))).astype(BF16)
        lse_ref[...] = m + jnp.log(l)

    head_col = lambda off: pl.BlockSpec((s, HEAD_DIM), lambda h, i: (0, off + h))
    in_specs = [pl.BlockSpec((bq, HEAD_DIM), lambda h, i: (i, h)), head_col(nh), head_col(2 * nh),
                pl.BlockSpec((bq, HEAD_DIM), lambda h, i: (i, h))]
    scratch = []
    if mode == "fox":
        in_specs += [pl.BlockSpec((None, bq, 1), lambda h, i: (h, i, 0)),
                     pl.BlockSpec((None, nq, 1, bk), lambda h, i: (h, 0, 0, 0))]
    else:
        in_specs += [pl.BlockSpec((None, 1, REL_PAD), lambda h, i: (h, 0, 0)),
                     pl.BlockSpec((BAND_TILES, TOEP, REL_PAD), lambda h, i: (0, 0, 0))]
        scratch = [pltpu.VMEM((BAND_TILES, CQ, CQ), F32)]
    outs, carried = _call(
        body, name=name, grid=(nh, nq), in_specs=in_specs,
        out_specs=[pl.BlockSpec((bq, HEAD_DIM), lambda h, i: (i, h)), pl.BlockSpec((bq, HEAD_DIM), lambda h, i: (i, h)),
                   pl.BlockSpec((None, bq, 1), lambda h, i: (h, i, 0))],
        out_shape=[jax.ShapeDtypeStruct((s, width), BF16), jax.ShapeDtypeStruct((s, width), F32),
                   jax.ShapeDtypeStruct((nh, s, 1), F32)],
        scratch=scratch, sem=("parallel", "arbitrary"), args=(qkv, qkv, qkv, gate, *aux), ride=ride)
    return outs if ride is None else (*outs, carried)


def _softmax_bwd(qkv, gate, o, dmixed, dm_off, lse, aux, *, mode, width, name, ride=None):
    s = qkv.shape[0]
    nh = width // HEAD_DIM
    bq = bk = CQ
    nq = s // bq
    dmo = dm_off // HEAD_DIM
    rel_pad = REL_PAD

    def body(q_ref, k_ref, v_ref, g_ref, o_ref, dm_ref, lse_ref, *rest):
        if mode == "fox":
            cc_ref, cr_ref, dq_ref, dk_ref, dv_ref, dg_ref, dc_ref, dcq_ref, dk_scr, dv_scr, dc_scr = rest
        else:
            rel_ref, oh_ref, dq_ref, dk_ref, dv_ref, dg_ref, drel_ref, dk_scr, dv_scr, bias_scr, db_scr = rest
        qi = pl.program_id(1)
        q0 = qi * bq

        @pl.when(qi == 0)
        def _():
            dk_scr[...] = jnp.zeros_like(dk_scr)
            dv_scr[...] = jnp.zeros_like(dv_scr)
            if mode == "fox":
                dc_scr[...] = jnp.zeros_like(dc_scr)
            else:
                db_scr[...] = jnp.zeros_like(db_scr)
                _fill_bias_tiles(rel_ref, oh_ref, bias_scr)

        g = g_ref[...]
        of = o_ref[...]
        dm = dm_ref[...]
        sig = _sigmoid(g)
        do = dm * (g * sig)
        dg_ref[...] = (dm * of * (sig * (1.0 + g * (1.0 - sig)))).astype(BF16)
        delta = jnp.sum(do * of, axis=-1, keepdims=True)
        do_b = do.astype(BF16)
        q = q_ref[...]
        lse_q = lse_ref[...]
        if mode == "fox":
            cq = cc_ref[...]
        else:
            lo, hi = jnp.maximum(qi - (BAND_TILES - 1), 0), qi + 1

        def step(ki, carry, diagonal=False):
            dq, rsum = carry
            k0 = pl.multiple_of(ki * bk, bk)
            kt = k_ref[pl.ds(k0, bk), :]
            vt = v_ref[pl.ds(k0, bk), :]
            sc = lax.dot_general(q, kt, NT, preferred_element_type=F32) * SCALE
            if mode == "fox":
                sc = _fox_scores(sc, cq, cr_ref[ki], diagonal)
            else:
                sc = _chunk_scores(sc, q0, k0, bias_scr[ki - qi + (BAND_TILES - 1)])
            p = jnp.exp(sc - lse_q)
            dp = lax.dot_general(do_b, vt, NT, preferred_element_type=F32)
            ds = p * (dp - delta)
            if mode == "fox":
                dc_scr[ki] += -jnp.sum(ds, axis=0, keepdims=True)
                rsum = rsum + jnp.sum(ds, axis=-1, keepdims=True)
            else:
                db_scr[ki - qi + (BAND_TILES - 1)] += ds
            ds_b = (ds * SCALE).astype(BF16)
            dk_scr[pl.ds(k0, bk), :] += lax.dot_general(ds_b, q, TN, preferred_element_type=F32)
            dv_scr[pl.ds(k0, bk), :] += lax.dot_general(p.astype(BF16), do_b, TN, preferred_element_type=F32)
            return dq + jnp.dot(ds_b, kt, preferred_element_type=F32), rsum

        init = (jnp.zeros((bq, HEAD_DIM), F32), jnp.zeros((bq, 1), F32))
        if mode == "fox":
            dq, rsum = step(qi, lax.fori_loop(0, qi, step, init), diagonal=True)
        else:
            dq, rsum = lax.fori_loop(lo, hi, step, init)
        dq_ref[...] = dq.astype(BF16)
        if mode == "fox":
            dcq_ref[...] = rsum

        @pl.when(qi == nq - 1)
        def _():
            dk_ref[...] = dk_scr[...].astype(BF16)
            dv_ref[...] = dv_scr[...].astype(BF16)
            if mode == "fox":
                dc_ref[...] = dc_scr[...]
            else:
                tot = jnp.zeros((8, rel_pad), F32)
                for j in range(BAND_TILES):
                    wide = jnp.concatenate([db_scr[j], jnp.zeros((CQ, TOEP - CQ), F32)], axis=1)
                    diag = jnp.sum(_skew_rows(wide, -1), axis=0, keepdims=True)
                    for part in _split3(jnp.broadcast_to(diag, (8, TOEP))):
                        tot = tot + jnp.dot(part, oh_ref[j], preferred_element_type=F32)
                drel_ref[...] = tot[0:1, :]

    head_col = lambda off: pl.BlockSpec((s, HEAD_DIM), lambda h, i: (0, off + h))
    qblk = lambda off: pl.BlockSpec((bq, HEAD_DIM), lambda h, i: (i, off + h))
    in_specs = [qblk(0), head_col(nh), head_col(2 * nh), qblk(0), qblk(0), qblk(dmo),
                pl.BlockSpec((None, bq, 1), lambda h, i: (h, i, 0))]
    out_specs = [qblk(0), head_col(0), head_col(0), qblk(0)]
    out_shape = [jax.ShapeDtypeStruct((s, width), BF16)] * 4
    scratch = [pltpu.VMEM((s, HEAD_DIM), F32), pltpu.VMEM((s, HEAD_DIM), F32)]
    if mode == "fox":
        in_specs += [pl.BlockSpec((None, bq, 1), lambda h, i: (h, i, 0)),
                     pl.BlockSpec((None, nq, 1, bk), lambda h, i: (h, 0, 0, 0))]
        out_specs += [pl.BlockSpec((None, nq, 1, bk), lambda h, i: (h, 0, 0, 0)),
                      pl.BlockSpec((None, bq, 1), lambda h, i: (h, i, 0))]
        out_shape += [jax.ShapeDtypeStruct((nh, nq, 1, bk), F32), jax.ShapeDtypeStruct((nh, s, 1), F32)]
        scratch += [pltpu.VMEM((nq, 1, bk), F32)]
    else:
        in_specs += [pl.BlockSpec((None, 1, rel_pad), lambda h, i: (h, 0, 0)),
                     pl.BlockSpec((BAND_TILES, TOEP, rel_pad), lambda h, i: (0, 0, 0))]
        out_specs += [pl.BlockSpec((None, 1, rel_pad), lambda h, i: (h, 0, 0))]
        out_shape += [jax.ShapeDtypeStruct((nh, 1, rel_pad), F32)]
        scratch += [pltpu.VMEM((BAND_TILES, CQ, CQ), F32), pltpu.VMEM((BAND_TILES, CQ, CQ), F32)]
    outs, carried = _call(
        body, name=name, grid=(nh, nq), in_specs=in_specs, out_specs=out_specs, out_shape=out_shape, scratch=scratch,
        sem=("parallel", "arbitrary"), args=(qkv, qkv, qkv, gate, o, dmixed, lse, *aux), ride=ride)
    return outs if ride is None else (*outs, carried)


SBK = 256
SBQ = 2 * SBK
SB_TILES_PER_TRIP = 1
SB_DEAD = -110.0


def _suffix_excl(x, tri):
    r = x.shape[0]
    both = jnp.dot(jnp.concatenate(_split2(x), axis=0), tri, preferred_element_type=F32)
    return both[:r] + both[r:]


def _sb_logits(qh, kt, diag):
    z = lax.dot_general(qh, kt, NT, preferred_element_type=F32) * SCALE
    lom = jnp.minimum(-z, 0.0) - jnp.log(1.0 + jnp.exp(-jnp.abs(z)))
    if diag is not None:
        lom = jnp.where(diag, lom, 0.0)
    return z, lom


def _sb_fwd_tile(qh, kt, vt, tri, diag, run, acc):
    z, lom = _sb_logits(qh, kt, diag)
    a = jnp.exp(lom + z + (_suffix_excl(lom, tri) + run))
    if diag is not None:
        a = jnp.where(diag, a, 0.0)
    acc = acc + jnp.dot(a.astype(BF16), vt, preferred_element_type=F32)
    return run + jnp.sum(lom, axis=-1, keepdims=True), acc


def _sb_alive(run_a, run_b):
    return (jnp.max(jnp.maximum(run_a, run_b)) > SB_DEAD).astype(jnp.int32)


def _sb_fwd(qkv, gate, *, width, name):
    s = qkv.shape[0]
    nh = width // HEAD_DIM
    nq = s // SBQ

    def body(q_ref, k_ref, v_ref, g_ref, mixed_ref, o_ref):
        qi = pl.program_id(1)
        r_i = lax.broadcasted_iota(jnp.int32, (SBK, SBK), 0)
        c_i = lax.broadcasted_iota(jnp.int32, (SBK, SBK), 1)
        tri = (r_i > c_i).astype(BF16)
        diag = c_i < r_i
        q_a = q_ref[0:SBK, :]
        q_b = q_ref[SBK:SBQ, :]

        def kv(tile):
            k0 = pl.multiple_of(tile * SBK, SBK)
            return k_ref[pl.ds(k0, SBK), :], v_ref[pl.ds(k0, SBK), :]

        zero = (jnp.zeros((SBK, 1), F32), jnp.zeros((SBK, HEAD_DIM), F32))
        kt, vt = kv(2 * qi + 1)
        run_b, acc_b = _sb_fwd_tile(q_b, kt, vt, tri, diag, *zero)
        kt, vt = kv(2 * qi)
        run_b, acc_b = _sb_fwd_tile(q_b, kt, vt, tri, None, run_b, acc_b)
        run_a, acc_a = _sb_fwd_tile(q_a, kt, vt, tri, diag, *zero)

        trips = (2 // SB_TILES_PER_TRIP) * qi

        def step(carry):
            t, _, run_a, acc_a, run_b, acc_b = carry
            for u in range(SB_TILES_PER_TRIP):
                kt, vt = kv(2 * qi - 1 - (SB_TILES_PER_TRIP * t + u))
                run_a, acc_a = _sb_fwd_tile(q_a, kt, vt, tri, None, run_a, acc_a)
                run_b, acc_b = _sb_fwd_tile(q_b, kt, vt, tri, None, run_b, acc_b)
            return t + 1, _sb_alive(run_a, run_b), run_a, acc_a, run_b, acc_b

        _, _, _, acc_a, _, acc_b = lax.while_loop(
            lambda c: jnp.logical_and(c[0] < trips, c[1] > 0), step,
            (jnp.int32(0), _sb_alive(run_a, run_b), run_a, acc_a, run_b, acc_b))
        o = jnp.concatenate([acc_a, acc_b], axis=0)
        g = g_ref[...]
        o_ref[...] = o
        mixed_ref[...] = (o * (g * _sigmoid(g))).astype(BF16)

    head_col = lambda off: pl.BlockSpec((s, HEAD_DIM), lambda h, i: (0, off + h))
    qblk = pl.BlockSpec((SBQ, HEAD_DIM), lambda h, i: (i, h))
    return pl.pallas_call(
        body, name=name, grid=(nh, nq), in_specs=[qblk, head_col(nh), head_col(2 * nh), qblk],
        out_specs=[qblk, qblk],
        out_shape=[jax.ShapeDtypeStruct((s, width), BF16), jax.ShapeDtypeStruct((s, width), F32)],
        compiler_params=_params(("parallel", "arbitrary")),
    )(qkv, qkv, qkv, gate)


def _sb_bwd(qkv, gate, o, dmixed, *, width, name):
    s = qkv.shape[0]
    nh = width // HEAD_DIM
    nq = s // SBQ

    def body(q_ref, k_ref, v_ref, g_ref, o_ref, dm_ref, dq_ref, dk_ref, dv_ref, dg_ref, dk_scr, dv_scr):
        qi = pl.program_id(1)

        @pl.when(qi == 0)
        def _():
            dk_scr[...] = jnp.zeros_like(dk_scr)
            dv_scr[...] = jnp.zeros_like(dv_scr)

        g = g_ref[...]
        of = o_ref[...]
        dm = dm_ref[...]
        sig = _sigmoid(g)
        do = dm * (g * sig)
        dg_ref[...] = (dm * of * (sig * (1.0 + g * (1.0 - sig)))).astype(BF16)
        do_b = do.astype(BF16)
        q = q_ref[...]
        r_i = lax.broadcasted_iota(jnp.int32, (SBK, SBK), 0)
        c_i = lax.broadcasted_iota(jnp.int32, (SBK, SBK), 1)
        tri = (r_i > c_i).astype(BF16)
        tri_pre = (r_i < c_i).astype(BF16)
        diag = c_i < r_i
        q_a, q_b = q[0:SBK], q[SBK:SBQ]
        do_a, do_b2 = do_b[0:SBK], do_b[SBK:SBQ]

        def kv(t):
            k0 = pl.multiple_of(t * SBK, SBK)
            return k0, k_ref[pl.ds(k0, SBK), :], v_ref[pl.ds(k0, SBK), :]

        def mass(qh, tile_, dg_):
            return jnp.sum(_sb_logits(qh, kv(tile_)[1], dg_)[1], axis=-1, keepdims=True)

        run_b = mass(q_b, 2 * qi + 1, diag) + mass(q_b, 2 * qi, None)
        run_a = mass(q_a, 2 * qi, diag)
        trips = (2 // SB_TILES_PER_TRIP) * qi

        def scout(carry):
            t, _, run_a, run_b = carry
            for u in range(SB_TILES_PER_TRIP):
                tile_ = 2 * qi - 1 - (SB_TILES_PER_TRIP * t + u)
                run_a = run_a + mass(q_a, tile_, None)
                run_b = run_b + mass(q_b, tile_, None)
            return t + 1, _sb_alive(run_a, run_b), run_a, run_b

        walked, _, run_a, run_b = lax.while_loop(
            lambda c: jnp.logical_and(c[0] < trips, c[1] > 0), scout,
            (jnp.int32(0), _sb_alive(run_a, run_b), run_a, run_b))

        def tile(qh, doh, kt, vt, dg_, carry):
            rem, gpre, dq = carry
            z, lom = _sb_logits(qh, kt, dg_)
            rem = rem - jnp.sum(lom, axis=-1, keepdims=True)
            a = jnp.exp(lom + z + (_suffix_excl(lom, tri) + rem))
            if dg_ is not None:
                a = jnp.where(dg_, a, 0.0)
            gg = lax.dot_general(doh, vt, NT, preferred_element_type=F32) * a
            pre = _suffix_excl(gg, tri_pre) + gpre
            dz = gg * jnp.exp(lom) - pre * jnp.exp(lom + z)
            if dg_ is not None:
                dz = jnp.where(dg_, dz, 0.0)
            dz_b = (dz * SCALE).astype(BF16)
            dq = dq + jnp.dot(dz_b, kt, preferred_element_type=F32)
            return (rem, gpre + jnp.sum(gg, axis=-1, keepdims=True), dq), dz_b, a.astype(BF16)

        def both(t, ca, cb, dg_a):
            k0, kt, vt = kv(t)
            ca, dz_a, a_a = tile(q_a, do_a, kt, vt, dg_a, ca)
            cb, dz_b_, a_b = tile(q_b, do_b2, kt, vt, None, cb)
            dk_scr[pl.ds(k0, SBK), :] += lax.dot_general(jnp.concatenate([dz_a, dz_b_], axis=0), q, TN,
                                                         preferred_element_type=F32)
            dv_scr[pl.ds(k0, SBK), :] += lax.dot_general(jnp.concatenate([a_a, a_b], axis=0), do_b, TN,
                                                         preferred_element_type=F32)
            return ca, cb

        def step(t, carry):
            ca, cb = carry
            for u in range(SB_TILES_PER_TRIP):
                ca, cb = both(SB_TILES_PER_TRIP * t + u, ca, cb, None)
            return ca, cb

        zero = (jnp.zeros((SBK, 1), F32), jnp.zeros((SBK, HEAD_DIM), F32))
        ca, cb = lax.fori_loop(trips - walked, trips, step, ((run_a, *zero), (run_b, *zero)))
        ca, cb = both(2 * qi, ca, cb, diag)
        k0, kt, vt = kv(2 * qi + 1)
        cb, dz_b_, a_b = tile(q_b, do_b2, kt, vt, diag, cb)
        dk_scr[pl.ds(k0, SBK), :] += lax.dot_general(dz_b_, q_b, TN, preferred_element_type=F32)
        dv_scr[pl.ds(k0, SBK), :] += lax.dot_general(a_b, do_b2, TN, preferred_element_type=F32)
        dq_ref[...] = jnp.concatenate([ca[2], cb[2]], axis=0).astype(BF16)

        @pl.when(qi == nq - 1)
        def _():
            dk_ref[...] = dk_scr[...].astype(BF16)
            dv_ref[...] = dv_scr[...].astype(BF16)

    head_col = lambda off: pl.BlockSpec((s, HEAD_DIM), lambda h, i: (0, off + h))
    qblk = pl.BlockSpec((SBQ, HEAD_DIM), lambda h, i: (i, h))
    return pl.pallas_call(
        body, name=name, grid=(nh, nq),
        in_specs=[qblk, head_col(nh), head_col(2 * nh), qblk, qblk, qblk],
        out_specs=[qblk, head_col(0), head_col(0), qblk],
        out_shape=[jax.ShapeDtypeStruct((s, width), BF16)] * 4,
        scratch_shapes=[pltpu.VMEM((s, HEAD_DIM), F32), pltpu.VMEM((s, HEAD_DIM), F32)],
        compiler_params=_params(("parallel", "arbitrary")),
    )(qkv, qkv, qkv, gate, o, dmixed)


HBM = pl.BlockSpec(memory_space=pl.ANY)
MESH = pl.DeviceIdType.MESH


def _all_gather(shard, *, name):
    r, c_ = shard.shape

    def body(x_ref, out_ref, send_sems, recv_sems, local_sem):
        x, y, c = lax.axis_index("x"), lax.axis_index("y"), lax.axis_index("c")
        me, sibling = (x, y, c), (x, y, 1 - c)
        chips = [(1 - x, y), (x, 1 - y), (1 - x, 1 - y)]

        def slot(px, py, pc):
            return out_ref.at[4 * px + 2 * py + pc]

        def copy(k, block, to, src=None):
            return pltpu.make_async_remote_copy(
                src_ref=slot(*block) if src is None else src, dst_ref=slot(*block),
                send_sem=send_sems.at[k], recv_sem=recv_sems.at[k], device_id=to, device_id_type=MESH)

        mine = pltpu.make_async_copy(x_ref, slot(*me), local_sem)
        mine.start()
        first = [copy(0, me, sibling, src=x_ref)]
        first += [copy(1 + j, me, (*chip, c), src=x_ref) for j, chip in enumerate(chips)]
        for cp in first:
            cp.start()
        passed = [copy(4 + j, (*chip, c), sibling) for j, chip in enumerate(chips)]
        for j, chip in enumerate(chips):
            copy(1 + j, (*chip, c), me).wait_recv()
            passed[j].start()
        copy(0, sibling, me).wait_recv()
        for j, chip in enumerate(chips):
            copy(4 + j, (*chip, 1 - c), me).wait_recv()
        for cp in first + passed:
            cp.wait_send()
        mine.wait()

    return pl.pallas_call(
        body, name=name, in_specs=[HBM], out_specs=HBM,
        out_shape=jax.ShapeDtypeStruct((N_DEV, r, c_), shard.dtype),
        scratch_shapes=[pltpu.SemaphoreType.DMA((7,)), pltpu.SemaphoreType.DMA((7,)), pltpu.SemaphoreType.DMA],
    )(shard)


class _Ride:
    def __init__(self, src, *, gather, chips=False):
        self.src, self.gather, self.chips = src, gather, chips
        n = N_DEV // 2 if chips else N_DEV
        self.out_shape = jax.ShapeDtypeStruct((n, *(src.shape if gather else src.shape[1:])), src.dtype)
        self.scratch = [pltpu.SemaphoreType.DMA((7,)), pltpu.SemaphoreType.DMA((7,)), pltpu.SemaphoreType.DMA]

    def _copies(self, src_ref, out_ref, send_sems, recv_sems, local_sem):
        x, y, c = lax.axis_index("x"), lax.axis_index("y"), lax.axis_index("c")
        slot = (lambda px, py, pc: 2 * px + py) if self.chips else (lambda px, py, pc: 4 * px + 2 * py + pc)
        me = slot(x, y, c)
        pick = (lambda j: src_ref) if self.gather else (lambda j: src_ref.at[j])
        mine = pltpu.make_async_copy(pick(me), out_ref.at[me], local_sem)
        copies = []
        for k in range(2 if self.chips else 1, N_DEV, 2 if self.chips else 1):
            px, py, pc = x ^ ((k >> 2) & 1), y ^ ((k >> 1) & 1), c ^ (k & 1)
            copies.append(pltpu.make_async_remote_copy(
                src_ref=pick(slot(px, py, pc)), dst_ref=out_ref.at[me],
                send_sem=send_sems.at[k - 1], recv_sem=recv_sems.at[k - 1],
                device_id=(px, py, pc), device_id_type=MESH))
        return mine, copies

    def start(self, *refs):
        mine, copies = self._copies(*refs)
        mine.start()
        for cp in copies:
            cp.start()

    def finish(self, *refs):
        mine, copies = self._copies(*refs)
        for cp in copies:
            cp.wait_recv()
        for cp in copies:
            cp.wait_send()
        mine.wait()


def _exchange(src, *, gather, name):
    ride = _Ride(src, gather=gather)

    def body(*refs):
        ride.start(*refs)
        ride.finish(*refs)

    return pl.pallas_call(body, name=name, in_specs=[HBM], out_specs=HBM, out_shape=ride.out_shape,
                          scratch_shapes=ride.scratch)(src)


def _sibling_swap(slab, *, name):
    def body(src_ref, out_ref, send_sem, recv_sem):
        x, y, c = lax.axis_index("x"), lax.axis_index("y"), lax.axis_index("c")
        cp = pltpu.make_async_remote_copy(src_ref=src_ref, dst_ref=out_ref, send_sem=send_sem, recv_sem=recv_sem,
                                          device_id=(x, y, 1 - c), device_id_type=MESH)
        cp.start()
        cp.wait()

    return pl.pallas_call(body, name=name, in_specs=[HBM], out_specs=HBM,
                          out_shape=jax.ShapeDtypeStruct(slab.shape, slab.dtype),
                          scratch_shapes=[pltpu.SemaphoreType.DMA, pltpu.SemaphoreType.DMA])(slab)


def _pair_sum(a, b, *, name, rows):
    n, r, c_ = a.shape
    rows = min(rows, r)
    assert r % rows == 0

    def body(a_ref, b_ref, o_ref):
        o_ref[...] = (a_ref[...].astype(F32) + b_ref[...].astype(F32)).astype(o_ref.dtype)

    blk = pl.BlockSpec((1, rows, c_), lambda j, i: (j, i, 0))
    return pl.pallas_call(body, name=name, grid=(n, r // rows), in_specs=[blk, blk], out_specs=blk,
                          out_shape=jax.ShapeDtypeStruct(a.shape, a.dtype),
                          compiler_params=_params(("parallel", "parallel")))(a, b)


def _presum_on_chip(parts, *, name):
    _, r, c_ = parts.shape
    c = lax.axis_index("c")
    by_core = parts.reshape(N_DEV // 2, 2, r, c_)
    keep = lax.dynamic_index_in_dim(by_core, c, axis=1, keepdims=False)
    give = lax.dynamic_index_in_dim(by_core, 1 - c, axis=1, keepdims=False)
    return _pair_sum(keep, _sibling_swap(give, name=name + "_swap"), name=name + "_sum", rows=256)


def _call(body, *, name, grid, in_specs, out_specs, out_shape, scratch, sem, args, ride=None):
    if ride is None:
        outs = pl.pallas_call(body, name=name, grid=grid, in_specs=in_specs, out_specs=out_specs, out_shape=out_shape,
                              scratch_shapes=scratch, compiler_params=_params(sem))(*args)
        return list(outs), None
    n_in, n_out = len(in_specs), len(out_specs)

    def carrying(*refs):
        ins, src_ref = refs[:n_in], refs[n_in]
        outs, dst_ref = refs[n_in + 1:n_in + 1 + n_out], refs[n_in + 1 + n_out]
        rest = refs[n_in + 2 + n_out:]
        own, sems = rest[:len(rest) - 3], rest[len(rest) - 3:]
        ids = [pl.program_id(a) for a in range(len(grid))]
        first = functools.reduce(jnp.logical_and, [i == 0 for i in ids])
        last = functools.reduce(jnp.logical_and, [i == n - 1 for i, n in zip(ids, grid)])

        @pl.when(first)
        def _():
            ride.start(src_ref, dst_ref, *sems)

        body(*ins, *outs, *own)

        @pl.when(last)
        def _():
            ride.finish(src_ref, dst_ref, *sems)

    outs = pl.pallas_call(
        carrying, name=name, grid=grid, in_specs=[*in_specs, HBM], out_specs=[*out_specs, HBM],
        out_shape=[*out_shape, ride.out_shape], scratch_shapes=[*scratch, *ride.scratch],
        compiler_params=_params(("arbitrary",) * len(grid)))(*args, ride.src)
    return list(outs[:-1]), outs[-1]


def _adamw(parts, w, m, v, *, name, rows):
    r, c_ = w.shape
    rows = min(rows, r)
    assert r % rows == 0
    c1 = 1.0 / (1.0 - ADAM_B1 ** ADAM_STEP)
    c2 = 1.0 / (1.0 - ADAM_B2 ** ADAM_STEP)

    def body(p_ref, w_ref, m_ref, v_ref, g_ref, d_ref, nm_ref, nv_ref):
        g = p_ref[0].astype(F32)
        for i in range(1, parts.shape[0]):
            g = g + p_ref[i].astype(F32)
        nm = ADAM_B1 * m_ref[...] + (1.0 - ADAM_B1) * g
        nv = ADAM_B2 * v_ref[...] + (1.0 - ADAM_B2) * (g * g)
        g_ref[...] = g
        nm_ref[...] = nm
        nv_ref[...] = nv
        d_ref[...] = -ADAM_LR * ((nm * c1) / (jnp.sqrt(nv * c2) + ADAM_EPS) + ADAM_WD * w_ref[...])

    blk = pl.BlockSpec((rows, c_), lambda i: (i, 0))
    return pl.pallas_call(
        body, name=name, grid=(r // rows,),
        in_specs=[pl.BlockSpec((parts.shape[0], rows, c_), lambda i: (0, i, 0)), blk, blk, blk],
        out_specs=[blk] * 4, out_shape=[jax.ShapeDtypeStruct((r, c_), F32)] * 4,
        compiler_params=_params(("parallel",)),
    )(parts, w, m, v)


def _pad_cols(a, n):
    return jnp.pad(a, ((0, 0), (0, n - a.shape[1])))


def _fox_dc(shares, nh, s):
    key_side, query_side = shares
    return _pad_cols(jnp.transpose(key_side.reshape(nh, s) + query_side.reshape(nh, s)), 128)


def _local_step(x, target, norm_pre, norm_post, w_in_e, w_f, b_f, rel_bias, sh_out_e, sh_in_o, sh_out_o):
    s, d = x.shape
    wa = d // 2
    nha = wa // HEAD_DIM
    nq = s // CQ
    gather = lambda shard: _Ride(shard, gather=True)
    scatter = lambda parts: _Ride(parts, gather=False)

    h0 = _rms_fwd(x, norm_pre[0:1], name="rms_pre0")
    proj = lambda w, off, n, dt, nm, **kw: _matmul(h0, w, mode="nn", m=s, n=n, k=d, out_dtype=dt, name=nm,
                                                   b_off=(0, off), **kw)
    qkv_a, w_out_e = proj(w_in_e, 0, 3 * wa, BF16, "proj_qkv_a", ride=gather(sh_out_e))
    w_out_e = w_out_e.reshape(d, d)
    g_a = proj(w_in_e, 3 * wa, wa, F32, "proj_gate_a")
    qkv_b = proj(w_in_e, 4 * wa, 3 * wa, BF16, "proj_qkv_b")
    g_b = proj(w_in_e, 7 * wa, wa, F32, "proj_gate_b")
    af = _matmul(h0, w_f, mode="nn", m=s, n=128, k=d, out_dtype=F32, name="proj_forget")
    bias128 = _pad_cols(b_f, 128)
    cum = _gate_fwd(af, bias128, name="forget_cumsum")
    c_t = jnp.transpose(cum[:, :nha])
    c_col = c_t.reshape(nha, s, 1)
    c_row = c_t.reshape(nha, nq, 1, CQ)
    mixed_a, o_a, lse_a, w_in_o = _softmax_fwd(qkv_a, g_a, (c_col, c_row), mode="fox", width=wa, name="fox_fwd",
                                               ride=gather(sh_in_o))
    w_in_o = jnp.transpose(w_in_o, (1, 0, 2)).reshape(d, 4 * d)
    rel_aux = (_pad_cols(rel_bias, REL_PAD).reshape(nha, 1, REL_PAD), _rel_onehot())
    mixed_b, o_b, lse_b, w_out_o = _softmax_fwd(qkv_b, g_b, rel_aux, mode="chunk", width=wa, name="chunk_fwd",
                                                ride=gather(sh_out_o))
    w_out_o = w_out_o.reshape(d, d)
    mixed0 = jnp.concatenate([mixed_a, mixed_b], axis=1)
    y0 = _matmul(mixed0, w_out_e, mode="nn", m=s, n=d, k=d, out_dtype=F32, name="out_proj0")
    x1 = _post_fwd(x, y0, norm_post[0:1], name="post0")

    h1 = _rms_fwd(x1, norm_pre[1:2], name="rms_pre1")
    qkv_c = _matmul(h1, w_in_o, mode="nn", m=s, n=3 * d, k=d, out_dtype=BF16, name="proj_qkv_c")
    g_c = _matmul(h1, w_in_o, mode="nn", m=s, n=d, k=d, out_dtype=F32, name="proj_gate_c", b_off=(0, 3 * d))
    mixed1, o_c = _sb_fwd(qkv_c, g_c, width=d, name="sb_fwd")
    y1 = _matmul(mixed1, w_out_o, mode="nn", m=s, n=d, k=d, out_dtype=F32, name="out_proj1")
    x2 = _post_fwd(x1, y1, norm_post[1:2], name="post1")

    loss, dx2 = _loss_head(x2, target, name="loss_head")

    dy1, dgpost1 = _post_bwd(dx2, y1, norm_post[1:2], name="post_bwd1")
    dmixed1 = _matmul(dy1, w_out_o, mode="nt", m=s, n=d, k=d, out_dtype=F32, name="dmixed1")
    dw_out_o = _matmul(mixed1, dy1, mode="tn", m=d, n=d, k=s, out_dtype=BF16, name="dw_out1")
    dq_c, dk_c, dv_c, dg_c = _sb_bwd(qkv_c, g_c, o_c, dmixed1, width=d, name="sb_bwd")
    dproj1 = jnp.concatenate([dq_c, dk_c, dv_c, dg_c], axis=1)
    dh1 = _matmul(dproj1, w_in_o, mode="nt", m=s, n=d, k=4 * d, out_dtype=F32, name="dh1")
    dw_in_o = _matmul(h1, dproj1, mode="tn", m=d, n=4 * d, k=s, out_dtype=BF16, name="dw_in1")
    dx1, dgpre1 = _pre_bwd((dh1,), x1, norm_pre[1:2], dx2, name="pre_bwd1")

    dy0, dgpost0 = _post_bwd(dx1, y0, norm_post[0:1], name="post_bwd0")
    dmixed0 = _matmul(dy0, w_out_e, mode="nt", m=s, n=d, k=d, out_dtype=F32, name="dmixed0")
    dw_out_e = _matmul(mixed0, dy0, mode="tn", m=d, n=d, k=s, out_dtype=BF16, name="dw_out0")
    parts_in_o = jnp.transpose(dw_in_o.reshape(d, N_DEV, 4 * d // N_DEV), (1, 0, 2))
    chip_in_o = _presum_on_chip(parts_in_o, name="rs_w_in_odd")
    dq_a, dk_a, dv_a, dg_a, *dc_shares, got_in_o = _softmax_bwd(
        qkv_a, g_a, o_a, dmixed0, 0, lse_a, (c_col, c_row), mode="fox", width=wa, name="fox_bwd",
        ride=_Ride(chip_in_o, gather=False, chips=True))
    dq_b, dk_b, dv_b, dg_b, drel, got_out_o = _softmax_bwd(
        qkv_b, g_b, o_b, dmixed0, wa, lse_b, rel_aux, mode="chunk", width=wa, name="chunk_bwd",
        ride=scatter(dw_out_o.reshape(N_DEV, d // N_DEV, d)))
    dc = _fox_dc(dc_shares, nha, s)
    daf, dbf = _gate_bwd(dc, af, bias128, name="forget_bwd")
    dproj0 = jnp.concatenate([dq_a, dk_a, dv_a, dg_a, dq_b, dk_b, dv_b, dg_b], axis=1)
    dw_in_e, got_out_e = _matmul(h0, dproj0, mode="tn", m=d, n=8 * wa, k=s, out_dtype=BF16, name="dw_in0",
                                 ride=scatter(dw_out_e.reshape(N_DEV, d // N_DEV, d)))
    dw_f = _matmul(h0, daf, mode="tn", m=d, n=128, k=s, out_dtype=BF16, name="dw_forget")
    dw_e = jnp.concatenate([dw_in_e, dw_f[:, :nha]], axis=1)
    parts_in_e = jnp.transpose(dw_e.reshape(d, N_DEV, dw_e.shape[1] // N_DEV), (1, 0, 2))
    chip_in_e = _presum_on_chip(parts_in_e, name="rs_w_in_even")
    dh0, got_in_e = _matmul(dproj0, w_in_e, mode="nt", m=s, n=d, k=8 * wa, out_dtype=F32, name="dh0_main",
                            ride=_Ride(chip_in_e, gather=False, chips=True))
    dh0f = _matmul(daf, w_f, mode="nt", m=s, n=d, k=128, out_dtype=F32, name="dh0_forget")
    dx0, dgpre0 = _pre_bwd((dh0, dh0f), x, norm_pre[0:1], dx1, name="pre_bwd0")

    small = dict(
        norm_pre=jnp.concatenate([dgpre0, dgpre1], axis=0),
        norm_post=jnp.concatenate([dgpost0, dgpost1], axis=0),
        b_f=dbf[:, :nha], rel_bias=drel[:, 0, :N_REL])
    got = dict(w_in_even=got_in_e, w_out_even=got_out_e, w_in_odd=got_in_o, w_out_odd=got_out_o)
    return loss, dx0, got, small


def _pack_small(norm_pre, norm_post, b_f, rel_bias):
    flat = jnp.concatenate([norm_pre.reshape(-1), norm_post.reshape(-1), b_f.reshape(-1), rel_bias.reshape(-1)])
    n = flat.shape[0]
    rows = -(-n // 128)
    rows = -(-rows // 8) * 8
    return jnp.pad(flat, (0, rows * 128 - n)).reshape(rows, 128)


def _unpack_small(slab, shapes):
    flat = slab.reshape(-1)
    out, off = [], 0
    for shp in shapes:
        n = int(np.prod(shp))
        out.append(flat[off:off + n].reshape(shp))
        off += n
    return out


def kernel(x, norm_pre, norm_post, w_in_even, b_f_even, rel_bias_even, w_out_even, w_in_odd, w_out_odd, loss_target, m_norm_pre, m_norm_post, m_w_in_even, m_b_f_even, m_rel_bias_even, m_w_out_even, m_w_in_odd, m_w_out_odd, v_norm_pre, v_norm_post, v_w_in_even, v_b_f_even, v_rel_bias_even, v_w_out_even, v_w_in_odd, v_w_out_odd):
    _, s, d = x.shape
    wa = d // 2
    nha = wa // HEAD_DIM
    in_e = w_in_even.shape[2] * N_DEV

    w_in_e_all = jnp.transpose(_all_gather(w_in_even[0].astype(BF16), name="ag_w_in_even"), (1, 0, 2)).reshape(d, in_e)
    w_main = w_in_e_all[:, :8 * wa]
    w_f = _pad_cols(w_in_e_all[:, 8 * wa:], 128)

    loss, dx, got, small = _local_step(
        x[0], loss_target[0], norm_pre, norm_post, w_main, w_f, b_f_even, rel_bias_even[0],
        w_out_even[0].astype(BF16), w_in_odd[0].astype(BF16), w_out_odd[0].astype(BF16))

    upd = {}
    upd["w_in_even"] = _adamw(got["w_in_even"], w_in_even[0], m_w_in_even[0], v_w_in_even[0],
                              name="adamw_w_in_even", rows=128)
    upd["w_out_even"] = _adamw(got["w_out_even"], w_out_even[0], m_w_out_even[0], v_w_out_even[0],
                               name="adamw_w_out_even", rows=64)
    upd["w_in_odd"] = _adamw(got["w_in_odd"], w_in_odd[0], m_w_in_odd[0], v_w_in_odd[0],
                             name="adamw_w_in_odd", rows=128)
    upd["w_out_odd"] = _adamw(got["w_out_odd"], w_out_odd[0], m_w_out_odd[0], v_w_out_odd[0],
                              name="adamw_w_out_odd", rows=64)

    shapes = [norm_pre.shape, norm_post.shape, b_f_even.shape, rel_bias_even.shape]
    g_slab = _pack_small(small["norm_pre"], small["norm_post"], small["b_f"], small["rel_bias"])
    parts_small = _exchange(g_slab, gather=True, name="ar_small")
    sm = _adamw(parts_small, _pack_small(norm_pre, norm_post, b_f_even, rel_bias_even),
                _pack_small(m_norm_pre, m_norm_post, m_b_f_even, m_rel_bias_even),
                _pack_small(v_norm_pre, v_norm_post, v_b_f_even, v_rel_bias_even), name="adamw_small", rows=g_slab.shape[0])
    sm = [_unpack_small(a, shapes) for a in sm]

    total = lax.psum(loss[0, 0], ("x", "y", "c"))

    def leaves(kind):
        return (sm[kind][0], sm[kind][1], upd["w_in_even"][kind][None], sm[kind][2], sm[kind][3],
                upd["w_out_even"][kind][None], upd["w_in_odd"][kind][None], upd["w_out_odd"][kind][None])

    return (total, dx[None], *leaves(0), *leaves(1), *leaves(2), *leaves(3))
```

```python
import functools

import numpy as np
import jax
import jax.numpy as jnp
from jax import lax
from jax.experimental import pallas as pl
from jax.experimental.pallas import tpu as pltpu

F32 = jnp.float32
BF16 = jnp.bfloat16

HEAD_DIM = 128
CHUNK = 64
LEFT_CHUNKS = 8
REL_CLIP = 128
N_REL = 2 * REL_CLIP + 1
RMS_EPS = 1e-6
SCALE = HEAD_DIM ** -0.5

ADAM_LR = 0.001
ADAM_B1 = 0.9
ADAM_B2 = 0.999
ADAM_EPS = 1e-08
ADAM_WD = 0.01
ADAM_STEP = 10

N_DEV = 8
V7X_VMEM_LIMIT_BYTES = 56 * 1024 * 1024
NEG = -1e30

NT = (((1,), (1,)), ((), ()))
TN = (((0,), (0,)), ((), ()))
NN = (((1,), (0,)), ((), ()))

CQ = 256
BAND_TILES = 3
TOEP = 2 * CQ
assert (BAND_TILES - 1) * CQ == LEFT_CHUNKS * CHUNK


def _params(sem):
    return pltpu.CompilerParams(dimension_semantics=sem, vmem_limit_bytes=V7X_VMEM_LIMIT_BYTES)


def _split3(x):
    hi = x.astype(BF16)
    r1 = x - hi.astype(F32)
    mid = r1.astype(BF16)
    lo = (r1 - mid.astype(F32)).astype(BF16)
    return hi, mid, lo


def _split2(x):
    hi = x.astype(BF16)
    lo = (x - hi.astype(F32)).astype(BF16)
    return hi, lo


def _sigmoid(g):
    return 1.0 / (1.0 + jnp.exp(-g))


def _tile(n, cap, *offsets):
    if n <= 128:
        return n
    t = (min(cap, n) // 128) * 128
    while n % t or any(o % t for o in offsets):
        t -= 128
    return t


def _matmul(a, b, *, mode, m, n, k, out_dtype, name, a_off=(0, 0), b_off=(0, 0), tm=1024, tn=1024, tk=1024,
            b_groups=None, out_groups=None, ride=None):
    a_m, a_k = (a_off if mode in ("nn", "nt") else a_off[::-1])
    b_k, b_n = (b_off if mode in ("nn", "tn") else b_off[::-1])
    b_group = (b.shape[2],) if b_groups else ()
    tm, tn, tk = _tile(m, tm, a_m), _tile(n, tn, b_n), _tile(k, tk, a_k, b_k)
    if b_groups and mode in ("nn", "tn"):
        tn = _tile(n, tn, b_n, *b_group)
    if b_groups and mode == "nt":
        tk = _tile(k, tk, a_k, b_k, *b_group)
    if out_groups:
        tn = _tile(n, tn, b_n, n // out_groups, *(b_group if mode != "nt" else ()))
    nk = k // tk
    if mode in ("nn", "nt"):
        ao = (a_off[0] // tm, a_off[1] // tk)
        a_spec = pl.BlockSpec((tm, tk), lambda i, j, l: (i + ao[0], l + ao[1]))
    else:
        ao = (a_off[0] // tk, a_off[1] // tm)
        a_spec = pl.BlockSpec((tk, tm), lambda i, j, l: (l + ao[0], i + ao[1]))
    if mode in ("nn", "tn"):
        bo = (b_off[0] // tk, b_off[1] // tn)
        if b_groups:
            per = b.shape[2] // tn
            b_spec = pl.BlockSpec((None, tk, tn), lambda i, j, l: ((j + bo[1]) // per, l + bo[0], (j + bo[1]) % per))
        else:
            b_spec = pl.BlockSpec((tk, tn), lambda i, j, l: (l + bo[0], j + bo[1]))
    else:
        bo = (b_off[0] // tn, b_off[1] // tk)
        if b_groups:
            per = b.shape[2] // tk
            b_spec = pl.BlockSpec((None, tn, tk), lambda i, j, l: ((l + bo[1]) // per, j + bo[0], (l + bo[1]) % per))
        else:
            b_spec = pl.BlockSpec((tn, tk), lambda i, j, l: (j + bo[0], l + bo[1]))
    if out_groups:
        oper = (n // out_groups) // tn
        out_spec = pl.BlockSpec((None, tm, tn), lambda i, j, l: (j // oper, i, j % oper))
        out_shape = jax.ShapeDtypeStruct((out_groups, m, n // out_groups), out_dtype)
    else:
        out_spec = pl.BlockSpec((tm, tn), lambda i, j, l: (i, j))
        out_shape = jax.ShapeDtypeStruct((m, n), out_dtype)
    dn = {"nn": NN, "nt": NT, "tn": TN}[mode]

    def body(a_ref, b_ref, o_ref, acc_ref):
        @pl.when(pl.program_id(2) == 0)
        def _():
            acc_ref[...] = jnp.zeros_like(acc_ref)

        acc_ref[...] += lax.dot_general(a_ref[...], b_ref[...], dn, preferred_element_type=F32)

        @pl.when(pl.program_id(2) == nk - 1)
        def _():
            o_ref[...] = acc_ref[...].astype(out_dtype)

    (out,), carried = _call(
        body, name=name, grid=(m // tm, n // tn, nk), in_specs=[a_spec, b_spec],
        out_specs=[out_spec], out_shape=[out_shape],
        scratch=[pltpu.VMEM((tm, tn), F32)], sem=("parallel", "parallel", "arbitrary"), args=(a, b), ride=ride)
    return out if ride is None else (out, carried)


ROWS = 128


def _rms_fwd(x, gain, *, name):
    s, d = x.shape

    def body(x_ref, g_ref, h_ref):
        xf = x_ref[...]
        r = lax.rsqrt(jnp.mean(xf * xf, axis=-1, keepdims=True) + RMS_EPS)
        h_ref[...] = ((xf * r) * g_ref[...]).astype(BF16)

    return pl.pallas_call(
        body, name=name, grid=(s // ROWS,),
        in_specs=[pl.BlockSpec((ROWS, d), lambda i: (i, 0)), pl.BlockSpec((1, d), lambda i: (0, 0))],
        out_specs=pl.BlockSpec((ROWS, d), lambda i: (i, 0)),
        out_shape=jax.ShapeDtypeStruct((s, d), BF16),
        compiler_params=_params(("parallel",)),
    )(x, gain)


def _post_fwd(x, y, gain, *, name):
    s, d = x.shape

    def body(x_ref, y_ref, g_ref, o_ref):
        yf = y_ref[...]
        r = lax.rsqrt(jnp.mean(yf * yf, axis=-1, keepdims=True) + RMS_EPS)
        o_ref[...] = x_ref[...] + (yf * r) * g_ref[...]

    return pl.pallas_call(
        body, name=name, grid=(s // ROWS,),
        in_specs=[pl.BlockSpec((ROWS, d), lambda i: (i, 0)), pl.BlockSpec((ROWS, d), lambda i: (i, 0)),
                  pl.BlockSpec((1, d), lambda i: (0, 0))],
        out_specs=pl.BlockSpec((ROWS, d), lambda i: (i, 0)),
        out_shape=jax.ShapeDtypeStruct((s, d), F32),
        compiler_params=_params(("parallel",)),
    )(x, y, gain)


def _loss_head(xo, target, *, name):
    s, d = xo.shape
    inv_d = 1.0 / d

    def body(x_ref, t_ref, loss_ref, dx_ref):
        @pl.when(pl.program_id(0) == 0)
        def _():
            loss_ref[...] = jnp.zeros_like(loss_ref)

        e = x_ref[...] - t_ref[...]
        dx_ref[...] = e * inv_d
        loss_ref[...] += 0.5 * jnp.sum(jnp.mean(e * e, axis=-1, keepdims=True), axis=0, keepdims=True)

    return pl.pallas_call(
        body, name=name, grid=(s // ROWS,),
        in_specs=[pl.BlockSpec((ROWS, d), lambda i: (i, 0)), pl.BlockSpec((ROWS, d), lambda i: (i, 0))],
        out_specs=[pl.BlockSpec((1, 1), lambda i: (0, 0)), pl.BlockSpec((ROWS, d), lambda i: (i, 0))],
        out_shape=[jax.ShapeDtypeStruct((1, 1), F32), jax.ShapeDtypeStruct((s, d), F32)],
        compiler_params=_params(("arbitrary",)),
    )(xo, target)


def _post_bwd(dxo, y, gain, *, name):
    s, d = y.shape

    def body(dx_ref, y_ref, g_ref, dy_ref, dg_ref):
        @pl.when(pl.program_id(0) == 0)
        def _():
            dg_ref[...] = jnp.zeros_like(dg_ref)

        yf = y_ref[...]
        dxo_ = dx_ref[...]
        r = lax.rsqrt(jnp.mean(yf * yf, axis=-1, keepdims=True) + RMS_EPS)
        nrm = yf * r
        dg_ref[...] += jnp.sum(dxo_ * nrm, axis=0, keepdims=True)
        dn = dxo_ * g_ref[...]
        dy_ref[...] = (r * (dn - nrm * jnp.mean(dn * nrm, axis=-1, keepdims=True))).astype(BF16)

    return pl.pallas_call(
        body, name=name, grid=(s // ROWS,),
        in_specs=[pl.BlockSpec((ROWS, d), lambda i: (i, 0)), pl.BlockSpec((ROWS, d), lambda i: (i, 0)),
                  pl.BlockSpec((1, d), lambda i: (0, 0))],
        out_specs=[pl.BlockSpec((ROWS, d), lambda i: (i, 0)), pl.BlockSpec((1, d), lambda i: (0, 0))],
        out_shape=[jax.ShapeDtypeStruct((s, d), BF16), jax.ShapeDtypeStruct((1, d), F32)],
        compiler_params=_params(("arbitrary",)),
    )(dxo, y, gain)


def _pre_bwd(dhs, x, gain, dres, *, name):
    s, d = x.shape
    n_dh = len(dhs)

    def body(*refs):
        dh_refs = refs[:n_dh]
        x_ref, g_ref, dr_ref, dx_ref, dg_ref = refs[n_dh:]

        @pl.when(pl.program_id(0) == 0)
        def _():
            dg_ref[...] = jnp.zeros_like(dg_ref)

        xf = x_ref[...]
        dh_ = dh_refs[0][...]
        for extra in dh_refs[1:]:
            dh_ = dh_ + extra[...]
        r = lax.rsqrt(jnp.mean(xf * xf, axis=-1, keepdims=True) + RMS_EPS)
        nrm = xf * r
        dg_ref[...] += jnp.sum(dh_ * nrm, axis=0, keepdims=True)
        dn = dh_ * g_ref[...]
        dx_ref[...] = dr_ref[...] + r * (dn - nrm * jnp.mean(dn * nrm, axis=-1, keepdims=True))

    return pl.pallas_call(
        body, name=name, grid=(s // ROWS,),
        in_specs=[pl.BlockSpec((ROWS, d), lambda i: (i, 0))] * (n_dh + 1)
        + [pl.BlockSpec((1, d), lambda i: (0, 0)), pl.BlockSpec((ROWS, d), lambda i: (i, 0))],
        out_specs=[pl.BlockSpec((ROWS, d), lambda i: (i, 0)), pl.BlockSpec((1, d), lambda i: (0, 0))],
        out_shape=[jax.ShapeDtypeStruct((s, d), F32), jax.ShapeDtypeStruct((1, d), F32)],
        compiler_params=_params(("arbitrary",)),
    )(*dhs, x, gain, dres)


GB = 256


def _gate_fwd(af, bias, *, name):
    s, w = af.shape

    def body(af_ref, b_ref, c_ref, carry_ref):
        @pl.when(pl.program_id(0) == 0)
        def _():
            carry_ref[...] = jnp.zeros_like(carry_ref)

        z = af_ref[...] + b_ref[...]
        lf = jnp.minimum(z, 0.0) - jnp.log(1.0 + jnp.exp(-jnp.abs(z)))
        r_i = lax.broadcasted_iota(jnp.int32, (GB, GB), 0)
        c_i = lax.broadcasted_iota(jnp.int32, (GB, GB), 1)
        tri = (c_i <= r_i).astype(BF16)
        hi, mid, lo = _split3(lf)
        pre = (jnp.dot(tri, hi, preferred_element_type=F32) + jnp.dot(tri, mid, preferred_element_type=F32)
               + jnp.dot(tri, lo, preferred_element_type=F32))
        c_ref[...] = pre + carry_ref[...]
        carry_ref[...] += jnp.sum(lf, axis=0, keepdims=True)

    return pl.pallas_call(
        body, name=name, grid=(s // GB,),
        in_specs=[pl.BlockSpec((GB, w), lambda i: (i, 0)), pl.BlockSpec((1, w), lambda i: (0, 0))],
        out_specs=pl.BlockSpec((GB, w), lambda i: (i, 0)),
        out_shape=jax.ShapeDtypeStruct((s, w), F32),
        scratch_shapes=[pltpu.VMEM((1, w), F32)],
        compiler_params=_params(("arbitrary",)),
    )(af, bias)


def _gate_bwd(dc, af, bias, *, name):
    s, w = af.shape
    nb = s // GB

    def body(dc_ref, af_ref, b_ref, daf_ref, db_ref, carry_ref):
        @pl.when(pl.program_id(0) == 0)
        def _():
            carry_ref[...] = jnp.zeros_like(carry_ref)
            db_ref[...] = jnp.zeros_like(db_ref)

        dcb = dc_ref[...]
        r_i = lax.broadcasted_iota(jnp.int32, (GB, GB), 0)
        c_i = lax.broadcasted_iota(jnp.int32, (GB, GB), 1)
        tri = (c_i >= r_i).astype(BF16)
        hi, mid, lo = _split3(dcb)
        suf = (jnp.dot(tri, hi, preferred_element_type=F32) + jnp.dot(tri, mid, preferred_element_type=F32)
               + jnp.dot(tri, lo, preferred_element_type=F32)) + carry_ref[...]
        carry_ref[...] += jnp.sum(dcb, axis=0, keepdims=True)
        z = af_ref[...] + b_ref[...]
        daf = suf * _sigmoid(-z)
        daf_ref[...] = daf.astype(BF16)
        db_ref[...] += jnp.sum(daf, axis=0, keepdims=True)

    return pl.pallas_call(
        body, name=name, grid=(nb,),
        in_specs=[pl.BlockSpec((GB, w), lambda i: (nb - 1 - i, 0)), pl.BlockSpec((GB, w), lambda i: (nb - 1 - i, 0)),
                  pl.BlockSpec((1, w), lambda i: (0, 0))],
        out_specs=[pl.BlockSpec((GB, w), lambda i: (nb - 1 - i, 0)), pl.BlockSpec((1, w), lambda i: (0, 0))],
        out_shape=[jax.ShapeDtypeStruct((s, w), BF16), jax.ShapeDtypeStruct((1, w), F32)],
        scratch_shapes=[pltpu.VMEM((1, w), F32)],
        compiler_params=_params(("arbitrary",)),
    )(dc, af, bias)


def _rel_index_rows():
    w = np.arange(TOEP)
    wp = np.where(w < CQ, w, w - TOEP)
    return np.stack([np.clip(LEFT_CHUNKS * CHUNK - CQ * j - wp, -REL_CLIP, REL_CLIP) + REL_CLIP
                     for j in range(BAND_TILES)]).astype(np.int32)


def _skew_rows(xw, sign):
    row = lax.broadcasted_iota(jnp.int32, xw.shape, 0)
    for b in range(CQ.bit_length() - 1):
        amt = (1 << b) if sign > 0 else TOEP - (1 << b)
        xw = jnp.where(((row >> b) & 1) == 1, pltpu.roll(xw, amt, 1), xw)
    return xw


REL_PAD = 384


def _rel_onehot():
    return jnp.asarray(_rel_index_rows()[:, :, None] == np.arange(REL_PAD)[None, None, :], BF16)


def _fill_bias_tiles(rel_ref, oh_ref, bias_scr):
    parts = _split3(jnp.broadcast_to(rel_ref[...], (8, REL_PAD)))
    for j in range(BAND_TILES):
        row = sum(lax.dot_general(p, oh_ref[j], NT, preferred_element_type=F32) for p in parts)[0:1]
        bias_scr[j] = _skew_rows(jnp.broadcast_to(row, (CQ, TOEP)), +1)[:, :CQ]


def _fox_scores(s, cq, cr, diagonal):
    s = s + (cq - cr)
    if not diagonal:
        return s
    bq, bk = s.shape
    return jnp.where(lax.broadcasted_iota(jnp.int32, (bq, bk), 1) <= lax.broadcasted_iota(jnp.int32, (bq, bk), 0), s, NEG)


def _chunk_scores(s, q0, k0, bias):
    bq, bk = s.shape
    qc = (q0 + lax.broadcasted_iota(jnp.int32, (bq, bk), 0)) >> 6
    kc = (k0 + lax.broadcasted_iota(jnp.int32, (bq, bk), 1)) >> 6
    return jnp.where((kc <= qc) & (kc >= qc - LEFT_CHUNKS), s + bias, NEG)


def _softmax_fwd(qkv, gate, aux, *, mode, width, name, ride=None):
    s = qkv.shape[0]
    nh = width // HEAD_DIM
    bq = bk = CQ
    nq = s // bq

    def body(q_ref, k_ref, v_ref, g_ref, *rest):
        if mode == "fox":
            cc_ref, cr_ref, mixed_ref, o_ref, lse_ref = rest
        else:
            rel_ref, oh_ref, mixed_ref, o_ref, lse_ref, bias_scr = rest
        qi = pl.program_id(1)
        q0 = qi * bq
        q = q_ref[...]

        if mode == "chunk":
            @pl.when(qi == 0)
            def _():
                _fill_bias_tiles(rel_ref, oh_ref, bias_scr)

        if mode == "fox":
            cq = cc_ref[...]
        else:
            lo, hi = jnp.maximum(qi - (BAND_TILES - 1), 0), qi + 1

        def step(ki, carry, diagonal=False):
            m, l, acc = carry
            k0 = pl.multiple_of(ki * bk, bk)
            kt = k_ref[pl.ds(k0, bk), :]
            vt = v_ref[pl.ds(k0, bk), :]
            sc = lax.dot_general(q, kt, NT, preferred_element_type=F32) * SCALE
            if mode == "fox":
                sc = _fox_scores(sc, cq, cr_ref[ki], diagonal)
            else:
                sc = _chunk_scores(sc, q0, k0, bias_scr[ki - qi + (BAND_TILES - 1)])
            m_new = jnp.maximum(m, jnp.max(sc, axis=-1, keepdims=True))
            p = jnp.exp(sc - m_new)
            alpha = jnp.exp(m - m_new)
            l = alpha * l + jnp.sum(p, axis=-1, keepdims=True)
            acc = alpha * acc + jnp.dot(p.astype(BF16), vt, preferred_element_type=F32)
            return m_new, l, acc

        init = (jnp.full((bq, 1), NEG, F32), jnp.zeros((bq, 1), F32), jnp.zeros((bq, HEAD_DIM), F32))
        if mode == "fox":
            m, l, acc = step(qi, lax.fori_loop(0, qi, step, init), diagonal=True)
        else:
            m, l, acc = lax.fori_loop(lo, hi, step, init)
        o = acc / l
        g = g_ref[...]
        o_ref[...] = o
        mixed_ref[...] = (o * (g * _sigmoid(g))).astype(BF16)
        lse_ref[...] = m + jnp.log(l)

    head_col = lambda off: pl.BlockSpec((s, HEAD_DIM), lambda h, i: (0, off + h))
    in_specs = [pl.BlockSpec((bq, HEAD_DIM), lambda h, i: (i, h)), head_col(nh), head_col(2 * nh),
                pl.BlockSpec((bq, HEAD_DIM), lambda h, i: (i, h))]
    scratch = []
    if mode == "fox":
        in_specs += [pl.BlockSpec((None, bq, 1), lambda h, i: (h, i, 0)),
                     pl.BlockSpec((None, nq, 1, bk), lambda h, i: (h, 0, 0, 0))]
    else:
        in_specs += [pl.BlockSpec((None, 1, REL_PAD), lambda h, i: (h, 0, 0)),
                     pl.BlockSpec((BAND_TILES, TOEP, REL_PAD), lambda h, i: (0, 0, 0))]
        scratch = [pltpu.VMEM((BAND_TILES, CQ, CQ), F32)]
    outs, carried = _call(
        body, name=name, grid=(nh, nq), in_specs=in_specs,
        out_specs=[pl.BlockSpec((bq, HEAD_DIM), lambda h, i: (i, h)), pl.BlockSpec((bq, HEAD_DIM), lambda h, i: (i, h)),
                   pl.BlockSpec((None, bq, 1), lambda h, i: (h, i, 0))],
        out_shape=[jax.ShapeDtypeStruct((s, width), BF16), jax.ShapeDtypeStruct((s, width), F32),
                   jax.ShapeDtypeStruct((nh, s, 1), F32)],
        scratch=scratch, sem=("parallel", "arbitrary"), args=(qkv, qkv, qkv, gate, *aux), ride=ride)
    return outs if ride is None else (*outs, carried)


def _softmax_bwd(qkv, gate, o, dmixed, dm_off, lse, aux, *, mode, width, name, ride=None):
    s = qkv.shape[0]
    nh = width // HEAD_DIM
    bq = bk = CQ
    nq = s // bq
    dmo = dm_off // HEAD_DIM
    rel_pad = REL_PAD

    def body(q_ref, k_ref, v_ref, g_ref, o_ref, dm_ref, lse_ref, *rest):
        if mode == "fox":
            cc_ref, cr_ref, dq_ref, dk_ref, dv_ref, dg_ref, dc_ref, dcq_ref, dk_scr, dv_scr, dc_scr = rest
        else:
            rel_ref, oh_ref, dq_ref, dk_ref, dv_ref, dg_ref, drel_ref, dk_scr, dv_scr, bias_scr, db_scr = rest
        qi = pl.program_id(1)
        q0 = qi * bq

        @pl.when(qi == 0)
        def _():
            dk_scr[...] = jnp.zeros_like(dk_scr)
            dv_scr[...] = jnp.zeros_like(dv_scr)
            if mode == "fox":
                dc_scr[...] = jnp.zeros_like(dc_scr)
            else:
                db_scr[...] = jnp.zeros_like(db_scr)
                _fill_bias_tiles(rel_ref, oh_ref, bias_scr)

        g = g_ref[...]
        of = o_ref[...]
        dm = dm_ref[...]
        sig = _sigmoid(g)
        do = dm * (g * sig)
        dg_ref[...] = (dm * of * (sig * (1.0 + g * (1.0 - sig)))).astype(BF16)
        delta = jnp.sum(do * of, axis=-1, keepdims=True)
        do_b = do.astype(BF16)
        q = q_ref[...]
        lse_q = lse_ref[...]
        if mode == "fox":
            cq = cc_ref[...]
        else:
            lo, hi = jnp.maximum(qi - (BAND_TILES - 1), 0), qi + 1

        def step(ki, carry, diagonal=False):
            dq, rsum = carry
            k0 = pl.multiple_of(ki * bk, bk)
            kt = k_ref[pl.ds(k0, bk), :]
            vt = v_ref[pl.ds(k0, bk), :]
            sc = lax.dot_general(q, kt, NT, preferred_element_type=F32) * SCALE
            if mode == "fox":
                sc = _fox_scores(sc, cq, cr_ref[ki], diagonal)
            else:
                sc = _chunk_scores(sc, q0, k0, bias_scr[ki - qi + (BAND_TILES - 1)])
            p = jnp.exp(sc - lse_q)
            dp = lax.dot_general(do_b, vt, NT, preferred_element_type=F32)
            ds = p * (dp - delta)
            if mode == "fox":
                dc_scr[ki] += -jnp.sum(ds, axis=0, keepdims=True)
                rsum = rsum + jnp.sum(ds, axis=-1, keepdims=True)
            else:
                db_scr[ki - qi + (BAND_TILES - 1)] += ds
            ds_b = (ds * SCALE).astype(BF16)
            dk_scr[pl.ds(k0, bk), :] += lax.dot_general(ds_b, q, TN, preferred_element_type=F32)
            dv_scr[pl.ds(k0, bk), :] += lax.dot_general(p.astype(BF16), do_b, TN, preferred_element_type=F32)
            return dq + jnp.dot(ds_b, kt, preferred_element_type=F32), rsum

        init = (jnp.zeros((bq, HEAD_DIM), F32), jnp.zeros((bq, 1), F32))
        if mode == "fox":
            dq, rsum = step(qi, lax.fori_loop(0, qi, step, init), diagonal=True)
        else:
            dq, rsum = lax.fori_loop(lo, hi, step, init)
        dq_ref[...] = dq.astype(BF16)
        if mode == "fox":
            dcq_ref[...] = rsum

        @pl.when(qi == nq - 1)
        def _():
            dk_ref[...] = dk_scr[...].astype(BF16)
            dv_ref[...] = dv_scr[...].astype(BF16)
            if mode == "fox":
                dc_ref[...] = dc_scr[...]
            else:
                tot = jnp.zeros((8, rel_pad), F32)
                for j in range(BAND_TILES):
                    wide = jnp.concatenate([db_scr[j], jnp.zeros((CQ, TOEP - CQ), F32)], axis=1)
                    diag = jnp.sum(_skew_rows(wide, -1), axis=0, keepdims=True)
                    for part in _split3(jnp.broadcast_to(diag, (8, TOEP))):
                        tot = tot + jnp.dot(part, oh_ref[j], preferred_element_type=F32)
                drel_ref[...] = tot[0:1, :]

    head_col = lambda off: pl.BlockSpec((s, HEAD_DIM), lambda h, i: (0, off + h))
    qblk = lambda off: pl.BlockSpec((bq, HEAD_DIM), lambda h, i: (i, off + h))
    in_specs = [qblk(0), head_col(nh), head_col(2 * nh), qblk(0), qblk(0), qblk(dmo),
                pl.BlockSpec((None, bq, 1), lambda h, i: (h, i, 0))]
    out_specs = [qblk(0), head_col(0), head_col(0), qblk(0)]
    out_shape = [jax.ShapeDtypeStruct((s, width), BF16)] * 4
    scratch = [pltpu.VMEM((s, HEAD_DIM), F32), pltpu.VMEM((s, HEAD_DIM), F32)]
    if mode == "fox":
        in_specs += [pl.BlockSpec((None, bq, 1), lambda h, i: (h, i, 0)),
                     pl.BlockSpec((None, nq, 1, bk), lambda h, i: (h, 0, 0, 0))]
        out_specs += [pl.BlockSpec((None, nq, 1, bk), lambda h, i: (h, 0, 0, 0)),
                      pl.BlockSpec((None, bq, 1), lambda h, i: (h, i, 0))]
        out_shape += [jax.ShapeDtypeStruct((nh, nq, 1, bk), F32), jax.ShapeDtypeStruct((nh, s, 1), F32)]
        scratch += [pltpu.VMEM((nq, 1, bk), F32)]
    else:
        in_specs += [pl.BlockSpec((None, 1, rel_pad), lambda h, i: (h, 0, 0)),
                     pl.BlockSpec((BAND_TILES, TOEP, rel_pad), lambda h, i: (0, 0, 0))]
        out_specs += [pl.BlockSpec((None, 1, rel_pad), lambda h, i: (h, 0, 0))]
        out_shape += [jax.ShapeDtypeStruct((nh, 1, rel_pad), F32)]
        scratch += [pltpu.VMEM((BAND_TILES, CQ, CQ), F32), pltpu.VMEM((BAND_TILES, CQ, CQ), F32)]
    outs, carried = _call(
        body, name=name, grid=(nh, nq), in_specs=in_specs, out_specs=out_specs, out_shape=out_shape, scratch=scratch,
        sem=("parallel", "arbitrary"), args=(qkv, qkv, qkv, gate, o, dmixed, lse, *aux), ride=ride)
    return outs if ride is None else (*outs, carried)


SBK = 256
SBQ = 2 * SBK
SB_TILES_PER_TRIP = 1
SB_DEAD = -110.0


def _suffix_excl(x, tri):
    r = x.shape[0]
    both = jnp.dot(jnp.concatenate(_split2(x), axis=0), tri, preferred_element_type=F32)
    return both[:r] + both[r:]


def _sb_logits(qh, kt, diag):
    z = lax.dot_general(qh, kt, NT, preferred_element_type=F32) * SCALE
    lom = jnp.minimum(-z, 0.0) - jnp.log(1.0 + jnp.exp(-jnp.abs(z)))
    if diag is not None:
        lom = jnp.where(diag, lom, 0.0)
    return z, lom


def _sb_fwd_tile(qh, kt, vt, tri, diag, run, acc):
    z, lom = _sb_logits(qh, kt, diag)
    a = jnp.exp(lom + z + (_suffix_excl(lom, tri) + run))
    if diag is not None:
        a = jnp.where(diag, a, 0.0)
    acc = acc + jnp.dot(a.astype(BF16), vt, preferred_element_type=F32)
    return run + jnp.sum(lom, axis=-1, keepdims=True), acc


def _sb_alive(run_a, run_b):
    return (jnp.max(jnp.maximum(run_a, run_b)) > SB_DEAD).astype(jnp.int32)


def _sb_fwd(qkv, gate, *, width, name):
    s = qkv.shape[0]
    nh = width // HEAD_DIM
    nq = s // SBQ

    def body(q_ref, k_ref, v_ref, g_ref, mixed_ref, o_ref):
        qi = pl.program_id(1)
        r_i = lax.broadcasted_iota(jnp.int32, (SBK, SBK), 0)
        c_i = lax.broadcasted_iota(jnp.int32, (SBK, SBK), 1)
        tri = (r_i > c_i).astype(BF16)
        diag = c_i < r_i
        q_a = q_ref[0:SBK, :]
        q_b = q_ref[SBK:SBQ, :]

        def kv(tile):
            k0 = pl.multiple_of(tile * SBK, SBK)
            return k_ref[pl.ds(k0, SBK), :], v_ref[pl.ds(k0, SBK), :]

        zero = (jnp.zeros((SBK, 1), F32), jnp.zeros((SBK, HEAD_DIM), F32))
        kt, vt = kv(2 * qi + 1)
        run_b, acc_b = _sb_fwd_tile(q_b, kt, vt, tri, diag, *zero)
        kt, vt = kv(2 * qi)
        run_b, acc_b = _sb_fwd_tile(q_b, kt, vt, tri, None, run_b, acc_b)
        run_a, acc_a = _sb_fwd_tile(q_a, kt, vt, tri, diag, *zero)

        trips = (2 // SB_TILES_PER_TRIP) * qi

        def step(carry):
            t, _, run_a, acc_a, run_b, acc_b = carry
            for u in range(SB_TILES_PER_TRIP):
                kt, vt = kv(2 * qi - 1 - (SB_TILES_PER_TRIP * t + u))
                run_a, acc_a = _sb_fwd_tile(q_a, kt, vt, tri, None, run_a, acc_a)
                run_b, acc_b = _sb_fwd_tile(q_b, kt, vt, tri, None, run_b, acc_b)
            return t + 1, _sb_alive(run_a, run_b), run_a, acc_a, run_b, acc_b

        _, _, _, acc_a, _, acc_b = lax.while_loop(
            lambda c: jnp.logical_and(c[0] < trips, c[1] > 0), step,
            (jnp.int32(0), _sb_alive(run_a, run_b), run_a, acc_a, run_b, acc_b))
        o = jnp.concatenate([acc_a, acc_b], axis=0)
        g = g_ref[...]
        o_ref[...] = o
        mixed_ref[...] = (o * (g * _sigmoid(g))).astype(BF16)

    head_col = lambda off: pl.BlockSpec((s, HEAD_DIM), lambda h, i: (0, off + h))
    qblk = pl.BlockSpec((SBQ, HEAD_DIM), lambda h, i: (i, h))
    return pl.pallas_call(
        body, name=name, grid=(nh, nq), in_specs=[qblk, head_col(nh), head_col(2 * nh), qblk],
        out_specs=[qblk, qblk],
        out_shape=[jax.ShapeDtypeStruct((s, width), BF16), jax.ShapeDtypeStruct((s, width), F32)],
        compiler_params=_params(("parallel", "arbitrary")),
    )(qkv, qkv, qkv, gate)


def _sb_bwd(qkv, gate, o, dmixed, *, width, name):
    s = qkv.shape[0]
    nh = width // HEAD_DIM
    nq = s // SBQ

    def body(q_ref, k_ref, v_ref, g_ref, o_ref, dm_ref, dq_ref, dk_ref, dv_ref, dg_ref, dk_scr, dv_scr):
        qi = pl.program_id(1)

        @pl.when(qi == 0)
        def _():
            dk_scr[...] = jnp.zeros_like(dk_scr)
            dv_scr[...] = jnp.zeros_like(dv_scr)

        g = g_ref[...]
        of = o_ref[...]
        dm = dm_ref[...]
        sig = _sigmoid(g)
        do = dm * (g * sig)
        dg_ref[...] = (dm * of * (sig * (1.0 + g * (1.0 - sig)))).astype(BF16)
        do_b = do.astype(BF16)
        q = q_ref[...]
        r_i = lax.broadcasted_iota(jnp.int32, (SBK, SBK), 0)
        c_i = lax.broadcasted_iota(jnp.int32, (SBK, SBK), 1)
        tri = (r_i > c_i).astype(BF16)
        tri_pre = (r_i < c_i).astype(BF16)
        diag = c_i < r_i
        q_a, q_b = q[0:SBK], q[SBK:SBQ]
        do_a, do_b2 = do_b[0:SBK], do_b[SBK:SBQ]

        def kv(t):
            k0 = pl.multiple_of(t * SBK, SBK)
            return k0, k_ref[pl.ds(k0, SBK), :], v_ref[pl.ds(k0, SBK), :]

        def mass(qh, tile_, dg_):
            return jnp.sum(_sb_logits(qh, kv(tile_)[1], dg_)[1], axis=-1, keepdims=True)

        run_b = mass(q_b, 2 * qi + 1, diag) + mass(q_b, 2 * qi, None)
        run_a = mass(q_a, 2 * qi, diag)
        trips = (2 // SB_TILES_PER_TRIP) * qi

        def scout(carry):
            t, _, run_a, run_b = carry
            for u in range(SB_TILES_PER_TRIP):
                tile_ = 2 * qi - 1 - (SB_TILES_PER_TRIP * t + u)
                run_a = run_a + mass(q_a, tile_, None)
                run_b = run_b + mass(q_b, tile_, None)
            return t + 1, _sb_alive(run_a, run_b), run_a, run_b

        walked, _, run_a, run_b = lax.while_loop(
            lambda c: jnp.logical_and(c[0] < trips, c[1] > 0), scout,
            (jnp.int32(0), _sb_alive(run_a, run_b), run_a, run_b))

        def tile(qh, doh, kt, vt, dg_, carry):
            rem, gpre, dq = carry
            z, lom = _sb_logits(qh, kt, dg_)
            rem = rem - jnp.sum(lom, axis=-1, keepdims=True)
            a = jnp.exp(lom + z + (_suffix_excl(lom, tri) + rem))
            if dg_ is not None:
                a = jnp.where(dg_, a, 0.0)
            gg = lax.dot_general(doh, vt, NT, preferred_element_type=F32) * a
            pre = _suffix_excl(gg, tri_pre) + gpre
            dz = gg * jnp.exp(lom) - pre * jnp.exp(lom + z)
            if dg_ is not None:
                dz = jnp.where(dg_, dz, 0.0)
            dz_b = (dz * SCALE).astype(BF16)
            dq = dq + jnp.dot(dz_b, kt, preferred_element_type=F32)
            return (rem, gpre + jnp.sum(gg, axis=-1, keepdims=True), dq), dz_b, a.astype(BF16)

        def both(t, ca, cb, dg_a):
            k0, kt, vt = kv(t)
            ca, dz_a, a_a = tile(q_a, do_a, kt, vt, dg_a, ca)
            cb, dz_b_, a_b = tile(q_b, do_b2, kt, vt, None, cb)
            dk_scr[pl.ds(k0, SBK), :] += lax.dot_general(jnp.concatenate([dz_a, dz_b_], axis=0), q, TN,
                                                         preferred_element_type=F32)
            dv_scr[pl.ds(k0, SBK), :] += lax.dot_general(jnp.concatenate([a_a, a_b], axis=0), do_b, TN,
                                                         preferred_element_type=F32)
            return ca, cb

        def step(t, carry):
            ca, cb = carry
            for u in range(SB_TILES_PER_TRIP):
                ca, cb = both(SB_TILES_PER_TRIP * t + u, ca, cb, None)
            return ca, cb

        zero = (jnp.zeros((SBK, 1), F32), jnp.zeros((SBK, HEAD_DIM), F32))
        ca, cb = lax.fori_loop(trips - walked, trips, step, ((run_a, *zero), (run_b, *zero)))
        ca, cb = both(2 * qi, ca, cb, diag)
        k0, kt, vt = kv(2 * qi + 1)
        cb, dz_b_, a_b = tile(q_b, do_b2, kt, vt, diag, cb)
        dk_scr[pl.ds(k0, SBK), :] += lax.dot_general(dz_b_, q_b, TN, preferred_element_type=F32)
        dv_scr[pl.ds(k0, SBK), :] += lax.dot_general(a_b, do_b2, TN, preferred_element_type=F32)
        dq_ref[...] = jnp.concatenate([ca[2], cb[2]], axis=0).astype(BF16)

        @pl.when(qi == nq - 1)
        def _():
            dk_ref[...] = dk_scr[...].astype(BF16)
            dv_ref[...] = dv_scr[...].astype(BF16)

    head_col = lambda off: pl.BlockSpec((s, HEAD_DIM), lambda h, i: (0, off + h))
    qblk = pl.BlockSpec((SBQ, HEAD_DIM), lambda h, i: (i, h))
    return pl.pallas_call(
        body, name=name, grid=(nh, nq),
        in_specs=[qblk, head_col(nh), head_col(2 * nh), qblk, qblk, qblk],
        out_specs=[qblk, head_col(0), head_col(0), qblk],
        out_shape=[jax.ShapeDtypeStruct((s, width), BF16)] * 4,
        scratch_shapes=[pltpu.VMEM((s, HEAD_DIM), F32), pltpu.VMEM((s, HEAD_DIM), F32)],
        compiler_params=_params(("parallel", "arbitrary")),
    )(qkv, qkv, qkv, gate, o, dmixed)


HBM = pl.BlockSpec(memory_space=pl.ANY)
MESH = pl.DeviceIdType.MESH


def _all_gather(shard, *, name):
    r, c_ = shard.shape

    def body(x_ref, out_ref, send_sems, recv_sems, local_sem):
        x, y, c = lax.axis_index("x"), lax.axis_index("y"), lax.axis_index("c")
        me, sibling = (x, y, c), (x, y, 1 - c)
        chips = [(1 - x, y), (x, 1 - y), (1 - x, 1 - y)]

        def slot(px, py, pc):
            return out_ref.at[4 * px + 2 * py + pc]

        def copy(k, block, to, src=None):
            return pltpu.make_async_remote_copy(
                src_ref=slot(*block) if src is None else src, dst_ref=slot(*block),
                send_sem=send_sems.at[k], recv_sem=recv_sems.at[k], device_id=to, device_id_type=MESH)

        mine = pltpu.make_async_copy(x_ref, slot(*me), local_sem)
        mine.start()
        first = [copy(0, me, sibling, src=x_ref)]
        first += [copy(1 + j, me, (*chip, c), src=x_ref) for j, chip in enumerate(chips)]
        for cp in first:
            cp.start()
        passed = [copy(4 + j, (*chip, c), sibling) for j, chip in enumerate(chips)]
        for j, chip in enumerate(chips):
            copy(1 + j, (*chip, c), me).wait_recv()
            passed[j].start()
        copy(0, sibling, me).wait_recv()
        for j, chip in enumerate(chips):
            copy(4 + j, (*chip, 1 - c), me).wait_recv()
        for cp in first + passed:
            cp.wait_send()
        mine.wait()

    return pl.pallas_call(
        body, name=name, in_specs=[HBM], out_specs=HBM,
        out_shape=jax.ShapeDtypeStruct((N_DEV, r, c_), shard.dtype),
        scratch_shapes=[pltpu.SemaphoreType.DMA((7,)), pltpu.SemaphoreType.DMA((7,)), pltpu.SemaphoreType.DMA],
    )(shard)


class _Ride:
    def __init__(self, src, *, gather, chips=False):
        self.src, self.gather, self.chips = src, gather, chips
        n = N_DEV // 2 if chips else N_DEV
        self.out_shape = jax.ShapeDtypeStruct((n, *(src.shape if gather else src.shape[1:])), src.dtype)
        self.scratch = [pltpu.SemaphoreType.DMA((7,)), pltpu.SemaphoreType.DMA((7,)), pltpu.SemaphoreType.DMA]

    def _copies(self, src_ref, out_ref, send_sems, recv_sems, local_sem):
        x, y, c = lax.axis_index("x"), lax.axis_index("y"), lax.axis_index("c")
        slot = (lambda px, py, pc: 2 * px + py) if self.chips else (lambda px, py, pc: 4 * px + 2 * py + pc)
        me = slot(x, y, c)
        pick = (lambda j: src_ref) if self.gather else (lambda j: src_ref.at[j])
        mine = pltpu.make_async_copy(pick(me), out_ref.at[me], local_sem)
        copies = []
        for k in range(2 if self.chips else 1, N_DEV, 2 if self.chips else 1):
            px, py, pc = x ^ ((k >> 2) & 1), y ^ ((k >> 1) & 1), c ^ (k & 1)
            copies.append(pltpu.make_async_remote_copy(
                src_ref=pick(slot(px, py, pc)), dst_ref=out_ref.at[me],
                send_sem=send_sems.at[k - 1], recv_sem=recv_sems.at[k - 1],
                device_id=(px, py, pc), device_id_type=MESH))
        return mine, copies

    def start(self, *refs):
        mine, copies = self._copies(*refs)
        mine.start()
        for cp in copies:
            cp.start()

    def finish(self, *refs):
        mine, copies = self._copies(*refs)
        for cp in copies:
            cp.wait_recv()
        for cp in copies:
            cp.wait_send()
        mine.wait()


def _exchange(src, *, gather, name):
    ride = _Ride(src, gather=gather)

    def body(*refs):
        ride.start(*refs)
        ride.finish(*refs)

    return pl.pallas_call(body, name=name, in_specs=[HBM], out_specs=HBM, out_shape=ride.out_shape,
                          scratch_shapes=ride.scratch)(src)


def _presum_on_chip(parts, *, name, rows=256):
    _, r, c_ = parts.shape
    rows = min(rows, r)
    assert r % rows == 0
    by_core = parts.reshape(N_DEV // 2, 2, r, c_)

    def swap(src_ref, out_ref, send_sem, recv_sem):
        x, y, c = lax.axis_index("x"), lax.axis_index("y"), lax.axis_index("c")
        cp = pltpu.make_async_remote_copy(
            src_ref=src_ref.at[:, 1 - c], dst_ref=out_ref, send_sem=send_sem, recv_sem=recv_sem,
            device_id=(x, y, 1 - c), device_id_type=MESH)
        cp.start()
        cp.wait()

    got = pl.pallas_call(swap, name=name + "_swap", in_specs=[HBM], out_specs=HBM,
                         out_shape=jax.ShapeDtypeStruct((N_DEV // 2, r, c_), parts.dtype),
                         scratch_shapes=[pltpu.SemaphoreType.DMA, pltpu.SemaphoreType.DMA])(by_core)

    def add(core_ref, a_ref, b_ref, o_ref):
        del core_ref
        o_ref[...] = (a_ref[...].astype(F32) + b_ref[...].astype(F32)).astype(o_ref.dtype)

    blk = pl.BlockSpec((1, rows, c_), lambda j, i, core: (j, i, 0))
    mine = pl.BlockSpec((1, None, rows, c_), lambda j, i, core: (j, core[0], i, 0))
    core = jnp.reshape(lax.axis_index("c"), (1,)).astype(jnp.int32)
    return pl.pallas_call(
        add, name=name + "_sum",
        grid_spec=pltpu.PrefetchScalarGridSpec(num_scalar_prefetch=1, grid=(N_DEV // 2, r // rows),
                                               in_specs=[mine, blk], out_specs=blk),
        out_shape=jax.ShapeDtypeStruct((N_DEV // 2, r, c_), parts.dtype),
        compiler_params=_params(("parallel", "parallel")))(core, by_core, got)


def _call(body, *, name, grid, in_specs, out_specs, out_shape, scratch, sem, args, ride=None):
    if ride is None:
        outs = pl.pallas_call(body, name=name, grid=grid, in_specs=in_specs, out_specs=out_specs, out_shape=out_shape,
                              scratch_shapes=scratch, compiler_params=_params(sem))(*args)
        return list(outs), None
    n_in, n_out = len(in_specs), len(out_specs)

    def carrying(*refs):
        ins, src_ref = refs[:n_in], refs[n_in]
        outs, dst_ref = refs[n_in + 1:n_in + 1 + n_out], refs[n_in + 1 + n_out]
        rest = refs[n_in + 2 + n_out:]
        own, sems = rest[:len(rest) - 3], rest[len(rest) - 3:]
        ids = [pl.program_id(a) for a in range(len(grid))]
        first = functools.reduce(jnp.logical_and, [i == 0 for i in ids])
        last = functools.reduce(jnp.logical_and, [i == n - 1 for i, n in zip(ids, grid)])

        @pl.when(first)
        def _():
            ride.start(src_ref, dst_ref, *sems)

        body(*ins, *outs, *own)

        @pl.when(last)
        def _():
            ride.finish(src_ref, dst_ref, *sems)

    outs = pl.pallas_call(
        carrying, name=name, grid=grid, in_specs=[*in_specs, HBM], out_specs=[*out_specs, HBM],
        out_shape=[*out_shape, ride.out_shape], scratch_shapes=[*scratch, *ride.scratch],
        compiler_params=_params(("arbitrary",) * len(grid)))(*args, ride.src)
    return list(outs[:-1]), outs[-1]


def _adamw(parts, w, m, v, *, name, rows):
    r, c_ = w.shape
    rows = min(rows, r)
    assert r % rows == 0
    c1 = 1.0 / (1.0 - ADAM_B1 ** ADAM_STEP)
    c2 = 1.0 / (1.0 - ADAM_B2 ** ADAM_STEP)

    def body(p_ref, w_ref, m_ref, v_ref, g_ref, d_ref, nm_ref, nv_ref):
        g = p_ref[0].astype(F32)
        for i in range(1, parts.shape[0]):
            g = g + p_ref[i].astype(F32)
        nm = ADAM_B1 * m_ref[...] + (1.0 - ADAM_B1) * g
        nv = ADAM_B2 * v_ref[...] + (1.0 - ADAM_B2) * (g * g)
        g_ref[...] = g
        nm_ref[...] = nm
        nv_ref[...] = nv
        d_ref[...] = -ADAM_LR * ((nm * c1) / (jnp.sqrt(nv * c2) + ADAM_EPS) + ADAM_WD * w_ref[...])

    blk = pl.BlockSpec((rows, c_), lambda i: (i, 0))
    return pl.pallas_call(
        body, name=name, grid=(r // rows,),
        in_specs=[pl.BlockSpec((parts.shape[0], rows, c_), lambda i: (0, i, 0)), blk, blk, blk],
        out_specs=[blk] * 4, out_shape=[jax.ShapeDtypeStruct((r, c_), F32)] * 4,
        compiler_params=_params(("parallel",)),
    )(parts, w, m, v)


def _pad_cols(a, n):
    return jnp.pad(a, ((0, 0), (0, n - a.shape[1])))


def _fox_dc(shares, nh, s):
    key_side, query_side = shares
    return _pad_cols(jnp.transpose(key_side.reshape(nh, s) + query_side.reshape(nh, s)), 128)


def _local_step(x, target, norm_pre, norm_post, w_in_e, w_f, b_f, rel_bias, sh_out_e, sh_in_o, sh_out_o):
    s, d = x.shape
    wa = d // 2
    nha = wa // HEAD_DIM
    nq = s // CQ
    gather = lambda shard: _Ride(shard, gather=True)
    scatter = lambda parts: _Ride(parts, gather=False)

    h0 = _rms_fwd(x, norm_pre[0:1], name="rms_pre0")
    proj = lambda w, off, n, dt, nm, **kw: _matmul(h0, w, mode="nn", m=s, n=n, k=d, out_dtype=dt, name=nm,
                                                   b_off=(0, off), **kw)
    qkv_a, w_out_e = proj(w_in_e, 0, 3 * wa, BF16, "proj_qkv_a", ride=gather(sh_out_e))
    w_out_e = w_out_e.reshape(d, d)
    g_a = proj(w_in_e, 3 * wa, wa, F32, "proj_gate_a")
    qkv_b = proj(w_in_e, 4 * wa, 3 * wa, BF16, "proj_qkv_b")
    g_b = proj(w_in_e, 7 * wa, wa, F32, "proj_gate_b")
    af = _matmul(h0, w_f, mode="nn", m=s, n=128, k=d, out_dtype=F32, name="proj_forget")
    bias128 = _pad_cols(b_f, 128)
    cum = _gate_fwd(af, bias128, name="forget_cumsum")
    c_t = jnp.transpose(cum[:, :nha])
    c_col = c_t.reshape(nha, s, 1)
    c_row = c_t.reshape(nha, nq, 1, CQ)
    mixed_a, o_a, lse_a, w_in_o = _softmax_fwd(qkv_a, g_a, (c_col, c_row), mode="fox", width=wa, name="fox_fwd",
                                               ride=gather(sh_in_o))
    rel_aux = (_pad_cols(rel_bias, REL_PAD).reshape(nha, 1, REL_PAD), _rel_onehot())
    mixed_b, o_b, lse_b, w_out_o = _softmax_fwd(qkv_b, g_b, rel_aux, mode="chunk", width=wa, name="chunk_fwd",
                                                ride=gather(sh_out_o))
    w_out_o = w_out_o.reshape(d, d)
    mixed0 = jnp.concatenate([mixed_a, mixed_b], axis=1)
    y0 = _matmul(mixed0, w_out_e, mode="nn", m=s, n=d, k=d, out_dtype=F32, name="out_proj0")
    x1 = _post_fwd(x, y0, norm_post[0:1], name="post0")

    h1 = _rms_fwd(x1, norm_pre[1:2], name="rms_pre1")
    qkv_c = _matmul(h1, w_in_o, mode="nn", m=s, n=3 * d, k=d, out_dtype=BF16, name="proj_qkv_c", b_groups=N_DEV)
    g_c = _matmul(h1, w_in_o, mode="nn", m=s, n=d, k=d, out_dtype=F32, name="proj_gate_c", b_off=(0, 3 * d),
                  b_groups=N_DEV)
    mixed1, o_c = _sb_fwd(qkv_c, g_c, width=d, name="sb_fwd")
    y1 = _matmul(mixed1, w_out_o, mode="nn", m=s, n=d, k=d, out_dtype=F32, name="out_proj1")
    x2 = _post_fwd(x1, y1, norm_post[1:2], name="post1")

    loss, dx2 = _loss_head(x2, target, name="loss_head")

    dy1, dgpost1 = _post_bwd(dx2, y1, norm_post[1:2], name="post_bwd1")
    dmixed1 = _matmul(dy1, w_out_o, mode="nt", m=s, n=d, k=d, out_dtype=F32, name="dmixed1")
    dw_out_o = _matmul(mixed1, dy1, mode="tn", m=d, n=d, k=s, out_dtype=BF16, name="dw_out1")
    dq_c, dk_c, dv_c, dg_c = _sb_bwd(qkv_c, g_c, o_c, dmixed1, width=d, name="sb_bwd")
    dproj1 = jnp.concatenate([dq_c, dk_c, dv_c, dg_c], axis=1)
    dh1 = _matmul(dproj1, w_in_o, mode="nt", m=s, n=d, k=4 * d, out_dtype=F32, name="dh1", b_groups=N_DEV)
    parts_in_o = _matmul(h1, dproj1, mode="tn", m=d, n=4 * d, k=s, out_dtype=BF16, name="dw_in1", out_groups=N_DEV)
    dx1, dgpre1 = _pre_bwd((dh1,), x1, norm_pre[1:2], dx2, name="pre_bwd1")

    dy0, dgpost0 = _post_bwd(dx1, y0, norm_post[0:1], name="post_bwd0")
    dmixed0 = _matmul(dy0, w_out_e, mode="nt", m=s, n=d, k=d, out_dtype=F32, name="dmixed0")
    dw_out_e = _matmul(mixed0, dy0, mode="tn", m=d, n=d, k=s, out_dtype=BF16, name="dw_out0")
    chip_in_o = _presum_on_chip(parts_in_o, name="rs_w_in_odd")
    dq_a, dk_a, dv_a, dg_a, *dc_shares, got_in_o = _softmax_bwd(
        qkv_a, g_a, o_a, dmixed0, 0, lse_a, (c_col, c_row), mode="fox", width=wa, name="fox_bwd",
        ride=_Ride(chip_in_o, gather=False, chips=True))
    dq_b, dk_b, dv_b, dg_b, drel, got_out_o = _softmax_bwd(
        qkv_b, g_b, o_b, dmixed0, wa, lse_b, rel_aux, mode="chunk", width=wa, name="chunk_bwd",
        ride=scatter(dw_out_o.reshape(N_DEV, d // N_DEV, d)))
    dc = _fox_dc(dc_shares, nha, s)
    daf, dbf = _gate_bwd(dc, af, bias128, name="forget_bwd")
    dproj0 = jnp.concatenate([dq_a, dk_a, dv_a, dg_a, dq_b, dk_b, dv_b, dg_b], axis=1)
    dw_in_e, got_out_e = _matmul(h0, dproj0, mode="tn", m=d, n=8 * wa, k=s, out_dtype=BF16, name="dw_in0",
                                 ride=scatter(dw_out_e.reshape(N_DEV, d // N_DEV, d)))
    dw_f = _matmul(h0, daf, mode="tn", m=d, n=128, k=s, out_dtype=BF16, name="dw_forget")
    dw_e = jnp.concatenate([dw_in_e, dw_f[:, :nha]], axis=1)
    parts_in_e = jnp.transpose(dw_e.reshape(d, N_DEV, dw_e.shape[1] // N_DEV), (1, 0, 2))
    chip_in_e = _presum_on_chip(parts_in_e, name="rs_w_in_even")
    dh0, got_in_e = _matmul(dproj0, w_in_e, mode="nt", m=s, n=d, k=8 * wa, out_dtype=F32, name="dh0_main",
                            ride=_Ride(chip_in_e, gather=False, chips=True))
    dh0f = _matmul(daf, w_f, mode="nt", m=s, n=d, k=128, out_dtype=F32, name="dh0_forget")
    dx0, dgpre0 = _pre_bwd((dh0, dh0f), x, norm_pre[0:1], dx1, name="pre_bwd0")

    small = dict(
        norm_pre=jnp.concatenate([dgpre0, dgpre1], axis=0),
        norm_post=jnp.concatenate([dgpost0, dgpost1], axis=0),
        b_f=dbf[:, :nha], rel_bias=drel[:, 0, :N_REL])
    got = dict(w_in_even=got_in_e, w_out_even=got_out_e, w_in_odd=got_in_o, w_out_odd=got_out_o)
    return loss, dx0, got, small


def _pack_small(norm_pre, norm_post, b_f, rel_bias):
    flat = jnp.concatenate([norm_pre.reshape(-1), norm_post.reshape(-1), b_f.reshape(-1), rel_bias.reshape(-1)])
    n = flat.shape[0]
    rows = -(-n // 128)
    rows = -(-rows // 8) * 8
    return jnp.pad(flat, (0, rows * 128 - n)).reshape(rows, 128)


def _unpack_small(slab, shapes):
    flat = slab.reshape(-1)
    out, off = [], 0
    for shp in shapes:
        n = int(np.prod(shp))
        out.append(flat[off:off + n].reshape(shp))
        off += n
    return out


def kernel(x, norm_pre, norm_post, w_in_even, b_f_even, rel_bias_even, w_out_even, w_in_odd, w_out_odd, loss_target, m_norm_pre, m_norm_post, m_w_in_even, m_b_f_even, m_rel_bias_even, m_w_out_even, m_w_in_odd, m_w_out_odd, v_norm_pre, v_norm_post, v_w_in_even, v_b_f_even, v_rel_bias_even, v_w_out_even, v_w_in_odd, v_w_out_odd):
    _, s, d = x.shape
    wa = d // 2
    nha = wa // HEAD_DIM
    in_e = w_in_even.shape[2] * N_DEV

    w_in_e_all = jnp.transpose(_all_gather(w_in_even[0].astype(BF16), name="ag_w_in_even"), (1, 0, 2)).reshape(d, in_e)
    w_main = w_in_e_all[:, :8 * wa]
    w_f = _pad_cols(w_in_e_all[:, 8 * wa:], 128)

    loss, dx, got, small = _local_step(
        x[0], loss_target[0], norm_pre, norm_post, w_main, w_f, b_f_even, rel_bias_even[0],
        w_out_even[0].astype(BF16), w_in_odd[0].astype(BF16), w_out_odd[0].astype(BF16))

    upd = {}
    upd["w_in_even"] = _adamw(got["w_in_even"], w_in_even[0], m_w_in_even[0], v_w_in_even[0],
                              name="adamw_w_in_even", rows=128)
    upd["w_out_even"] = _adamw(got["w_out_even"], w_out_even[0], m_w_out_even[0], v_w_out_even[0],
                               name="adamw_w_out_even", rows=64)
    upd["w_in_odd"] = _adamw(got["w_in_odd"], w_in_odd[0], m_w_in_odd[0], v_w_in_odd[0],
                             name="adamw_w_in_odd", rows=128)
    upd["w_out_odd"] = _adamw(got["w_out_odd"], w_out_odd[0], m_w_out_odd[0], v_w_out_odd[0],
                              name="adamw_w_out_odd", rows=64)

    shapes = [norm_pre.shape, norm_post.shape, b_f_even.shape, rel_bias_even.shape]
    g_slab = _pack_small(small["norm_pre"], small["norm_post"], small["b_f"], small["rel_bias"])
    parts_small = _exchange(g_slab, gather=True, name="ar_small")
    sm = _adamw(parts_small, _pack_small(norm_pre, norm_post, b_f_even, rel_bias_even),
                _pack_small(m_norm_pre, m_norm_post, m_b_f_even, m_rel_bias_even),
                _pack_small(v_norm_pre, v_norm_post, v_b_f_even, v_rel_bias_even), name="adamw_small", rows=g_slab.shape[0])
    sm = [_unpack_small(a, shapes) for a in sm]

    total = lax.psum(loss[0, 0], ("x", "y", "c"))

    def leaves(kind):
        return (sm[kind][0], sm[kind][1], upd["w_in_even"][kind][None], sm[kind][2], sm[kind][3],
                upd["w_out_even"][kind][None], upd["w_in_odd"][kind][None], upd["w_out_odd"][kind][None])

    return (total, dx[None], *leaves(0), *leaves(1), *leaves(2), *leaves(3))
```

```python
import functools

import numpy as np
import jax
import jax.numpy as jnp
from jax import lax
from jax.experimental import pallas as pl
from jax.experimental.pallas import tpu as pltpu

F32 = jnp.float32
BF16 = jnp.bfloat16

HEAD_DIM = 128
CHUNK = 64
LEFT_CHUNKS = 8
REL_CLIP = 128
N_REL = 2 * REL_CLIP + 1
RMS_EPS = 1e-6
SCALE = HEAD_DIM ** -0.5

ADAM_LR = 0.001
ADAM_B1 = 0.9
ADAM_B2 = 0.999
ADAM_EPS = 1e-08
ADAM_WD = 0.01
ADAM_STEP = 10

N_DEV = 8
V7X_VMEM_LIMIT_BYTES = 56 * 1024 * 1024
NEG = -1e30

NT = (((1,), (1,)), ((), ()))
TN = (((0,), (0,)), ((), ()))
NN = (((1,), (0,)), ((), ()))

CQ = 256
BAND_TILES = 3
TOEP = 2 * CQ
assert (BAND_TILES - 1) * CQ == LEFT_CHUNKS * CHUNK


def _params(sem):
    return pltpu.CompilerParams(dimension_semantics=sem, vmem_limit_bytes=V7X_VMEM_LIMIT_BYTES)


def _split3(x):
    hi = x.astype(BF16)
    r1 = x - hi.astype(F32)
    mid = r1.astype(BF16)
    lo = (r1 - mid.astype(F32)).astype(BF16)
    return hi, mid, lo


def _split2(x):
    hi = x.astype(BF16)
    lo = (x - hi.astype(F32)).astype(BF16)
    return hi, lo


def _sigmoid(g):
    return 1.0 / (1.0 + jnp.exp(-g))


def _tile(n, cap, *offsets):
    if n <= 128:
        return n
    t = (min(cap, n) // 128) * 128
    while n % t or any(o % t for o in offsets):
        t -= 128
    return t


def _matmul(a, b, *, mode, m, n, k, out_dtype, name, a_off=(0, 0), b_off=(0, 0), tm=1024, tn=1024, tk=1024,
            b_groups=None, out_groups=None, ride=None):
    a_m, a_k = (a_off if mode in ("nn", "nt") else a_off[::-1])
    b_k, b_n = (b_off if mode in ("nn", "tn") else b_off[::-1])
    b_group = (b.shape[2],) if b_groups else ()
    tm, tn, tk = _tile(m, tm, a_m), _tile(n, tn, b_n), _tile(k, tk, a_k, b_k)
    if b_groups and mode in ("nn", "tn"):
        tn = _tile(n, tn, b_n, *b_group)
    if b_groups and mode == "nt":
        tk = _tile(k, tk, a_k, b_k, *b_group)
    if out_groups:
        tn = _tile(n, tn, b_n, n // out_groups, *(b_group if mode != "nt" else ()))
    nk = k // tk
    if mode in ("nn", "nt"):
        ao = (a_off[0] // tm, a_off[1] // tk)
        a_spec = pl.BlockSpec((tm, tk), lambda i, j, l: (i + ao[0], l + ao[1]))
    else:
        ao = (a_off[0] // tk, a_off[1] // tm)
        a_spec = pl.BlockSpec((tk, tm), lambda i, j, l: (l + ao[0], i + ao[1]))
    if mode in ("nn", "tn"):
        bo = (b_off[0] // tk, b_off[1] // tn)
        if b_groups:
            per = b.shape[2] // tn
            b_spec = pl.BlockSpec((None, tk, tn), lambda i, j, l: ((j + bo[1]) // per, l + bo[0], (j + bo[1]) % per))
        else:
            b_spec = pl.BlockSpec((tk, tn), lambda i, j, l: (l + bo[0], j + bo[1]))
    else:
        bo = (b_off[0] // tn, b_off[1] // tk)
        if b_groups:
            per = b.shape[2] // tk
            b_spec = pl.BlockSpec((None, tn, tk), lambda i, j, l: ((l + bo[1]) // per, j + bo[0], (l + bo[1]) % per))
        else:
            b_spec = pl.BlockSpec((tn, tk), lambda i, j, l: (j + bo[0], l + bo[1]))
    if out_groups:
        oper = (n // out_groups) // tn
        out_spec = pl.BlockSpec((None, tm, tn), lambda i, j, l: (j // oper, i, j % oper))
        out_shape = jax.ShapeDtypeStruct((out_groups, m, n // out_groups), out_dtype)
    else:
        out_spec = pl.BlockSpec((tm, tn), lambda i, j, l: (i, j))
        out_shape = jax.ShapeDtypeStruct((m, n), out_dtype)
    dn = {"nn": NN, "nt": NT, "tn": TN}[mode]

    def body(a_ref, b_ref, o_ref, acc_ref):
        @pl.when(pl.program_id(2) == 0)
        def _():
            acc_ref[...] = jnp.zeros_like(acc_ref)

        acc_ref[...] += lax.dot_general(a_ref[...], b_ref[...], dn, preferred_element_type=F32)

        @pl.when(pl.program_id(2) == nk - 1)
        def _():
            o_ref[...] = acc_ref[...].astype(out_dtype)

    (out,), carried = _call(
        body, name=name, grid=(m // tm, n // tn, nk), in_specs=[a_spec, b_spec],
        out_specs=[out_spec], out_shape=[out_shape],
        scratch=[pltpu.VMEM((tm, tn), F32)], sem=("parallel", "parallel", "arbitrary"), args=(a, b), ride=ride)
    return out if ride is None else (out, carried)


ROWS = 128


def _rms_fwd(x, gain, *, name):
    s, d = x.shape

    def body(x_ref, g_ref, h_ref):
        xf = x_ref[...]
        r = lax.rsqrt(jnp.mean(xf * xf, axis=-1, keepdims=True) + RMS_EPS)
        h_ref[...] = ((xf * r) * g_ref[...]).astype(BF16)

    return pl.pallas_call(
        body, name=name, grid=(s // ROWS,),
        in_specs=[pl.BlockSpec((ROWS, d), lambda i: (i, 0)), pl.BlockSpec((1, d), lambda i: (0, 0))],
        out_specs=pl.BlockSpec((ROWS, d), lambda i: (i, 0)),
        out_shape=jax.ShapeDtypeStruct((s, d), BF16),
        compiler_params=_params(("parallel",)),
    )(x, gain)


def _post_fwd(x, y, gain, *, name):
    s, d = x.shape

    def body(x_ref, y_ref, g_ref, o_ref):
        yf = y_ref[...]
        r = lax.rsqrt(jnp.mean(yf * yf, axis=-1, keepdims=True) + RMS_EPS)
        o_ref[...] = x_ref[...] + (yf * r) * g_ref[...]

    return pl.pallas_call(
        body, name=name, grid=(s // ROWS,),
        in_specs=[pl.BlockSpec((ROWS, d), lambda i: (i, 0)), pl.BlockSpec((ROWS, d), lambda i: (i, 0)),
                  pl.BlockSpec((1, d), lambda i: (0, 0))],
        out_specs=pl.BlockSpec((ROWS, d), lambda i: (i, 0)),
        out_shape=jax.ShapeDtypeStruct((s, d), F32),
        compiler_params=_params(("parallel",)),
    )(x, y, gain)


def _loss_head(xo, target, *, name):
    s, d = xo.shape
    inv_d = 1.0 / d

    def body(x_ref, t_ref, loss_ref, dx_ref):
        @pl.when(pl.program_id(0) == 0)
        def _():
            loss_ref[...] = jnp.zeros_like(loss_ref)

        e = x_ref[...] - t_ref[...]
        dx_ref[...] = e * inv_d
        loss_ref[...] += 0.5 * jnp.sum(jnp.mean(e * e, axis=-1, keepdims=True), axis=0, keepdims=True)

    return pl.pallas_call(
        body, name=name, grid=(s // ROWS,),
        in_specs=[pl.BlockSpec((ROWS, d), lambda i: (i, 0)), pl.BlockSpec((ROWS, d), lambda i: (i, 0))],
        out_specs=[pl.BlockSpec((1, 1), lambda i: (0, 0)), pl.BlockSpec((ROWS, d), lambda i: (i, 0))],
        out_shape=[jax.ShapeDtypeStruct((1, 1), F32), jax.ShapeDtypeStruct((s, d), F32)],
        compiler_params=_params(("arbitrary",)),
    )(xo, target)


def _post_bwd(dxo, y, gain, *, name):
    s, d = y.shape

    def body(dx_ref, y_ref, g_ref, dy_ref, dg_ref):
        @pl.when(pl.program_id(0) == 0)
        def _():
            dg_ref[...] = jnp.zeros_like(dg_ref)

        yf = y_ref[...]
        dxo_ = dx_ref[...]
        r = lax.rsqrt(jnp.mean(yf * yf, axis=-1, keepdims=True) + RMS_EPS)
        nrm = yf * r
        dg_ref[...] += jnp.sum(dxo_ * nrm, axis=0, keepdims=True)
        dn = dxo_ * g_ref[...]
        dy_ref[...] = (r * (dn - nrm * jnp.mean(dn * nrm, axis=-1, keepdims=True))).astype(BF16)

    return pl.pallas_call(
        body, name=name, grid=(s // ROWS,),
        in_specs=[pl.BlockSpec((ROWS, d), lambda i: (i, 0)), pl.BlockSpec((ROWS, d), lambda i: (i, 0)),
                  pl.BlockSpec((1, d), lambda i: (0, 0))],
        out_specs=[pl.BlockSpec((ROWS, d), lambda i: (i, 0)), pl.BlockSpec((1, d), lambda i: (0, 0))],
        out_shape=[jax.ShapeDtypeStruct((s, d), BF16), jax.ShapeDtypeStruct((1, d), F32)],
        compiler_params=_params(("arbitrary",)),
    )(dxo, y, gain)


def _pre_bwd(dhs, x, gain, dres, *, name):
    s, d = x.shape
    n_dh = len(dhs)

    def body(*refs):
        dh_refs = refs[:n_dh]
        x_ref, g_ref, dr_ref, dx_ref, dg_ref = refs[n_dh:]

        @pl.when(pl.program_id(0) == 0)
        def _():
            dg_ref[...] = jnp.zeros_like(dg_ref)

        xf = x_ref[...]
        dh_ = dh_refs[0][...]
        for extra in dh_refs[1:]:
            dh_ = dh_ + extra[...]
        r = lax.rsqrt(jnp.mean(xf * xf, axis=-1, keepdims=True) + RMS_EPS)
        nrm = xf * r
        dg_ref[...] += jnp.sum(dh_ * nrm, axis=0, keepdims=True)
        dn = dh_ * g_ref[...]
        dx_ref[...] = dr_ref[...] + r * (dn - nrm * jnp.mean(dn * nrm, axis=-1, keepdims=True))

    return pl.pallas_call(
        body, name=name, grid=(s // ROWS,),
        in_specs=[pl.BlockSpec((ROWS, d), lambda i: (i, 0))] * (n_dh + 1)
        + [pl.BlockSpec((1, d), lambda i: (0, 0)), pl.BlockSpec((ROWS, d), lambda i: (i, 0))],
        out_specs=[pl.BlockSpec((ROWS, d), lambda i: (i, 0)), pl.BlockSpec((1, d), lambda i: (0, 0))],
        out_shape=[jax.ShapeDtypeStruct((s, d), F32), jax.ShapeDtypeStruct((1, d), F32)],
        compiler_params=_params(("arbitrary",)),
    )(*dhs, x, gain, dres)


GB = 256


def _gate_fwd(af, bias, *, name):
    s, w = af.shape

    def body(af_ref, b_ref, c_ref, carry_ref):
        @pl.when(pl.program_id(0) == 0)
        def _():
            carry_ref[...] = jnp.zeros_like(carry_ref)

        z = af_ref[...] + b_ref[...]
        lf = jnp.minimum(z, 0.0) - jnp.log(1.0 + jnp.exp(-jnp.abs(z)))
        r_i = lax.broadcasted_iota(jnp.int32, (GB, GB), 0)
        c_i = lax.broadcasted_iota(jnp.int32, (GB, GB), 1)
        tri = (c_i <= r_i).astype(BF16)
        hi, mid, lo = _split3(lf)
        pre = (jnp.dot(tri, hi, preferred_element_type=F32) + jnp.dot(tri, mid, preferred_element_type=F32)
               + jnp.dot(tri, lo, preferred_element_type=F32))
        c_ref[...] = pre + carry_ref[...]
        carry_ref[...] += jnp.sum(lf, axis=0, keepdims=True)

    return pl.pallas_call(
        body, name=name, grid=(s // GB,),
        in_specs=[pl.BlockSpec((GB, w), lambda i: (i, 0)), pl.BlockSpec((1, w), lambda i: (0, 0))],
        out_specs=pl.BlockSpec((GB, w), lambda i: (i, 0)),
        out_shape=jax.ShapeDtypeStruct((s, w), F32),
        scratch_shapes=[pltpu.VMEM((1, w), F32)],
        compiler_params=_params(("arbitrary",)),
    )(af, bias)


def _gate_bwd(dc, af, bias, *, name):
    s, w = af.shape
    nb = s // GB

    def body(dc_ref, af_ref, b_ref, daf_ref, db_ref, carry_ref):
        @pl.when(pl.program_id(0) == 0)
        def _():
            carry_ref[...] = jnp.zeros_like(carry_ref)
            db_ref[...] = jnp.zeros_like(db_ref)

        dcb = dc_ref[...]
        r_i = lax.broadcasted_iota(jnp.int32, (GB, GB), 0)
        c_i = lax.broadcasted_iota(jnp.int32, (GB, GB), 1)
        tri = (c_i >= r_i).astype(BF16)
        hi, mid, lo = _split3(dcb)
        suf = (jnp.dot(tri, hi, preferred_element_type=F32) + jnp.dot(tri, mid, preferred_element_type=F32)
               + jnp.dot(tri, lo, preferred_element_type=F32)) + carry_ref[...]
        carry_ref[...] += jnp.sum(dcb, axis=0, keepdims=True)
        z = af_ref[...] + b_ref[...]
        daf = suf * _sigmoid(-z)
        daf_ref[...] = daf.astype(BF16)
        db_ref[...] += jnp.sum(daf, axis=0, keepdims=True)

    return pl.pallas_call(
        body, name=name, grid=(nb,),
        in_specs=[pl.BlockSpec((GB, w), lambda i: (nb - 1 - i, 0)), pl.BlockSpec((GB, w), lambda i: (nb - 1 - i, 0)),
                  pl.BlockSpec((1, w), lambda i: (0, 0))],
        out_specs=[pl.BlockSpec((GB, w), lambda i: (nb - 1 - i, 0)), pl.BlockSpec((1, w), lambda i: (0, 0))],
        out_shape=[jax.ShapeDtypeStruct((s, w), BF16), jax.ShapeDtypeStruct((1, w), F32)],
        scratch_shapes=[pltpu.VMEM((1, w), F32)],
        compiler_params=_params(("arbitrary",)),
    )(dc, af, bias)


def _rel_index_rows():
    w = np.arange(TOEP)
    wp = np.where(w < CQ, w, w - TOEP)
    return np.stack([np.clip(LEFT_CHUNKS * CHUNK - CQ * j - wp, -REL_CLIP, REL_CLIP) + REL_CLIP
                     for j in range(BAND_TILES)]).astype(np.int32)


def _skew_rows(xw, sign):
    row = lax.broadcasted_iota(jnp.int32, xw.shape, 0)
    for b in range(CQ.bit_length() - 1):
        amt = (1 << b) if sign > 0 else TOEP - (1 << b)
        xw = jnp.where(((row >> b) & 1) == 1, pltpu.roll(xw, amt, 1), xw)
    return xw


REL_PAD = 384
HP = 2


def _rel_onehot():
    return jnp.asarray(_rel_index_rows()[:, :, None] == np.arange(REL_PAD)[None, None, :], BF16)


def _fill_bias_tiles(rel_ref, oh_ref, bias_scr):
    parts = _split3(jnp.broadcast_to(rel_ref[...], (8, REL_PAD)))
    for j in range(BAND_TILES):
        row = sum(lax.dot_general(p, oh_ref[j], NT, preferred_element_type=F32) for p in parts)[0:1]
        bias_scr[j] = _skew_rows(jnp.broadcast_to(row, (CQ, TOEP)), +1)[:, :CQ]


def _fox_scores(s, cq, cr, diagonal):
    s = s + (cq - cr)
    if not diagonal:
        return s
    bq, bk = s.shape
    return jnp.where(lax.broadcasted_iota(jnp.int32, (bq, bk), 1) <= lax.broadcasted_iota(jnp.int32, (bq, bk), 0), s, NEG)


def _chunk_scores(s, q0, k0, bias):
    bq, bk = s.shape
    qc = (q0 + lax.broadcasted_iota(jnp.int32, (bq, bk), 0)) >> 6
    kc = (k0 + lax.broadcasted_iota(jnp.int32, (bq, bk), 1)) >> 6
    return jnp.where((kc <= qc) & (kc >= qc - LEFT_CHUNKS), s + bias, NEG)


def _softmax_fwd(qkv, gate, aux, *, mode, width, name, ride=None):
    s = qkv.shape[0]
    nh = width // HEAD_DIM
    bq = bk = CQ
    nq = s // bq

    assert nh % HP == 0
    hcol = lambda hh: slice(hh * HEAD_DIM, (hh + 1) * HEAD_DIM)

    def body(q_ref, k_ref, v_ref, g_ref, *rest):
        if mode == "fox":
            cc_ref, cr_ref, mixed_ref, o_ref, lse_ref = rest
        else:
            rel_ref, oh_ref, mixed_ref, o_ref, lse_ref, bias_scr = rest
        qi = pl.program_id(1)
        q0 = qi * bq
        q = q_ref[...]

        if mode == "chunk":
            @pl.when(qi == 0)
            def _():
                for hh in range(HP):
                    _fill_bias_tiles(rel_ref.at[hh], oh_ref, bias_scr.at[hh])

            lo, hi = jnp.maximum(qi - (BAND_TILES - 1), 0), qi + 1

        def step(ki, carry, diagonal=False):
            k0 = pl.multiple_of(ki * bk, bk)
            kt = k_ref[pl.ds(k0, bk), :]
            vt = v_ref[pl.ds(k0, bk), :]
            out = []
            for hh in range(HP):
                m, l, acc = carry[hh]
                sc = lax.dot_general(q[:, hcol(hh)], kt[:, hcol(hh)], NT, preferred_element_type=F32) * SCALE
                if mode == "fox":
                    sc = _fox_scores(sc, cc_ref[hh], cr_ref[hh, ki], diagonal)
                else:
                    sc = _chunk_scores(sc, q0, k0, bias_scr[hh, ki - qi + (BAND_TILES - 1)])
                m_new = jnp.maximum(m, jnp.max(sc, axis=-1, keepdims=True))
                p = jnp.exp(sc - m_new)
                alpha = jnp.exp(m - m_new)
                l = alpha * l + jnp.sum(p, axis=-1, keepdims=True)
                acc = alpha * acc + jnp.dot(p.astype(BF16), vt[:, hcol(hh)], preferred_element_type=F32)
                out.append((m_new, l, acc))
            return tuple(out)

        init = ((jnp.full((bq, 1), NEG, F32), jnp.zeros((bq, 1), F32), jnp.zeros((bq, HEAD_DIM), F32)),) * HP
        if mode == "fox":
            done = step(qi, lax.fori_loop(0, qi, step, init), diagonal=True)
        else:
            done = lax.fori_loop(lo, hi, step, init)
        o = jnp.concatenate([acc / l for _, l, acc in done], axis=1)
        g = g_ref[...]
        o_ref[...] = o
        mixed_ref[...] = (o * (g * _sigmoid(g))).astype(BF16)
        for hh, (m, l, _) in enumerate(done):
            lse_ref[hh] = m + jnp.log(l)

    wide = HP * HEAD_DIM
    head_col = lambda off: pl.BlockSpec((s, wide), lambda h, i: (0, off // HP + h))
    qblk = pl.BlockSpec((bq, wide), lambda h, i: (i, h))
    stat = pl.BlockSpec((HP, bq, 1), lambda h, i: (h, i, 0))
    in_specs = [qblk, head_col(nh), head_col(2 * nh), qblk]
    scratch = []
    if mode == "fox":
        in_specs += [stat, pl.BlockSpec((HP, nq, 1, bk), lambda h, i: (h, 0, 0, 0))]
    else:
        in_specs += [pl.BlockSpec((HP, 1, REL_PAD), lambda h, i: (h, 0, 0)),
                     pl.BlockSpec((BAND_TILES, TOEP, REL_PAD), lambda h, i: (0, 0, 0))]
        scratch = [pltpu.VMEM((HP, BAND_TILES, CQ, CQ), F32)]
    outs, carried = _call(
        body, name=name, grid=(nh // HP, nq), in_specs=in_specs, out_specs=[qblk, qblk, stat],
        out_shape=[jax.ShapeDtypeStruct((s, width), BF16), jax.ShapeDtypeStruct((s, width), F32),
                   jax.ShapeDtypeStruct((nh, s, 1), F32)],
        scratch=scratch, sem=("parallel", "arbitrary"), args=(qkv, qkv, qkv, gate, *aux), ride=ride)
    return outs if ride is None else (*outs, carried)


def _softmax_bwd(qkv, gate, o, dmixed, dm_off, lse, aux, *, mode, width, name, ride=None):
    s = qkv.shape[0]
    nh = width // HEAD_DIM
    bq = bk = CQ
    nq = s // bq
    dmo = dm_off // HEAD_DIM
    rel_pad = REL_PAD
    assert nh % HP == 0 and dmo % HP == 0
    hcol = lambda hh: slice(hh * HEAD_DIM, (hh + 1) * HEAD_DIM)

    def body(q_ref, k_ref, v_ref, g_ref, o_ref, dm_ref, lse_ref, *rest):
        if mode == "fox":
            cc_ref, cr_ref, dq_ref, dk_ref, dv_ref, dg_ref, dc_ref, dcq_ref, dk_scr, dv_scr, dc_scr = rest
        else:
            rel_ref, oh_ref, dq_ref, dk_ref, dv_ref, dg_ref, drel_ref, dk_scr, dv_scr, bias_scr, db_scr = rest
        qi = pl.program_id(1)
        q0 = qi * bq

        @pl.when(qi == 0)
        def _():
            dk_scr[...] = jnp.zeros_like(dk_scr)
            dv_scr[...] = jnp.zeros_like(dv_scr)
            if mode == "fox":
                dc_scr[...] = jnp.zeros_like(dc_scr)
            else:
                db_scr[...] = jnp.zeros_like(db_scr)
                for hh in range(HP):
                    _fill_bias_tiles(rel_ref.at[hh], oh_ref, bias_scr.at[hh])

        g = g_ref[...]
        of = o_ref[...]
        dm = dm_ref[...]
        sig = _sigmoid(g)
        do = dm * (g * sig)
        dg_ref[...] = (dm * of * (sig * (1.0 + g * (1.0 - sig)))).astype(BF16)
        do_o = do * of
        delta = [jnp.sum(do_o[:, hcol(hh)], axis=-1, keepdims=True) for hh in range(HP)]
        do_b = do.astype(BF16)
        q = q_ref[...]
        if mode == "chunk":
            lo, hi = jnp.maximum(qi - (BAND_TILES - 1), 0), qi + 1

        def step(ki, carry, diagonal=False):
            k0 = pl.multiple_of(ki * bk, bk)
            kt = k_ref[pl.ds(k0, bk), :]
            vt = v_ref[pl.ds(k0, bk), :]
            out = []
            for hh in range(HP):
                dq, rsum = carry[hh]
                qh, kh, doh = q[:, hcol(hh)], kt[:, hcol(hh)], do_b[:, hcol(hh)]
                sc = lax.dot_general(qh, kh, NT, preferred_element_type=F32) * SCALE
                if mode == "fox":
                    sc = _fox_scores(sc, cc_ref[hh], cr_ref[hh, ki], diagonal)
                else:
                    sc = _chunk_scores(sc, q0, k0, bias_scr[hh, ki - qi + (BAND_TILES - 1)])
                p = jnp.exp(sc - lse_ref[hh])
                dp = lax.dot_general(doh, vt[:, hcol(hh)], NT, preferred_element_type=F32)
                ds = p * (dp - delta[hh])
                if mode == "fox":
                    dc_scr[hh, ki] += -jnp.sum(ds, axis=0, keepdims=True)
                    rsum = rsum + jnp.sum(ds, axis=-1, keepdims=True)
                else:
                    db_scr[hh, ki - qi + (BAND_TILES - 1)] += ds
                ds_b = (ds * SCALE).astype(BF16)
                dk_scr[pl.ds(k0, bk), hcol(hh)] += lax.dot_general(ds_b, qh, TN, preferred_element_type=F32)
                dv_scr[pl.ds(k0, bk), hcol(hh)] += lax.dot_general(p.astype(BF16), doh, TN, preferred_element_type=F32)
                out.append((dq + jnp.dot(ds_b, kh, preferred_element_type=F32), rsum))
            return tuple(out)

        init = ((jnp.zeros((bq, HEAD_DIM), F32), jnp.zeros((bq, 1), F32)),) * HP
        if mode == "fox":
            done = step(qi, lax.fori_loop(0, qi, step, init), diagonal=True)
        else:
            done = lax.fori_loop(lo, hi, step, init)
        dq_ref[...] = jnp.concatenate([dq for dq, _ in done], axis=1).astype(BF16)
        if mode == "fox":
            for hh, (_, rsum) in enumerate(done):
                dcq_ref[hh] = rsum

        @pl.when(qi == nq - 1)
        def _():
            dk_ref[...] = dk_scr[...].astype(BF16)
            dv_ref[...] = dv_scr[...].astype(BF16)
            if mode == "fox":
                dc_ref[...] = dc_scr[...]
            else:
                for hh in range(HP):
                    tot = jnp.zeros((8, rel_pad), F32)
                    for j in range(BAND_TILES):
                        wide_ = jnp.concatenate([db_scr[hh, j], jnp.zeros((CQ, TOEP - CQ), F32)], axis=1)
                        diag = jnp.sum(_skew_rows(wide_, -1), axis=0, keepdims=True)
                        for part in _split3(jnp.broadcast_to(diag, (8, TOEP))):
                            tot = tot + jnp.dot(part, oh_ref[j], preferred_element_type=F32)
                    drel_ref[hh] = tot[0:1, :]

    wide = HP * HEAD_DIM
    head_col = lambda off: pl.BlockSpec((s, wide), lambda h, i: (0, off // HP + h))
    qblk = lambda off: pl.BlockSpec((bq, wide), lambda h, i: (i, off // HP + h))
    stat = pl.BlockSpec((HP, bq, 1), lambda h, i: (h, i, 0))
    in_specs = [qblk(0), head_col(nh), head_col(2 * nh), qblk(0), qblk(0), qblk(dmo), stat]
    out_specs = [qblk(0), head_col(0), head_col(0), qblk(0)]
    out_shape = [jax.ShapeDtypeStruct((s, width), BF16)] * 4
    scratch = [pltpu.VMEM((s, wide), F32), pltpu.VMEM((s, wide), F32)]
    if mode == "fox":
        rows = pl.BlockSpec((HP, nq, 1, bk), lambda h, i: (h, 0, 0, 0))
        in_specs += [stat, rows]
        out_specs += [rows, stat]
        out_shape += [jax.ShapeDtypeStruct((nh, nq, 1, bk), F32), jax.ShapeDtypeStruct((nh, s, 1), F32)]
        scratch += [pltpu.VMEM((HP, nq, 1, bk), F32)]
    else:
        rel = pl.BlockSpec((HP, 1, rel_pad), lambda h, i: (h, 0, 0))
        in_specs += [rel, pl.BlockSpec((BAND_TILES, TOEP, rel_pad), lambda h, i: (0, 0, 0))]
        out_specs += [rel]
        out_shape += [jax.ShapeDtypeStruct((nh, 1, rel_pad), F32)]
        scratch += [pltpu.VMEM((HP, BAND_TILES, CQ, CQ), F32), pltpu.VMEM((HP, BAND_TILES, CQ, CQ), F32)]
    outs, carried = _call(
        body, name=name, grid=(nh // HP, nq), in_specs=in_specs, out_specs=out_specs, out_shape=out_shape,
        scratch=scratch, sem=("parallel", "arbitrary"), args=(qkv, qkv, qkv, gate, o, dmixed, lse, *aux), ride=ride)
    return outs if ride is None else (*outs, carried)


SBK = 256
SBQ = 2 * SBK
SB_TILES_PER_TRIP = 1
SB_DEAD = -110.0


def _suffix_excl(x, tri):
    r = x.shape[0]
    both = jnp.dot(jnp.concatenate(_split2(x), axis=0), tri, preferred_element_type=F32)
    return both[:r] + both[r:]


def _sb_logits(qh, kt, diag):
    z = lax.dot_general(qh, kt, NT, preferred_element_type=F32) * SCALE
    lom = jnp.minimum(-z, 0.0) - jnp.log(1.0 + jnp.exp(-jnp.abs(z)))
    if diag is not None:
        lom = jnp.where(diag, lom, 0.0)
    return z, lom


def _sb_fwd_tile(qh, kt, vt, tri, diag, run, acc):
    z, lom = _sb_logits(qh, kt, diag)
    a = jnp.exp(lom + z + (_suffix_excl(lom, tri) + run))
    if diag is not None:
        a = jnp.where(diag, a, 0.0)
    acc = acc + jnp.dot(a.astype(BF16), vt, preferred_element_type=F32)
    return run + jnp.sum(lom, axis=-1, keepdims=True), acc


def _sb_alive(run_a, run_b):
    return (jnp.max(jnp.maximum(run_a, run_b)) > SB_DEAD).astype(jnp.int32)


def _sb_fwd(qkv, gate, *, width, name):
    s = qkv.shape[0]
    nh = width // HEAD_DIM
    nq = s // SBQ

    def body(q_ref, k_ref, v_ref, g_ref, mixed_ref, o_ref):
        qi = pl.program_id(1)
        r_i = lax.broadcasted_iota(jnp.int32, (SBK, SBK), 0)
        c_i = lax.broadcasted_iota(jnp.int32, (SBK, SBK), 1)
        tri = (r_i > c_i).astype(BF16)
        diag = c_i < r_i
        q_a = q_ref[0:SBK, :]
        q_b = q_ref[SBK:SBQ, :]

        def kv(tile):
            k0 = pl.multiple_of(tile * SBK, SBK)
            return k_ref[pl.ds(k0, SBK), :], v_ref[pl.ds(k0, SBK), :]

        zero = (jnp.zeros((SBK, 1), F32), jnp.zeros((SBK, HEAD_DIM), F32))
        kt, vt = kv(2 * qi + 1)
        run_b, acc_b = _sb_fwd_tile(q_b, kt, vt, tri, diag, *zero)
        kt, vt = kv(2 * qi)
        run_b, acc_b = _sb_fwd_tile(q_b, kt, vt, tri, None, run_b, acc_b)
        run_a, acc_a = _sb_fwd_tile(q_a, kt, vt, tri, diag, *zero)

        trips = (2 // SB_TILES_PER_TRIP) * qi

        def step(carry):
            t, _, run_a, acc_a, run_b, acc_b = carry
            for u in range(SB_TILES_PER_TRIP):
                kt, vt = kv(2 * qi - 1 - (SB_TILES_PER_TRIP * t + u))
                run_a, acc_a = _sb_fwd_tile(q_a, kt, vt, tri, None, run_a, acc_a)
                run_b, acc_b = _sb_fwd_tile(q_b, kt, vt, tri, None, run_b, acc_b)
            return t + 1, _sb_alive(run_a, run_b), run_a, acc_a, run_b, acc_b

        _, _, _, acc_a, _, acc_b = lax.while_loop(
            lambda c: jnp.logical_and(c[0] < trips, c[1] > 0), step,
            (jnp.int32(0), _sb_alive(run_a, run_b), run_a, acc_a, run_b, acc_b))
        o = jnp.concatenate([acc_a, acc_b], axis=0)
        g = g_ref[...]
        o_ref[...] = o
        mixed_ref[...] = (o * (g * _sigmoid(g))).astype(BF16)

    head_col = lambda off: pl.BlockSpec((s, HEAD_DIM), lambda h, i: (0, off + h))
    qblk = pl.BlockSpec((SBQ, HEAD_DIM), lambda h, i: (i, h))
    return pl.pallas_call(
        body, name=name, grid=(nh, nq), in_specs=[qblk, head_col(nh), head_col(2 * nh), qblk],
        out_specs=[qblk, qblk],
        out_shape=[jax.ShapeDtypeStruct((s, width), BF16), jax.ShapeDtypeStruct((s, width), F32)],
        compiler_params=_params(("parallel", "arbitrary")),
    )(qkv, qkv, qkv, gate)


def _sb_bwd(qkv, gate, o, dmixed, *, width, name):
    s = qkv.shape[0]
    nh = width // HEAD_DIM
    nq = s // SBQ

    def body(q_ref, k_ref, v_ref, g_ref, o_ref, dm_ref, dq_ref, dk_ref, dv_ref, dg_ref, dk_scr, dv_scr):
        qi = pl.program_id(1)

        @pl.when(qi == 0)
        def _():
            dk_scr[...] = jnp.zeros_like(dk_scr)
            dv_scr[...] = jnp.zeros_like(dv_scr)

        g = g_ref[...]
        of = o_ref[...]
        dm = dm_ref[...]
        sig = _sigmoid(g)
        do = dm * (g * sig)
        dg_ref[...] = (dm * of * (sig * (1.0 + g * (1.0 - sig)))).astype(BF16)
        do_b = do.astype(BF16)
        q = q_ref[...]
        r_i = lax.broadcasted_iota(jnp.int32, (SBK, SBK), 0)
        c_i = lax.broadcasted_iota(jnp.int32, (SBK, SBK), 1)
        tri = (r_i > c_i).astype(BF16)
        tri_pre = (r_i < c_i).astype(BF16)
        diag = c_i < r_i
        q_a, q_b = q[0:SBK], q[SBK:SBQ]
        do_a, do_b2 = do_b[0:SBK], do_b[SBK:SBQ]

        def kv(t):
            k0 = pl.multiple_of(t * SBK, SBK)
            return k0, k_ref[pl.ds(k0, SBK), :], v_ref[pl.ds(k0, SBK), :]

        def mass(qh, tile_, dg_):
            return jnp.sum(_sb_logits(qh, kv(tile_)[1], dg_)[1], axis=-1, keepdims=True)

        run_b = mass(q_b, 2 * qi + 1, diag) + mass(q_b, 2 * qi, None)
        run_a = mass(q_a, 2 * qi, diag)
        trips = (2 // SB_TILES_PER_TRIP) * qi

        def scout(carry):
            t, _, run_a, run_b = carry
            for u in range(SB_TILES_PER_TRIP):
                tile_ = 2 * qi - 1 - (SB_TILES_PER_TRIP * t + u)
                run_a = run_a + mass(q_a, tile_, None)
                run_b = run_b + mass(q_b, tile_, None)
            return t + 1, _sb_alive(run_a, run_b), run_a, run_b

        walked, _, run_a, run_b = lax.while_loop(
            lambda c: jnp.logical_and(c[0] < trips, c[1] > 0), scout,
            (jnp.int32(0), _sb_alive(run_a, run_b), run_a, run_b))

        def tile(qh, doh, kt, vt, dg_, carry):
            rem, gpre, dq = carry
            z, lom = _sb_logits(qh, kt, dg_)
            rem = rem - jnp.sum(lom, axis=-1, keepdims=True)
            a = jnp.exp(lom + z + (_suffix_excl(lom, tri) + rem))
            if dg_ is not None:
                a = jnp.where(dg_, a, 0.0)
            gg = lax.dot_general(doh, vt, NT, preferred_element_type=F32) * a
            pre = _suffix_excl(gg, tri_pre) + gpre
            dz = gg * jnp.exp(lom) - pre * jnp.exp(lom + z)
            if dg_ is not None:
                dz = jnp.where(dg_, dz, 0.0)
            dz_b = (dz * SCALE).astype(BF16)
            dq = dq + jnp.dot(dz_b, kt, preferred_element_type=F32)
            return (rem, gpre + jnp.sum(gg, axis=-1, keepdims=True), dq), dz_b, a.astype(BF16)

        def both(t, ca, cb, dg_a):
            k0, kt, vt = kv(t)
            ca, dz_a, a_a = tile(q_a, do_a, kt, vt, dg_a, ca)
            cb, dz_b_, a_b = tile(q_b, do_b2, kt, vt, None, cb)
            dk_scr[pl.ds(k0, SBK), :] += lax.dot_general(jnp.concatenate([dz_a, dz_b_], axis=0), q, TN,
                                                         preferred_element_type=F32)
            dv_scr[pl.ds(k0, SBK), :] += lax.dot_general(jnp.concatenate([a_a, a_b], axis=0), do_b, TN,
                                                         preferred_element_type=F32)
            return ca, cb

        def step(t, carry):
            ca, cb = carry
            for u in range(SB_TILES_PER_TRIP):
                ca, cb = both(SB_TILES_PER_TRIP * t + u, ca, cb, None)
            return ca, cb

        zero = (jnp.zeros((SBK, 1), F32), jnp.zeros((SBK, HEAD_DIM), F32))
        ca, cb = lax.fori_loop(trips - walked, trips, step, ((run_a, *zero), (run_b, *zero)))
        ca, cb = both(2 * qi, ca, cb, diag)
        k0, kt, vt = kv(2 * qi + 1)
        cb, dz_b_, a_b = tile(q_b, do_b2, kt, vt, diag, cb)
        dk_scr[pl.ds(k0, SBK), :] += lax.dot_general(dz_b_, q_b, TN, preferred_element_type=F32)
        dv_scr[pl.ds(k0, SBK), :] += lax.dot_general(a_b, do_b2, TN, preferred_element_type=F32)
        dq_ref[...] = jnp.concatenate([ca[2], cb[2]], axis=0).astype(BF16)

        @pl.when(qi == nq - 1)
        def _():
            dk_ref[...] = dk_scr[...].astype(BF16)
            dv_ref[...] = dv_scr[...].astype(BF16)

    head_col = lambda off: pl.BlockSpec((s, HEAD_DIM), lambda h, i: (0, off + h))
    qblk = pl.BlockSpec((SBQ, HEAD_DIM), lambda h, i: (i, h))
    return pl.pallas_call(
        body, name=name, grid=(nh, nq),
        in_specs=[qblk, head_col(nh), head_col(2 * nh), qblk, qblk, qblk],
        out_specs=[qblk, head_col(0), head_col(0), qblk],
        out_shape=[jax.ShapeDtypeStruct((s, width), BF16)] * 4,
        scratch_shapes=[pltpu.VMEM((s, HEAD_DIM), F32), pltpu.VMEM((s, HEAD_DIM), F32)],
        compiler_params=_params(("parallel", "arbitrary")),
    )(qkv, qkv, qkv, gate, o, dmixed)


HBM = pl.BlockSpec(memory_space=pl.ANY)
MESH = pl.DeviceIdType.MESH


class _GatherRide:
    def __init__(self, shard):
        self.src = shard
        self.out_shape = jax.ShapeDtypeStruct((N_DEV, *shard.shape), shard.dtype)
        self.scratch = [pltpu.SemaphoreType.DMA((7,)), pltpu.SemaphoreType.DMA((7,)), pltpu.SemaphoreType.DMA]

    def _copies(self, x_ref, out_ref, send_sems, recv_sems, local_sem):
        x, y, c = lax.axis_index("x"), lax.axis_index("y"), lax.axis_index("c")
        me, sibling = (x, y, c), (x, y, 1 - c)
        chips = [(1 - x, y), (x, 1 - y), (1 - x, 1 - y)]

        def slot(px, py, pc):
            return out_ref.at[4 * px + 2 * py + pc]

        def copy(k, block, to, src=None):
            return pltpu.make_async_remote_copy(
                src_ref=slot(*block) if src is None else src, dst_ref=slot(*block),
                send_sem=send_sems.at[k], recv_sem=recv_sems.at[k], device_id=to, device_id_type=MESH)

        mine = pltpu.make_async_copy(x_ref, slot(*me), local_sem)
        first = [copy(0, me, sibling, src=x_ref)]
        first += [copy(1 + j, me, (*chip, c), src=x_ref) for j, chip in enumerate(chips)]
        landed = [copy(1 + j, (*chip, c), me) for j, chip in enumerate(chips)]
        passed = [copy(4 + j, (*chip, c), sibling) for j, chip in enumerate(chips)]
        from_sibling = [copy(0, sibling, me)] + [copy(4 + j, (*chip, 1 - c), me) for j, chip in enumerate(chips)]
        return mine, first, landed, passed, from_sibling

    def start(self, *refs):
        mine, first, _, _, _ = self._copies(*refs)
        mine.start()
        for cp in first:
            cp.start()

    def relay(self, *refs):
        _, _, landed, passed, _ = self._copies(*refs)
        for got, onward in zip(landed, passed):
            got.wait_recv()
            onward.start()

    def finish(self, *refs):
        mine, first, _, passed, from_sibling = self._copies(*refs)
        for cp in from_sibling:
            cp.wait_recv()
        for cp in first + passed:
            cp.wait_send()
        mine.wait()


def _all_gather(shard, *, name):
    ride = _GatherRide(shard)

    def body(*refs):
        ride.start(*refs)
        ride.relay(*refs)
        ride.finish(*refs)

    return pl.pallas_call(body, name=name, in_specs=[HBM], out_specs=HBM, out_shape=ride.out_shape,
                          scratch_shapes=ride.scratch)(shard)


class _Ride:
    def __init__(self, src, *, gather, chips=False):
        self.src, self.gather, self.chips = src, gather, chips
        n = N_DEV // 2 if chips else N_DEV
        self.out_shape = jax.ShapeDtypeStruct((n, *(src.shape if gather else src.shape[1:])), src.dtype)
        self.scratch = [pltpu.SemaphoreType.DMA((7,)), pltpu.SemaphoreType.DMA((7,)), pltpu.SemaphoreType.DMA]

    def _copies(self, src_ref, out_ref, send_sems, recv_sems, local_sem):
        x, y, c = lax.axis_index("x"), lax.axis_index("y"), lax.axis_index("c")
        slot = (lambda px, py, pc: 2 * px + py) if self.chips else (lambda px, py, pc: 4 * px + 2 * py + pc)
        me = slot(x, y, c)
        pick = (lambda j: src_ref) if self.gather else (lambda j: src_ref.at[j])
        mine = pltpu.make_async_copy(pick(me), out_ref.at[me], local_sem)
        copies = []
        for k in range(2 if self.chips else 1, N_DEV, 2 if self.chips else 1):
            px, py, pc = x ^ ((k >> 2) & 1), y ^ ((k >> 1) & 1), c ^ (k & 1)
            copies.append(pltpu.make_async_remote_copy(
                src_ref=pick(slot(px, py, pc)), dst_ref=out_ref.at[me],
                send_sem=send_sems.at[k - 1], recv_sem=recv_sems.at[k - 1],
                device_id=(px, py, pc), device_id_type=MESH))
        return mine, copies

    def start(self, *refs):
        mine, copies = self._copies(*refs)
        mine.start()
        for cp in copies:
            cp.start()

    def finish(self, *refs):
        mine, copies = self._copies(*refs)
        for cp in copies:
            cp.wait_recv()
        for cp in copies:
            cp.wait_send()
        mine.wait()


def _exchange(src, *, gather, name):
    ride = _Ride(src, gather=gather)

    def body(*refs):
        ride.start(*refs)
        ride.finish(*refs)

    return pl.pallas_call(body, name=name, in_specs=[HBM], out_specs=HBM, out_shape=ride.out_shape,
                          scratch_shapes=ride.scratch)(src)


def _presum_on_chip(parts, *, name, rows=256):
    _, r, c_ = parts.shape
    rows = min(rows, r)
    assert r % rows == 0
    by_core = parts.reshape(N_DEV // 2, 2, r, c_)

    def swap(src_ref, out_ref, send_sem, recv_sem):
        x, y, c = lax.axis_index("x"), lax.axis_index("y"), lax.axis_index("c")
        cp = pltpu.make_async_remote_copy(
            src_ref=src_ref.at[:, 1 - c], dst_ref=out_ref, send_sem=send_sem, recv_sem=recv_sem,
            device_id=(x, y, 1 - c), device_id_type=MESH)
        cp.start()
        cp.wait()

    got = pl.pallas_call(swap, name=name + "_swap", in_specs=[HBM], out_specs=HBM,
                         out_shape=jax.ShapeDtypeStruct((N_DEV // 2, r, c_), parts.dtype),
                         scratch_shapes=[pltpu.SemaphoreType.DMA, pltpu.SemaphoreType.DMA])(by_core)

    def add(core_ref, a_ref, b_ref, o_ref):
        del core_ref
        o_ref[...] = (a_ref[...].astype(F32) + b_ref[...].astype(F32)).astype(o_ref.dtype)

    blk = pl.BlockSpec((1, rows, c_), lambda j, i, core: (j, i, 0))
    mine = pl.BlockSpec((1, None, rows, c_), lambda j, i, core: (j, core[0], i, 0))
    core = jnp.reshape(lax.axis_index("c"), (1,)).astype(jnp.int32)
    return pl.pallas_call(
        add, name=name + "_sum",
        grid_spec=pltpu.PrefetchScalarGridSpec(num_scalar_prefetch=1, grid=(N_DEV // 2, r // rows),
                                               in_specs=[mine, blk], out_specs=blk),
        out_shape=jax.ShapeDtypeStruct((N_DEV // 2, r, c_), parts.dtype),
        compiler_params=_params(("parallel", "parallel")))(core, by_core, got)


def _call(body, *, name, grid, in_specs, out_specs, out_shape, scratch, sem, args, ride=None):
    if ride is None:
        outs = pl.pallas_call(body, name=name, grid=grid, in_specs=in_specs, out_specs=out_specs, out_shape=out_shape,
                              scratch_shapes=scratch, compiler_params=_params(sem))(*args)
        return list(outs), None
    n_in, n_out = len(in_specs), len(out_specs)

    def carrying(*refs):
        ins, src_ref = refs[:n_in], refs[n_in]
        outs, dst_ref = refs[n_in + 1:n_in + 1 + n_out], refs[n_in + 1 + n_out]
        rest = refs[n_in + 2 + n_out:]
        own, sems = rest[:len(rest) - 3], rest[len(rest) - 3:]
        ids = [pl.program_id(a) for a in range(len(grid))]
        first = functools.reduce(jnp.logical_and, [i == 0 for i in ids])
        last = functools.reduce(jnp.logical_and, [i == n - 1 for i, n in zip(ids, grid)])

        @pl.when(first)
        def _():
            ride.start(src_ref, dst_ref, *sems)

        if hasattr(ride, "relay"):
            at = functools.reduce(jnp.logical_and, [ids[0] == (3 * grid[0]) // 4] + [i == 0 for i in ids[1:]])

            @pl.when(at)
            def _():
                ride.relay(src_ref, dst_ref, *sems)

        body(*ins, *outs, *own)

        @pl.when(last)
        def _():
            ride.finish(src_ref, dst_ref, *sems)

    outs = pl.pallas_call(
        carrying, name=name, grid=grid, in_specs=[*in_specs, HBM], out_specs=[*out_specs, HBM],
        out_shape=[*out_shape, ride.out_shape], scratch_shapes=[*scratch, *ride.scratch],
        compiler_params=_params(("arbitrary",) * len(grid)))(*args, ride.src)
    return list(outs[:-1]), outs[-1]


def _adamw(parts, w, m, v, *, name, rows):
    r, c_ = w.shape
    rows = min(rows, r)
    assert r % rows == 0
    c1 = 1.0 / (1.0 - ADAM_B1 ** ADAM_STEP)
    c2 = 1.0 / (1.0 - ADAM_B2 ** ADAM_STEP)

    def body(p_ref, w_ref, m_ref, v_ref, g_ref, d_ref, nm_ref, nv_ref):
        g = p_ref[0].astype(F32)
        for i in range(1, parts.shape[0]):
            g = g + p_ref[i].astype(F32)
        nm = ADAM_B1 * m_ref[...] + (1.0 - ADAM_B1) * g
        nv = ADAM_B2 * v_ref[...] + (1.0 - ADAM_B2) * (g * g)
        g_ref[...] = g
        nm_ref[...] = nm
        nv_ref[...] = nv
        d_ref[...] = -ADAM_LR * ((nm * c1) / (jnp.sqrt(nv * c2) + ADAM_EPS) + ADAM_WD * w_ref[...])

    blk = pl.BlockSpec((rows, c_), lambda i: (i, 0))
    return pl.pallas_call(
        body, name=name, grid=(r // rows,),
        in_specs=[pl.BlockSpec((parts.shape[0], rows, c_), lambda i: (0, i, 0)), blk, blk, blk],
        out_specs=[blk] * 4, out_shape=[jax.ShapeDtypeStruct((r, c_), F32)] * 4,
        compiler_params=_params(("parallel",)),
    )(parts, w, m, v)


def _pad_cols(a, n):
    return jnp.pad(a, ((0, 0), (0, n - a.shape[1])))


def _fox_dc(shares, nh, s):
    key_side, query_side = shares
    return _pad_cols(jnp.transpose(key_side.reshape(nh, s) + query_side.reshape(nh, s)), 128)


def _local_step(x, target, norm_pre, norm_post, w_in_e, w_f, b_f, rel_bias, sh_out_e, sh_in_o, sh_out_o):
    s, d = x.shape
    wa = d // 2
    nha = wa // HEAD_DIM
    nq = s // CQ
    gather = lambda shard: _Ride(shard, gather=True)
    scatter = lambda parts: _Ride(parts, gather=False)

    h0 = _rms_fwd(x, norm_pre[0:1], name="rms_pre0")
    proj = lambda w, off, n, dt, nm, **kw: _matmul(h0, w, mode="nn", m=s, n=n, k=d, out_dtype=dt, name=nm,
                                                   b_off=(0, off), **kw)
    qkv_a, w_out_e = proj(w_in_e, 0, 3 * wa, BF16, "proj_qkv_a", ride=gather(sh_out_e))
    w_out_e = w_out_e.reshape(d, d)
    g_a = proj(w_in_e, 3 * wa, wa, F32, "proj_gate_a")
    qkv_b = proj(w_in_e, 4 * wa, 3 * wa, BF16, "proj_qkv_b")
    g_b = proj(w_in_e, 7 * wa, wa, F32, "proj_gate_b")
    af = _matmul(h0, w_f, mode="nn", m=s, n=128, k=d, out_dtype=F32, name="proj_forget")
    bias128 = _pad_cols(b_f, 128)
    cum = _gate_fwd(af, bias128, name="forget_cumsum")
    c_t = jnp.transpose(cum[:, :nha])
    c_col = c_t.reshape(nha, s, 1)
    c_row = c_t.reshape(nha, nq, 1, CQ)
    mixed_a, o_a, lse_a, w_in_o = _softmax_fwd(qkv_a, g_a, (c_col, c_row), mode="fox", width=wa, name="fox_fwd",
                                               ride=_GatherRide(sh_in_o))
    rel_aux = (_pad_cols(rel_bias, REL_PAD).reshape(nha, 1, REL_PAD), _rel_onehot())
    mixed_b, o_b, lse_b, w_out_o = _softmax_fwd(qkv_b, g_b, rel_aux, mode="chunk", width=wa, name="chunk_fwd",
                                                ride=gather(sh_out_o))
    w_out_o = w_out_o.reshape(d, d)
    mixed0 = jnp.concatenate([mixed_a, mixed_b], axis=1)
    y0 = _matmul(mixed0, w_out_e, mode="nn", m=s, n=d, k=d, out_dtype=F32, name="out_proj0")
    x1 = _post_fwd(x, y0, norm_post[0:1], name="post0")

    h1 = _rms_fwd(x1, norm_pre[1:2], name="rms_pre1")
    qkv_c = _matmul(h1, w_in_o, mode="nn", m=s, n=3 * d, k=d, out_dtype=BF16, name="proj_qkv_c", b_groups=N_DEV)
    g_c = _matmul(h1, w_in_o, mode="nn", m=s, n=d, k=d, out_dtype=F32, name="proj_gate_c", b_off=(0, 3 * d),
                  b_groups=N_DEV)
    mixed1, o_c = _sb_fwd(qkv_c, g_c, width=d, name="sb_fwd")
    y1 = _matmul(mixed1, w_out_o, mode="nn", m=s, n=d, k=d, out_dtype=F32, name="out_proj1")
    x2 = _post_fwd(x1, y1, norm_post[1:2], name="post1")

    loss, dx2 = _loss_head(x2, target, name="loss_head")

    dy1, dgpost1 = _post_bwd(dx2, y1, norm_post[1:2], name="post_bwd1")
    dmixed1 = _matmul(dy1, w_out_o, mode="nt", m=s, n=d, k=d, out_dtype=F32, name="dmixed1")
    dw_out_o = _matmul(mixed1, dy1, mode="tn", m=d, n=d, k=s, out_dtype=BF16, name="dw_out1")
    dq_c, dk_c, dv_c, dg_c = _sb_bwd(qkv_c, g_c, o_c, dmixed1, width=d, name="sb_bwd")
    dproj1 = jnp.concatenate([dq_c, dk_c, dv_c, dg_c], axis=1)
    dh1 = _matmul(dproj1, w_in_o, mode="nt", m=s, n=d, k=4 * d, out_dtype=F32, name="dh1", b_groups=N_DEV)
    parts_in_o = _matmul(h1, dproj1, mode="tn", m=d, n=4 * d, k=s, out_dtype=BF16, name="dw_in1", out_groups=N_DEV)
    dx1, dgpre1 = _pre_bwd((dh1,), x1, norm_pre[1:2], dx2, name="pre_bwd1")

    dy0, dgpost0 = _post_bwd(dx1, y0, norm_post[0:1], name="post_bwd0")
    dmixed0 = _matmul(dy0, w_out_e, mode="nt", m=s, n=d, k=d, out_dtype=F32, name="dmixed0")
    dw_out_e = _matmul(mixed0, dy0, mode="tn", m=d, n=d, k=s, out_dtype=BF16, name="dw_out0")
    chip_in_o = _presum_on_chip(parts_in_o, name="rs_w_in_odd")
    dq_a, dk_a, dv_a, dg_a, *dc_shares, got_in_o = _softmax_bwd(
        qkv_a, g_a, o_a, dmixed0, 0, lse_a, (c_col, c_row), mode="fox", width=wa, name="fox_bwd",
        ride=_Ride(chip_in_o, gather=False, chips=True))
    dq_b, dk_b, dv_b, dg_b, drel, got_out_o = _softmax_bwd(
        qkv_b, g_b, o_b, dmixed0, wa, lse_b, rel_aux, mode="chunk", width=wa, name="chunk_bwd",
        ride=scatter(dw_out_o.reshape(N_DEV, d // N_DEV, d)))
    dc = _fox_dc(dc_shares, nha, s)
    daf, dbf = _gate_bwd(dc, af, bias128, name="forget_bwd")
    dproj0 = jnp.concatenate([dq_a, dk_a, dv_a, dg_a, dq_b, dk_b, dv_b, dg_b], axis=1)
    dw_in_e, got_out_e = _matmul(h0, dproj0, mode="tn", m=d, n=8 * wa, k=s, out_dtype=BF16, name="dw_in0",
                                 ride=scatter(dw_out_e.reshape(N_DEV, d // N_DEV, d)))
    dw_f = _matmul(h0, daf, mode="tn", m=d, n=128, k=s, out_dtype=BF16, name="dw_forget")
    dw_e = jnp.concatenate([dw_in_e, dw_f[:, :nha]], axis=1)
    parts_in_e = jnp.transpose(dw_e.reshape(d, N_DEV, dw_e.shape[1] // N_DEV), (1, 0, 2))
    chip_in_e = _presum_on_chip(parts_in_e, name="rs_w_in_even")
    dh0, got_in_e = _matmul(dproj0, w_in_e, mode="nt", m=s, n=d, k=8 * wa, out_dtype=F32, name="dh0_main",
                            ride=_Ride(chip_in_e, gather=False, chips=True))
    dh0f = _matmul(daf, w_f, mode="nt", m=s, n=d, k=128, out_dtype=F32, name="dh0_forget")
    dx0, dgpre0 = _pre_bwd((dh0, dh0f), x, norm_pre[0:1], dx1, name="pre_bwd0")

    small = dict(
        norm_pre=jnp.concatenate([dgpre0, dgpre1], axis=0),
        norm_post=jnp.concatenate([dgpost0, dgpost1], axis=0),
        b_f=dbf[:, :nha], rel_bias=drel[:, 0, :N_REL])
    got = dict(w_in_even=got_in_e, w_out_even=got_out_e, w_in_odd=got_in_o, w_out_odd=got_out_o)
    return loss, dx0, got, small


def _pack_small(norm_pre, norm_post, b_f, rel_bias):
    flat = jnp.concatenate([norm_pre.reshape(-1), norm_post.reshape(-1), b_f.reshape(-1), rel_bias.reshape(-1)])
    n = flat.shape[0]
    rows = -(-n // 128)
    rows = -(-rows // 8) * 8
    return jnp.pad(flat, (0, rows * 128 - n)).reshape(rows, 128)


def _unpack_small(slab, shapes):
    flat = slab.reshape(-1)
    out, off = [], 0
    for shp in shapes:
        n = int(np.prod(shp))
        out.append(flat[off:off + n].reshape(shp))
        off += n
    return out


def kernel(x, norm_pre, norm_post, w_in_even, b_f_even, rel_bias_even, w_out_even, w_in_odd, w_out_odd, loss_target, m_norm_pre, m_norm_post, m_w_in_even, m_b_f_even, m_rel_bias_even, m_w_out_even, m_w_in_odd, m_w_out_odd, v_norm_pre, v_norm_post, v_w_in_even, v_b_f_even, v_rel_bias_even, v_w_out_even, v_w_in_odd, v_w_out_odd):
    _, s, d = x.shape
    wa = d // 2
    nha = wa // HEAD_DIM
    in_e = w_in_even.shape[2] * N_DEV

    w_in_e_all = jnp.transpose(_all_gather(w_in_even[0].astype(BF16), name="ag_w_in_even"), (1, 0, 2)).reshape(d, in_e)
    w_main = w_in_e_all[:, :8 * wa]
    w_f = _pad_cols(w_in_e_all[:, 8 * wa:], 128)

    loss, dx, got, small = _local_step(
        x[0], loss_target[0], norm_pre, norm_post, w_main, w_f, b_f_even, rel_bias_even[0],
        w_out_even[0].astype(BF16), w_in_odd[0].astype(BF16), w_out_odd[0].astype(BF16))

    upd = {}
    upd["w_in_even"] = _adamw(got["w_in_even"], w_in_even[0], m_w_in_even[0], v_w_in_even[0],
                              name="adamw_w_in_even", rows=128)
    upd["w_out_even"] = _adamw(got["w_out_even"], w_out_even[0], m_w_out_even[0], v_w_out_even[0],
                               name="adamw_w_out_even", rows=64)
    upd["w_in_odd"] = _adamw(got["w_in_odd"], w_in_odd[0], m_w_in_odd[0], v_w_in_odd[0],
                             name="adamw_w_in_odd", rows=128)
    upd["w_out_odd"] = _adamw(got["w_out_odd"], w_out_odd[0], m_w_out_odd[0], v_w_out_odd[0],
                              name="adamw_w_out_odd", rows=64)

    shapes = [norm_pre.shape, norm_post.shape, b_f_even.shape, rel_bias_even.shape]
    g_slab = _pack_small(small["norm_pre"], small["norm_post"], small["b_f"], small["rel_bias"])
    parts_small = _exchange(g_slab, gather=True, name="ar_small")
    sm = _adamw(parts_small, _pack_small(norm_pre, norm_post, b_f_even, rel_bias_even),
                _pack_small(m_norm_pre, m_norm_post, m_b_f_even, m_rel_bias_even),
                _pack_small(v_norm_pre, v_norm_post, v_b_f_even, v_rel_bias_even), name="adamw_small", rows=g_slab.shape[0])
    sm = [_unpack_small(a, shapes) for a in sm]

    total = lax.psum(loss[0, 0], ("x", "y", "c"))

    def leaves(kind):
        return (sm[kind][0], sm[kind][1], upd["w_in_even"][kind][None], sm[kind][2], sm[kind][3],
                upd["w_out_even"][kind][None], upd["w_in_odd"][kind][None], upd["w_out_odd"][kind][None])

    return (total, dx[None], *leaves(0), *leaves(1), *leaves(2), *leaves(3))
```

```python
import functools

import numpy as np
import jax
import jax.numpy as jnp
from jax import lax
from jax.experimental import pallas as pl
from jax.experimental.pallas import tpu as pltpu

F32 = jnp.float32
BF16 = jnp.bfloat16

HEAD_DIM = 128
CHUNK = 64
LEFT_CHUNKS = 8
REL_CLIP = 128
N_REL = 2 * REL_CLIP + 1
RMS_EPS = 1e-6
SCALE = HEAD_DIM ** -0.5

ADAM_LR = 0.001
ADAM_B1 = 0.9
ADAM_B2 = 0.999
ADAM_EPS = 1e-08
ADAM_WD = 0.01
ADAM_STEP = 10

N_DEV = 8
V7X_VMEM_LIMIT_BYTES = 56 * 1024 * 1024
NEG = -1e30

NT = (((1,), (1,)), ((), ()))
TN = (((0,), (0,)), ((), ()))
NN = (((1,), (0,)), ((), ()))

CQ = 256
BAND_TILES = 3
TOEP = 2 * CQ
assert (BAND_TILES - 1) * CQ == LEFT_CHUNKS * CHUNK


def _params(sem):
    return pltpu.CompilerParams(dimension_semantics=sem, vmem_limit_bytes=V7X_VMEM_LIMIT_BYTES)


def _split3(x):
    hi = x.astype(BF16)
    r1 = x - hi.astype(F32)
    mid = r1.astype(BF16)
    lo = (r1 - mid.astype(F32)).astype(BF16)
    return hi, mid, lo


def _split2(x):
    hi = x.astype(BF16)
    lo = (x - hi.astype(F32)).astype(BF16)
    return hi, lo


def _sigmoid(g):
    return 1.0 / (1.0 + jnp.exp(-g))


def _tile(n, cap, *offsets):
    if n <= 128:
        return n
    t = (min(cap, n) // 128) * 128
    while n % t or any(o % t for o in offsets):
        t -= 128
    return t


def _matmul(a, b, *, mode, m, n, k, out_dtype, name, a_off=(0, 0), b_off=(0, 0), tm=1024, tn=1024, tk=1024,
            b_groups=None, out_groups=None, ride=None):
    a_m, a_k = (a_off if mode in ("nn", "nt") else a_off[::-1])
    b_k, b_n = (b_off if mode in ("nn", "tn") else b_off[::-1])
    b_group = (b.shape[2],) if b_groups else ()
    a_parts, b_parts = (a if isinstance(a, tuple) else None), (b if isinstance(b, tuple) else None)
    assert not (a_parts and b_parts) and not (b_parts and mode == "nt")
    a_piece = (a_parts[0].shape[1],) if a_parts else ()
    b_piece = (b_parts[0].shape[1],) if b_parts else ()
    a_on_k = mode in ("nn", "nt")
    tm = _tile(m, tm, a_m, *(() if a_on_k else a_piece))
    tn = _tile(n, tn, b_n, *b_piece)
    tk = _tile(k, tk, a_k, b_k, *(a_piece if a_on_k else ()))
    if b_groups and mode in ("nn", "tn"):
        tn = _tile(n, tn, b_n, *b_group)
    if b_groups and mode == "nt":
        tk = _tile(k, tk, a_k, b_k, *b_group, *a_piece)
    if out_groups:
        tn = _tile(n, tn, b_n, n // out_groups, *b_piece, *(b_group if mode != "nt" else ()))
    nk = k // tk

    def piece_specs(parts, tile, walk, block, place):
        per = parts[0].shape[1] // tile
        def spec(g):
            return pl.BlockSpec(block, lambda i, j, l: place(i, j, l, jnp.clip(walk(i, j, l) - g * per, 0, per - 1)))
        return [spec(g) for g in range(len(parts))], per

    if a_parts:
        if a_on_k:
            a_specs, a_per = piece_specs(a_parts, tk, lambda i, j, l: l, (tm, tk), lambda i, j, l, c: (i, c))
        else:
            a_specs, a_per = piece_specs(a_parts, tm, lambda i, j, l: i, (tk, tm), lambda i, j, l, c: (l, c))
    elif mode in ("nn", "nt"):
        ao = (a_off[0] // tm, a_off[1] // tk)
        a_specs = [pl.BlockSpec((tm, tk), lambda i, j, l: (i + ao[0], l + ao[1]))]
    else:
        ao = (a_off[0] // tk, a_off[1] // tm)
        a_specs = [pl.BlockSpec((tk, tm), lambda i, j, l: (l + ao[0], i + ao[1]))]
    if b_parts:
        b_specs, b_per = piece_specs(b_parts, tn, lambda i, j, l: j, (tk, tn), lambda i, j, l, c: (l, c))
        b_spec = None
    elif mode in ("nn", "tn"):
        bo = (b_off[0] // tk, b_off[1] // tn)
        if b_groups:
            per = b.shape[2] // tn
            b_spec = pl.BlockSpec((None, tk, tn), lambda i, j, l: ((j + bo[1]) // per, l + bo[0], (j + bo[1]) % per))
        else:
            b_spec = pl.BlockSpec((tk, tn), lambda i, j, l: (l + bo[0], j + bo[1]))
    else:
        bo = (b_off[0] // tn, b_off[1] // tk)
        if b_groups:
            per = b.shape[2] // tk
            b_spec = pl.BlockSpec((None, tn, tk), lambda i, j, l: ((l + bo[1]) // per, j + bo[0], (l + bo[1]) % per))
        else:
            b_spec = pl.BlockSpec((tn, tk), lambda i, j, l: (j + bo[0], l + bo[1]))
    if out_groups:
        oper = (n // out_groups) // tn
        out_spec = pl.BlockSpec((None, tm, tn), lambda i, j, l: (j // oper, i, j % oper))
        out_shape = jax.ShapeDtypeStruct((out_groups, m, n // out_groups), out_dtype)
    else:
        out_spec = pl.BlockSpec((tm, tn), lambda i, j, l: (i, j))
        out_shape = jax.ShapeDtypeStruct((m, n), out_dtype)
    dn = {"nn": NN, "nt": NT, "tn": TN}[mode]
    if not b_parts:
        b_specs = [b_spec]
    na = len(a_specs)

    def body(*refs):
        a_refs, b_refs, (o_ref, acc_ref) = refs[:na], refs[na:len(refs) - 2], refs[len(refs) - 2:]

        @pl.when(pl.program_id(2) == 0)
        def _():
            acc_ref[...] = jnp.zeros_like(acc_ref)

        def add(a_ref, b_ref):
            acc_ref[...] += lax.dot_general(a_ref[...], b_ref[...], dn, preferred_element_type=F32)

        if a_parts or b_parts:
            if a_parts:
                active = pl.program_id(2 if a_on_k else 0) // a_per
            else:
                active = pl.program_id(1) // b_per
            for g in range(max(na, len(b_refs))):
                pl.when(active == g)(functools.partial(add, a_refs[g if a_parts else 0], b_refs[g if b_parts else 0]))
        else:
            add(a_refs[0], b_refs[0])

        @pl.when(pl.program_id(2) == nk - 1)
        def _():
            o_ref[...] = acc_ref[...].astype(out_dtype)

    (out,), carried = _call(
        body, name=name, grid=(m // tm, n // tn, nk), in_specs=[*a_specs, *b_specs],
        out_specs=[out_spec], out_shape=[out_shape],
        scratch=[pltpu.VMEM((tm, tn), F32)], sem=("parallel", "parallel", "arbitrary"),
        args=(*(a_parts or (a,)), *(b_parts or (b,))), ride=ride)
    return out if ride is None else (out, carried)


ROWS = 128


def _rms_fwd(x, gain, *, name):
    s, d = x.shape

    def body(x_ref, g_ref, h_ref):
        xf = x_ref[...]
        r = lax.rsqrt(jnp.mean(xf * xf, axis=-1, keepdims=True) + RMS_EPS)
        h_ref[...] = ((xf * r) * g_ref[...]).astype(BF16)

    return pl.pallas_call(
        body, name=name, grid=(s // ROWS,),
        in_specs=[pl.BlockSpec((ROWS, d), lambda i: (i, 0)), pl.BlockSpec((1, d), lambda i: (0, 0))],
        out_specs=pl.BlockSpec((ROWS, d), lambda i: (i, 0)),
        out_shape=jax.ShapeDtypeStruct((s, d), BF16),
        compiler_params=_params(("parallel",)),
    )(x, gain)


def _post_fwd(x, y, gain, *, name):
    s, d = x.shape

    def body(x_ref, y_ref, g_ref, o_ref):
        yf = y_ref[...]
        r = lax.rsqrt(jnp.mean(yf * yf, axis=-1, keepdims=True) + RMS_EPS)
        o_ref[...] = x_ref[...] + (yf * r) * g_ref[...]

    return pl.pallas_call(
        body, name=name, grid=(s // ROWS,),
        in_specs=[pl.BlockSpec((ROWS, d), lambda i: (i, 0)), pl.BlockSpec((ROWS, d), lambda i: (i, 0)),
                  pl.BlockSpec((1, d), lambda i: (0, 0))],
        out_specs=pl.BlockSpec((ROWS, d), lambda i: (i, 0)),
        out_shape=jax.ShapeDtypeStruct((s, d), F32),
        compiler_params=_params(("parallel",)),
    )(x, y, gain)


def _loss_head(xo, target, *, name):
    s, d = xo.shape
    inv_d = 1.0 / d

    def body(x_ref, t_ref, loss_ref, dx_ref):
        @pl.when(pl.program_id(0) == 0)
        def _():
            loss_ref[...] = jnp.zeros_like(loss_ref)

        e = x_ref[...] - t_ref[...]
        dx_ref[...] = e * inv_d
        loss_ref[...] += 0.5 * jnp.sum(jnp.mean(e * e, axis=-1, keepdims=True), axis=0, keepdims=True)

    return pl.pallas_call(
        body, name=name, grid=(s // ROWS,),
        in_specs=[pl.BlockSpec((ROWS, d), lambda i: (i, 0)), pl.BlockSpec((ROWS, d), lambda i: (i, 0))],
        out_specs=[pl.BlockSpec((1, 1), lambda i: (0, 0)), pl.BlockSpec((ROWS, d), lambda i: (i, 0))],
        out_shape=[jax.ShapeDtypeStruct((1, 1), F32), jax.ShapeDtypeStruct((s, d), F32)],
        compiler_params=_params(("arbitrary",)),
    )(xo, target)


def _post_bwd(dxo, y, gain, *, name):
    s, d = y.shape

    def body(dx_ref, y_ref, g_ref, dy_ref, dg_ref):
        @pl.when(pl.program_id(0) == 0)
        def _():
            dg_ref[...] = jnp.zeros_like(dg_ref)

        yf = y_ref[...]
        dxo_ = dx_ref[...]
        r = lax.rsqrt(jnp.mean(yf * yf, axis=-1, keepdims=True) + RMS_EPS)
        nrm = yf * r
        dg_ref[...] += jnp.sum(dxo_ * nrm, axis=0, keepdims=True)
        dn = dxo_ * g_ref[...]
        dy_ref[...] = (r * (dn - nrm * jnp.mean(dn * nrm, axis=-1, keepdims=True))).astype(BF16)

    return pl.pallas_call(
        body, name=name, grid=(s // ROWS,),
        in_specs=[pl.BlockSpec((ROWS, d), lambda i: (i, 0)), pl.BlockSpec((ROWS, d), lambda i: (i, 0)),
                  pl.BlockSpec((1, d), lambda i: (0, 0))],
        out_specs=[pl.BlockSpec((ROWS, d), lambda i: (i, 0)), pl.BlockSpec((1, d), lambda i: (0, 0))],
        out_shape=[jax.ShapeDtypeStruct((s, d), BF16), jax.ShapeDtypeStruct((1, d), F32)],
        compiler_params=_params(("arbitrary",)),
    )(dxo, y, gain)


def _pre_bwd(dhs, x, gain, dres, *, name):
    s, d = x.shape
    n_dh = len(dhs)

    def body(*refs):
        dh_refs = refs[:n_dh]
        x_ref, g_ref, dr_ref, dx_ref, dg_ref = refs[n_dh:]

        @pl.when(pl.program_id(0) == 0)
        def _():
            dg_ref[...] = jnp.zeros_like(dg_ref)

        xf = x_ref[...]
        dh_ = dh_refs[0][...]
        for extra in dh_refs[1:]:
            dh_ = dh_ + extra[...]
        r = lax.rsqrt(jnp.mean(xf * xf, axis=-1, keepdims=True) + RMS_EPS)
        nrm = xf * r
        dg_ref[...] += jnp.sum(dh_ * nrm, axis=0, keepdims=True)
        dn = dh_ * g_ref[...]
        dx_ref[...] = dr_ref[...] + r * (dn - nrm * jnp.mean(dn * nrm, axis=-1, keepdims=True))

    return pl.pallas_call(
        body, name=name, grid=(s // ROWS,),
        in_specs=[pl.BlockSpec((ROWS, d), lambda i: (i, 0))] * (n_dh + 1)
        + [pl.BlockSpec((1, d), lambda i: (0, 0)), pl.BlockSpec((ROWS, d), lambda i: (i, 0))],
        out_specs=[pl.BlockSpec((ROWS, d), lambda i: (i, 0)), pl.BlockSpec((1, d), lambda i: (0, 0))],
        out_shape=[jax.ShapeDtypeStruct((s, d), F32), jax.ShapeDtypeStruct((1, d), F32)],
        compiler_params=_params(("arbitrary",)),
    )(*dhs, x, gain, dres)


GB = 256


def _gate_fwd(af, bias, *, name):
    s, w = af.shape

    def body(af_ref, b_ref, c_ref, carry_ref):
        @pl.when(pl.program_id(0) == 0)
        def _():
            carry_ref[...] = jnp.zeros_like(carry_ref)

        z = af_ref[...] + b_ref[...]
        lf = jnp.minimum(z, 0.0) - jnp.log(1.0 + jnp.exp(-jnp.abs(z)))
        r_i = lax.broadcasted_iota(jnp.int32, (GB, GB), 0)
        c_i = lax.broadcasted_iota(jnp.int32, (GB, GB), 1)
        tri = (c_i <= r_i).astype(BF16)
        hi, mid, lo = _split3(lf)
        pre = (jnp.dot(tri, hi, preferred_element_type=F32) + jnp.dot(tri, mid, preferred_element_type=F32)
               + jnp.dot(tri, lo, preferred_element_type=F32))
        c_ref[...] = pre + carry_ref[...]
        carry_ref[...] += jnp.sum(lf, axis=0, keepdims=True)

    return pl.pallas_call(
        body, name=name, grid=(s // GB,),
        in_specs=[pl.BlockSpec((GB, w), lambda i: (i, 0)), pl.BlockSpec((1, w), lambda i: (0, 0))],
        out_specs=pl.BlockSpec((GB, w), lambda i: (i, 0)),
        out_shape=jax.ShapeDtypeStruct((s, w), F32),
        scratch_shapes=[pltpu.VMEM((1, w), F32)],
        compiler_params=_params(("arbitrary",)),
    )(af, bias)


def _gate_bwd(dc, af, bias, *, name):
    s, w = af.shape
    nb = s // GB

    def body(dc_ref, af_ref, b_ref, daf_ref, db_ref, carry_ref):
        @pl.when(pl.program_id(0) == 0)
        def _():
            carry_ref[...] = jnp.zeros_like(carry_ref)
            db_ref[...] = jnp.zeros_like(db_ref)

        dcb = dc_ref[...]
        r_i = lax.broadcasted_iota(jnp.int32, (GB, GB), 0)
        c_i = lax.broadcasted_iota(jnp.int32, (GB, GB), 1)
        tri = (c_i >= r_i).astype(BF16)
        hi, mid, lo = _split3(dcb)
        suf = (jnp.dot(tri, hi, preferred_element_type=F32) + jnp.dot(tri, mid, preferred_element_type=F32)
               + jnp.dot(tri, lo, preferred_element_type=F32)) + carry_ref[...]
        carry_ref[...] += jnp.sum(dcb, axis=0, keepdims=True)
        z = af_ref[...] + b_ref[...]
        daf = suf * _sigmoid(-z)
        daf_ref[...] = daf.astype(BF16)
        db_ref[...] += jnp.sum(daf, axis=0, keepdims=True)

    return pl.pallas_call(
        body, name=name, grid=(nb,),
        in_specs=[pl.BlockSpec((GB, w), lambda i: (nb - 1 - i, 0)), pl.BlockSpec((GB, w), lambda i: (nb - 1 - i, 0)),
                  pl.BlockSpec((1, w), lambda i: (0, 0))],
        out_specs=[pl.BlockSpec((GB, w), lambda i: (nb - 1 - i, 0)), pl.BlockSpec((1, w), lambda i: (0, 0))],
        out_shape=[jax.ShapeDtypeStruct((s, w), BF16), jax.ShapeDtypeStruct((1, w), F32)],
        scratch_shapes=[pltpu.VMEM((1, w), F32)],
        compiler_params=_params(("arbitrary",)),
    )(dc, af, bias)


def _rel_index_rows():
    w = np.arange(TOEP)
    wp = np.where(w < CQ, w, w - TOEP)
    return np.stack([np.clip(LEFT_CHUNKS * CHUNK - CQ * j - wp, -REL_CLIP, REL_CLIP) + REL_CLIP
                     for j in range(BAND_TILES)]).astype(np.int32)


def _skew_rows(xw, sign):
    row = lax.broadcasted_iota(jnp.int32, xw.shape, 0)
    for b in range(CQ.bit_length() - 1):
        amt = (1 << b) if sign > 0 else TOEP - (1 << b)
        xw = jnp.where(((row >> b) & 1) == 1, pltpu.roll(xw, amt, 1), xw)
    return xw


REL_PAD = 384
HP = 2


def _rel_onehot():
    return jnp.asarray(_rel_index_rows()[:, :, None] == np.arange(REL_PAD)[None, None, :], BF16)


def _fill_bias_tiles(rel_ref, oh_ref, bias_scr):
    parts = _split3(jnp.broadcast_to(rel_ref[...], (8, REL_PAD)))
    for j in range(BAND_TILES):
        row = sum(lax.dot_general(p, oh_ref[j], NT, preferred_element_type=F32) for p in parts)[0:1]
        bias_scr[j] = _skew_rows(jnp.broadcast_to(row, (CQ, TOEP)), +1)[:, :CQ]


def _fox_scores(s, cq, cr, diagonal):
    s = s + (cq - cr)
    if not diagonal:
        return s
    bq, bk = s.shape
    return jnp.where(lax.broadcasted_iota(jnp.int32, (bq, bk), 1) <= lax.broadcasted_iota(jnp.int32, (bq, bk), 0), s, NEG)


def _chunk_scores(s, q0, k0, bias):
    bq, bk = s.shape
    qc = (q0 + lax.broadcasted_iota(jnp.int32, (bq, bk), 0)) >> 6
    kc = (k0 + lax.broadcasted_iota(jnp.int32, (bq, bk), 1)) >> 6
    return jnp.where((kc <= qc) & (kc >= qc - LEFT_CHUNKS), s + bias, NEG)


def _softmax_fwd(qkv, gate, aux, *, mode, width, name, ride=None):
    s = qkv.shape[0]
    nh = width // HEAD_DIM
    bq = bk = CQ
    nq = s // bq

    assert nh % HP == 0
    hcol = lambda hh: slice(hh * HEAD_DIM, (hh + 1) * HEAD_DIM)

    def body(q_ref, k_ref, v_ref, g_ref, *rest):
        if mode == "fox":
            cc_ref, cr_ref, mixed_ref, o_ref, lse_ref = rest
        else:
            rel_ref, oh_ref, mixed_ref, o_ref, lse_ref, bias_scr = rest
        qi = pl.program_id(1)
        q0 = qi * bq
        q = q_ref[...]

        if mode == "chunk":
            @pl.when(qi == 0)
            def _():
                for hh in range(HP):
                    _fill_bias_tiles(rel_ref.at[hh], oh_ref, bias_scr.at[hh])

            lo, hi = jnp.maximum(qi - (BAND_TILES - 1), 0), qi + 1

        def step(ki, carry, diagonal=False):
            k0 = pl.multiple_of(ki * bk, bk)
            kt = k_ref[pl.ds(k0, bk), :]
            vt = v_ref[pl.ds(k0, bk), :]
            out = []
            for hh in range(HP):
                m, l, acc = carry[hh]
                sc = lax.dot_general(q[:, hcol(hh)], kt[:, hcol(hh)], NT, preferred_element_type=F32) * SCALE
                if mode == "fox":
                    sc = _fox_scores(sc, cc_ref[hh], cr_ref[hh, ki], diagonal)
                else:
                    sc = _chunk_scores(sc, q0, k0, bias_scr[hh, ki - qi + (BAND_TILES - 1)])
                m_new = jnp.maximum(m, jnp.max(sc, axis=-1, keepdims=True))
                p = jnp.exp(sc - m_new)
                alpha = jnp.exp(m - m_new)
                l = alpha * l + jnp.sum(p, axis=-1, keepdims=True)
                acc = alpha * acc + jnp.dot(p.astype(BF16), vt[:, hcol(hh)], preferred_element_type=F32)
                out.append((m_new, l, acc))
            return tuple(out)

        init = ((jnp.full((bq, 1), NEG, F32), jnp.zeros((bq, 1), F32), jnp.zeros((bq, HEAD_DIM), F32)),) * HP
        if mode == "fox":
            done = step(qi, lax.fori_loop(0, qi, step, init), diagonal=True)
        else:
            done = lax.fori_loop(lo, hi, step, init)
        o = jnp.concatenate([acc / l for _, l, acc in done], axis=1)
        g = g_ref[...]
        o_ref[...] = o
        mixed_ref[...] = (o * (g * _sigmoid(g))).astype(BF16)
        for hh, (m, l, _) in enumerate(done):
            lse_ref[hh] = m + jnp.log(l)

    wide = HP * HEAD_DIM
    head_col = lambda off: pl.BlockSpec((s, wide), lambda h, i: (0, off // HP + h))
    qblk = pl.BlockSpec((bq, wide), lambda h, i: (i, h))
    stat = pl.BlockSpec((HP, bq, 1), lambda h, i: (h, i, 0))
    in_specs = [qblk, head_col(nh), head_col(2 * nh), qblk]
    scratch = []
    if mode == "fox":
        in_specs += [stat, pl.BlockSpec((HP, nq, 1, bk), lambda h, i: (h, 0, 0, 0))]
    else:
        in_specs += [pl.BlockSpec((HP, 1, REL_PAD), lambda h, i: (h, 0, 0)),
                     pl.BlockSpec((BAND_TILES, TOEP, REL_PAD), lambda h, i: (0, 0, 0))]
        scratch = [pltpu.VMEM((HP, BAND_TILES, CQ, CQ), F32)]
    outs, carried = _call(
        body, name=name, grid=(nh // HP, nq), in_specs=in_specs, out_specs=[qblk, qblk, stat],
        out_shape=[jax.ShapeDtypeStruct((s, width), BF16), jax.ShapeDtypeStruct((s, width), F32),
                   jax.ShapeDtypeStruct((nh, s, 1), F32)],
        scratch=scratch, sem=("parallel", "arbitrary"), args=(qkv, qkv, qkv, gate, *aux), ride=ride)
    return outs if ride is None else (*outs, carried)


def _softmax_bwd(qkv, gate, o, dmixed, dm_off, lse, aux, *, mode, width, name, ride=None):
    s = qkv.shape[0]
    nh = width // HEAD_DIM
    bq = bk = CQ
    nq = s // bq
    dmo = dm_off // HEAD_DIM
    rel_pad = REL_PAD
    assert nh % HP == 0 and dmo % HP == 0
    hcol = lambda hh: slice(hh * HEAD_DIM, (hh + 1) * HEAD_DIM)

    def body(q_ref, k_ref, v_ref, g_ref, o_ref, dm_ref, lse_ref, *rest):
        if mode == "fox":
            cc_ref, cr_ref, dq_ref, dk_ref, dv_ref, dg_ref, dc_ref, dcq_ref, dk_scr, dv_scr, dc_scr = rest
        else:
            rel_ref, oh_ref, dq_ref, dk_ref, dv_ref, dg_ref, drel_ref, dk_scr, dv_scr, bias_scr, db_scr = rest
        qi = pl.program_id(1)
        q0 = qi * bq

        @pl.when(qi == 0)
        def _():
            dk_scr[...] = jnp.zeros_like(dk_scr)
            dv_scr[...] = jnp.zeros_like(dv_scr)
            if mode == "fox":
                dc_scr[...] = jnp.zeros_like(dc_scr)
            else:
                db_scr[...] = jnp.zeros_like(db_scr)
                for hh in range(HP):
                    _fill_bias_tiles(rel_ref.at[hh], oh_ref, bias_scr.at[hh])

        g = g_ref[...]
        of = o_ref[...]
        dm = dm_ref[...]
        sig = _sigmoid(g)
        do = dm * (g * sig)
        dg_ref[...] = (dm * of * (sig * (1.0 + g * (1.0 - sig)))).astype(BF16)
        do_o = do * of
        delta = [jnp.sum(do_o[:, hcol(hh)], axis=-1, keepdims=True) for hh in range(HP)]
        do_b = do.astype(BF16)
        q = q_ref[...]
        if mode == "chunk":
            lo, hi = jnp.maximum(qi - (BAND_TILES - 1), 0), qi + 1

        def step(ki, carry, diagonal=False):
            k0 = pl.multiple_of(ki * bk, bk)
            kt = k_ref[pl.ds(k0, bk), :]
            vt = v_ref[pl.ds(k0, bk), :]
            out = []
            for hh in range(HP):
                dq, rsum = carry[hh]
                qh, kh, doh = q[:, hcol(hh)], kt[:, hcol(hh)], do_b[:, hcol(hh)]
                sc = lax.dot_general(qh, kh, NT, preferred_element_type=F32) * SCALE
                if mode == "fox":
                    sc = _fox_scores(sc, cc_ref[hh], cr_ref[hh, ki], diagonal)
                else:
                    sc = _chunk_scores(sc, q0, k0, bias_scr[hh, ki - qi + (BAND_TILES - 1)])
                p = jnp.exp(sc - lse_ref[hh])
                dp = lax.dot_general(doh, vt[:, hcol(hh)], NT, preferred_element_type=F32)
                ds = p * (dp - delta[hh])
                if mode == "fox":
                    dc_scr[hh, ki] += -jnp.sum(ds, axis=0, keepdims=True)
                    rsum = rsum + jnp.sum(ds, axis=-1, keepdims=True)
                else:
                    db_scr[hh, ki - qi + (BAND_TILES - 1)] += ds
                ds_b = (ds * SCALE).astype(BF16)
                dk_scr[pl.ds(k0, bk), hcol(hh)] += lax.dot_general(ds_b, qh, TN, preferred_element_type=F32)
                dv_scr[pl.ds(k0, bk), hcol(hh)] += lax.dot_general(p.astype(BF16), doh, TN, preferred_element_type=F32)
                out.append((dq + jnp.dot(ds_b, kh, preferred_element_type=F32), rsum))
            return tuple(out)

        init = ((jnp.zeros((bq, HEAD_DIM), F32), jnp.zeros((bq, 1), F32)),) * HP
        if mode == "fox":
            done = step(qi, lax.fori_loop(0, qi, step, init), diagonal=True)
        else:
            done = lax.fori_loop(lo, hi, step, init)
        dq_ref[...] = jnp.concatenate([dq for dq, _ in done], axis=1).astype(BF16)
        if mode == "fox":
            for hh, (_, rsum) in enumerate(done):
                dcq_ref[hh] = rsum

        @pl.when(qi == nq - 1)
        def _():
            dk_ref[...] = dk_scr[...].astype(BF16)
            dv_ref[...] = dv_scr[...].astype(BF16)
            if mode == "fox":
                dc_ref[...] = dc_scr[...]
            else:
                for hh in range(HP):
                    tot = jnp.zeros((8, rel_pad), F32)
                    for j in range(BAND_TILES):
                        wide_ = jnp.concatenate([db_scr[hh, j], jnp.zeros((CQ, TOEP - CQ), F32)], axis=1)
                        diag = jnp.sum(_skew_rows(wide_, -1), axis=0, keepdims=True)
                        for part in _split3(jnp.broadcast_to(diag, (8, TOEP))):
                            tot = tot + jnp.dot(part, oh_ref[j], preferred_element_type=F32)
                    drel_ref[hh] = tot[0:1, :]

    wide = HP * HEAD_DIM
    head_col = lambda off: pl.BlockSpec((s, wide), lambda h, i: (0, off // HP + h))
    qblk = lambda off: pl.BlockSpec((bq, wide), lambda h, i: (i, off // HP + h))
    stat = pl.BlockSpec((HP, bq, 1), lambda h, i: (h, i, 0))
    in_specs = [qblk(0), head_col(nh), head_col(2 * nh), qblk(0), qblk(0), qblk(dmo), stat]
    out_specs = [qblk(0), head_col(0), head_col(0), qblk(0)]
    out_shape = [jax.ShapeDtypeStruct((s, width), BF16)] * 4
    scratch = [pltpu.VMEM((s, wide), F32), pltpu.VMEM((s, wide), F32)]
    if mode == "fox":
        rows = pl.BlockSpec((HP, nq, 1, bk), lambda h, i: (h, 0, 0, 0))
        in_specs += [stat, rows]
        out_specs += [rows, stat]
        out_shape += [jax.ShapeDtypeStruct((nh, nq, 1, bk), F32), jax.ShapeDtypeStruct((nh, s, 1), F32)]
        scratch += [pltpu.VMEM((HP, nq, 1, bk), F32)]
    else:
        rel = pl.BlockSpec((HP, 1, rel_pad), lambda h, i: (h, 0, 0))
        in_specs += [rel, pl.BlockSpec((BAND_TILES, TOEP, rel_pad), lambda h, i: (0, 0, 0))]
        out_specs += [rel]
        out_shape += [jax.ShapeDtypeStruct((nh, 1, rel_pad), F32)]
        scratch += [pltpu.VMEM((HP, BAND_TILES, CQ, CQ), F32), pltpu.VMEM((HP, BAND_TILES, CQ, CQ), F32)]
    outs, carried = _call(
        body, name=name, grid=(nh // HP, nq), in_specs=in_specs, out_specs=out_specs, out_shape=out_shape,
        scratch=scratch, sem=("parallel", "arbitrary"), args=(qkv, qkv, qkv, gate, o, dmixed, lse, *aux), ride=ride)
    return outs if ride is None else (*outs, carried)


SBK = 256
SBQ = 2 * SBK
SB_TILES_PER_TRIP = 1
SB_DEAD = -110.0


def _suffix_excl(x, tri):
    r = x.shape[0]
    both = jnp.dot(jnp.concatenate(_split2(x), axis=0), tri, preferred_element_type=F32)
    return both[:r] + both[r:]


def _sb_logits(qh, kt, diag):
    z = lax.dot_general(qh, kt, NT, preferred_element_type=F32) * SCALE
    lom = jnp.minimum(-z, 0.0) - jnp.log(1.0 + jnp.exp(-jnp.abs(z)))
    if diag is not None:
        lom = jnp.where(diag, lom, 0.0)
    return z, lom


def _sb_fwd_tile(qh, kt, vt, tri, diag, run, acc):
    z, lom = _sb_logits(qh, kt, diag)
    a = jnp.exp(lom + z + (_suffix_excl(lom, tri) + run))
    if diag is not None:
        a = jnp.where(diag, a, 0.0)
    acc = acc + jnp.dot(a.astype(BF16), vt, preferred_element_type=F32)
    return run + jnp.sum(lom, axis=-1, keepdims=True), acc


def _sb_alive(run_a, run_b):
    return (jnp.max(jnp.maximum(run_a, run_b)) > SB_DEAD).astype(jnp.int32)


def _sb_fwd(qkv, gate, *, width, name):
    s = qkv.shape[0]
    nh = width // HEAD_DIM
    nq = s // SBQ

    def body(q_ref, k_ref, v_ref, g_ref, mixed_ref, o_ref):
        qi = pl.program_id(1)
        r_i = lax.broadcasted_iota(jnp.int32, (SBK, SBK), 0)
        c_i = lax.broadcasted_iota(jnp.int32, (SBK, SBK), 1)
        tri = (r_i > c_i).astype(BF16)
        diag = c_i < r_i
        q_a = q_ref[0:SBK, :]
        q_b = q_ref[SBK:SBQ, :]

        def kv(tile):
            k0 = pl.multiple_of(tile * SBK, SBK)
            return k_ref[pl.ds(k0, SBK), :], v_ref[pl.ds(k0, SBK), :]

        zero = (jnp.zeros((SBK, 1), F32), jnp.zeros((SBK, HEAD_DIM), F32))
        kt, vt = kv(2 * qi + 1)
        run_b, acc_b = _sb_fwd_tile(q_b, kt, vt, tri, diag, *zero)
        kt, vt = kv(2 * qi)
        run_b, acc_b = _sb_fwd_tile(q_b, kt, vt, tri, None, run_b, acc_b)
        run_a, acc_a = _sb_fwd_tile(q_a, kt, vt, tri, diag, *zero)

        trips = (2 // SB_TILES_PER_TRIP) * qi

        def step(carry):
            t, _, run_a, acc_a, run_b, acc_b = carry
            for u in range(SB_TILES_PER_TRIP):
                kt, vt = kv(2 * qi - 1 - (SB_TILES_PER_TRIP * t + u))
                run_a, acc_a = _sb_fwd_tile(q_a, kt, vt, tri, None, run_a, acc_a)
                run_b, acc_b = _sb_fwd_tile(q_b, kt, vt, tri, None, run_b, acc_b)
            return t + 1, _sb_alive(run_a, run_b), run_a, acc_a, run_b, acc_b

        _, _, _, acc_a, _, acc_b = lax.while_loop(
            lambda c: jnp.logical_and(c[0] < trips, c[1] > 0), step,
            (jnp.int32(0), _sb_alive(run_a, run_b), run_a, acc_a, run_b, acc_b))
        o = jnp.concatenate([acc_a, acc_b], axis=0)
        g = g_ref[...]
        o_ref[...] = o
        mixed_ref[...] = (o * (g * _sigmoid(g))).astype(BF16)

    head_col = lambda off: pl.BlockSpec((s, HEAD_DIM), lambda h, i: (0, off + h))
    qblk = pl.BlockSpec((SBQ, HEAD_DIM), lambda h, i: (i, h))
    return pl.pallas_call(
        body, name=name, grid=(nh, nq), in_specs=[qblk, head_col(nh), head_col(2 * nh), qblk],
        out_specs=[qblk, qblk],
        out_shape=[jax.ShapeDtypeStruct((s, width), BF16), jax.ShapeDtypeStruct((s, width), F32)],
        compiler_params=_params(("parallel", "arbitrary")),
    )(qkv, qkv, qkv, gate)


def _sb_bwd(qkv, gate, o, dmixed, *, width, name):
    s = qkv.shape[0]
    nh = width // HEAD_DIM
    nq = s // SBQ

    def body(q_ref, k_ref, v_ref, g_ref, o_ref, dm_ref, dq_ref, dk_ref, dv_ref, dg_ref, dk_scr, dv_scr):
        qi = pl.program_id(1)

        @pl.when(qi == 0)
        def _():
            dk_scr[...] = jnp.zeros_like(dk_scr)
            dv_scr[...] = jnp.zeros_like(dv_scr)

        g = g_ref[...]
        of = o_ref[...]
        dm = dm_ref[...]
        sig = _sigmoid(g)
        do = dm * (g * sig)
        dg_ref[...] = (dm * of * (sig * (1.0 + g * (1.0 - sig)))).astype(BF16)
        do_b = do.astype(BF16)
        q = q_ref[...]
        r_i = lax.broadcasted_iota(jnp.int32, (SBK, SBK), 0)
        c_i = lax.broadcasted_iota(jnp.int32, (SBK, SBK), 1)
        tri = (r_i > c_i).astype(BF16)
        tri_pre = (r_i < c_i).astype(BF16)
        diag = c_i < r_i
        q_a, q_b = q[0:SBK], q[SBK:SBQ]
        do_a, do_b2 = do_b[0:SBK], do_b[SBK:SBQ]

        def kv(t):
            k0 = pl.multiple_of(t * SBK, SBK)
            return k0, k_ref[pl.ds(k0, SBK), :], v_ref[pl.ds(k0, SBK), :]

        def mass(qh, tile_, dg_):
            return jnp.sum(_sb_logits(qh, kv(tile_)[1], dg_)[1], axis=-1, keepdims=True)

        run_b = mass(q_b, 2 * qi + 1, diag) + mass(q_b, 2 * qi, None)
        run_a = mass(q_a, 2 * qi, diag)
        trips = (2 // SB_TILES_PER_TRIP) * qi

        def scout(carry):
            t, _, run_a, run_b = carry
            for u in range(SB_TILES_PER_TRIP):
                tile_ = 2 * qi - 1 - (SB_TILES_PER_TRIP * t + u)
                run_a = run_a + mass(q_a, tile_, None)
                run_b = run_b + mass(q_b, tile_, None)
            return t + 1, _sb_alive(run_a, run_b), run_a, run_b

        walked, _, run_a, run_b = lax.while_loop(
            lambda c: jnp.logical_and(c[0] < trips, c[1] > 0), scout,
            (jnp.int32(0), _sb_alive(run_a, run_b), run_a, run_b))

        def tile(qh, doh, kt, vt, dg_, carry):
            rem, gpre, dq = carry
            z, lom = _sb_logits(qh, kt, dg_)
            rem = rem - jnp.sum(lom, axis=-1, keepdims=True)
            a = jnp.exp(lom + z + (_suffix_excl(lom, tri) + rem))
            if dg_ is not None:
                a = jnp.where(dg_, a, 0.0)
            gg = lax.dot_general(doh, vt, NT, preferred_element_type=F32) * a
            pre = _suffix_excl(gg, tri_pre) + gpre
            dz = gg * jnp.exp(lom) - pre * jnp.exp(lom + z)
            if dg_ is not None:
                dz = jnp.where(dg_, dz, 0.0)
            dz_b = (dz * SCALE).astype(BF16)
            dq = dq + jnp.dot(dz_b, kt, preferred_element_type=F32)
            return (rem, gpre + jnp.sum(gg, axis=-1, keepdims=True), dq), dz_b, a.astype(BF16)

        def both(t, ca, cb, dg_a):
            k0, kt, vt = kv(t)
            ca, dz_a, a_a = tile(q_a, do_a, kt, vt, dg_a, ca)
            cb, dz_b_, a_b = tile(q_b, do_b2, kt, vt, None, cb)
            dk_scr[pl.ds(k0, SBK), :] += lax.dot_general(jnp.concatenate([dz_a, dz_b_], axis=0), q, TN,
                                                         preferred_element_type=F32)
            dv_scr[pl.ds(k0, SBK), :] += lax.dot_general(jnp.concatenate([a_a, a_b], axis=0), do_b, TN,
                                                         preferred_element_type=F32)
            return ca, cb

        def step(t, carry):
            ca, cb = carry
            for u in range(SB_TILES_PER_TRIP):
                ca, cb = both(SB_TILES_PER_TRIP * t + u, ca, cb, None)
            return ca, cb

        zero = (jnp.zeros((SBK, 1), F32), jnp.zeros((SBK, HEAD_DIM), F32))
        ca, cb = lax.fori_loop(trips - walked, trips, step, ((run_a, *zero), (run_b, *zero)))
        ca, cb = both(2 * qi, ca, cb, diag)
        k0, kt, vt = kv(2 * qi + 1)
        cb, dz_b_, a_b = tile(q_b, do_b2, kt, vt, diag, cb)
        dk_scr[pl.ds(k0, SBK), :] += lax.dot_general(dz_b_, q_b, TN, preferred_element_type=F32)
        dv_scr[pl.ds(k0, SBK), :] += lax.dot_general(a_b, do_b2, TN, preferred_element_type=F32)
        dq_ref[...] = jnp.concatenate([ca[2], cb[2]], axis=0).astype(BF16)

        @pl.when(qi == nq - 1)
        def _():
            dk_ref[...] = dk_scr[...].astype(BF16)
            dv_ref[...] = dv_scr[...].astype(BF16)

    head_col = lambda off: pl.BlockSpec((s, HEAD_DIM), lambda h, i: (0, off + h))
    qblk = pl.BlockSpec((SBQ, HEAD_DIM), lambda h, i: (i, h))
    return pl.pallas_call(
        body, name=name, grid=(nh, nq),
        in_specs=[qblk, head_col(nh), head_col(2 * nh), qblk, qblk, qblk],
        out_specs=[qblk, head_col(0), head_col(0), qblk],
        out_shape=[jax.ShapeDtypeStruct((s, width), BF16)] * 4,
        scratch_shapes=[pltpu.VMEM((s, HEAD_DIM), F32), pltpu.VMEM((s, HEAD_DIM), F32)],
        compiler_params=_params(("parallel", "arbitrary")),
    )(qkv, qkv, qkv, gate, o, dmixed)


HBM = pl.BlockSpec(memory_space=pl.ANY)
MESH = pl.DeviceIdType.MESH


class _GatherRide:
    def __init__(self, shard):
        self.src = shard
        self.out_shape = jax.ShapeDtypeStruct((N_DEV, *shard.shape), shard.dtype)
        self.scratch = [pltpu.SemaphoreType.DMA((7,)), pltpu.SemaphoreType.DMA((7,)), pltpu.SemaphoreType.DMA]

    def _copies(self, x_ref, out_ref, send_sems, recv_sems, local_sem):
        x, y, c = lax.axis_index("x"), lax.axis_index("y"), lax.axis_index("c")
        me, sibling = (x, y, c), (x, y, 1 - c)
        chips = [(1 - x, y), (x, 1 - y), (1 - x, 1 - y)]

        def slot(px, py, pc):
            return out_ref.at[4 * px + 2 * py + pc]

        def copy(k, block, to, src=None):
            return pltpu.make_async_remote_copy(
                src_ref=slot(*block) if src is None else src, dst_ref=slot(*block),
                send_sem=send_sems.at[k], recv_sem=recv_sems.at[k], device_id=to, device_id_type=MESH)

        mine = pltpu.make_async_copy(x_ref, slot(*me), local_sem)
        first = [copy(0, me, sibling, src=x_ref)]
        first += [copy(1 + j, me, (*chip, c), src=x_ref) for j, chip in enumerate(chips)]
        landed = [copy(1 + j, (*chip, c), me) for j, chip in enumerate(chips)]
        passed = [copy(4 + j, (*chip, c), sibling) for j, chip in enumerate(chips)]
        from_sibling = [copy(0, sibling, me)] + [copy(4 + j, (*chip, 1 - c), me) for j, chip in enumerate(chips)]
        return mine, first, landed, passed, from_sibling

    def start(self, *refs):
        mine, first, _, _, _ = self._copies(*refs)
        mine.start()
        for cp in first:
            cp.start()

    def relay(self, *refs):
        _, _, landed, passed, _ = self._copies(*refs)
        for got, onward in zip(landed, passed):
            got.wait_recv()
            onward.start()

    def finish(self, *refs):
        mine, first, _, passed, from_sibling = self._copies(*refs)
        for cp in from_sibling:
            cp.wait_recv()
        for cp in first + passed:
            cp.wait_send()
        mine.wait()


def _all_gather(shard, *, name):
    ride = _GatherRide(shard)

    def body(*refs):
        ride.start(*refs)
        ride.relay(*refs)
        ride.finish(*refs)

    return pl.pallas_call(body, name=name, in_specs=[HBM], out_specs=HBM, out_shape=ride.out_shape,
                          scratch_shapes=ride.scratch)(shard)


class _Ride:
    def __init__(self, src, *, gather, chips=False):
        self.src, self.gather, self.chips = src, gather, chips
        n = N_DEV // 2 if chips else N_DEV
        self.out_shape = jax.ShapeDtypeStruct((n, *(src.shape if gather else src.shape[1:])), src.dtype)
        self.scratch = [pltpu.SemaphoreType.DMA((7,)), pltpu.SemaphoreType.DMA((7,)), pltpu.SemaphoreType.DMA]

    def _copies(self, src_ref, out_ref, send_sems, recv_sems, local_sem):
        x, y, c = lax.axis_index("x"), lax.axis_index("y"), lax.axis_index("c")
        slot = (lambda px, py, pc: 2 * px + py) if self.chips else (lambda px, py, pc: 4 * px + 2 * py + pc)
        me = slot(x, y, c)
        pick = (lambda j: src_ref) if self.gather else (lambda j: src_ref.at[j])
        mine = pltpu.make_async_copy(pick(me), out_ref.at[me], local_sem)
        copies = []
        for k in range(2 if self.chips else 1, N_DEV, 2 if self.chips else 1):
            px, py, pc = x ^ ((k >> 2) & 1), y ^ ((k >> 1) & 1), c ^ (k & 1)
            copies.append(pltpu.make_async_remote_copy(
                src_ref=pick(slot(px, py, pc)), dst_ref=out_ref.at[me],
                send_sem=send_sems.at[k - 1], recv_sem=recv_sems.at[k - 1],
                device_id=(px, py, pc), device_id_type=MESH))
        return mine, copies

    def start(self, *refs):
        mine, copies = self._copies(*refs)
        mine.start()
        for cp in copies:
            cp.start()

    def finish(self, *refs):
        mine, copies = self._copies(*refs)
        for cp in copies:
            cp.wait_recv()
        for cp in copies:
            cp.wait_send()
        mine.wait()


def _exchange(src, *, gather, name):
    ride = _Ride(src, gather=gather)

    def body(*refs):
        ride.start(*refs)
        ride.finish(*refs)

    return pl.pallas_call(body, name=name, in_specs=[HBM], out_specs=HBM, out_shape=ride.out_shape,
                          scratch_shapes=ride.scratch)(src)


def _presum_on_chip(parts, *, name, rows=256):
    _, r, c_ = parts.shape
    rows = min(rows, r)
    assert r % rows == 0
    by_core = parts.reshape(N_DEV // 2, 2, r, c_)

    def swap(src_ref, out_ref, send_sem, recv_sem):
        x, y, c = lax.axis_index("x"), lax.axis_index("y"), lax.axis_index("c")
        cp = pltpu.make_async_remote_copy(
            src_ref=src_ref.at[:, 1 - c], dst_ref=out_ref, send_sem=send_sem, recv_sem=recv_sem,
            device_id=(x, y, 1 - c), device_id_type=MESH)
        cp.start()
        cp.wait()

    got = pl.pallas_call(swap, name=name + "_swap", in_specs=[HBM], out_specs=HBM,
                         out_shape=jax.ShapeDtypeStruct((N_DEV // 2, r, c_), parts.dtype),
                         scratch_shapes=[pltpu.SemaphoreType.DMA, pltpu.SemaphoreType.DMA])(by_core)

    def add(core_ref, a_ref, b_ref, o_ref):
        del core_ref
        o_ref[...] = (a_ref[...].astype(F32) + b_ref[...].astype(F32)).astype(o_ref.dtype)

    blk = pl.BlockSpec((1, rows, c_), lambda j, i, core: (j, i, 0))
    mine = pl.BlockSpec((1, None, rows, c_), lambda j, i, core: (j, core[0], i, 0))
    core = jnp.reshape(lax.axis_index("c"), (1,)).astype(jnp.int32)
    return pl.pallas_call(
        add, name=name + "_sum",
        grid_spec=pltpu.PrefetchScalarGridSpec(num_scalar_prefetch=1, grid=(N_DEV // 2, r // rows),
                                               in_specs=[mine, blk], out_specs=blk),
        out_shape=jax.ShapeDtypeStruct((N_DEV // 2, r, c_), parts.dtype),
        compiler_params=_params(("parallel", "parallel")))(core, by_core, got)


def _call(body, *, name, grid, in_specs, out_specs, out_shape, scratch, sem, args, ride=None):
    if ride is None:
        outs = pl.pallas_call(body, name=name, grid=grid, in_specs=in_specs, out_specs=out_specs, out_shape=out_shape,
                              scratch_shapes=scratch, compiler_params=_params(sem))(*args)
        return list(outs), None
    n_in, n_out = len(in_specs), len(out_specs)

    def carrying(*refs):
        ins, src_ref = refs[:n_in], refs[n_in]
        outs, dst_ref = refs[n_in + 1:n_in + 1 + n_out], refs[n_in + 1 + n_out]
        rest = refs[n_in + 2 + n_out:]
        own, sems = rest[:len(rest) - 3], rest[len(rest) - 3:]
        ids = [pl.program_id(a) for a in range(len(grid))]
        first = functools.reduce(jnp.logical_and, [i == 0 for i in ids])
        last = functools.reduce(jnp.logical_and, [i == n - 1 for i, n in zip(ids, grid)])

        @pl.when(first)
        def _():
            ride.start(src_ref, dst_ref, *sems)

        if hasattr(ride, "relay"):
            at = functools.reduce(jnp.logical_and, [ids[0] == (3 * grid[0]) // 4] + [i == 0 for i in ids[1:]])

            @pl.when(at)
            def _():
                ride.relay(src_ref, dst_ref, *sems)

        body(*ins, *outs, *own)

        @pl.when(last)
        def _():
            ride.finish(src_ref, dst_ref, *sems)

    outs = pl.pallas_call(
        carrying, name=name, grid=grid, in_specs=[*in_specs, HBM], out_specs=[*out_specs, HBM],
        out_shape=[*out_shape, ride.out_shape], scratch_shapes=[*scratch, *ride.scratch],
        compiler_params=_params(("arbitrary",) * len(grid)))(*args, ride.src)
    return list(outs[:-1]), outs[-1]


def _adamw(parts, w, m, v, *, name, rows):
    r, c_ = w.shape
    rows = min(rows, r)
    assert r % rows == 0
    c1 = 1.0 / (1.0 - ADAM_B1 ** ADAM_STEP)
    c2 = 1.0 / (1.0 - ADAM_B2 ** ADAM_STEP)

    def body(p_ref, w_ref, m_ref, v_ref, g_ref, d_ref, nm_ref, nv_ref):
        g = p_ref[0].astype(F32)
        for i in range(1, parts.shape[0]):
            g = g + p_ref[i].astype(F32)
        nm = ADAM_B1 * m_ref[...] + (1.0 - ADAM_B1) * g
        nv = ADAM_B2 * v_ref[...] + (1.0 - ADAM_B2) * (g * g)
        g_ref[...] = g
        nm_ref[...] = nm
        nv_ref[...] = nv
        d_ref[...] = -ADAM_LR * ((nm * c1) / (jnp.sqrt(nv * c2) + ADAM_EPS) + ADAM_WD * w_ref[...])

    blk = pl.BlockSpec((rows, c_), lambda i: (i, 0))
    return pl.pallas_call(
        body, name=name, grid=(r // rows,),
        in_specs=[pl.BlockSpec((parts.shape[0], rows, c_), lambda i: (0, i, 0)), blk, blk, blk],
        out_specs=[blk] * 4, out_shape=[jax.ShapeDtypeStruct((r, c_), F32)] * 4,
        compiler_params=_params(("parallel",)),
    )(parts, w, m, v)


def _pad_cols(a, n):
    return jnp.pad(a, ((0, 0), (0, n - a.shape[1])))


def _fox_dc(shares, nh, s):
    key_side, query_side = shares
    return _pad_cols(jnp.transpose(key_side.reshape(nh, s) + query_side.reshape(nh, s)), 128)


def _local_step(x, target, norm_pre, norm_post, w_in_e, w_f, b_f, rel_bias, sh_out_e, sh_in_o, sh_out_o):
    s, d = x.shape
    wa = d // 2
    nha = wa // HEAD_DIM
    nq = s // CQ
    gather = lambda shard: _Ride(shard, gather=True)
    scatter = lambda parts: _Ride(parts, gather=False)

    h0 = _rms_fwd(x, norm_pre[0:1], name="rms_pre0")
    proj = lambda w, off, n, dt, nm, **kw: _matmul(h0, w, mode="nn", m=s, n=n, k=d, out_dtype=dt, name=nm,
                                                   b_off=(0, off), **kw)
    qkv_a, w_out_e = proj(w_in_e, 0, 3 * wa, BF16, "proj_qkv_a", ride=gather(sh_out_e))
    w_out_e = w_out_e.reshape(d, d)
    g_a = proj(w_in_e, 3 * wa, wa, F32, "proj_gate_a")
    qkv_b = proj(w_in_e, 4 * wa, 3 * wa, BF16, "proj_qkv_b")
    g_b = proj(w_in_e, 7 * wa, wa, F32, "proj_gate_b")
    af = _matmul(h0, w_f, mode="nn", m=s, n=128, k=d, out_dtype=F32, name="proj_forget")
    bias128 = _pad_cols(b_f, 128)
    cum = _gate_fwd(af, bias128, name="forget_cumsum")
    c_t = jnp.transpose(cum[:, :nha])
    c_col = c_t.reshape(nha, s, 1)
    c_row = c_t.reshape(nha, nq, 1, CQ)
    mixed_a, o_a, lse_a, w_in_o = _softmax_fwd(qkv_a, g_a, (c_col, c_row), mode="fox", width=wa, name="fox_fwd",
                                               ride=_GatherRide(sh_in_o))
    rel_aux = (_pad_cols(rel_bias, REL_PAD).reshape(nha, 1, REL_PAD), _rel_onehot())
    mixed_b, o_b, lse_b, w_out_o = _softmax_fwd(qkv_b, g_b, rel_aux, mode="chunk", width=wa, name="chunk_fwd",
                                                ride=gather(sh_out_o))
    w_out_o = w_out_o.reshape(d, d)
    mixed0 = (mixed_a, mixed_b)
    y0 = _matmul(mixed0, w_out_e, mode="nn", m=s, n=d, k=d, out_dtype=F32, name="out_proj0")
    x1 = _post_fwd(x, y0, norm_post[0:1], name="post0")

    h1 = _rms_fwd(x1, norm_pre[1:2], name="rms_pre1")
    qkv_c = _matmul(h1, w_in_o, mode="nn", m=s, n=3 * d, k=d, out_dtype=BF16, name="proj_qkv_c", b_groups=N_DEV)
    g_c = _matmul(h1, w_in_o, mode="nn", m=s, n=d, k=d, out_dtype=F32, name="proj_gate_c", b_off=(0, 3 * d),
                  b_groups=N_DEV)
    mixed1, o_c = _sb_fwd(qkv_c, g_c, width=d, name="sb_fwd")
    y1 = _matmul(mixed1, w_out_o, mode="nn", m=s, n=d, k=d, out_dtype=F32, name="out_proj1")
    x2 = _post_fwd(x1, y1, norm_post[1:2], name="post1")

    loss, dx2 = _loss_head(x2, target, name="loss_head")

    dy1, dgpost1 = _post_bwd(dx2, y1, norm_post[1:2], name="post_bwd1")
    dmixed1 = _matmul(dy1, w_out_o, mode="nt", m=s, n=d, k=d, out_dtype=F32, name="dmixed1")
    dw_out_o = _matmul(mixed1, dy1, mode="tn", m=d, n=d, k=s, out_dtype=BF16, name="dw_out1")
    dq_c, dk_c, dv_c, dg_c = _sb_bwd(qkv_c, g_c, o_c, dmixed1, width=d, name="sb_bwd")
    dproj1 = (dq_c, dk_c, dv_c, dg_c)
    dh1 = _matmul(dproj1, w_in_o, mode="nt", m=s, n=d, k=4 * d, out_dtype=F32, name="dh1", b_groups=N_DEV)
    parts_in_o = _matmul(h1, dproj1, mode="tn", m=d, n=4 * d, k=s, out_dtype=BF16, name="dw_in1", out_groups=N_DEV)
    dx1, dgpre1 = _pre_bwd((dh1,), x1, norm_pre[1:2], dx2, name="pre_bwd1")

    dy0, dgpost0 = _post_bwd(dx1, y0, norm_post[0:1], name="post_bwd0")
    dmixed0 = _matmul(dy0, w_out_e, mode="nt", m=s, n=d, k=d, out_dtype=F32, name="dmixed0")
    dw_out_e = _matmul(mixed0, dy0, mode="tn", m=d, n=d, k=s, out_dtype=BF16, name="dw_out0")
    chip_in_o = _presum_on_chip(parts_in_o, name="rs_w_in_odd")
    dq_a, dk_a, dv_a, dg_a, *dc_shares, got_in_o = _softmax_bwd(
        qkv_a, g_a, o_a, dmixed0, 0, lse_a, (c_col, c_row), mode="fox", width=wa, name="fox_bwd",
        ride=_Ride(chip_in_o, gather=False, chips=True))
    dq_b, dk_b, dv_b, dg_b, drel, got_out_o = _softmax_bwd(
        qkv_b, g_b, o_b, dmixed0, wa, lse_b, rel_aux, mode="chunk", width=wa, name="chunk_bwd",
        ride=scatter(dw_out_o.reshape(N_DEV, d // N_DEV, d)))
    dc = _fox_dc(dc_shares, nha, s)
    daf, dbf = _gate_bwd(dc, af, bias128, name="forget_bwd")
    dproj0 = (dq_a, dk_a, dv_a, dg_a, dq_b, dk_b, dv_b, dg_b)
    dw_in_e, got_out_e = _matmul(h0, dproj0, mode="tn", m=d, n=8 * wa, k=s, out_dtype=BF16, name="dw_in0", tk=512,
                                 ride=scatter(dw_out_e.reshape(N_DEV, d // N_DEV, d)))
    dw_f = _matmul(h0, daf, mode="tn", m=d, n=128, k=s, out_dtype=BF16, name="dw_forget")
    dw_e = jnp.concatenate([dw_in_e, dw_f[:, :nha]], axis=1)
    parts_in_e = jnp.transpose(dw_e.reshape(d, N_DEV, dw_e.shape[1] // N_DEV), (1, 0, 2))
    chip_in_e = _presum_on_chip(parts_in_e, name="rs_w_in_even")
    dh0, got_in_e = _matmul(dproj0, w_in_e, mode="nt", m=s, n=d, k=8 * wa, out_dtype=F32, name="dh0_main", tm=512,
                            ride=_Ride(chip_in_e, gather=False, chips=True))
    dh0f = _matmul(daf, w_f, mode="nt", m=s, n=d, k=128, out_dtype=F32, name="dh0_forget")
    dx0, dgpre0 = _pre_bwd((dh0, dh0f), x, norm_pre[0:1], dx1, name="pre_bwd0")

    small = dict(
        norm_pre=jnp.concatenate([dgpre0, dgpre1], axis=0),
        norm_post=jnp.concatenate([dgpost0, dgpost1], axis=0),
        b_f=dbf[:, :nha], rel_bias=drel[:, 0, :N_REL])
    got = dict(w_in_even=got_in_e, w_out_even=got_out_e, w_in_odd=got_in_o, w_out_odd=got_out_o)
    return loss, dx0, got, small


def _pack_small(norm_pre, norm_post, b_f, rel_bias):
    flat = jnp.concatenate([norm_pre.reshape(-1), norm_post.reshape(-1), b_f.reshape(-1), rel_bias.reshape(-1)])
    n = flat.shape[0]
    rows = -(-n // 128)
    rows = -(-rows // 8) * 8
    return jnp.pad(flat, (0, rows * 128 - n)).reshape(rows, 128)


def _unpack_small(slab, shapes):
    flat = slab.reshape(-1)
    out, off = [], 0
    for shp in shapes:
        n = int(np.prod(shp))
        out.append(flat[off:off + n].reshape(shp))
        off += n
    return out


def kernel(x, norm_pre, norm_post, w_in_even, b_f_even, rel_bias_even, w_out_even, w_in_odd, w_out_odd, loss_target, m_norm_pre, m_norm_post, m_w_in_even, m_b_f_even, m_rel_bias_even, m_w_out_even, m_w_in_odd, m_w_out_odd, v_norm_pre, v_norm_post, v_w_in_even, v_b_f_even, v_rel_bias_even, v_w_out_even, v_w_in_odd, v_w_out_odd):
    _, s, d = x.shape
    wa = d // 2
    nha = wa // HEAD_DIM
    in_e = w_in_even.shape[2] * N_DEV

    w_in_e_all = jnp.transpose(_all_gather(w_in_even[0].astype(BF16), name="ag_w_in_even"), (1, 0, 2)).reshape(d, in_e)
    w_main = w_in_e_all
    w_f = _pad_cols(w_in_e_all[:, 8 * wa:], 128)

    loss, dx, got, small = _local_step(
        x[0], loss_target[0], norm_pre, norm_post, w_main, w_f, b_f_even, rel_bias_even[0],
        w_out_even[0].astype(BF16), w_in_odd[0].astype(BF16), w_out_odd[0].astype(BF16))

    upd = {}
    upd["w_in_even"] = _adamw(got["w_in_even"], w_in_even[0], m_w_in_even[0], v_w_in_even[0],
                              name="adamw_w_in_even", rows=128)
    upd["w_out_even"] = _adamw(got["w_out_even"], w_out_even[0], m_w_out_even[0], v_w_out_even[0],
                               name="adamw_w_out_even", rows=64)
    upd["w_in_odd"] = _adamw(got["w_in_odd"], w_in_odd[0], m_w_in_odd[0], v_w_in_odd[0],
                             name="adamw_w_in_odd", rows=128)
    upd["w_out_odd"] = _adamw(got["w_out_odd"], w_out_odd[0], m_w_out_odd[0], v_w_out_odd[0],
                              name="adamw_w_out_odd", rows=64)

    shapes = [norm_pre.shape, norm_post.shape, b_f_even.shape, rel_bias_even.shape]
    g_slab = _pack_small(small["norm_pre"], small["norm_post"], small["b_f"], small["rel_bias"])
    parts_small = _exchange(g_slab, gather=True, name="ar_small")
    sm = _adamw(parts_small, _pack_small(norm_pre, norm_post, b_f_even, rel_bias_even),
                _pack_small(m_norm_pre, m_norm_post, m_b_f_even, m_rel_bias_even),
                _pack_small(v_norm_pre, v_norm_post, v_b_f_even, v_rel_bias_even), name="adamw_small", rows=g_slab.shape[0])
    sm = [_unpack_small(a, shapes) for a in sm]

    total = lax.psum(loss[0, 0], ("x", "y", "c"))

    def leaves(kind):
        return (sm[kind][0], sm[kind][1], upd["w_in_even"][kind][None], sm[kind][2], sm[kind][3],
                upd["w_out_even"][kind][None], upd["w_in_odd"][kind][None], upd["w_out_odd"][kind][None])

    return (total, dx[None], *leaves(0), *leaves(1), *leaves(2), *leaves(3))
```

```python
import functools

import numpy as np
import jax
import jax.numpy as jnp
from jax import lax
from jax.experimental import pallas as pl
from jax.experimental.pallas import tpu as pltpu

F32 = jnp.float32
BF16 = jnp.bfloat16

HEAD_DIM = 128
CHUNK = 64
LEFT_CHUNKS = 8
REL_CLIP = 128
N_REL = 2 * REL_CLIP + 1
RMS_EPS = 1e-6
SCALE = HEAD_DIM ** -0.5

ADAM_LR = 0.001
ADAM_B1 = 0.9
ADAM_B2 = 0.999
ADAM_EPS = 1e-08
ADAM_WD = 0.01
ADAM_STEP = 10

N_DEV = 8
V7X_VMEM_LIMIT_BYTES = 56 * 1024 * 1024
NEG = -1e30

NT = (((1,), (1,)), ((), ()))
TN = (((0,), (0,)), ((), ()))
NN = (((1,), (0,)), ((), ()))

CQ = 256
BAND_TILES = 3
TOEP = 2 * CQ
assert (BAND_TILES - 1) * CQ == LEFT_CHUNKS * CHUNK


def _params(sem):
    return pltpu.CompilerParams(dimension_semantics=sem, vmem_limit_bytes=V7X_VMEM_LIMIT_BYTES)


def _split3(x):
    hi = x.astype(BF16)
    r1 = x - hi.astype(F32)
    mid = r1.astype(BF16)
    lo = (r1 - mid.astype(F32)).astype(BF16)
    return hi, mid, lo


def _split2(x):
    hi = x.astype(BF16)
    lo = (x - hi.astype(F32)).astype(BF16)
    return hi, lo


def _sigmoid(g):
    return 1.0 / (1.0 + jnp.exp(-g))


def _tile(n, cap, *offsets):
    if n <= 128:
        return n
    t = (min(cap, n) // 128) * 128
    while n % t or any(o % t for o in offsets):
        t -= 128
    return t


def _matmul(a, b, *, mode, m, n, k, out_dtype, name, a_off=(0, 0), b_off=(0, 0), tm=1024, tn=1024, tk=1024,
            b_groups=None, out_groups=None, ride=None):
    a_m, a_k = (a_off if mode in ("nn", "nt") else a_off[::-1])
    b_k, b_n = (b_off if mode in ("nn", "tn") else b_off[::-1])
    b_group = (b.shape[2],) if b_groups else ()
    a_parts, b_parts = (a if isinstance(a, tuple) else None), (b if isinstance(b, tuple) else None)
    assert not (a_parts and b_parts) and not (b_parts and mode == "nt")
    a_piece = (a_parts[0].shape[1],) if a_parts else ()
    b_piece = (b_parts[0].shape[1],) if b_parts else ()
    a_on_k = mode in ("nn", "nt")
    tm = _tile(m, tm, a_m, *(() if a_on_k else a_piece))
    tn = _tile(n, tn, b_n, *b_piece)
    tk = _tile(k, tk, a_k, b_k, *(a_piece if a_on_k else ()))
    if b_groups and mode in ("nn", "tn"):
        tn = _tile(n, tn, b_n, *b_group)
    if b_groups and mode == "nt":
        tk = _tile(k, tk, a_k, b_k, *b_group, *a_piece)
    if out_groups:
        tn = _tile(n, tn, b_n, n // out_groups, *b_piece, *(b_group if mode != "nt" else ()))
    nk = k // tk

    def piece_specs(parts, tile, walk, block, place):
        per = parts[0].shape[1] // tile
        def spec(g):
            def index(i, j, l):
                at = walk(i, j, l) - g * per
                return place(i, j, l, jnp.clip(at, 0, per - 1), jnp.logical_and(at >= 0, at < per))
            return pl.BlockSpec(block, index)
        return [spec(g) for g in range(len(parts))], per

    if a_parts:
        if a_on_k:
            a_specs, a_per = piece_specs(a_parts, tk, lambda i, j, l: l, (tm, tk), lambda i, j, l, c, on: (i, c))
        else:
            a_specs, a_per = piece_specs(a_parts, tm, lambda i, j, l: i, (tk, tm),
                                         lambda i, j, l, c, on: (jnp.where(on, l, 0), c))
    elif mode in ("nn", "nt"):
        ao = (a_off[0] // tm, a_off[1] // tk)
        a_specs = [pl.BlockSpec((tm, tk), lambda i, j, l: (i + ao[0], l + ao[1]))]
    else:
        ao = (a_off[0] // tk, a_off[1] // tm)
        a_specs = [pl.BlockSpec((tk, tm), lambda i, j, l: (l + ao[0], i + ao[1]))]
    if b_parts:
        b_specs, b_per = piece_specs(b_parts, tn, lambda i, j, l: j, (tk, tn),
                                     lambda i, j, l, c, on: (jnp.where(on, l, 0), c))
        b_spec = None
    elif mode in ("nn", "tn"):
        bo = (b_off[0] // tk, b_off[1] // tn)
        if b_groups:
            per = b.shape[2] // tn
            b_spec = pl.BlockSpec((None, tk, tn), lambda i, j, l: ((j + bo[1]) // per, l + bo[0], (j + bo[1]) % per))
        else:
            b_spec = pl.BlockSpec((tk, tn), lambda i, j, l: (l + bo[0], j + bo[1]))
    else:
        bo = (b_off[0] // tn, b_off[1] // tk)
        if b_groups:
            per = b.shape[2] // tk
            b_spec = pl.BlockSpec((None, tn, tk), lambda i, j, l: ((l + bo[1]) // per, j + bo[0], (l + bo[1]) % per))
        else:
            b_spec = pl.BlockSpec((tn, tk), lambda i, j, l: (j + bo[0], l + bo[1]))
    if out_groups:
        oper = (n // out_groups) // tn
        out_spec = pl.BlockSpec((None, tm, tn), lambda i, j, l: (j // oper, i, j % oper))
        out_shape = jax.ShapeDtypeStruct((out_groups, m, n // out_groups), out_dtype)
    else:
        out_spec = pl.BlockSpec((tm, tn), lambda i, j, l: (i, j))
        out_shape = jax.ShapeDtypeStruct((m, n), out_dtype)
    dn = {"nn": NN, "nt": NT, "tn": TN}[mode]
    if not b_parts:
        b_specs = [b_spec]
    na = len(a_specs)

    def body(*refs):
        a_refs, b_refs, (o_ref, acc_ref) = refs[:na], refs[na:len(refs) - 2], refs[len(refs) - 2:]

        @pl.when(pl.program_id(2) == 0)
        def _():
            acc_ref[...] = jnp.zeros_like(acc_ref)

        def add(a_ref, b_ref):
            acc_ref[...] += lax.dot_general(a_ref[...], b_ref[...], dn, preferred_element_type=F32)

        if a_parts or b_parts:
            if a_parts:
                active = pl.program_id(2 if a_on_k else 0) // a_per
            else:
                active = pl.program_id(1) // b_per
            for g in range(max(na, len(b_refs))):
                pl.when(active == g)(functools.partial(add, a_refs[g if a_parts else 0], b_refs[g if b_parts else 0]))
        else:
            add(a_refs[0], b_refs[0])

        @pl.when(pl.program_id(2) == nk - 1)
        def _():
            o_ref[...] = acc_ref[...].astype(out_dtype)

    (out,), carried = _call(
        body, name=name, grid=(m // tm, n // tn, nk), in_specs=[*a_specs, *b_specs],
        out_specs=[out_spec], out_shape=[out_shape],
        scratch=[pltpu.VMEM((tm, tn), F32)], sem=("parallel", "parallel", "arbitrary"),
        args=(*(a_parts or (a,)), *(b_parts or (b,))), ride=ride)
    return out if ride is None else (out, carried)


ROWS = 128


def _rms_fwd(x, gain, *, name):
    s, d = x.shape

    def body(x_ref, g_ref, h_ref):
        xf = x_ref[...]
        r = lax.rsqrt(jnp.mean(xf * xf, axis=-1, keepdims=True) + RMS_EPS)
        h_ref[...] = ((xf * r) * g_ref[...]).astype(BF16)

    return pl.pallas_call(
        body, name=name, grid=(s // ROWS,),
        in_specs=[pl.BlockSpec((ROWS, d), lambda i: (i, 0)), pl.BlockSpec((1, d), lambda i: (0, 0))],
        out_specs=pl.BlockSpec((ROWS, d), lambda i: (i, 0)),
        out_shape=jax.ShapeDtypeStruct((s, d), BF16),
        compiler_params=_params(("parallel",)),
    )(x, gain)


def _post_fwd(x, y, gain, *, name):
    s, d = x.shape

    def body(x_ref, y_ref, g_ref, o_ref):
        yf = y_ref[...]
        r = lax.rsqrt(jnp.mean(yf * yf, axis=-1, keepdims=True) + RMS_EPS)
        o_ref[...] = x_ref[...] + (yf * r) * g_ref[...]

    return pl.pallas_call(
        body, name=name, grid=(s // ROWS,),
        in_specs=[pl.BlockSpec((ROWS, d), lambda i: (i, 0)), pl.BlockSpec((ROWS, d), lambda i: (i, 0)),
                  pl.BlockSpec((1, d), lambda i: (0, 0))],
        out_specs=pl.BlockSpec((ROWS, d), lambda i: (i, 0)),
        out_shape=jax.ShapeDtypeStruct((s, d), F32),
        compiler_params=_params(("parallel",)),
    )(x, y, gain)


def _loss_head(xo, target, *, name):
    s, d = xo.shape
    inv_d = 1.0 / d

    def body(x_ref, t_ref, loss_ref, dx_ref):
        @pl.when(pl.program_id(0) == 0)
        def _():
            loss_ref[...] = jnp.zeros_like(loss_ref)

        e = x_ref[...] - t_ref[...]
        dx_ref[...] = e * inv_d
        loss_ref[...] += 0.5 * jnp.sum(jnp.mean(e * e, axis=-1, keepdims=True), axis=0, keepdims=True)

    return pl.pallas_call(
        body, name=name, grid=(s // ROWS,),
        in_specs=[pl.BlockSpec((ROWS, d), lambda i: (i, 0)), pl.BlockSpec((ROWS, d), lambda i: (i, 0))],
        out_specs=[pl.BlockSpec((1, 1), lambda i: (0, 0)), pl.BlockSpec((ROWS, d), lambda i: (i, 0))],
        out_shape=[jax.ShapeDtypeStruct((1, 1), F32), jax.ShapeDtypeStruct((s, d), F32)],
        compiler_params=_params(("arbitrary",)),
    )(xo, target)


def _post_bwd(dxo, y, gain, *, name):
    s, d = y.shape

    def body(dx_ref, y_ref, g_ref, dy_ref, dg_ref):
        @pl.when(pl.program_id(0) == 0)
        def _():
            dg_ref[...] = jnp.zeros_like(dg_ref)

        yf = y_ref[...]
        dxo_ = dx_ref[...]
        r = lax.rsqrt(jnp.mean(yf * yf, axis=-1, keepdims=True) + RMS_EPS)
        nrm = yf * r
        dg_ref[...] += jnp.sum(dxo_ * nrm, axis=0, keepdims=True)
        dn = dxo_ * g_ref[...]
        dy_ref[...] = (r * (dn - nrm * jnp.mean(dn * nrm, axis=-1, keepdims=True))).astype(BF16)

    return pl.pallas_call(
        body, name=name, grid=(s // ROWS,),
        in_specs=[pl.BlockSpec((ROWS, d), lambda i: (i, 0)), pl.BlockSpec((ROWS, d), lambda i: (i, 0)),
                  pl.BlockSpec((1, d), lambda i: (0, 0))],
        out_specs=[pl.BlockSpec((ROWS, d), lambda i: (i, 0)), pl.BlockSpec((1, d), lambda i: (0, 0))],
        out_shape=[jax.ShapeDtypeStruct((s, d), BF16), jax.ShapeDtypeStruct((1, d), F32)],
        compiler_params=_params(("arbitrary",)),
    )(dxo, y, gain)


def _pre_bwd(dhs, x, gain, dres, *, name):
    s, d = x.shape
    n_dh = len(dhs)

    def body(*refs):
        dh_refs = refs[:n_dh]
        x_ref, g_ref, dr_ref, dx_ref, dg_ref = refs[n_dh:]

        @pl.when(pl.program_id(0) == 0)
        def _():
            dg_ref[...] = jnp.zeros_like(dg_ref)

        xf = x_ref[...]
        dh_ = dh_refs[0][...]
        for extra in dh_refs[1:]:
            dh_ = dh_ + extra[...]
        r = lax.rsqrt(jnp.mean(xf * xf, axis=-1, keepdims=True) + RMS_EPS)
        nrm = xf * r
        dg_ref[...] += jnp.sum(dh_ * nrm, axis=0, keepdims=True)
        dn = dh_ * g_ref[...]
        dx_ref[...] = dr_ref[...] + r * (dn - nrm * jnp.mean(dn * nrm, axis=-1, keepdims=True))

    return pl.pallas_call(
        body, name=name, grid=(s // ROWS,),
        in_specs=[pl.BlockSpec((ROWS, d), lambda i: (i, 0))] * (n_dh + 1)
        + [pl.BlockSpec((1, d), lambda i: (0, 0)), pl.BlockSpec((ROWS, d), lambda i: (i, 0))],
        out_specs=[pl.BlockSpec((ROWS, d), lambda i: (i, 0)), pl.BlockSpec((1, d), lambda i: (0, 0))],
        out_shape=[jax.ShapeDtypeStruct((s, d), F32), jax.ShapeDtypeStruct((1, d), F32)],
        compiler_params=_params(("arbitrary",)),
    )(*dhs, x, gain, dres)


GB = 256


def _gate_fwd(af, bias, *, name):
    s, w = af.shape

    def body(af_ref, b_ref, c_ref, carry_ref):
        @pl.when(pl.program_id(0) == 0)
        def _():
            carry_ref[...] = jnp.zeros_like(carry_ref)

        z = af_ref[...] + b_ref[...]
        lf = jnp.minimum(z, 0.0) - jnp.log(1.0 + jnp.exp(-jnp.abs(z)))
        r_i = lax.broadcasted_iota(jnp.int32, (GB, GB), 0)
        c_i = lax.broadcasted_iota(jnp.int32, (GB, GB), 1)
        tri = (c_i <= r_i).astype(BF16)
        hi, mid, lo = _split3(lf)
        pre = (jnp.dot(tri, hi, preferred_element_type=F32) + jnp.dot(tri, mid, preferred_element_type=F32)
               + jnp.dot(tri, lo, preferred_element_type=F32))
        c_ref[...] = pre + carry_ref[...]
        carry_ref[...] += jnp.sum(lf, axis=0, keepdims=True)

    return pl.pallas_call(
        body, name=name, grid=(s // GB,),
        in_specs=[pl.BlockSpec((GB, w), lambda i: (i, 0)), pl.BlockSpec((1, w), lambda i: (0, 0))],
        out_specs=pl.BlockSpec((GB, w), lambda i: (i, 0)),
        out_shape=jax.ShapeDtypeStruct((s, w), F32),
        scratch_shapes=[pltpu.VMEM((1, w), F32)],
        compiler_params=_params(("arbitrary",)),
    )(af, bias)


def _gate_bwd(dc, af, bias, *, name):
    s, w = af.shape
    nb = s // GB

    def body(dc_ref, af_ref, b_ref, daf_ref, db_ref, carry_ref):
        @pl.when(pl.program_id(0) == 0)
        def _():
            carry_ref[...] = jnp.zeros_like(carry_ref)
            db_ref[...] = jnp.zeros_like(db_ref)

        dcb = dc_ref[...]
        r_i = lax.broadcasted_iota(jnp.int32, (GB, GB), 0)
        c_i = lax.broadcasted_iota(jnp.int32, (GB, GB), 1)
        tri = (c_i >= r_i).astype(BF16)
        hi, mid, lo = _split3(dcb)
        suf = (jnp.dot(tri, hi, preferred_element_type=F32) + jnp.dot(tri, mid, preferred_element_type=F32)
               + jnp.dot(tri, lo, preferred_element_type=F32)) + carry_ref[...]
        carry_ref[...] += jnp.sum(dcb, axis=0, keepdims=True)
        z = af_ref[...] + b_ref[...]
        daf = suf * _sigmoid(-z)
        daf_ref[...] = daf.astype(BF16)
        db_ref[...] += jnp.sum(daf, axis=0, keepdims=True)

    return pl.pallas_call(
        body, name=name, grid=(nb,),
        in_specs=[pl.BlockSpec((GB, w), lambda i: (nb - 1 - i, 0)), pl.BlockSpec((GB, w), lambda i: (nb - 1 - i, 0)),
                  pl.BlockSpec((1, w), lambda i: (0, 0))],
        out_specs=[pl.BlockSpec((GB, w), lambda i: (nb - 1 - i, 0)), pl.BlockSpec((1, w), lambda i: (0, 0))],
        out_shape=[jax.ShapeDtypeStruct((s, w), BF16), jax.ShapeDtypeStruct((1, w), F32)],
        scratch_shapes=[pltpu.VMEM((1, w), F32)],
        compiler_params=_params(("arbitrary",)),
    )(dc, af, bias)


def _rel_index_rows():
    w = np.arange(TOEP)
    wp = np.where(w < CQ, w, w - TOEP)
    return np.stack([np.clip(LEFT_CHUNKS * CHUNK - CQ * j - wp, -REL_CLIP, REL_CLIP) + REL_CLIP
                     for j in range(BAND_TILES)]).astype(np.int32)


def _skew_rows(xw, sign):
    row = lax.broadcasted_iota(jnp.int32, xw.shape, 0)
    for b in range(CQ.bit_length() - 1):
        amt = (1 << b) if sign > 0 else TOEP - (1 << b)
        xw = jnp.where(((row >> b) & 1) == 1, pltpu.roll(xw, amt, 1), xw)
    return xw


REL_PAD = 384
HP = 2


def _rel_onehot():
    return jnp.asarray(_rel_index_rows()[:, :, None] == np.arange(REL_PAD)[None, None, :], BF16)


def _fill_bias_tiles(rel_ref, oh_ref, bias_scr):
    parts = _split3(jnp.broadcast_to(rel_ref[...], (8, REL_PAD)))
    for j in range(BAND_TILES):
        row = sum(lax.dot_general(p, oh_ref[j], NT, preferred_element_type=F32) for p in parts)[0:1]
        bias_scr[j] = _skew_rows(jnp.broadcast_to(row, (CQ, TOEP)), +1)[:, :CQ]


FOX_Q = 512


def _fox_scores(s, cq, cr, diagonal):
    s = s + (cq - cr)
    if diagonal is None:
        return s
    bq, bk = s.shape
    key = lax.broadcasted_iota(jnp.int32, (bq, bk), 1) + diagonal
    return jnp.where(key <= lax.broadcasted_iota(jnp.int32, (bq, bk), 0), s, NEG)


def _chunk_scores(s, q0, k0, bias):
    bq, bk = s.shape
    qc = (q0 + lax.broadcasted_iota(jnp.int32, (bq, bk), 0)) >> 6
    kc = (k0 + lax.broadcasted_iota(jnp.int32, (bq, bk), 1)) >> 6
    return jnp.where((kc <= qc) & (kc >= qc - LEFT_CHUNKS), s + bias, NEG)


def _softmax_fwd(qkv, gate, aux, *, mode, width, name, ride=None):
    s = qkv.shape[0]
    nh = width // HEAD_DIM
    bk = CQ
    bq = FOX_Q if mode == "fox" else CQ
    nq, nkt, wide_q = s // bq, s // bk, bq // bk

    assert nh % HP == 0
    hcol = lambda hh: slice(hh * HEAD_DIM, (hh + 1) * HEAD_DIM)

    def body(q_ref, k_ref, v_ref, g_ref, *rest):
        if mode == "fox":
            cc_ref, cr_ref, mixed_ref, o_ref, lse_ref = rest
        else:
            rel_ref, oh_ref, mixed_ref, o_ref, lse_ref, bias_scr = rest
        qi = pl.program_id(1)
        q0 = qi * bq
        q = q_ref[...]

        if mode == "chunk":
            @pl.when(qi == 0)
            def _():
                for hh in range(HP):
                    _fill_bias_tiles(rel_ref.at[hh], oh_ref, bias_scr.at[hh])

            lo, hi = jnp.maximum(qi - (BAND_TILES - 1), 0), qi + 1

        def step(ki, carry, diagonal=None):
            k0 = pl.multiple_of(ki * bk, bk)
            kt = k_ref[pl.ds(k0, bk), :]
            vt = v_ref[pl.ds(k0, bk), :]
            out = []
            for hh in range(HP):
                m, l, acc = carry[hh]
                sc = lax.dot_general(q[:, hcol(hh)], kt[:, hcol(hh)], NT, preferred_element_type=F32) * SCALE
                if mode == "fox":
                    sc = _fox_scores(sc, cc_ref[hh], cr_ref[hh, ki], diagonal)
                else:
                    sc = _chunk_scores(sc, q0, k0, bias_scr[hh, ki - qi + (BAND_TILES - 1)])
                m_new = jnp.maximum(m, jnp.max(sc, axis=-1, keepdims=True))
                p = jnp.exp(sc - m_new)
                alpha = jnp.exp(m - m_new)
                l = alpha * l + jnp.sum(p, axis=-1, keepdims=True)
                acc = alpha * acc + jnp.dot(p.astype(BF16), vt[:, hcol(hh)], preferred_element_type=F32)
                out.append((m_new, l, acc))
            return tuple(out)

        init = ((jnp.full((bq, 1), NEG, F32), jnp.zeros((bq, 1), F32), jnp.zeros((bq, HEAD_DIM), F32)),) * HP
        if mode == "fox":
            done = lax.fori_loop(0, wide_q * qi, step, init)
            for u in range(wide_q):
                done = step(wide_q * qi + u, done, diagonal=u * bk)
        else:
            done = lax.fori_loop(lo, hi, step, init)
        o = jnp.concatenate([acc / l for _, l, acc in done], axis=1)
        g = g_ref[...]
        o_ref[...] = o
        mixed_ref[...] = (o * (g * _sigmoid(g))).astype(BF16)
        for hh, (m, l, _) in enumerate(done):
            lse_ref[hh] = m + jnp.log(l)

    wide = HP * HEAD_DIM
    head_col = lambda off: pl.BlockSpec((s, wide), lambda h, i: (0, off // HP + h))
    qblk = pl.BlockSpec((bq, wide), lambda h, i: (i, h))
    stat = pl.BlockSpec((HP, bq, 1), lambda h, i: (h, i, 0))
    in_specs = [qblk, head_col(nh), head_col(2 * nh), qblk]
    scratch = []
    if mode == "fox":
        in_specs += [stat, pl.BlockSpec((HP, nkt, 1, bk), lambda h, i: (h, 0, 0, 0))]
    else:
        in_specs += [pl.BlockSpec((HP, 1, REL_PAD), lambda h, i: (h, 0, 0)),
                     pl.BlockSpec((BAND_TILES, TOEP, REL_PAD), lambda h, i: (0, 0, 0))]
        scratch = [pltpu.VMEM((HP, BAND_TILES, CQ, CQ), F32)]
    outs, carried = _call(
        body, name=name, grid=(nh // HP, nq), in_specs=in_specs, out_specs=[qblk, qblk, stat],
        out_shape=[jax.ShapeDtypeStruct((s, width), BF16), jax.ShapeDtypeStruct((s, width), F32),
                   jax.ShapeDtypeStruct((nh, s, 1), F32)],
        scratch=scratch, sem=("parallel", "arbitrary"), args=(qkv, qkv, qkv, gate, *aux), ride=ride)
    return outs if ride is None else (*outs, carried)


def _softmax_bwd(qkv, gate, o, dmixed, dm_off, lse, aux, *, mode, width, name, ride=None):
    s = qkv.shape[0]
    nh = width // HEAD_DIM
    bk = CQ
    bq = FOX_Q if mode == "fox" else CQ
    nq, nkt, wide_q = s // bq, s // bk, bq // bk
    dmo = dm_off // HEAD_DIM
    rel_pad = REL_PAD
    assert nh % HP == 0 and dmo % HP == 0
    hcol = lambda hh: slice(hh * HEAD_DIM, (hh + 1) * HEAD_DIM)

    def body(q_ref, k_ref, v_ref, g_ref, o_ref, dm_ref, lse_ref, *rest):
        if mode == "fox":
            cc_ref, cr_ref, dq_ref, dk_ref, dv_ref, dg_ref, dc_ref, dcq_ref, dk_scr, dv_scr, dc_scr = rest
        else:
            rel_ref, oh_ref, dq_ref, dk_ref, dv_ref, dg_ref, drel_ref, dk_scr, dv_scr, bias_scr, db_scr = rest
        qi = pl.program_id(1)
        q0 = qi * bq

        @pl.when(qi == 0)
        def _():
            dk_scr[...] = jnp.zeros_like(dk_scr)
            dv_scr[...] = jnp.zeros_like(dv_scr)
            if mode == "fox":
                dc_scr[...] = jnp.zeros_like(dc_scr)
            else:
                db_scr[...] = jnp.zeros_like(db_scr)
                for hh in range(HP):
                    _fill_bias_tiles(rel_ref.at[hh], oh_ref, bias_scr.at[hh])

        g = g_ref[...]
        of = o_ref[...]
        dm = dm_ref[...]
        sig = _sigmoid(g)
        do = dm * (g * sig)
        dg_ref[...] = (dm * of * (sig * (1.0 + g * (1.0 - sig)))).astype(BF16)
        do_o = do * of
        delta = [jnp.sum(do_o[:, hcol(hh)], axis=-1, keepdims=True) for hh in range(HP)]
        do_b = do.astype(BF16)
        q = q_ref[...]
        if mode == "chunk":
            lo, hi = jnp.maximum(qi - (BAND_TILES - 1), 0), qi + 1

        def step(ki, carry, diagonal=None):
            k0 = pl.multiple_of(ki * bk, bk)
            kt = k_ref[pl.ds(k0, bk), :]
            vt = v_ref[pl.ds(k0, bk), :]
            out = []
            for hh in range(HP):
                dq, rsum = carry[hh]
                qh, kh, doh = q[:, hcol(hh)], kt[:, hcol(hh)], do_b[:, hcol(hh)]
                sc = lax.dot_general(qh, kh, NT, preferred_element_type=F32) * SCALE
                if mode == "fox":
                    sc = _fox_scores(sc, cc_ref[hh], cr_ref[hh, ki], diagonal)
                else:
                    sc = _chunk_scores(sc, q0, k0, bias_scr[hh, ki - qi + (BAND_TILES - 1)])
                p = jnp.exp(sc - lse_ref[hh])
                dp = lax.dot_general(doh, vt[:, hcol(hh)], NT, preferred_element_type=F32)
                ds = p * (dp - delta[hh])
                if mode == "fox":
                    dc_scr[hh, ki] += -jnp.sum(ds, axis=0, keepdims=True)
                    rsum = rsum + jnp.sum(ds, axis=-1, keepdims=True)
                else:
                    db_scr[hh, ki - qi + (BAND_TILES - 1)] += ds
                ds_b = (ds * SCALE).astype(BF16)
                dk_scr[pl.ds(k0, bk), hcol(hh)] += lax.dot_general(ds_b, qh, TN, preferred_element_type=F32)
                dv_scr[pl.ds(k0, bk), hcol(hh)] += lax.dot_general(p.astype(BF16), doh, TN, preferred_element_type=F32)
                out.append((dq + jnp.dot(ds_b, kh, preferred_element_type=F32), rsum))
            return tuple(out)

        init = ((jnp.zeros((bq, HEAD_DIM), F32), jnp.zeros((bq, 1), F32)),) * HP
        if mode == "fox":
            done = lax.fori_loop(0, wide_q * qi, step, init)
            for u in range(wide_q):
                done = step(wide_q * qi + u, done, diagonal=u * bk)
        else:
            done = lax.fori_loop(lo, hi, step, init)
        dq_ref[...] = jnp.concatenate([dq for dq, _ in done], axis=1).astype(BF16)
        if mode == "fox":
            for hh, (_, rsum) in enumerate(done):
                dcq_ref[hh] = rsum

        @pl.when(qi == nq - 1)
        def _():
            dk_ref[...] = dk_scr[...].astype(BF16)
            dv_ref[...] = dv_scr[...].astype(BF16)
            if mode == "fox":
                dc_ref[...] = dc_scr[...]
            else:
                for hh in range(HP):
                    tot = jnp.zeros((8, rel_pad), F32)
                    for j in range(BAND_TILES):
                        wide_ = jnp.concatenate([db_scr[hh, j], jnp.zeros((CQ, TOEP - CQ), F32)], axis=1)
                        diag = jnp.sum(_skew_rows(wide_, -1), axis=0, keepdims=True)
                        for part in _split3(jnp.broadcast_to(diag, (8, TOEP))):
                            tot = tot + jnp.dot(part, oh_ref[j], preferred_element_type=F32)
                    drel_ref[hh] = tot[0:1, :]

    wide = HP * HEAD_DIM
    head_col = lambda off: pl.BlockSpec((s, wide), lambda h, i: (0, off // HP + h))
    qblk = lambda off: pl.BlockSpec((bq, wide), lambda h, i: (i, off // HP + h))
    stat = pl.BlockSpec((HP, bq, 1), lambda h, i: (h, i, 0))
    in_specs = [qblk(0), head_col(nh), head_col(2 * nh), qblk(0), qblk(0), qblk(dmo), stat]
    out_specs = [qblk(0), head_col(0), head_col(0), qblk(0)]
    out_shape = [jax.ShapeDtypeStruct((s, width), BF16)] * 4
    scratch = [pltpu.VMEM((s, wide), F32), pltpu.VMEM((s, wide), F32)]
    if mode == "fox":
        rows = pl.BlockSpec((HP, nkt, 1, bk), lambda h, i: (h, 0, 0, 0))
        in_specs += [stat, rows]
        out_specs += [rows, stat]
        out_shape += [jax.ShapeDtypeStruct((nh, nkt, 1, bk), F32), jax.ShapeDtypeStruct((nh, s, 1), F32)]
        scratch += [pltpu.VMEM((HP, nkt, 1, bk), F32)]
    else:
        rel = pl.BlockSpec((HP, 1, rel_pad), lambda h, i: (h, 0, 0))
        in_specs += [rel, pl.BlockSpec((BAND_TILES, TOEP, rel_pad), lambda h, i: (0, 0, 0))]
        out_specs += [rel]
        out_shape += [jax.ShapeDtypeStruct((nh, 1, rel_pad), F32)]
        scratch += [pltpu.VMEM((HP, BAND_TILES, CQ, CQ), F32), pltpu.VMEM((HP, BAND_TILES, CQ, CQ), F32)]
    outs, carried = _call(
        body, name=name, grid=(nh // HP, nq), in_specs=in_specs, out_specs=out_specs, out_shape=out_shape,
        scratch=scratch, sem=("parallel", "arbitrary"), args=(qkv, qkv, qkv, gate, o, dmixed, lse, *aux), ride=ride)
    return outs if ride is None else (*outs, carried)


SBK = 256
SBQ = 2 * SBK
SB_TILES_PER_TRIP = 1
SB_DEAD = -110.0


def _suffix_excl(x, tri):
    r = x.shape[0]
    both = jnp.dot(jnp.concatenate(_split2(x), axis=0), tri, preferred_element_type=F32)
    return both[:r] + both[r:]


def _sb_logits(qh, kt, diag):
    z = lax.dot_general(qh, kt, NT, preferred_element_type=F32) * SCALE
    lom = jnp.minimum(-z, 0.0) - jnp.log(1.0 + jnp.exp(-jnp.abs(z)))
    if diag is not None:
        lom = jnp.where(diag, lom, 0.0)
    return z, lom


def _sb_fwd_tile(qh, kt, vt, tri, diag, run, acc):
    z, lom = _sb_logits(qh, kt, diag)
    a = jnp.exp(lom + z + (_suffix_excl(lom, tri) + run))
    if diag is not None:
        a = jnp.where(diag, a, 0.0)
    acc = acc + jnp.dot(a.astype(BF16), vt, preferred_element_type=F32)
    return run + jnp.sum(lom, axis=-1, keepdims=True), acc


def _sb_alive(run_a, run_b):
    return (jnp.max(jnp.maximum(run_a, run_b)) > SB_DEAD).astype(jnp.int32)


def _sb_fwd(qkv, gate, *, width, name):
    s = qkv.shape[0]
    nh = width // HEAD_DIM
    nq = s // SBQ

    def body(q_ref, k_ref, v_ref, g_ref, mixed_ref, o_ref):
        qi = pl.program_id(1)
        r_i = lax.broadcasted_iota(jnp.int32, (SBK, SBK), 0)
        c_i = lax.broadcasted_iota(jnp.int32, (SBK, SBK), 1)
        tri = (r_i > c_i).astype(BF16)
        diag = c_i < r_i
        q_a = q_ref[0:SBK, :]
        q_b = q_ref[SBK:SBQ, :]

        def kv(tile):
            k0 = pl.multiple_of(tile * SBK, SBK)
            return k_ref[pl.ds(k0, SBK), :], v_ref[pl.ds(k0, SBK), :]

        zero = (jnp.zeros((SBK, 1), F32), jnp.zeros((SBK, HEAD_DIM), F32))
        kt, vt = kv(2 * qi + 1)
        run_b, acc_b = _sb_fwd_tile(q_b, kt, vt, tri, diag, *zero)
        kt, vt = kv(2 * qi)
        run_b, acc_b = _sb_fwd_tile(q_b, kt, vt, tri, None, run_b, acc_b)
        run_a, acc_a = _sb_fwd_tile(q_a, kt, vt, tri, diag, *zero)

        trips = (2 // SB_TILES_PER_TRIP) * qi

        def step(carry):
            t, _, run_a, acc_a, run_b, acc_b = carry
            for u in range(SB_TILES_PER_TRIP):
                kt, vt = kv(2 * qi - 1 - (SB_TILES_PER_TRIP * t + u))
                run_a, acc_a = _sb_fwd_tile(q_a, kt, vt, tri, None, run_a, acc_a)
                run_b, acc_b = _sb_fwd_tile(q_b, kt, vt, tri, None, run_b, acc_b)
            return t + 1, _sb_alive(run_a, run_b), run_a, acc_a, run_b, acc_b

        _, _, _, acc_a, _, acc_b = lax.while_loop(
            lambda c: jnp.logical_and(c[0] < trips, c[1] > 0), step,
            (jnp.int32(0), _sb_alive(run_a, run_b), run_a, acc_a, run_b, acc_b))
        o = jnp.concatenate([acc_a, acc_b], axis=0)
        g = g_ref[...]
        o_ref[...] = o
        mixed_ref[...] = (o * (g * _sigmoid(g))).astype(BF16)

    head_col = lambda off: pl.BlockSpec((s, HEAD_DIM), lambda h, i: (0, off + h))
    qblk = pl.BlockSpec((SBQ, HEAD_DIM), lambda h, i: (i, h))
    return pl.pallas_call(
        body, name=name, grid=(nh, nq), in_specs=[qblk, head_col(nh), head_col(2 * nh), qblk],
        out_specs=[qblk, qblk],
        out_shape=[jax.ShapeDtypeStruct((s, width), BF16), jax.ShapeDtypeStruct((s, width), F32)],
        compiler_params=_params(("parallel", "arbitrary")),
    )(qkv, qkv, qkv, gate)


def _sb_bwd(qkv, gate, o, dmixed, *, width, name):
    s = qkv.shape[0]
    nh = width // HEAD_DIM
    nq = s // SBQ

    def body(q_ref, k_ref, v_ref, g_ref, o_ref, dm_ref, dq_ref, dk_ref, dv_ref, dg_ref, dk_scr, dv_scr):
        qi = pl.program_id(1)

        @pl.when(qi == 0)
        def _():
            dk_scr[...] = jnp.zeros_like(dk_scr)
            dv_scr[...] = jnp.zeros_like(dv_scr)

        g = g_ref[...]
        of = o_ref[...]
        dm = dm_ref[...]
        sig = _sigmoid(g)
        do = dm * (g * sig)
        dg_ref[...] = (dm * of * (sig * (1.0 + g * (1.0 - sig)))).astype(BF16)
        do_b = do.astype(BF16)
        q = q_ref[...]
        r_i = lax.broadcasted_iota(jnp.int32, (SBK, SBK), 0)
        c_i = lax.broadcasted_iota(jnp.int32, (SBK, SBK), 1)
        tri = (r_i > c_i).astype(BF16)
        tri_pre = (r_i < c_i).astype(BF16)
        diag = c_i < r_i
        q_a, q_b = q[0:SBK], q[SBK:SBQ]
        do_a, do_b2 = do_b[0:SBK], do_b[SBK:SBQ]

        def kv(t):
            k0 = pl.multiple_of(t * SBK, SBK)
            return k0, k_ref[pl.ds(k0, SBK), :], v_ref[pl.ds(k0, SBK), :]

        def mass(qh, tile_, dg_):
            return jnp.sum(_sb_logits(qh, kv(tile_)[1], dg_)[1], axis=-1, keepdims=True)

        run_b = mass(q_b, 2 * qi + 1, diag) + mass(q_b, 2 * qi, None)
        run_a = mass(q_a, 2 * qi, diag)
        trips = (2 // SB_TILES_PER_TRIP) * qi

        def scout(carry):
            t, _, run_a, run_b = carry
            for u in range(SB_TILES_PER_TRIP):
                tile_ = 2 * qi - 1 - (SB_TILES_PER_TRIP * t + u)
                run_a = run_a + mass(q_a, tile_, None)
                run_b = run_b + mass(q_b, tile_, None)
            return t + 1, _sb_alive(run_a, run_b), run_a, run_b

        walked, _, run_a, run_b = lax.while_loop(
            lambda c: jnp.logical_and(c[0] < trips, c[1] > 0), scout,
            (jnp.int32(0), _sb_alive(run_a, run_b), run_a, run_b))

        def tile(qh, doh, kt, vt, dg_, carry):
            rem, gpre, dq = carry
            z, lom = _sb_logits(qh, kt, dg_)
            rem = rem - jnp.sum(lom, axis=-1, keepdims=True)
            a = jnp.exp(lom + z + (_suffix_excl(lom, tri) + rem))
            if dg_ is not None:
                a = jnp.where(dg_, a, 0.0)
            gg = lax.dot_general(doh, vt, NT, preferred_element_type=F32) * a
            pre = _suffix_excl(gg, tri_pre) + gpre
            dz = gg * jnp.exp(lom) - pre * jnp.exp(lom + z)
            if dg_ is not None:
                dz = jnp.where(dg_, dz, 0.0)
            dz_b = (dz * SCALE).astype(BF16)
            dq = dq + jnp.dot(dz_b, kt, preferred_element_type=F32)
            return (rem, gpre + jnp.sum(gg, axis=-1, keepdims=True), dq), dz_b, a.astype(BF16)

        def both(t, ca, cb, dg_a):
            k0, kt, vt = kv(t)
            ca, dz_a, a_a = tile(q_a, do_a, kt, vt, dg_a, ca)
            cb, dz_b_, a_b = tile(q_b, do_b2, kt, vt, None, cb)
            dk_scr[pl.ds(k0, SBK), :] += lax.dot_general(jnp.concatenate([dz_a, dz_b_], axis=0), q, TN,
                                                         preferred_element_type=F32)
            dv_scr[pl.ds(k0, SBK), :] += lax.dot_general(jnp.concatenate([a_a, a_b], axis=0), do_b, TN,
                                                         preferred_element_type=F32)
            return ca, cb

        def step(t, carry):
            ca, cb = carry
            for u in range(SB_TILES_PER_TRIP):
                ca, cb = both(SB_TILES_PER_TRIP * t + u, ca, cb, None)
            return ca, cb

        zero = (jnp.zeros((SBK, 1), F32), jnp.zeros((SBK, HEAD_DIM), F32))
        ca, cb = lax.fori_loop(trips - walked, trips, step, ((run_a, *zero), (run_b, *zero)))
        ca, cb = both(2 * qi, ca, cb, diag)
        k0, kt, vt = kv(2 * qi + 1)
        cb, dz_b_, a_b = tile(q_b, do_b2, kt, vt, diag, cb)
        dk_scr[pl.ds(k0, SBK), :] += lax.dot_general(dz_b_, q_b, TN, preferred_element_type=F32)
        dv_scr[pl.ds(k0, SBK), :] += lax.dot_general(a_b, do_b2, TN, preferred_element_type=F32)
        dq_ref[...] = jnp.concatenate([ca[2], cb[2]], axis=0).astype(BF16)

        @pl.when(qi == nq - 1)
        def _():
            dk_ref[...] = dk_scr[...].astype(BF16)
            dv_ref[...] = dv_scr[...].astype(BF16)

    head_col = lambda off: pl.BlockSpec((s, HEAD_DIM), lambda h, i: (0, off + h))
    qblk = pl.BlockSpec((SBQ, HEAD_DIM), lambda h, i: (i, h))
    return pl.pallas_call(
        body, name=name, grid=(nh, nq),
        in_specs=[qblk, head_col(nh), head_col(2 * nh), qblk, qblk, qblk],
        out_specs=[qblk, head_col(0), head_col(0), qblk],
        out_shape=[jax.ShapeDtypeStruct((s, width), BF16)] * 4,
        scratch_shapes=[pltpu.VMEM((s, HEAD_DIM), F32), pltpu.VMEM((s, HEAD_DIM), F32)],
        compiler_params=_params(("parallel", "arbitrary")),
    )(qkv, qkv, qkv, gate, o, dmixed)


HBM = pl.BlockSpec(memory_space=pl.ANY)
MESH = pl.DeviceIdType.MESH


class _GatherRide:
    def __init__(self, shard):
        self.src = shard
        self.out_shape = jax.ShapeDtypeStruct((N_DEV, *shard.shape), shard.dtype)
        self.scratch = [pltpu.SemaphoreType.DMA((7,)), pltpu.SemaphoreType.DMA((7,)), pltpu.SemaphoreType.DMA]

    def _copies(self, x_ref, out_ref, send_sems, recv_sems, local_sem):
        x, y, c = lax.axis_index("x"), lax.axis_index("y"), lax.axis_index("c")
        me, sibling = (x, y, c), (x, y, 1 - c)
        chips = [(1 - x, y), (x, 1 - y), (1 - x, 1 - y)]

        def slot(px, py, pc):
            return out_ref.at[4 * px + 2 * py + pc]

        def copy(k, block, to, src=None):
            return pltpu.make_async_remote_copy(
                src_ref=slot(*block) if src is None else src, dst_ref=slot(*block),
                send_sem=send_sems.at[k], recv_sem=recv_sems.at[k], device_id=to, device_id_type=MESH)

        mine = pltpu.make_async_copy(x_ref, slot(*me), local_sem)
        first = [copy(0, me, sibling, src=x_ref)]
        first += [copy(1 + j, me, (*chip, c), src=x_ref) for j, chip in enumerate(chips)]
        landed = [copy(1 + j, (*chip, c), me) for j, chip in enumerate(chips)]
        passed = [copy(4 + j, (*chip, c), sibling) for j, chip in enumerate(chips)]
        from_sibling = [copy(0, sibling, me)] + [copy(4 + j, (*chip, 1 - c), me) for j, chip in enumerate(chips)]
        return mine, first, landed, passed, from_sibling

    def start(self, *refs):
        mine, first, _, _, _ = self._copies(*refs)
        mine.start()
        for cp in first:
            cp.start()

    def relay(self, *refs):
        _, _, landed, passed, _ = self._copies(*refs)
        for got, onward in zip(landed, passed):
            got.wait_recv()
            onward.start()

    def finish(self, *refs):
        mine, first, _, passed, from_sibling = self._copies(*refs)
        for cp in from_sibling:
            cp.wait_recv()
        for cp in first + passed:
            cp.wait_send()
        mine.wait()


def _all_gather(shard, *, name):
    ride = _GatherRide(shard)

    def body(*refs):
        ride.start(*refs)
        ride.relay(*refs)
        ride.finish(*refs)

    return pl.pallas_call(body, name=name, in_specs=[HBM], out_specs=HBM, out_shape=ride.out_shape,
                          scratch_shapes=ride.scratch)(shard)


class _Ride:
    def __init__(self, src, *, gather, chips=False):
        self.src, self.gather, self.chips = src, gather, chips
        n = N_DEV // 2 if chips else N_DEV
        self.out_shape = jax.ShapeDtypeStruct((n, *(src.shape if gather else src.shape[1:])), src.dtype)
        self.scratch = [pltpu.SemaphoreType.DMA((7,)), pltpu.SemaphoreType.DMA((7,)), pltpu.SemaphoreType.DMA]

    def _copies(self, src_ref, out_ref, send_sems, recv_sems, local_sem):
        x, y, c = lax.axis_index("x"), lax.axis_index("y"), lax.axis_index("c")
        slot = (lambda px, py, pc: 2 * px + py) if self.chips else (lambda px, py, pc: 4 * px + 2 * py + pc)
        me = slot(x, y, c)
        pick = (lambda j: src_ref) if self.gather else (lambda j: src_ref.at[j])
        mine = pltpu.make_async_copy(pick(me), out_ref.at[me], local_sem)
        copies = []
        for k in range(2 if self.chips else 1, N_DEV, 2 if self.chips else 1):
            px, py, pc = x ^ ((k >> 2) & 1), y ^ ((k >> 1) & 1), c ^ (k & 1)
            copies.append(pltpu.make_async_remote_copy(
                src_ref=pick(slot(px, py, pc)), dst_ref=out_ref.at[me],
                send_sem=send_sems.at[k - 1], recv_sem=recv_sems.at[k - 1],
                device_id=(px, py, pc), device_id_type=MESH))
        return mine, copies

    def start(self, *refs):
        mine, copies = self._copies(*refs)
        mine.start()
        for cp in copies:
            cp.start()

    def finish(self, *refs):
        mine, copies = self._copies(*refs)
        for cp in copies:
            cp.wait_recv()
        for cp in copies:
            cp.wait_send()
        mine.wait()


def _exchange(src, *, gather, name):
    ride = _Ride(src, gather=gather)

    def body(*refs):
        ride.start(*refs)
        ride.finish(*refs)

    return pl.pallas_call(body, name=name, in_specs=[HBM], out_specs=HBM, out_shape=ride.out_shape,
                          scratch_shapes=ride.scratch)(src)


def _presum_on_chip(parts, *, name, rows=256):
    _, r, c_ = parts.shape
    rows = min(rows, r)
    assert r % rows == 0
    by_core = parts.reshape(N_DEV // 2, 2, r, c_)

    def swap(src_ref, out_ref, send_sem, recv_sem):
        x, y, c = lax.axis_index("x"), lax.axis_index("y"), lax.axis_index("c")
        cp = pltpu.make_async_remote_copy(
            src_ref=src_ref.at[:, 1 - c], dst_ref=out_ref, send_sem=send_sem, recv_sem=recv_sem,
            device_id=(x, y, 1 - c), device_id_type=MESH)
        cp.start()
        cp.wait()

    got = pl.pallas_call(swap, name=name + "_swap", in_specs=[HBM], out_specs=HBM,
                         out_shape=jax.ShapeDtypeStruct((N_DEV // 2, r, c_), parts.dtype),
                         scratch_shapes=[pltpu.SemaphoreType.DMA, pltpu.SemaphoreType.DMA])(by_core)

    def add(core_ref, a_ref, b_ref, o_ref):
        del core_ref
        o_ref[...] = (a_ref[...].astype(F32) + b_ref[...].astype(F32)).astype(o_ref.dtype)

    blk = pl.BlockSpec((1, rows, c_), lambda j, i, core: (j, i, 0))
    mine = pl.BlockSpec((1, None, rows, c_), lambda j, i, core: (j, core[0], i, 0))
    core = jnp.reshape(lax.axis_index("c"), (1,)).astype(jnp.int32)
    return pl.pallas_call(
        add, name=name + "_sum",
        grid_spec=pltpu.PrefetchScalarGridSpec(num_scalar_prefetch=1, grid=(N_DEV // 2, r // rows),
                                               in_specs=[mine, blk], out_specs=blk),
        out_shape=jax.ShapeDtypeStruct((N_DEV // 2, r, c_), parts.dtype),
        compiler_params=_params(("parallel", "parallel")))(core, by_core, got)


def _call(body, *, name, grid, in_specs, out_specs, out_shape, scratch, sem, args, ride=None):
    if ride is None:
        outs = pl.pallas_call(body, name=name, grid=grid, in_specs=in_specs, out_specs=out_specs, out_shape=out_shape,
                              scratch_shapes=scratch, compiler_params=_params(sem))(*args)
        return list(outs), None
    n_in, n_out = len(in_specs), len(out_specs)

    def carrying(*refs):
        ins, src_ref = refs[:n_in], refs[n_in]
        outs, dst_ref = refs[n_in + 1:n_in + 1 + n_out], refs[n_in + 1 + n_out]
        rest = refs[n_in + 2 + n_out:]
        own, sems = rest[:len(rest) - 3], rest[len(rest) - 3:]
        ids = [pl.program_id(a) for a in range(len(grid))]
        first = functools.reduce(jnp.logical_and, [i == 0 for i in ids])
        last = functools.reduce(jnp.logical_and, [i == n - 1 for i, n in zip(ids, grid)])

        @pl.when(first)
        def _():
            ride.start(src_ref, dst_ref, *sems)

        if hasattr(ride, "relay"):
            at = functools.reduce(jnp.logical_and, [ids[0] == (3 * grid[0]) // 4] + [i == 0 for i in ids[1:]])

            @pl.when(at)
            def _():
                ride.relay(src_ref, dst_ref, *sems)

        body(*ins, *outs, *own)

        @pl.when(last)
        def _():
            ride.finish(src_ref, dst_ref, *sems)

    outs = pl.pallas_call(
        carrying, name=name, grid=grid, in_specs=[*in_specs, HBM], out_specs=[*out_specs, HBM],
        out_shape=[*out_shape, ride.out_shape], scratch_shapes=[*scratch, *ride.scratch],
        compiler_params=_params(("arbitrary",) * len(grid)))(*args, ride.src)
    return list(outs[:-1]), outs[-1]


def _adamw(parts, w, m, v, *, name, rows):
    r, c_ = w.shape
    rows = min(rows, r)
    assert r % rows == 0
    c1 = 1.0 / (1.0 - ADAM_B1 ** ADAM_STEP)
    c2 = 1.0 / (1.0 - ADAM_B2 ** ADAM_STEP)

    def body(p_ref, w_ref, m_ref, v_ref, g_ref, d_ref, nm_ref, nv_ref):
        g = p_ref[0].astype(F32)
        for i in range(1, parts.shape[0]):
            g = g + p_ref[i].astype(F32)
        nm = ADAM_B1 * m_ref[...] + (1.0 - ADAM_B1) * g
        nv = ADAM_B2 * v_ref[...] + (1.0 - ADAM_B2) * (g * g)
        g_ref[...] = g
        nm_ref[...] = nm
        nv_ref[...] = nv
        d_ref[...] = -ADAM_LR * ((nm * c1) / (jnp.sqrt(nv * c2) + ADAM_EPS) + ADAM_WD * w_ref[...])

    blk = pl.BlockSpec((rows, c_), lambda i: (i, 0))
    return pl.pallas_call(
        body, name=name, grid=(r // rows,),
        in_specs=[pl.BlockSpec((parts.shape[0], rows, c_), lambda i: (0, i, 0)), blk, blk, blk],
        out_specs=[blk] * 4, out_shape=[jax.ShapeDtypeStruct((r, c_), F32)] * 4,
        compiler_params=_params(("parallel",)),
    )(parts, w, m, v)


def _pad_cols(a, n):
    return jnp.pad(a, ((0, 0), (0, n - a.shape[1])))


def _fox_dc(shares, nh, s):
    key_side, query_side = shares
    return _pad_cols(jnp.transpose(key_side.reshape(nh, s) + query_side.reshape(nh, s)), 128)


def _local_step(x, target, norm_pre, norm_post, w_in_e, w_f, b_f, rel_bias, sh_out_e, sh_in_o, sh_out_o):
    s, d = x.shape
    wa = d // 2
    nha = wa // HEAD_DIM
    nq = s // CQ
    gather = lambda shard: _Ride(shard, gather=True)
    scatter = lambda parts: _Ride(parts, gather=False)

    h0 = _rms_fwd(x, norm_pre[0:1], name="rms_pre0")
    proj = lambda w, off, n, dt, nm, **kw: _matmul(h0, w, mode="nn", m=s, n=n, k=d, out_dtype=dt, name=nm,
                                                   b_off=(0, off), **kw)
    qkv_a, w_out_e = proj(w_in_e, 0, 3 * wa, BF16, "proj_qkv_a", ride=gather(sh_out_e))
    w_out_e = w_out_e.reshape(d, d)
    g_a = proj(w_in_e, 3 * wa, wa, F32, "proj_gate_a")
    qkv_b = proj(w_in_e, 4 * wa, 3 * wa, BF16, "proj_qkv_b")
    g_b = proj(w_in_e, 7 * wa, wa, F32, "proj_gate_b")
    af = _matmul(h0, w_f, mode="nn", m=s, n=128, k=d, out_dtype=F32, name="proj_forget")
    bias128 = _pad_cols(b_f, 128)
    cum = _gate_fwd(af, bias128, name="forget_cumsum")
    c_t = jnp.transpose(cum[:, :nha])
    c_col = c_t.reshape(nha, s, 1)
    c_row = c_t.reshape(nha, nq, 1, CQ)
    mixed_a, o_a, lse_a, w_in_o = _softmax_fwd(qkv_a, g_a, (c_col, c_row), mode="fox", width=wa, name="fox_fwd",
                                               ride=_GatherRide(sh_in_o))
    rel_aux = (_pad_cols(rel_bias, REL_PAD).reshape(nha, 1, REL_PAD), _rel_onehot())
    mixed_b, o_b, lse_b, w_out_o = _softmax_fwd(qkv_b, g_b, rel_aux, mode="chunk", width=wa, name="chunk_fwd",
                                                ride=gather(sh_out_o))
    w_out_o = w_out_o.reshape(d, d)
    mixed0 = (mixed_a, mixed_b)
    y0 = _matmul(mixed0, w_out_e, mode="nn", m=s, n=d, k=d, out_dtype=F32, name="out_proj0")
    x1 = _post_fwd(x, y0, norm_post[0:1], name="post0")

    h1 = _rms_fwd(x1, norm_pre[1:2], name="rms_pre1")
    qkv_c = _matmul(h1, w_in_o, mode="nn", m=s, n=3 * d, k=d, out_dtype=BF16, name="proj_qkv_c", b_groups=N_DEV)
    g_c = _matmul(h1, w_in_o, mode="nn", m=s, n=d, k=d, out_dtype=F32, name="proj_gate_c", b_off=(0, 3 * d),
                  b_groups=N_DEV)
    mixed1, o_c = _sb_fwd(qkv_c, g_c, width=d, name="sb_fwd")
    y1 = _matmul(mixed1, w_out_o, mode="nn", m=s, n=d, k=d, out_dtype=F32, name="out_proj1")
    x2 = _post_fwd(x1, y1, norm_post[1:2], name="post1")

    loss, dx2 = _loss_head(x2, target, name="loss_head")

    dy1, dgpost1 = _post_bwd(dx2, y1, norm_post[1:2], name="post_bwd1")
    dmixed1 = _matmul(dy1, w_out_o, mode="nt", m=s, n=d, k=d, out_dtype=F32, name="dmixed1")
    dw_out_o = _matmul(mixed1, dy1, mode="tn", m=d, n=d, k=s, out_dtype=BF16, name="dw_out1")
    dq_c, dk_c, dv_c, dg_c = _sb_bwd(qkv_c, g_c, o_c, dmixed1, width=d, name="sb_bwd")
    dproj1 = (dq_c, dk_c, dv_c, dg_c)
    dh1 = _matmul(dproj1, w_in_o, mode="nt", m=s, n=d, k=4 * d, out_dtype=F32, name="dh1", b_groups=N_DEV)
    parts_in_o = _matmul(h1, dproj1, mode="tn", m=d, n=4 * d, k=s, out_dtype=BF16, name="dw_in1", out_groups=N_DEV)
    dx1, dgpre1 = _pre_bwd((dh1,), x1, norm_pre[1:2], dx2, name="pre_bwd1")

    dy0, dgpost0 = _post_bwd(dx1, y0, norm_post[0:1], name="post_bwd0")
    dmixed0 = _matmul(dy0, w_out_e, mode="nt", m=s, n=d, k=d, out_dtype=F32, name="dmixed0")
    dw_out_e = _matmul(mixed0, dy0, mode="tn", m=d, n=d, k=s, out_dtype=BF16, name="dw_out0")
    chip_in_o = _presum_on_chip(parts_in_o, name="rs_w_in_odd")
    dq_a, dk_a, dv_a, dg_a, *dc_shares, got_in_o = _softmax_bwd(
        qkv_a, g_a, o_a, dmixed0, 0, lse_a, (c_col, c_row), mode="fox", width=wa, name="fox_bwd",
        ride=_Ride(chip_in_o, gather=False, chips=True))
    dq_b, dk_b, dv_b, dg_b, drel, got_out_o = _softmax_bwd(
        qkv_b, g_b, o_b, dmixed0, wa, lse_b, rel_aux, mode="chunk", width=wa, name="chunk_bwd",
        ride=scatter(dw_out_o.reshape(N_DEV, d // N_DEV, d)))
    dc = _fox_dc(dc_shares, nha, s)
    daf, dbf = _gate_bwd(dc, af, bias128, name="forget_bwd")
    dproj0 = jnp.concatenate([dq_a, dk_a, dv_a, dg_a, dq_b, dk_b, dv_b, dg_b], axis=1)
    dw_in_e, got_out_e = _matmul(h0, dproj0, mode="tn", m=d, n=8 * wa, k=s, out_dtype=BF16, name="dw_in0",
                                 ride=scatter(dw_out_e.reshape(N_DEV, d // N_DEV, d)))
    dw_f = _matmul(h0, daf, mode="tn", m=d, n=128, k=s, out_dtype=BF16, name="dw_forget")
    dw_e = jnp.concatenate([dw_in_e, dw_f[:, :nha]], axis=1)
    parts_in_e = jnp.transpose(dw_e.reshape(d, N_DEV, dw_e.shape[1] // N_DEV), (1, 0, 2))
    chip_in_e = _presum_on_chip(parts_in_e, name="rs_w_in_even")
    dh0, got_in_e = _matmul(dproj0, w_in_e, mode="nt", m=s, n=d, k=8 * wa, out_dtype=F32, name="dh0_main",
                            ride=_Ride(chip_in_e, gather=False, chips=True))
    dh0f = _matmul(daf, w_f, mode="nt", m=s, n=d, k=128, out_dtype=F32, name="dh0_forget")
    dx0, dgpre0 = _pre_bwd((dh0, dh0f), x, norm_pre[0:1], dx1, name="pre_bwd0")

    small = dict(
        norm_pre=jnp.concatenate([dgpre0, dgpre1], axis=0),
        norm_post=jnp.concatenate([dgpost0, dgpost1], axis=0),
        b_f=dbf[:, :nha], rel_bias=drel[:, 0, :N_REL])
    got = dict(w_in_even=got_in_e, w_out_even=got_out_e, w_in_odd=got_in_o, w_out_odd=got_out_o)
    return loss, dx0, got, small


def _pack_small(norm_pre, norm_post, b_f, rel_bias):
    flat = jnp.concatenate([norm_pre.reshape(-1), norm_post.reshape(-1), b_f.reshape(-1), rel_bias.reshape(-1)])
    n = flat.shape[0]
    rows = -(-n // 128)
    rows = -(-rows // 8) * 8
    return jnp.pad(flat, (0, rows * 128 - n)).reshape(rows, 128)


def _unpack_small(slab, shapes):
    flat = slab.reshape(-1)
    out, off = [], 0
    for shp in shapes:
        n = int(np.prod(shp))
        out.append(flat[off:off + n].reshape(shp))
        off += n
    return out


def kernel(x, norm_pre, norm_post, w_in_even, b_f_even, rel_bias_even, w_out_even, w_in_odd, w_out_odd, loss_target, m_norm_pre, m_norm_post, m_w_in_even, m_b_f_even, m_rel_bias_even, m_w_out_even, m_w_in_odd, m_w_out_odd, v_norm_pre, v_norm_post, v_w_in_even, v_b_f_even, v_rel_bias_even, v_w_out_even, v_w_in_odd, v_w_out_odd):
    _, s, d = x.shape
    wa = d // 2
    nha = wa // HEAD_DIM
    in_e = w_in_even.shape[2] * N_DEV

    w_in_e_all = jnp.transpose(_all_gather(w_in_even[0].astype(BF16), name="ag_w_in_even"), (1, 0, 2)).reshape(d, in_e)
    w_main = w_in_e_all
    w_f = _pad_cols(w_in_e_all[:, 8 * wa:], 128)

    loss, dx, got, small = _local_step(
        x[0], loss_target[0], norm_pre, norm_post, w_main, w_f, b_f_even, rel_bias_even[0],
        w_out_even[0].astype(BF16), w_in_odd[0].astype(BF16), w_out_odd[0].astype(BF16))

    upd = {}
    upd["w_in_even"] = _adamw(got["w_in_even"], w_in_even[0], m_w_in_even[0], v_w_in_even[0],
                              name="adamw_w_in_even", rows=128)
    upd["w_out_even"] = _adamw(got["w_out_even"], w_out_even[0], m_w_out_even[0], v_w_out_even[0],
                               name="adamw_w_out_even", rows=64)
    upd["w_in_odd"] = _adamw(got["w_in_odd"], w_in_odd[0], m_w_in_odd[0], v_w_in_odd[0],
                             name="adamw_w_in_odd", rows=128)
    upd["w_out_odd"] = _adamw(got["w_out_odd"], w_out_odd[0], m_w_out_odd[0], v_w_out_odd[0],
                              name="adamw_w_out_odd", rows=64)

    shapes = [norm_pre.shape, norm_post.shape, b_f_even.shape, rel_bias_even.shape]
    g_slab = _pack_small(small["norm_pre"], small["norm_post"], small["b_f"], small["rel_bias"])
    parts_small = _exchange(g_slab, gather=True, name="ar_small")
    sm = _adamw(parts_small, _pack_small(norm_pre, norm_post, b_f_even, rel_bias_even),
                _pack_small(m_norm_pre, m_norm_post, m_b_f_even, m_rel_bias_even),
                _pack_small(v_norm_pre, v_norm_post, v_b_f_even, v_rel_bias_even), name="adamw_small", rows=g_slab.shape[0])
    sm = [_unpack_small(a, shapes) for a in sm]

    total = lax.psum(loss[0, 0], ("x", "y", "c"))

    def leaves(kind):
        return (sm[kind][0], sm[kind][1], upd["w_in_even"][kind][None], sm[kind][2], sm[kind][3],
                upd["w_out_even"][kind][None], upd["w_in_odd"][kind][None], upd["w_out_odd"][kind][None])

    return (total, dx[None], *leaves(0), *leaves(1), *leaves(2), *leaves(3))
```

```python
import functools

import numpy as np
import jax
import jax.numpy as jnp
from jax import lax
from jax.experimental import pallas as pl
from jax.experimental.pallas import tpu as pltpu

F32 = jnp.float32
BF16 = jnp.bfloat16

HEAD_DIM = 128
CHUNK = 64
LEFT_CHUNKS = 8
REL_CLIP = 128
N_REL = 2 * REL_CLIP + 1
RMS_EPS = 1e-6
SCALE = HEAD_DIM ** -0.5

ADAM_LR = 0.001
ADAM_B1 = 0.9
ADAM_B2 = 0.999
ADAM_EPS = 1e-08
ADAM_WD = 0.01
ADAM_STEP = 10

N_DEV = 8
V7X_VMEM_LIMIT_BYTES = 56 * 1024 * 1024
NEG = -1e30

NT = (((1,), (1,)), ((), ()))
TN = (((0,), (0,)), ((), ()))
NN = (((1,), (0,)), ((), ()))

CQ = 256
BAND_TILES = 3
TOEP = 2 * CQ
assert (BAND_TILES - 1) * CQ == LEFT_CHUNKS * CHUNK


def _params(sem):
    return pltpu.CompilerParams(dimension_semantics=sem, vmem_limit_bytes=V7X_VMEM_LIMIT_BYTES)


def _split3(x):
    hi = x.astype(BF16)
    r1 = x - hi.astype(F32)
    mid = r1.astype(BF16)
    lo = (r1 - mid.astype(F32)).astype(BF16)
    return hi, mid, lo


def _split2(x):
    hi = x.astype(BF16)
    lo = (x - hi.astype(F32)).astype(BF16)
    return hi, lo


def _sigmoid(g):
    return 1.0 / (1.0 + jnp.exp(-g))


def _tile(n, cap, *offsets):
    if n <= 128:
        return n
    t = (min(cap, n) // 128) * 128
    while n % t or any(o % t for o in offsets):
        t -= 128
    return t


def _matmul(a, b, *, mode, m, n, k, out_dtype, name, a_off=(0, 0), b_off=(0, 0), tm=1024, tn=2048, tk=1024,
            b_groups=None, out_groups=None, ride=None):
    a_m, a_k = (a_off if mode in ("nn", "nt") else a_off[::-1])
    b_k, b_n = (b_off if mode in ("nn", "tn") else b_off[::-1])
    b_group = (b.shape[2],) if b_groups else ()
    a_parts, b_parts = (a if isinstance(a, tuple) else None), (b if isinstance(b, tuple) else None)
    assert not (a_parts and b_parts) and not (b_parts and mode == "nt")
    a_piece = (a_parts[0].shape[1],) if a_parts else ()
    b_piece = (b_parts[0].shape[1],) if b_parts else ()
    a_on_k = mode in ("nn", "nt")
    tm = _tile(m, tm, a_m, *(() if a_on_k else a_piece))
    tn = _tile(n, tn, b_n, *b_piece)
    tk = _tile(k, tk, a_k, b_k, *(a_piece if a_on_k else ()))
    if b_groups and mode in ("nn", "tn"):
        tn = _tile(n, tn, b_n, *b_group)
    if b_groups and mode == "nt":
        tk = _tile(k, tk, a_k, b_k, *b_group, *a_piece)
    if out_groups:
        tn = _tile(n, tn, b_n, n // out_groups, *b_piece, *(b_group if mode != "nt" else ()))
    nk = k // tk

    def piece_specs(parts, tile, walk, block, place):
        per = parts[0].shape[1] // tile
        def spec(g):
            def index(i, j, l):
                at = walk(i, j, l) - g * per
                return place(i, j, l, jnp.clip(at, 0, per - 1), jnp.logical_and(at >= 0, at < per))
            return pl.BlockSpec(block, index)
        return [spec(g) for g in range(len(parts))], per

    if a_parts:
        if a_on_k:
            a_specs, a_per = piece_specs(a_parts, tk, lambda i, j, l: l, (tm, tk), lambda i, j, l, c, on: (i, c))
        else:
            a_specs, a_per = piece_specs(a_parts, tm, lambda i, j, l: i, (tk, tm),
                                         lambda i, j, l, c, on: (jnp.where(on, l, 0), c))
    elif mode in ("nn", "nt"):
        ao = (a_off[0] // tm, a_off[1] // tk)
        a_specs = [pl.BlockSpec((tm, tk), lambda i, j, l: (i + ao[0], l + ao[1]))]
    else:
        ao = (a_off[0] // tk, a_off[1] // tm)
        a_specs = [pl.BlockSpec((tk, tm), lambda i, j, l: (l + ao[0], i + ao[1]))]
    if b_parts:
        b_specs, b_per = piece_specs(b_parts, tn, lambda i, j, l: j, (tk, tn),
                                     lambda i, j, l, c, on: (jnp.where(on, l, 0), c))
        b_spec = None
    elif mode in ("nn", "tn"):
        bo = (b_off[0] // tk, b_off[1] // tn)
        if b_groups:
            per = b.shape[2] // tn
            b_spec = pl.BlockSpec((None, tk, tn), lambda i, j, l: ((j + bo[1]) // per, l + bo[0], (j + bo[1]) % per))
        else:
            b_spec = pl.BlockSpec((tk, tn), lambda i, j, l: (l + bo[0], j + bo[1]))
    else:
        bo = (b_off[0] // tn, b_off[1] // tk)
        if b_groups:
            per = b.shape[2] // tk
            b_spec = pl.BlockSpec((None, tn, tk), lambda i, j, l: ((l + bo[1]) // per, j + bo[0], (l + bo[1]) % per))
        else:
            b_spec = pl.BlockSpec((tn, tk), lambda i, j, l: (j + bo[0], l + bo[1]))
    if out_groups:
        oper = (n // out_groups) // tn
        out_spec = pl.BlockSpec((None, tm, tn), lambda i, j, l: (j // oper, i, j % oper))
        out_shape = jax.ShapeDtypeStruct((out_groups, m, n // out_groups), out_dtype)
    else:
        out_spec = pl.BlockSpec((tm, tn), lambda i, j, l: (i, j))
        out_shape = jax.ShapeDtypeStruct((m, n), out_dtype)
    dn = {"nn": NN, "nt": NT, "tn": TN}[mode]
    if not b_parts:
        b_specs = [b_spec]
    na = len(a_specs)

    def body(*refs):
        a_refs, b_refs, (o_ref, acc_ref) = refs[:na], refs[na:len(refs) - 2], refs[len(refs) - 2:]

        @pl.when(pl.program_id(2) == 0)
        def _():
            acc_ref[...] = jnp.zeros_like(acc_ref)

        def add(a_ref, b_ref):
            acc_ref[...] += lax.dot_general(a_ref[...], b_ref[...], dn, preferred_element_type=F32)

        if a_parts or b_parts:
            if a_parts:
                active = pl.program_id(2 if a_on_k else 0) // a_per
            else:
                active = pl.program_id(1) // b_per
            for g in range(max(na, len(b_refs))):
                pl.when(active == g)(functools.partial(add, a_refs[g if a_parts else 0], b_refs[g if b_parts else 0]))
        else:
            add(a_refs[0], b_refs[0])

        @pl.when(pl.program_id(2) == nk - 1)
        def _():
            o_ref[...] = acc_ref[...].astype(out_dtype)

    (out,), carried = _call(
        body, name=name, grid=(m // tm, n // tn, nk), in_specs=[*a_specs, *b_specs],
        out_specs=[out_spec], out_shape=[out_shape],
        scratch=[pltpu.VMEM((tm, tn), F32)], sem=("parallel", "parallel", "arbitrary"),
        args=(*(a_parts or (a,)), *(b_parts or (b,))), ride=ride)
    return out if ride is None else (out, carried)


ROWS = 128


def _rms_fwd(x, gain, *, name):
    s, d = x.shape

    def body(x_ref, g_ref, h_ref):
        xf = x_ref[...]
        r = lax.rsqrt(jnp.mean(xf * xf, axis=-1, keepdims=True) + RMS_EPS)
        h_ref[...] = ((xf * r) * g_ref[...]).astype(BF16)

    return pl.pallas_call(
        body, name=name, grid=(s // ROWS,),
        in_specs=[pl.BlockSpec((ROWS, d), lambda i: (i, 0)), pl.BlockSpec((1, d), lambda i: (0, 0))],
        out_specs=pl.BlockSpec((ROWS, d), lambda i: (i, 0)),
        out_shape=jax.ShapeDtypeStruct((s, d), BF16),
        compiler_params=_params(("parallel",)),
    )(x, gain)


def _post_fwd(x, y, gain, *, name):
    s, d = x.shape

    def body(x_ref, y_ref, g_ref, o_ref):
        yf = y_ref[...]
        r = lax.rsqrt(jnp.mean(yf * yf, axis=-1, keepdims=True) + RMS_EPS)
        o_ref[...] = x_ref[...] + (yf * r) * g_ref[...]

    return pl.pallas_call(
        body, name=name, grid=(s // ROWS,),
        in_specs=[pl.BlockSpec((ROWS, d), lambda i: (i, 0)), pl.BlockSpec((ROWS, d), lambda i: (i, 0)),
                  pl.BlockSpec((1, d), lambda i: (0, 0))],
        out_specs=pl.BlockSpec((ROWS, d), lambda i: (i, 0)),
        out_shape=jax.ShapeDtypeStruct((s, d), F32),
        compiler_params=_params(("parallel",)),
    )(x, y, gain)


def _loss_head(xo, target, *, name):
    s, d = xo.shape
    inv_d = 1.0 / d

    def body(x_ref, t_ref, loss_ref, dx_ref):
        @pl.when(pl.program_id(0) == 0)
        def _():
            loss_ref[...] = jnp.zeros_like(loss_ref)

        e = x_ref[...] - t_ref[...]
        dx_ref[...] = e * inv_d
        loss_ref[...] += 0.5 * jnp.sum(jnp.mean(e * e, axis=-1, keepdims=True), axis=0, keepdims=True)

    return pl.pallas_call(
        body, name=name, grid=(s // ROWS,),
        in_specs=[pl.BlockSpec((ROWS, d), lambda i: (i, 0)), pl.BlockSpec((ROWS, d), lambda i: (i, 0))],
        out_specs=[pl.BlockSpec((1, 1), lambda i: (0, 0)), pl.BlockSpec((ROWS, d), lambda i: (i, 0))],
        out_shape=[jax.ShapeDtypeStruct((1, 1), F32), jax.ShapeDtypeStruct((s, d), F32)],
        compiler_params=_params(("arbitrary",)),
    )(xo, target)


def _post_bwd(dxo, y, gain, *, name):
    s, d = y.shape

    def body(dx_ref, y_ref, g_ref, dy_ref, dg_ref):
        @pl.when(pl.program_id(0) == 0)
        def _():
            dg_ref[...] = jnp.zeros_like(dg_ref)

        yf = y_ref[...]
        dxo_ = dx_ref[...]
        r = lax.rsqrt(jnp.mean(yf * yf, axis=-1, keepdims=True) + RMS_EPS)
        nrm = yf * r
        dg_ref[...] += jnp.sum(dxo_ * nrm, axis=0, keepdims=True)
        dn = dxo_ * g_ref[...]
        dy_ref[...] = (r * (dn - nrm * jnp.mean(dn * nrm, axis=-1, keepdims=True))).astype(BF16)

    return pl.pallas_call(
        body, name=name, grid=(s // ROWS,),
        in_specs=[pl.BlockSpec((ROWS, d), lambda i: (i, 0)), pl.BlockSpec((ROWS, d), lambda i: (i, 0)),
                  pl.BlockSpec((1, d), lambda i: (0, 0))],
        out_specs=[pl.BlockSpec((ROWS, d), lambda i: (i, 0)), pl.BlockSpec((1, d), lambda i: (0, 0))],
        out_shape=[jax.ShapeDtypeStruct((s, d), BF16), jax.ShapeDtypeStruct((1, d), F32)],
        compiler_params=_params(("arbitrary",)),
    )(dxo, y, gain)


def _pre_bwd(dhs, x, gain, dres, *, name):
    s, d = x.shape
    n_dh = len(dhs)

    def body(*refs):
        dh_refs = refs[:n_dh]
        x_ref, g_ref, dr_ref, dx_ref, dg_ref = refs[n_dh:]

        @pl.when(pl.program_id(0) == 0)
        def _():
            dg_ref[...] = jnp.zeros_like(dg_ref)

        xf = x_ref[...]
        dh_ = dh_refs[0][...]
        for extra in dh_refs[1:]:
            dh_ = dh_ + extra[...]
        r = lax.rsqrt(jnp.mean(xf * xf, axis=-1, keepdims=True) + RMS_EPS)
        nrm = xf * r
        dg_ref[...] += jnp.sum(dh_ * nrm, axis=0, keepdims=True)
        dn = dh_ * g_ref[...]
        dx_ref[...] = dr_ref[...] + r * (dn - nrm * jnp.mean(dn * nrm, axis=-1, keepdims=True))

    return pl.pallas_call(
        body, name=name, grid=(s // ROWS,),
        in_specs=[pl.BlockSpec((ROWS, d), lambda i: (i, 0))] * (n_dh + 1)
        + [pl.BlockSpec((1, d), lambda i: (0, 0)), pl.BlockSpec((ROWS, d), lambda i: (i, 0))],
        out_specs=[pl.BlockSpec((ROWS, d), lambda i: (i, 0)), pl.BlockSpec((1, d), lambda i: (0, 0))],
        out_shape=[jax.ShapeDtypeStruct((s, d), F32), jax.ShapeDtypeStruct((1, d), F32)],
        compiler_params=_params(("arbitrary",)),
    )(*dhs, x, gain, dres)


GB = 256


def _gate_fwd(af, bias, *, name):
    s, w = af.shape

    def body(af_ref, b_ref, c_ref, carry_ref):
        @pl.when(pl.program_id(0) == 0)
        def _():
            carry_ref[...] = jnp.zeros_like(carry_ref)

        z = af_ref[...] + b_ref[...]
        lf = jnp.minimum(z, 0.0) - jnp.log(1.0 + jnp.exp(-jnp.abs(z)))
        r_i = lax.broadcasted_iota(jnp.int32, (GB, GB), 0)
        c_i = lax.broadcasted_iota(jnp.int32, (GB, GB), 1)
        tri = (c_i <= r_i).astype(BF16)
        hi, mid, lo = _split3(lf)
        pre = (jnp.dot(tri, hi, preferred_element_type=F32) + jnp.dot(tri, mid, preferred_element_type=F32)
               + jnp.dot(tri, lo, preferred_element_type=F32))
        c_ref[...] = pre + carry_ref[...]
        carry_ref[...] += jnp.sum(lf, axis=0, keepdims=True)

    return pl.pallas_call(
        body, name=name, grid=(s // GB,),
        in_specs=[pl.BlockSpec((GB, w), lambda i: (i, 0)), pl.BlockSpec((1, w), lambda i: (0, 0))],
        out_specs=pl.BlockSpec((GB, w), lambda i: (i, 0)),
        out_shape=jax.ShapeDtypeStruct((s, w), F32),
        scratch_shapes=[pltpu.VMEM((1, w), F32)],
        compiler_params=_params(("arbitrary",)),
    )(af, bias)


def _gate_bwd(dc, af, bias, *, name):
    s, w = af.shape
    nb = s // GB

    def body(dc_ref, af_ref, b_ref, daf_ref, db_ref, carry_ref):
        @pl.when(pl.program_id(0) == 0)
        def _():
            carry_ref[...] = jnp.zeros_like(carry_ref)
            db_ref[...] = jnp.zeros_like(db_ref)

        dcb = dc_ref[...]
        r_i = lax.broadcasted_iota(jnp.int32, (GB, GB), 0)
        c_i = lax.broadcasted_iota(jnp.int32, (GB, GB), 1)
        tri = (c_i >= r_i).astype(BF16)
        hi, mid, lo = _split3(dcb)
        suf = (jnp.dot(tri, hi, preferred_element_type=F32) + jnp.dot(tri, mid, preferred_element_type=F32)
               + jnp.dot(tri, lo, preferred_element_type=F32)) + carry_ref[...]
        carry_ref[...] += jnp.sum(dcb, axis=0, keepdims=True)
        z = af_ref[...] + b_ref[...]
        daf = suf * _sigmoid(-z)
        daf_ref[...] = daf.astype(BF16)
        db_ref[...] += jnp.sum(daf, axis=0, keepdims=True)

    return pl.pallas_call(
        body, name=name, grid=(nb,),
        in_specs=[pl.BlockSpec((GB, w), lambda i: (nb - 1 - i, 0)), pl.BlockSpec((GB, w), lambda i: (nb - 1 - i, 0)),
                  pl.BlockSpec((1, w), lambda i: (0, 0))],
        out_specs=[pl.BlockSpec((GB, w), lambda i: (nb - 1 - i, 0)), pl.BlockSpec((1, w), lambda i: (0, 0))],
        out_shape=[jax.ShapeDtypeStruct((s, w), BF16), jax.ShapeDtypeStruct((1, w), F32)],
        scratch_shapes=[pltpu.VMEM((1, w), F32)],
        compiler_params=_params(("arbitrary",)),
    )(dc, af, bias)


def _rel_index_rows():
    w = np.arange(TOEP)
    wp = np.where(w < CQ, w, w - TOEP)
    return np.stack([np.clip(LEFT_CHUNKS * CHUNK - CQ * j - wp, -REL_CLIP, REL_CLIP) + REL_CLIP
                     for j in range(BAND_TILES)]).astype(np.int32)


def _skew_rows(xw, sign):
    row = lax.broadcasted_iota(jnp.int32, xw.shape, 0)
    for b in range(CQ.bit_length() - 1):
        amt = (1 << b) if sign > 0 else TOEP - (1 << b)
        xw = jnp.where(((row >> b) & 1) == 1, pltpu.roll(xw, amt, 1), xw)
    return xw


REL_PAD = 384
HP = 2


def _rel_onehot():
    return jnp.asarray(_rel_index_rows()[:, :, None] == np.arange(REL_PAD)[None, None, :], BF16)


def _fill_bias_tiles(rel_ref, oh_ref, bias_scr):
    parts = _split3(jnp.broadcast_to(rel_ref[...], (8, REL_PAD)))
    for j in range(BAND_TILES):
        row = sum(lax.dot_general(p, oh_ref[j], NT, preferred_element_type=F32) for p in parts)[0:1]
        bias_scr[j] = _skew_rows(jnp.broadcast_to(row, (CQ, TOEP)), +1)[:, :CQ]


FOX_Q = 512


def _fox_scores(s, cq, cr, diagonal):
    s = s + (cq - cr)
    if diagonal is None:
        return s
    bq, bk = s.shape
    key = lax.broadcasted_iota(jnp.int32, (bq, bk), 1) + diagonal
    return jnp.where(key <= lax.broadcasted_iota(jnp.int32, (bq, bk), 0), s, NEG)


def _chunk_scores(s, q0, k0, bias):
    bq, bk = s.shape
    qc = (q0 + lax.broadcasted_iota(jnp.int32, (bq, bk), 0)) >> 6
    kc = (k0 + lax.broadcasted_iota(jnp.int32, (bq, bk), 1)) >> 6
    return jnp.where((kc <= qc) & (kc >= qc - LEFT_CHUNKS), s + bias, NEG)


def _softmax_fwd(qkv, gate, aux, *, mode, width, name, ride=None):
    s = qkv.shape[0]
    nh = width // HEAD_DIM
    bk = CQ
    bq = FOX_Q if mode == "fox" else CQ
    nq, nkt, wide_q = s // bq, s // bk, bq // bk

    assert nh % HP == 0
    hcol = lambda hh: slice(hh * HEAD_DIM, (hh + 1) * HEAD_DIM)

    def body(q_ref, k_ref, v_ref, g_ref, *rest):
        if mode == "fox":
            cc_ref, cr_ref, mixed_ref, o_ref, lse_ref = rest
        else:
            rel_ref, oh_ref, mixed_ref, o_ref, lse_ref, bias_scr = rest
        qi = pl.program_id(1)
        q0 = qi * bq
        q = q_ref[...]

        if mode == "chunk":
            @pl.when(qi == 0)
            def _():
                for hh in range(HP):
                    _fill_bias_tiles(rel_ref.at[hh], oh_ref, bias_scr.at[hh])

            lo, hi = jnp.maximum(qi - (BAND_TILES - 1), 0), qi + 1

        def step(ki, carry, diagonal=None):
            k0 = pl.multiple_of(ki * bk, bk)
            kt = k_ref[pl.ds(k0, bk), :]
            vt = v_ref[pl.ds(k0, bk), :]
            out = []
            for hh in range(HP):
                m, l, acc = carry[hh]
                sc = lax.dot_general(q[:, hcol(hh)], kt[:, hcol(hh)], NT, preferred_element_type=F32) * SCALE
                if mode == "fox":
                    sc = _fox_scores(sc, cc_ref[hh], cr_ref[hh, ki], diagonal)
                else:
                    sc = _chunk_scores(sc, q0, k0, bias_scr[hh, ki - qi + (BAND_TILES - 1)])
                m_new = jnp.maximum(m, jnp.max(sc, axis=-1, keepdims=True))
                p = jnp.exp(sc - m_new)
                alpha = jnp.exp(m - m_new)
                l = alpha * l + jnp.sum(p, axis=-1, keepdims=True)
                acc = alpha * acc + jnp.dot(p.astype(BF16), vt[:, hcol(hh)], preferred_element_type=F32)
                out.append((m_new, l, acc))
            return tuple(out)

        init = ((jnp.full((bq, 1), NEG, F32), jnp.zeros((bq, 1), F32), jnp.zeros((bq, HEAD_DIM), F32)),) * HP
        if mode == "fox":
            done = lax.fori_loop(0, wide_q * qi, step, init)
            for u in range(wide_q):
                done = step(wide_q * qi + u, done, diagonal=u * bk)
        else:
            done = lax.fori_loop(lo, hi, step, init)
        o = jnp.concatenate([acc / l for _, l, acc in done], axis=1)
        g = g_ref[...]
        o_ref[...] = o
        mixed_ref[...] = (o * (g * _sigmoid(g))).astype(BF16)
        for hh, (m, l, _) in enumerate(done):
            lse_ref[hh] = m + jnp.log(l)

    wide = HP * HEAD_DIM
    head_col = lambda off: pl.BlockSpec((s, wide), lambda h, i: (0, off // HP + h))
    qblk = pl.BlockSpec((bq, wide), lambda h, i: (i, h))
    stat = pl.BlockSpec((HP, bq, 1), lambda h, i: (h, i, 0))
    in_specs = [qblk, head_col(nh), head_col(2 * nh), qblk]
    scratch = []
    if mode == "fox":
        in_specs += [stat, pl.BlockSpec((HP, nkt, 1, bk), lambda h, i: (h, 0, 0, 0))]
    else:
        in_specs += [pl.BlockSpec((HP, 1, REL_PAD), lambda h, i: (h, 0, 0)),
                     pl.BlockSpec((BAND_TILES, TOEP, REL_PAD), lambda h, i: (0, 0, 0))]
        scratch = [pltpu.VMEM((HP, BAND_TILES, CQ, CQ), F32)]
    outs, carried = _call(
        body, name=name, grid=(nh // HP, nq), in_specs=in_specs, out_specs=[qblk, qblk, stat],
        out_shape=[jax.ShapeDtypeStruct((s, width), BF16), jax.ShapeDtypeStruct((s, width), F32),
                   jax.ShapeDtypeStruct((nh, s, 1), F32)],
        scratch=scratch, sem=("parallel", "arbitrary"), args=(qkv, qkv, qkv, gate, *aux), ride=ride)
    return outs if ride is None else (*outs, carried)


def _softmax_bwd(qkv, gate, o, dmixed, dm_off, lse, aux, *, mode, width, name, ride=None):
    s = qkv.shape[0]
    nh = width // HEAD_DIM
    bk = CQ
    bq = FOX_Q if mode == "fox" else CQ
    nq, nkt, wide_q = s // bq, s // bk, bq // bk
    dmo = dm_off // HEAD_DIM
    rel_pad = REL_PAD
    assert nh % HP == 0 and dmo % HP == 0
    hcol = lambda hh: slice(hh * HEAD_DIM, (hh + 1) * HEAD_DIM)

    def body(q_ref, k_ref, v_ref, g_ref, o_ref, dm_ref, lse_ref, *rest):
        if mode == "fox":
            cc_ref, cr_ref, dq_ref, dk_ref, dv_ref, dg_ref, dc_ref, dcq_ref, dk_scr, dv_scr, dc_scr = rest
        else:
            rel_ref, oh_ref, dq_ref, dk_ref, dv_ref, dg_ref, drel_ref, dk_scr, dv_scr, bias_scr, db_scr = rest
        qi = pl.program_id(1)
        q0 = qi * bq

        @pl.when(qi == 0)
        def _():
            dk_scr[...] = jnp.zeros_like(dk_scr)
            dv_scr[...] = jnp.zeros_like(dv_scr)
            if mode == "fox":
                dc_scr[...] = jnp.zeros_like(dc_scr)
            else:
                db_scr[...] = jnp.zeros_like(db_scr)
                for hh in range(HP):
                    _fill_bias_tiles(rel_ref.at[hh], oh_ref, bias_scr.at[hh])

        g = g_ref[...]
        of = o_ref[...]
        dm = dm_ref[...]
        sig = _sigmoid(g)
        do = dm * (g * sig)
        dg_ref[...] = (dm * of * (sig * (1.0 + g * (1.0 - sig)))).astype(BF16)
        do_o = do * of
        delta = [jnp.sum(do_o[:, hcol(hh)], axis=-1, keepdims=True) for hh in range(HP)]
        do_b = do.astype(BF16)
        q = q_ref[...]
        if mode == "chunk":
            lo, hi = jnp.maximum(qi - (BAND_TILES - 1), 0), qi + 1

        def step(ki, carry, diagonal=None):
            k0 = pl.multiple_of(ki * bk, bk)
            kt = k_ref[pl.ds(k0, bk), :]
            vt = v_ref[pl.ds(k0, bk), :]
            out = []
            for hh in range(HP):
                dq, rsum = carry[hh]
                qh, kh, doh = q[:, hcol(hh)], kt[:, hcol(hh)], do_b[:, hcol(hh)]
                sc = lax.dot_general(qh, kh, NT, preferred_element_type=F32) * SCALE
                if mode == "fox":
                    sc = _fox_scores(sc, cc_ref[hh], cr_ref[hh, ki], diagonal)
                else:
                    sc = _chunk_scores(sc, q0, k0, bias_scr[hh, ki - qi + (BAND_TILES - 1)])
                p = jnp.exp(sc - lse_ref[hh])
                dp = lax.dot_general(doh, vt[:, hcol(hh)], NT, preferred_element_type=F32)
                ds = p * (dp - delta[hh])
                if mode == "fox":
                    dc_scr[hh, ki] += -jnp.sum(ds, axis=0, keepdims=True)
                    rsum = rsum + jnp.sum(ds, axis=-1, keepdims=True)
                else:
                    db_scr[hh, ki - qi + (BAND_TILES - 1)] += ds
                ds_b = (ds * SCALE).astype(BF16)
                dk_scr[pl.ds(k0, bk), hcol(hh)] += lax.dot_general(ds_b, qh, TN, preferred_element_type=F32)
                dv_scr[pl.ds(k0, bk), hcol(hh)] += lax.dot_general(p.astype(BF16), doh, TN, preferred_element_type=F32)
                out.append((dq + jnp.dot(ds_b, kh, preferred_element_type=F32), rsum))
            return tuple(out)

        init = ((jnp.zeros((bq, HEAD_DIM), F32), jnp.zeros((bq, 1), F32)),) * HP
        if mode == "fox":
            done = lax.fori_loop(0, wide_q * qi, step, init)
            for u in range(wide_q):
                done = step(wide_q * qi + u, done, diagonal=u * bk)
        else:
            done = lax.fori_loop(lo, hi, step, init)
        dq_ref[...] = jnp.concatenate([dq for dq, _ in done], axis=1).astype(BF16)
        if mode == "fox":
            for hh, (_, rsum) in enumerate(done):
                dcq_ref[hh] = rsum

        @pl.when(qi == nq - 1)
        def _():
            dk_ref[...] = dk_scr[...].astype(BF16)
            dv_ref[...] = dv_scr[...].astype(BF16)
            if mode == "fox":
                dc_ref[...] = dc_scr[...]
            else:
                for hh in range(HP):
                    tot = jnp.zeros((8, rel_pad), F32)
                    for j in range(BAND_TILES):
                        wide_ = jnp.concatenate([db_scr[hh, j], jnp.zeros((CQ, TOEP - CQ), F32)], axis=1)
                        diag = jnp.sum(_skew_rows(wide_, -1), axis=0, keepdims=True)
                        for part in _split3(jnp.broadcast_to(diag, (8, TOEP))):
                            tot = tot + jnp.dot(part, oh_ref[j], preferred_element_type=F32)
                    drel_ref[hh] = tot[0:1, :]

    wide = HP * HEAD_DIM
    head_col = lambda off: pl.BlockSpec((s, wide), lambda h, i: (0, off // HP + h))
    qblk = lambda off: pl.BlockSpec((bq, wide), lambda h, i: (i, off // HP + h))
    stat = pl.BlockSpec((HP, bq, 1), lambda h, i: (h, i, 0))
    in_specs = [qblk(0), head_col(nh), head_col(2 * nh), qblk(0), qblk(0), qblk(dmo), stat]
    out_specs = [qblk(0), head_col(0), head_col(0), qblk(0)]
    out_shape = [jax.ShapeDtypeStruct((s, width), BF16)] * 4
    scratch = [pltpu.VMEM((s, wide), F32), pltpu.VMEM((s, wide), F32)]
    if mode == "fox":
        rows = pl.BlockSpec((HP, nkt, 1, bk), lambda h, i: (h, 0, 0, 0))
        in_specs += [stat, rows]
        out_specs += [rows, stat]
        out_shape += [jax.ShapeDtypeStruct((nh, nkt, 1, bk), F32), jax.ShapeDtypeStruct((nh, s, 1), F32)]
        scratch += [pltpu.VMEM((HP, nkt, 1, bk), F32)]
    else:
        rel = pl.BlockSpec((HP, 1, rel_pad), lambda h, i: (h, 0, 0))
        in_specs += [rel, pl.BlockSpec((BAND_TILES, TOEP, rel_pad), lambda h, i: (0, 0, 0))]
        out_specs += [rel]
        out_shape += [jax.ShapeDtypeStruct((nh, 1, rel_pad), F32)]
        scratch += [pltpu.VMEM((HP, BAND_TILES, CQ, CQ), F32), pltpu.VMEM((HP, BAND_TILES, CQ, CQ), F32)]
    outs, carried = _call(
        body, name=name, grid=(nh // HP, nq), in_specs=in_specs, out_specs=out_specs, out_shape=out_shape,
        scratch=scratch, sem=("parallel", "arbitrary"), args=(qkv, qkv, qkv, gate, o, dmixed, lse, *aux), ride=ride)
    return outs if ride is None else (*outs, carried)


SBK = 256
SBQ = 2 * SBK
SB_TILES_PER_TRIP = 1
SB_DEAD = -110.0


def _suffix_excl(x, tri):
    r = x.shape[0]
    both = jnp.dot(jnp.concatenate(_split2(x), axis=0), tri, preferred_element_type=F32)
    return both[:r] + both[r:]


def _sb_logits(qh, kt, diag):
    z = lax.dot_general(qh, kt, NT, preferred_element_type=F32) * SCALE
    lom = jnp.minimum(-z, 0.0) - jnp.log(1.0 + jnp.exp(-jnp.abs(z)))
    if diag is not None:
        lom = jnp.where(diag, lom, 0.0)
    return z, lom


def _sb_fwd_tile(qh, kt, vt, tri, diag, run, acc):
    z, lom = _sb_logits(qh, kt, diag)
    a = jnp.exp(lom + z + (_suffix_excl(lom, tri) + run))
    if diag is not None:
        a = jnp.where(diag, a, 0.0)
    acc = acc + jnp.dot(a.astype(BF16), vt, preferred_element_type=F32)
    return run + jnp.sum(lom, axis=-1, keepdims=True), acc


def _sb_alive(run_a, run_b):
    return (jnp.max(jnp.maximum(run_a, run_b)) > SB_DEAD).astype(jnp.int32)


def _sb_fwd(qkv, gate, *, width, name):
    s = qkv.shape[0]
    nh = width // HEAD_DIM
    nq = s // SBQ

    def body(q_ref, k_ref, v_ref, g_ref, mixed_ref, o_ref):
        qi = pl.program_id(1)
        r_i = lax.broadcasted_iota(jnp.int32, (SBK, SBK), 0)
        c_i = lax.broadcasted_iota(jnp.int32, (SBK, SBK), 1)
        tri = (r_i > c_i).astype(BF16)
        diag = c_i < r_i
        q_a = q_ref[0:SBK, :]
        q_b = q_ref[SBK:SBQ, :]

        def kv(tile):
            k0 = pl.multiple_of(tile * SBK, SBK)
            return k_ref[pl.ds(k0, SBK), :], v_ref[pl.ds(k0, SBK), :]

        zero = (jnp.zeros((SBK, 1), F32), jnp.zeros((SBK, HEAD_DIM), F32))
        kt, vt = kv(2 * qi + 1)
        run_b, acc_b = _sb_fwd_tile(q_b, kt, vt, tri, diag, *zero)
        kt, vt = kv(2 * qi)
        run_b, acc_b = _sb_fwd_tile(q_b, kt, vt, tri, None, run_b, acc_b)
        run_a, acc_a = _sb_fwd_tile(q_a, kt, vt, tri, diag, *zero)

        trips = (2 // SB_TILES_PER_TRIP) * qi

        def step(carry):
            t, _, run_a, acc_a, run_b, acc_b = carry
            for u in range(SB_TILES_PER_TRIP):
                kt, vt = kv(2 * qi - 1 - (SB_TILES_PER_TRIP * t + u))
                run_a, acc_a = _sb_fwd_tile(q_a, kt, vt, tri, None, run_a, acc_a)
                run_b, acc_b = _sb_fwd_tile(q_b, kt, vt, tri, None, run_b, acc_b)
            return t + 1, _sb_alive(run_a, run_b), run_a, acc_a, run_b, acc_b

        _, _, _, acc_a, _, acc_b = lax.while_loop(
            lambda c: jnp.logical_and(c[0] < trips, c[1] > 0), step,
            (jnp.int32(0), _sb_alive(run_a, run_b), run_a, acc_a, run_b, acc_b))
        o = jnp.concatenate([acc_a, acc_b], axis=0)
        g = g_ref[...]
        o_ref[...] = o
        mixed_ref[...] = (o * (g * _sigmoid(g))).astype(BF16)

    head_col = lambda off: pl.BlockSpec((s, HEAD_DIM), lambda h, i: (0, off + h))
    qblk = pl.BlockSpec((SBQ, HEAD_DIM), lambda h, i: (i, h))
    return pl.pallas_call(
        body, name=name, grid=(nh, nq), in_specs=[qblk, head_col(nh), head_col(2 * nh), qblk],
        out_specs=[qblk, qblk],
        out_shape=[jax.ShapeDtypeStruct((s, width), BF16), jax.ShapeDtypeStruct((s, width), F32)],
        compiler_params=_params(("parallel", "arbitrary")),
    )(qkv, qkv, qkv, gate)


def _sb_bwd(qkv, gate, o, dmixed, *, width, name):
    s = qkv.shape[0]
    nh = width // HEAD_DIM
    nq = s // SBQ

    def body(q_ref, k_ref, v_ref, g_ref, o_ref, dm_ref, dq_ref, dk_ref, dv_ref, dg_ref, dk_scr, dv_scr):
        qi = pl.program_id(1)

        @pl.when(qi == 0)
        def _():
            dk_scr[...] = jnp.zeros_like(dk_scr)
            dv_scr[...] = jnp.zeros_like(dv_scr)

        g = g_ref[...]
        of = o_ref[...]
        dm = dm_ref[...]
        sig = _sigmoid(g)
        do = dm * (g * sig)
        dg_ref[...] = (dm * of * (sig * (1.0 + g * (1.0 - sig)))).astype(BF16)
        do_b = do.astype(BF16)
        q = q_ref[...]
        r_i = lax.broadcasted_iota(jnp.int32, (SBK, SBK), 0)
        c_i = lax.broadcasted_iota(jnp.int32, (SBK, SBK), 1)
        tri = (r_i > c_i).astype(BF16)
        tri_pre = (r_i < c_i).astype(BF16)
        diag = c_i < r_i
        q_a, q_b = q[0:SBK], q[SBK:SBQ]
        do_a, do_b2 = do_b[0:SBK], do_b[SBK:SBQ]

        def kv(t):
            k0 = pl.multiple_of(t * SBK, SBK)
            return k0, k_ref[pl.ds(k0, SBK), :], v_ref[pl.ds(k0, SBK), :]

        def mass(qh, tile_, dg_):
            return jnp.sum(_sb_logits(qh, kv(tile_)[1], dg_)[1], axis=-1, keepdims=True)

        run_b = mass(q_b, 2 * qi + 1, diag) + mass(q_b, 2 * qi, None)
        run_a = mass(q_a, 2 * qi, diag)
        trips = (2 // SB_TILES_PER_TRIP) * qi

        def scout(carry):
            t, _, run_a, run_b = carry
            for u in range(SB_TILES_PER_TRIP):
                tile_ = 2 * qi - 1 - (SB_TILES_PER_TRIP * t + u)
                run_a = run_a + mass(q_a, tile_, None)
                run_b = run_b + mass(q_b, tile_, None)
            return t + 1, _sb_alive(run_a, run_b), run_a, run_b

        walked, _, run_a, run_b = lax.while_loop(
            lambda c: jnp.logical_and(c[0] < trips, c[1] > 0), scout,
            (jnp.int32(0), _sb_alive(run_a, run_b), run_a, run_b))

        def tile(qh, doh, kt, vt, dg_, carry):
            rem, gpre, dq = carry
            z, lom = _sb_logits(qh, kt, dg_)
            rem = rem - jnp.sum(lom, axis=-1, keepdims=True)
            a = jnp.exp(lom + z + (_suffix_excl(lom, tri) + rem))
            if dg_ is not None:
                a = jnp.where(dg_, a, 0.0)
            gg = lax.dot_general(doh, vt, NT, preferred_element_type=F32) * a
            pre = _suffix_excl(gg, tri_pre) + gpre
            dz = gg * jnp.exp(lom) - pre * jnp.exp(lom + z)
            if dg_ is not None:
                dz = jnp.where(dg_, dz, 0.0)
            dz_b = (dz * SCALE).astype(BF16)
            dq = dq + jnp.dot(dz_b, kt, preferred_element_type=F32)
            return (rem, gpre + jnp.sum(gg, axis=-1, keepdims=True), dq), dz_b, a.astype(BF16)

        def both(t, ca, cb, dg_a):
            k0, kt, vt = kv(t)
            ca, dz_a, a_a = tile(q_a, do_a, kt, vt, dg_a, ca)
            cb, dz_b_, a_b = tile(q_b, do_b2, kt, vt, None, cb)
            dk_scr[pl.ds(k0, SBK), :] += lax.dot_general(jnp.concatenate([dz_a, dz_b_], axis=0), q, TN,
                                                         preferred_element_type=F32)
            dv_scr[pl.ds(k0, SBK), :] += lax.dot_general(jnp.concatenate([a_a, a_b], axis=0), do_b, TN,
                                                         preferred_element_type=F32)
            return ca, cb

        def step(t, carry):
            ca, cb = carry
            for u in range(SB_TILES_PER_TRIP):
                ca, cb = both(SB_TILES_PER_TRIP * t + u, ca, cb, None)
            return ca, cb

        zero = (jnp.zeros((SBK, 1), F32), jnp.zeros((SBK, HEAD_DIM), F32))
        ca, cb = lax.fori_loop(trips - walked, trips, step, ((run_a, *zero), (run_b, *zero)))
        ca, cb = both(2 * qi, ca, cb, diag)
        k0, kt, vt = kv(2 * qi + 1)
        cb, dz_b_, a_b = tile(q_b, do_b2, kt, vt, diag, cb)
        dk_scr[pl.ds(k0, SBK), :] += lax.dot_general(dz_b_, q_b, TN, preferred_element_type=F32)
        dv_scr[pl.ds(k0, SBK), :] += lax.dot_general(a_b, do_b2, TN, preferred_element_type=F32)
        dq_ref[...] = jnp.concatenate([ca[2], cb[2]], axis=0).astype(BF16)

        @pl.when(qi == nq - 1)
        def _():
            dk_ref[...] = dk_scr[...].astype(BF16)
            dv_ref[...] = dv_scr[...].astype(BF16)

    head_col = lambda off: pl.BlockSpec((s, HEAD_DIM), lambda h, i: (0, off + h))
    qblk = pl.BlockSpec((SBQ, HEAD_DIM), lambda h, i: (i, h))
    return pl.pallas_call(
        body, name=name, grid=(nh, nq),
        in_specs=[qblk, head_col(nh), head_col(2 * nh), qblk, qblk, qblk],
        out_specs=[qblk, head_col(0), head_col(0), qblk],
        out_shape=[jax.ShapeDtypeStruct((s, width), BF16)] * 4,
        scratch_shapes=[pltpu.VMEM((s, HEAD_DIM), F32), pltpu.VMEM((s, HEAD_DIM), F32)],
        compiler_params=_params(("parallel", "arbitrary")),
    )(qkv, qkv, qkv, gate, o, dmixed)


HBM = pl.BlockSpec(memory_space=pl.ANY)
MESH = pl.DeviceIdType.MESH


class _GatherRide:
    def __init__(self, shard):
        self.src = shard
        self.out_shape = jax.ShapeDtypeStruct((N_DEV, *shard.shape), shard.dtype)
        self.scratch = [pltpu.SemaphoreType.DMA((7,)), pltpu.SemaphoreType.DMA((7,)), pltpu.SemaphoreType.DMA]

    def _copies(self, x_ref, out_ref, send_sems, recv_sems, local_sem):
        x, y, c = lax.axis_index("x"), lax.axis_index("y"), lax.axis_index("c")
        me, sibling = (x, y, c), (x, y, 1 - c)
        chips = [(1 - x, y), (x, 1 - y), (1 - x, 1 - y)]

        def slot(px, py, pc):
            return out_ref.at[4 * px + 2 * py + pc]

        def copy(k, block, to, src=None):
            return pltpu.make_async_remote_copy(
                src_ref=slot(*block) if src is None else src, dst_ref=slot(*block),
                send_sem=send_sems.at[k], recv_sem=recv_sems.at[k], device_id=to, device_id_type=MESH)

        mine = pltpu.make_async_copy(x_ref, slot(*me), local_sem)
        first = [copy(0, me, sibling, src=x_ref)]
        first += [copy(1 + j, me, (*chip, c), src=x_ref) for j, chip in enumerate(chips)]
        landed = [copy(1 + j, (*chip, c), me) for j, chip in enumerate(chips)]
        passed = [copy(4 + j, (*chip, c), sibling) for j, chip in enumerate(chips)]
        from_sibling = [copy(0, sibling, me)] + [copy(4 + j, (*chip, 1 - c), me) for j, chip in enumerate(chips)]
        return mine, first, landed, passed, from_sibling

    def start(self, *refs):
        mine, first, _, _, _ = self._copies(*refs)
        mine.start()
        for cp in first:
            cp.start()

    def relay(self, *refs):
        _, _, landed, passed, _ = self._copies(*refs)
        for got, onward in zip(landed, passed):
            got.wait_recv()
            onward.start()

    def finish(self, *refs):
        mine, first, _, passed, from_sibling = self._copies(*refs)
        for cp in from_sibling:
            cp.wait_recv()
        for cp in first + passed:
            cp.wait_send()
        mine.wait()


def _all_gather(shard, *, name):
    ride = _GatherRide(shard)

    def body(*refs):
        ride.start(*refs)
        ride.relay(*refs)
        ride.finish(*refs)

    return pl.pallas_call(body, name=name, in_specs=[HBM], out_specs=HBM, out_shape=ride.out_shape,
                          scratch_shapes=ride.scratch)(shard)


class _Ride:
    def __init__(self, src, *, gather, chips=False):
        self.src, self.gather, self.chips = src, gather, chips
        n = N_DEV // 2 if chips else N_DEV
        self.out_shape = jax.ShapeDtypeStruct((n, *(src.shape if gather else src.shape[1:])), src.dtype)
        self.scratch = [pltpu.SemaphoreType.DMA((7,)), pltpu.SemaphoreType.DMA((7,)), pltpu.SemaphoreType.DMA]

    def _copies(self, src_ref, out_ref, send_sems, recv_sems, local_sem):
        x, y, c = lax.axis_index("x"), lax.axis_index("y"), lax.axis_index("c")
        slot = (lambda px, py, pc: 2 * px + py) if self.chips else (lambda px, py, pc: 4 * px + 2 * py + pc)
        me = slot(x, y, c)
        pick = (lambda j: src_ref) if self.gather else (lambda j: src_ref.at[j])
        mine = pltpu.make_async_copy(pick(me), out_ref.at[me], local_sem)
        copies = []
        for k in range(2 if self.chips else 1, N_DEV, 2 if self.chips else 1):
            px, py, pc = x ^ ((k >> 2) & 1), y ^ ((k >> 1) & 1), c ^ (k & 1)
            copies.append(pltpu.make_async_remote_copy(
                src_ref=pick(slot(px, py, pc)), dst_ref=out_ref.at[me],
                send_sem=send_sems.at[k - 1], recv_sem=recv_sems.at[k - 1],
                device_id=(px, py, pc), device_id_type=MESH))
        return mine, copies

    def start(self, *refs):
        mine, copies = self._copies(*refs)
        mine.start()
        for cp in copies:
            cp.start()

    def finish(self, *refs):
        mine, copies = self._copies(*refs)
        for cp in copies:
            cp.wait_recv()
        for cp in copies:
            cp.wait_send()
        mine.wait()


def _exchange(src, *, gather, name):
    ride = _Ride(src, gather=gather)

    def body(*refs):
        ride.start(*refs)
        ride.finish(*refs)

    return pl.pallas_call(body, name=name, in_specs=[HBM], out_specs=HBM, out_shape=ride.out_shape,
                          scratch_shapes=ride.scratch)(src)


def _presum_on_chip(parts, *, name, rows=256):
    _, r, c_ = parts.shape
    rows = min(rows, r)
    assert r % rows == 0
    by_core = parts.reshape(N_DEV // 2, 2, r, c_)

    def swap(src_ref, out_ref, send_sem, recv_sem):
        x, y, c = lax.axis_index("x"), lax.axis_index("y"), lax.axis_index("c")
        cp = pltpu.make_async_remote_copy(
            src_ref=src_ref.at[:, 1 - c], dst_ref=out_ref, send_sem=send_sem, recv_sem=recv_sem,
            device_id=(x, y, 1 - c), device_id_type=MESH)
        cp.start()
        cp.wait()

    got = pl.pallas_call(swap, name=name + "_swap", in_specs=[HBM], out_specs=HBM,
                         out_shape=jax.ShapeDtypeStruct((N_DEV // 2, r, c_), parts.dtype),
                         scratch_shapes=[pltpu.SemaphoreType.DMA, pltpu.SemaphoreType.DMA])(by_core)

    def add(core_ref, a_ref, b_ref, o_ref):
        del core_ref
        o_ref[...] = (a_ref[...].astype(F32) + b_ref[...].astype(F32)).astype(o_ref.dtype)

    blk = pl.BlockSpec((1, rows, c_), lambda j, i, core: (j, i, 0))
    mine = pl.BlockSpec((1, None, rows, c_), lambda j, i, core: (j, core[0], i, 0))
    core = jnp.reshape(lax.axis_index("c"), (1,)).astype(jnp.int32)
    return pl.pallas_call(
        add, name=name + "_sum",
        grid_spec=pltpu.PrefetchScalarGridSpec(num_scalar_prefetch=1, grid=(N_DEV // 2, r // rows),
                                               in_specs=[mine, blk], out_specs=blk),
        out_shape=jax.ShapeDtypeStruct((N_DEV // 2, r, c_), parts.dtype),
        compiler_params=_params(("parallel", "parallel")))(core, by_core, got)


def _call(body, *, name, grid, in_specs, out_specs, out_shape, scratch, sem, args, ride=None):
    if ride is None:
        outs = pl.pallas_call(body, name=name, grid=grid, in_specs=in_specs, out_specs=out_specs, out_shape=out_shape,
                              scratch_shapes=scratch, compiler_params=_params(sem))(*args)
        return list(outs), None
    n_in, n_out = len(in_specs), len(out_specs)

    def carrying(*refs):
        ins, src_ref = refs[:n_in], refs[n_in]
        outs, dst_ref = refs[n_in + 1:n_in + 1 + n_out], refs[n_in + 1 + n_out]
        rest = refs[n_in + 2 + n_out:]
        own, sems = rest[:len(rest) - 3], rest[len(rest) - 3:]
        ids = [pl.program_id(a) for a in range(len(grid))]
        first = functools.reduce(jnp.logical_and, [i == 0 for i in ids])
        last = functools.reduce(jnp.logical_and, [i == n - 1 for i, n in zip(ids, grid)])

        @pl.when(first)
        def _():
            ride.start(src_ref, dst_ref, *sems)

        if hasattr(ride, "relay"):
            at = functools.reduce(jnp.logical_and, [ids[0] == (3 * grid[0]) // 4] + [i == 0 for i in ids[1:]])

            @pl.when(at)
            def _():
                ride.relay(src_ref, dst_ref, *sems)

        body(*ins, *outs, *own)

        @pl.when(last)
        def _():
            ride.finish(src_ref, dst_ref, *sems)

    outs = pl.pallas_call(
        carrying, name=name, grid=grid, in_specs=[*in_specs, HBM], out_specs=[*out_specs, HBM],
        out_shape=[*out_shape, ride.out_shape], scratch_shapes=[*scratch, *ride.scratch],
        compiler_params=_params(("arbitrary",) * len(grid)))(*args, ride.src)
    return list(outs[:-1]), outs[-1]


def _adamw(parts, w, m, v, *, name, rows):
    r, c_ = w.shape
    rows = min(rows, r)
    assert r % rows == 0
    c1 = 1.0 / (1.0 - ADAM_B1 ** ADAM_STEP)
    c2 = 1.0 / (1.0 - ADAM_B2 ** ADAM_STEP)

    def body(p_ref, w_ref, m_ref, v_ref, g_ref, d_ref, nm_ref, nv_ref):
        g = p_ref[0].astype(F32)
        for i in range(1, parts.shape[0]):
            g = g + p_ref[i].astype(F32)
        nm = ADAM_B1 * m_ref[...] + (1.0 - ADAM_B1) * g
        nv = ADAM_B2 * v_ref[...] + (1.0 - ADAM_B2) * (g * g)
        g_ref[...] = g
        nm_ref[...] = nm
        nv_ref[...] = nv
        d_ref[...] = -ADAM_LR * ((nm * c1) / (jnp.sqrt(nv * c2) + ADAM_EPS) + ADAM_WD * w_ref[...])

    blk = pl.BlockSpec((rows, c_), lambda i: (i, 0))
    return pl.pallas_call(
        body, name=name, grid=(r // rows,),
        in_specs=[pl.BlockSpec((parts.shape[0], rows, c_), lambda i: (0, i, 0)), blk, blk, blk],
        out_specs=[blk] * 4, out_shape=[jax.ShapeDtypeStruct((r, c_), F32)] * 4,
        compiler_params=_params(("parallel",)),
    )(parts, w, m, v)


def _pad_cols(a, n):
    return jnp.pad(a, ((0, 0), (0, n - a.shape[1])))


def _fox_dc(shares, nh, s):
    key_side, query_side = shares
    return _pad_cols(jnp.transpose(key_side.reshape(nh, s) + query_side.reshape(nh, s)), 128)


def _local_step(x, target, norm_pre, norm_post, w_in_e, w_f, b_f, rel_bias, sh_out_e, sh_in_o, sh_out_o):
    s, d = x.shape
    wa = d // 2
    nha = wa // HEAD_DIM
    nq = s // CQ
    gather = lambda shard: _Ride(shard, gather=True)
    scatter = lambda parts: _Ride(parts, gather=False)

    h0 = _rms_fwd(x, norm_pre[0:1], name="rms_pre0")
    proj = lambda w, off, n, dt, nm, **kw: _matmul(h0, w, mode="nn", m=s, n=n, k=d, out_dtype=dt, name=nm,
                                                   b_off=(0, off), **kw)
    qkv_a, w_out_e = proj(w_in_e, 0, 3 * wa, BF16, "proj_qkv_a", ride=gather(sh_out_e))
    w_out_e = w_out_e.reshape(d, d)
    g_a = proj(w_in_e, 3 * wa, wa, F32, "proj_gate_a")
    qkv_b = proj(w_in_e, 4 * wa, 3 * wa, BF16, "proj_qkv_b")
    g_b = proj(w_in_e, 7 * wa, wa, F32, "proj_gate_b")
    af = _matmul(h0, w_f, mode="nn", m=s, n=128, k=d, out_dtype=F32, name="proj_forget")
    bias128 = _pad_cols(b_f, 128)
    cum = _gate_fwd(af, bias128, name="forget_cumsum")
    c_t = jnp.transpose(cum[:, :nha])
    c_col = c_t.reshape(nha, s, 1)
    c_row = c_t.reshape(nha, nq, 1, CQ)
    mixed_a, o_a, lse_a, w_in_o = _softmax_fwd(qkv_a, g_a, (c_col, c_row), mode="fox", width=wa, name="fox_fwd",
                                               ride=_GatherRide(sh_in_o))
    rel_aux = (_pad_cols(rel_bias, REL_PAD).reshape(nha, 1, REL_PAD), _rel_onehot())
    mixed_b, o_b, lse_b, w_out_o = _softmax_fwd(qkv_b, g_b, rel_aux, mode="chunk", width=wa, name="chunk_fwd",
                                                ride=gather(sh_out_o))
    w_out_o = w_out_o.reshape(d, d)
    mixed0 = (mixed_a, mixed_b)
    y0 = _matmul(mixed0, w_out_e, mode="nn", m=s, n=d, k=d, out_dtype=F32, name="out_proj0")
    x1 = _post_fwd(x, y0, norm_post[0:1], name="post0")

    h1 = _rms_fwd(x1, norm_pre[1:2], name="rms_pre1")
    qkv_c = _matmul(h1, w_in_o, mode="nn", m=s, n=3 * d, k=d, out_dtype=BF16, name="proj_qkv_c", b_groups=N_DEV)
    g_c = _matmul(h1, w_in_o, mode="nn", m=s, n=d, k=d, out_dtype=F32, name="proj_gate_c", b_off=(0, 3 * d),
                  b_groups=N_DEV)
    mixed1, o_c = _sb_fwd(qkv_c, g_c, width=d, name="sb_fwd")
    y1 = _matmul(mixed1, w_out_o, mode="nn", m=s, n=d, k=d, out_dtype=F32, name="out_proj1")
    x2 = _post_fwd(x1, y1, norm_post[1:2], name="post1")

    loss, dx2 = _loss_head(x2, target, name="loss_head")

    dy1, dgpost1 = _post_bwd(dx2, y1, norm_post[1:2], name="post_bwd1")
    dmixed1 = _matmul(dy1, w_out_o, mode="nt", m=s, n=d, k=d, out_dtype=F32, name="dmixed1")
    dw_out_o = _matmul(mixed1, dy1, mode="tn", m=d, n=d, k=s, out_dtype=BF16, name="dw_out1")
    dq_c, dk_c, dv_c, dg_c = _sb_bwd(qkv_c, g_c, o_c, dmixed1, width=d, name="sb_bwd")
    dproj1 = (dq_c, dk_c, dv_c, dg_c)
    dh1 = _matmul(dproj1, w_in_o, mode="nt", m=s, n=d, k=4 * d, out_dtype=F32, name="dh1", b_groups=N_DEV, tn=1024)
    parts_in_o = _matmul(h1, dproj1, mode="tn", m=d, n=4 * d, k=s, out_dtype=BF16, name="dw_in1", out_groups=N_DEV,
                         tn=1024)
    dx1, dgpre1 = _pre_bwd((dh1,), x1, norm_pre[1:2], dx2, name="pre_bwd1")

    dy0, dgpost0 = _post_bwd(dx1, y0, norm_post[0:1], name="post_bwd0")
    dmixed0 = _matmul(dy0, w_out_e, mode="nt", m=s, n=d, k=d, out_dtype=F32, name="dmixed0")
    dw_out_e = _matmul(mixed0, dy0, mode="tn", m=d, n=d, k=s, out_dtype=BF16, name="dw_out0")
    chip_in_o = _presum_on_chip(parts_in_o, name="rs_w_in_odd")
    dq_a, dk_a, dv_a, dg_a, *dc_shares, got_in_o = _softmax_bwd(
        qkv_a, g_a, o_a, dmixed0, 0, lse_a, (c_col, c_row), mode="fox", width=wa, name="fox_bwd",
        ride=_Ride(chip_in_o, gather=False, chips=True))
    dq_b, dk_b, dv_b, dg_b, drel, got_out_o = _softmax_bwd(
        qkv_b, g_b, o_b, dmixed0, wa, lse_b, rel_aux, mode="chunk", width=wa, name="chunk_bwd",
        ride=scatter(dw_out_o.reshape(N_DEV, d // N_DEV, d)))
    dc = _fox_dc(dc_shares, nha, s)
    daf, dbf = _gate_bwd(dc, af, bias128, name="forget_bwd")
    dproj0 = jnp.concatenate([dq_a, dk_a, dv_a, dg_a, dq_b, dk_b, dv_b, dg_b], axis=1)
    dw_in_e, got_out_e = _matmul(h0, dproj0, mode="tn", m=d, n=8 * wa, k=s, out_dtype=BF16, name="dw_in0",
                                 ride=scatter(dw_out_e.reshape(N_DEV, d // N_DEV, d)))
    dw_f = _matmul(h0, daf, mode="tn", m=d, n=128, k=s, out_dtype=BF16, name="dw_forget")
    dw_e = jnp.concatenate([dw_in_e, dw_f[:, :nha]], axis=1)
    parts_in_e = jnp.transpose(dw_e.reshape(d, N_DEV, dw_e.shape[1] // N_DEV), (1, 0, 2))
    chip_in_e = _presum_on_chip(parts_in_e, name="rs_w_in_even")
    dh0, got_in_e = _matmul(dproj0, w_in_e, mode="nt", m=s, n=d, k=8 * wa, out_dtype=F32, name="dh0_main",
                            ride=_Ride(chip_in_e, gather=False, chips=True))
    dh0f = _matmul(daf, w_f, mode="nt", m=s, n=d, k=128, out_dtype=F32, name="dh0_forget")
    dx0, dgpre0 = _pre_bwd((dh0, dh0f), x, norm_pre[0:1], dx1, name="pre_bwd0")

    small = dict(
        norm_pre=jnp.concatenate([dgpre0, dgpre1], axis=0),
        norm_post=jnp.concatenate([dgpost0, dgpost1], axis=0),
        b_f=dbf[:, :nha], rel_bias=drel[:, 0, :N_REL])
    got = dict(w_in_even=got_in_e, w_out_even=got_out_e, w_in_odd=got_in_o, w_out_odd=got_out_o)
    return loss, dx0, got, small


def _pack_small(norm_pre, norm_post, b_f, rel_bias):
    flat = jnp.concatenate([norm_pre.reshape(-1), norm_post.reshape(-1), b_f.reshape(-1), rel_bias.reshape(-1)])
    n = flat.shape[0]
    rows = -(-n // 128)
    rows = -(-rows // 8) * 8
    return jnp.pad(flat, (0, rows * 128 - n)).reshape(rows, 128)


def _unpack_small(slab, shapes):
    flat = slab.reshape(-1)
    out, off = [], 0
    for shp in shapes:
        n = int(np.prod(shp))
        out.append(flat[off:off + n].reshape(shp))
        off += n
    return out


def kernel(x, norm_pre, norm_post, w_in_even, b_f_even, rel_bias_even, w_out_even, w_in_odd, w_out_odd, loss_target, m_norm_pre, m_norm_post, m_w_in_even, m_b_f_even, m_rel_bias_even, m_w_out_even, m_w_in_odd, m_w_out_odd, v_norm_pre, v_norm_post, v_w_in_even, v_b_f_even, v_rel_bias_even, v_w_out_even, v_w_in_odd, v_w_out_odd):
    _, s, d = x.shape
    wa = d // 2
    nha = wa // HEAD_DIM
    in_e = w_in_even.shape[2] * N_DEV

    w_in_e_all = jnp.transpose(_all_gather(w_in_even[0].astype(BF16), name="ag_w_in_even"), (1, 0, 2)).reshape(d, in_e)
    w_main = w_in_e_all
    w_f = _pad_cols(w_in_e_all[:, 8 * wa:], 128)

    loss, dx, got, small = _local_step(
        x[0], loss_target[0], norm_pre, norm_post, w_main, w_f, b_f_even, rel_bias_even[0],
        w_out_even[0].astype(BF16), w_in_odd[0].astype(BF16), w_out_odd[0].astype(BF16))

    upd = {}
    upd["w_in_even"] = _adamw(got["w_in_even"], w_in_even[0], m_w_in_even[0], v_w_in_even[0],
                              name="adamw_w_in_even", rows=128)
    upd["w_out_even"] = _adamw(got["w_out_even"], w_out_even[0], m_w_out_even[0], v_w_out_even[0],
                               name="adamw_w_out_even", rows=64)
    upd["w_in_odd"] = _adamw(got["w_in_odd"], w_in_odd[0], m_w_in_odd[0], v_w_in_odd[0],
                             name="adamw_w_in_odd", rows=128)
    upd["w_out_odd"] = _adamw(got["w_out_odd"], w_out_odd[0], m_w_out_odd[0], v_w_out_odd[0],
                              name="adamw_w_out_odd", rows=64)

    shapes = [norm_pre.shape, norm_post.shape, b_f_even.shape, rel_bias_even.shape]
    g_slab = _pack_small(small["norm_pre"], small["norm_post"], small["b_f"], small["rel_bias"])
    parts_small = _exchange(g_slab, gather=True, name="ar_small")
    sm = _adamw(parts_small, _pack_small(norm_pre, norm_post, b_f_even, rel_bias_even),
                _pack_small(m_norm_pre, m_norm_post, m_b_f_even, m_rel_bias_even),
                _pack_small(v_norm_pre, v_norm_post, v_b_f_even, v_rel_bias_even), name="adamw_small", rows=g_slab.shape[0])
    sm = [_unpack_small(a, shapes) for a in sm]

    total = lax.psum(loss[0, 0], ("x", "y", "c"))

    def leaves(kind):
        return (sm[kind][0], sm[kind][1], upd["w_in_even"][kind][None], sm[kind][2], sm[kind][3],
                upd["w_out_even"][kind][None], upd["w_in_odd"][kind][None], upd["w_out_odd"][kind][None])

    return (total, dx[None], *leaves(0), *leaves(1), *leaves(2), *leaves(3))
```

```python
import functools

import numpy as np
import jax
import jax.numpy as jnp
from jax import lax
from jax.experimental import pallas as pl
from jax.experimental.pallas import tpu as pltpu

F32 = jnp.float32
BF16 = jnp.bfloat16

HEAD_DIM = 128
CHUNK = 64
LEFT_CHUNKS = 8
REL_CLIP = 128
N_REL = 2 * REL_CLIP + 1
RMS_EPS = 1e-6
SCALE = HEAD_DIM ** -0.5

ADAM_LR = 0.001
ADAM_B1 = 0.9
ADAM_B2 = 0.999
ADAM_EPS = 1e-08
ADAM_WD = 0.01
ADAM_STEP = 10

N_DEV = 8
V7X_VMEM_LIMIT_BYTES = 56 * 1024 * 1024
NEG = -1e30

NT = (((1,), (1,)), ((), ()))
TN = (((0,), (0,)), ((), ()))
NN = (((1,), (0,)), ((), ()))

CQ = 256
BAND_TILES = 3
TOEP = 2 * CQ
assert (BAND_TILES - 1) * CQ == LEFT_CHUNKS * CHUNK


def _params(sem):
    return pltpu.CompilerParams(dimension_semantics=sem, vmem_limit_bytes=V7X_VMEM_LIMIT_BYTES)


def _split3(x):
    hi = x.astype(BF16)
    r1 = x - hi.astype(F32)
    mid = r1.astype(BF16)
    lo = (r1 - mid.astype(F32)).astype(BF16)
    return hi, mid, lo


def _split2(x):
    hi = x.astype(BF16)
    lo = (x - hi.astype(F32)).astype(BF16)
    return hi, lo


def _sigmoid(g):
    return 1.0 / (1.0 + jnp.exp(-g))


def _tile(n, cap, *offsets):
    if n <= 128:
        return n
    t = (min(cap, n) // 128) * 128
    while n % t or any(o % t for o in offsets):
        t -= 128
    return t


def _matmul(a, b, *, mode, m, n, k, out_dtype, name, a_off=(0, 0), b_off=(0, 0), tm=1024, tn=2048, tk=1024,
            b_groups=None, out_groups=None, ride=None):
    a_m, a_k = (a_off if mode in ("nn", "nt") else a_off[::-1])
    b_k, b_n = (b_off if mode in ("nn", "tn") else b_off[::-1])
    b_group = (b.shape[2],) if b_groups else ()
    a_parts, b_parts = (a if isinstance(a, tuple) else None), (b if isinstance(b, tuple) else None)
    assert not (a_parts and b_parts) and not (b_parts and mode == "nt")
    a_piece = (a_parts[0].shape[1],) if a_parts else ()
    b_piece = (b_parts[0].shape[1],) if b_parts else ()
    a_on_k = mode in ("nn", "nt")
    tm = _tile(m, tm, a_m, *(() if a_on_k else a_piece))
    tn = _tile(n, tn, b_n, *b_piece)
    tk = _tile(k, tk, a_k, b_k, *(a_piece if a_on_k else ()))
    if b_groups and mode in ("nn", "tn"):
        tn = _tile(n, tn, b_n, *b_group)
    if b_groups and mode == "nt":
        tk = _tile(k, tk, a_k, b_k, *b_group, *a_piece)
    if out_groups:
        tn = _tile(n, tn, b_n, n // out_groups, *b_piece, *(b_group if mode != "nt" else ()))
    nk = k // tk

    def piece_specs(parts, tile, walk, block, place):
        per = parts[0].shape[1] // tile
        def spec(g):
            def index(i, j, l):
                at = walk(i, j, l) - g * per
                return place(i, j, l, jnp.clip(at, 0, per - 1), jnp.logical_and(at >= 0, at < per))
            return pl.BlockSpec(block, index)
        return [spec(g) for g in range(len(parts))], per

    if a_parts:
        if a_on_k:
            a_specs, a_per = piece_specs(a_parts, tk, lambda i, j, l: l, (tm, tk), lambda i, j, l, c, on: (i, c))
        else:
            a_specs, a_per = piece_specs(a_parts, tm, lambda i, j, l: i, (tk, tm),
                                         lambda i, j, l, c, on: (jnp.where(on, l, 0), c))
    elif mode in ("nn", "nt"):
        ao = (a_off[0] // tm, a_off[1] // tk)
        a_specs = [pl.BlockSpec((tm, tk), lambda i, j, l: (i + ao[0], l + ao[1]))]
    else:
        ao = (a_off[0] // tk, a_off[1] // tm)
        a_specs = [pl.BlockSpec((tk, tm), lambda i, j, l: (l + ao[0], i + ao[1]))]
    if b_parts:
        b_specs, b_per = piece_specs(b_parts, tn, lambda i, j, l: j, (tk, tn),
                                     lambda i, j, l, c, on: (jnp.where(on, l, 0), c))
        b_spec = None
    elif mode in ("nn", "tn"):
        bo = (b_off[0] // tk, b_off[1] // tn)
        if b_groups:
            per = b.shape[2] // tn
            b_spec = pl.BlockSpec((None, tk, tn), lambda i, j, l: ((j + bo[1]) // per, l + bo[0], (j + bo[1]) % per))
        else:
            b_spec = pl.BlockSpec((tk, tn), lambda i, j, l: (l + bo[0], j + bo[1]))
    else:
        bo = (b_off[0] // tn, b_off[1] // tk)
        if b_groups:
            per = b.shape[2] // tk
            b_spec = pl.BlockSpec((None, tn, tk), lambda i, j, l: ((l + bo[1]) // per, j + bo[0], (l + bo[1]) % per))
        else:
            b_spec = pl.BlockSpec((tn, tk), lambda i, j, l: (j + bo[0], l + bo[1]))
    if out_groups:
        oper = (n // out_groups) // tn
        out_spec = pl.BlockSpec((None, tm, tn), lambda i, j, l: (j // oper, i, j % oper))
        out_shape = jax.ShapeDtypeStruct((out_groups, m, n // out_groups), out_dtype)
    else:
        out_spec = pl.BlockSpec((tm, tn), lambda i, j, l: (i, j))
        out_shape = jax.ShapeDtypeStruct((m, n), out_dtype)
    dn = {"nn": NN, "nt": NT, "tn": TN}[mode]
    if not b_parts:
        b_specs = [b_spec]
    na = len(a_specs)

    def body(*refs):
        a_refs, b_refs, (o_ref, acc_ref) = refs[:na], refs[na:len(refs) - 2], refs[len(refs) - 2:]

        @pl.when(pl.program_id(2) == 0)
        def _():
            acc_ref[...] = jnp.zeros_like(acc_ref)

        def add(a_ref, b_ref):
            acc_ref[...] += lax.dot_general(a_ref[...], b_ref[...], dn, preferred_element_type=F32)

        if a_parts or b_parts:
            if a_parts:
                active = pl.program_id(2 if a_on_k else 0) // a_per
            else:
                active = pl.program_id(1) // b_per
            for g in range(max(na, len(b_refs))):
                pl.when(active == g)(functools.partial(add, a_refs[g if a_parts else 0], b_refs[g if b_parts else 0]))
        else:
            add(a_refs[0], b_refs[0])

        @pl.when(pl.program_id(2) == nk - 1)
        def _():
            o_ref[...] = acc_ref[...].astype(out_dtype)

    (out,), carried = _call(
        body, name=name, grid=(m // tm, n // tn, nk), in_specs=[*a_specs, *b_specs],
        out_specs=[out_spec], out_shape=[out_shape],
        scratch=[pltpu.VMEM((tm, tn), F32)], sem=("parallel", "parallel", "arbitrary"),
        args=(*(a_parts or (a,)), *(b_parts or (b,))), ride=ride)
    return out if ride is None else (out, carried)


ROWS = 128


def _rms_fwd(x, gain, *, name):
    s, d = x.shape

    def body(x_ref, g_ref, h_ref):
        xf = x_ref[...]
        r = lax.rsqrt(jnp.mean(xf * xf, axis=-1, keepdims=True) + RMS_EPS)
        h_ref[...] = ((xf * r) * g_ref[...]).astype(BF16)

    return pl.pallas_call(
        body, name=name, grid=(s // ROWS,),
        in_specs=[pl.BlockSpec((ROWS, d), lambda i: (i, 0)), pl.BlockSpec((1, d), lambda i: (0, 0))],
        out_specs=pl.BlockSpec((ROWS, d), lambda i: (i, 0)),
        out_shape=jax.ShapeDtypeStruct((s, d), BF16),
        compiler_params=_params(("parallel",)),
    )(x, gain)


def _post_fwd(x, y, gain, *, name):
    s, d = x.shape

    def body(x_ref, y_ref, g_ref, o_ref):
        yf = y_ref[...]
        r = lax.rsqrt(jnp.mean(yf * yf, axis=-1, keepdims=True) + RMS_EPS)
        o_ref[...] = x_ref[...] + (yf * r) * g_ref[...]

    return pl.pallas_call(
        body, name=name, grid=(s // ROWS,),
        in_specs=[pl.BlockSpec((ROWS, d), lambda i: (i, 0)), pl.BlockSpec((ROWS, d), lambda i: (i, 0)),
                  pl.BlockSpec((1, d), lambda i: (0, 0))],
        out_specs=pl.BlockSpec((ROWS, d), lambda i: (i, 0)),
        out_shape=jax.ShapeDtypeStruct((s, d), F32),
        compiler_params=_params(("parallel",)),
    )(x, y, gain)


def _loss_head(xo, target, *, name):
    s, d = xo.shape
    inv_d = 1.0 / d

    def body(x_ref, t_ref, loss_ref, dx_ref):
        @pl.when(pl.program_id(0) == 0)
        def _():
            loss_ref[...] = jnp.zeros_like(loss_ref)

        e = x_ref[...] - t_ref[...]
        dx_ref[...] = e * inv_d
        loss_ref[...] += 0.5 * jnp.sum(jnp.mean(e * e, axis=-1, keepdims=True), axis=0, keepdims=True)

    return pl.pallas_call(
        body, name=name, grid=(s // ROWS,),
        in_specs=[pl.BlockSpec((ROWS, d), lambda i: (i, 0)), pl.BlockSpec((ROWS, d), lambda i: (i, 0))],
        out_specs=[pl.BlockSpec((1, 1), lambda i: (0, 0)), pl.BlockSpec((ROWS, d), lambda i: (i, 0))],
        out_shape=[jax.ShapeDtypeStruct((1, 1), F32), jax.ShapeDtypeStruct((s, d), F32)],
        compiler_params=_params(("arbitrary",)),
    )(xo, target)


def _post_bwd(dxo, y, gain, *, name):
    s, d = y.shape

    def body(dx_ref, y_ref, g_ref, dy_ref, dg_ref):
        @pl.when(pl.program_id(0) == 0)
        def _():
            dg_ref[...] = jnp.zeros_like(dg_ref)

        yf = y_ref[...]
        dxo_ = dx_ref[...]
        r = lax.rsqrt(jnp.mean(yf * yf, axis=-1, keepdims=True) + RMS_EPS)
        nrm = yf * r
        dg_ref[...] += jnp.sum(dxo_ * nrm, axis=0, keepdims=True)
        dn = dxo_ * g_ref[...]
        dy_ref[...] = (r * (dn - nrm * jnp.mean(dn * nrm, axis=-1, keepdims=True))).astype(BF16)

    return pl.pallas_call(
        body, name=name, grid=(s // ROWS,),
        in_specs=[pl.BlockSpec((ROWS, d), lambda i: (i, 0)), pl.BlockSpec((ROWS, d), lambda i: (i, 0)),
                  pl.BlockSpec((1, d), lambda i: (0, 0))],
        out_specs=[pl.BlockSpec((ROWS, d), lambda i: (i, 0)), pl.BlockSpec((1, d), lambda i: (0, 0))],
        out_shape=[jax.ShapeDtypeStruct((s, d), BF16), jax.ShapeDtypeStruct((1, d), F32)],
        compiler_params=_params(("arbitrary",)),
    )(dxo, y, gain)


def _pre_bwd(dhs, x, gain, dres, *, name):
    s, d = x.shape
    n_dh = len(dhs)

    def body(*refs):
        dh_refs = refs[:n_dh]
        x_ref, g_ref, dr_ref, dx_ref, dg_ref = refs[n_dh:]

        @pl.when(pl.program_id(0) == 0)
        def _():
            dg_ref[...] = jnp.zeros_like(dg_ref)

        xf = x_ref[...]
        dh_ = dh_refs[0][...]
        for extra in dh_refs[1:]:
            dh_ = dh_ + extra[...]
        r = lax.rsqrt(jnp.mean(xf * xf, axis=-1, keepdims=True) + RMS_EPS)
        nrm = xf * r
        dg_ref[...] += jnp.sum(dh_ * nrm, axis=0, keepdims=True)
        dn = dh_ * g_ref[...]
        dx_ref[...] = dr_ref[...] + r * (dn - nrm * jnp.mean(dn * nrm, axis=-1, keepdims=True))

    return pl.pallas_call(
        body, name=name, grid=(s // ROWS,),
        in_specs=[pl.BlockSpec((ROWS, d), lambda i: (i, 0))] * (n_dh + 1)
        + [pl.BlockSpec((1, d), lambda i: (0, 0)), pl.BlockSpec((ROWS, d), lambda i: (i, 0))],
        out_specs=[pl.BlockSpec((ROWS, d), lambda i: (i, 0)), pl.BlockSpec((1, d), lambda i: (0, 0))],
        out_shape=[jax.ShapeDtypeStruct((s, d), F32), jax.ShapeDtypeStruct((1, d), F32)],
        compiler_params=_params(("arbitrary",)),
    )(*dhs, x, gain, dres)


GB = 256


def _gate_fwd(af, bias, *, name):
    s, w = af.shape

    def body(af_ref, b_ref, c_ref, carry_ref):
        @pl.when(pl.program_id(0) == 0)
        def _():
            carry_ref[...] = jnp.zeros_like(carry_ref)

        z = af_ref[...] + b_ref[...]
        lf = jnp.minimum(z, 0.0) - jnp.log(1.0 + jnp.exp(-jnp.abs(z)))
        r_i = lax.broadcasted_iota(jnp.int32, (GB, GB), 0)
        c_i = lax.broadcasted_iota(jnp.int32, (GB, GB), 1)
        tri = (c_i <= r_i).astype(BF16)
        hi, mid, lo = _split3(lf)
        pre = (jnp.dot(tri, hi, preferred_element_type=F32) + jnp.dot(tri, mid, preferred_element_type=F32)
               + jnp.dot(tri, lo, preferred_element_type=F32))
        c_ref[...] = pre + carry_ref[...]
        carry_ref[...] += jnp.sum(lf, axis=0, keepdims=True)

    return pl.pallas_call(
        body, name=name, grid=(s // GB,),
        in_specs=[pl.BlockSpec((GB, w), lambda i: (i, 0)), pl.BlockSpec((1, w), lambda i: (0, 0))],
        out_specs=pl.BlockSpec((GB, w), lambda i: (i, 0)),
        out_shape=jax.ShapeDtypeStruct((s, w), F32),
        scratch_shapes=[pltpu.VMEM((1, w), F32)],
        compiler_params=_params(("arbitrary",)),
    )(af, bias)


def _gate_bwd(dc, af, bias, *, name):
    s, w = af.shape
    nb = s // GB

    def body(dc_ref, af_ref, b_ref, daf_ref, db_ref, carry_ref):
        @pl.when(pl.program_id(0) == 0)
        def _():
            carry_ref[...] = jnp.zeros_like(carry_ref)
            db_ref[...] = jnp.zeros_like(db_ref)

        dcb = dc_ref[...]
        r_i = lax.broadcasted_iota(jnp.int32, (GB, GB), 0)
        c_i = lax.broadcasted_iota(jnp.int32, (GB, GB), 1)
        tri = (c_i >= r_i).astype(BF16)
        hi, mid, lo = _split3(dcb)
        suf = (jnp.dot(tri, hi, preferred_element_type=F32) + jnp.dot(tri, mid, preferred_element_type=F32)
               + jnp.dot(tri, lo, preferred_element_type=F32)) + carry_ref[...]
        carry_ref[...] += jnp.sum(dcb, axis=0, keepdims=True)
        z = af_ref[...] + b_ref[...]
        daf = suf * _sigmoid(-z)
        daf_ref[...] = daf.astype(BF16)
        db_ref[...] += jnp.sum(daf, axis=0, keepdims=True)

    return pl.pallas_call(
        body, name=name, grid=(nb,),
        in_specs=[pl.BlockSpec((GB, w), lambda i: (nb - 1 - i, 0)), pl.BlockSpec((GB, w), lambda i: (nb - 1 - i, 0)),
                  pl.BlockSpec((1, w), lambda i: (0, 0))],
        out_specs=[pl.BlockSpec((GB, w), lambda i: (nb - 1 - i, 0)), pl.BlockSpec((1, w), lambda i: (0, 0))],
        out_shape=[jax.ShapeDtypeStruct((s, w), BF16), jax.ShapeDtypeStruct((1, w), F32)],
        scratch_shapes=[pltpu.VMEM((1, w), F32)],
        compiler_params=_params(("arbitrary",)),
    )(dc, af, bias)


def _rel_index_rows():
    w = np.arange(TOEP)
    wp = np.where(w < CQ, w, w - TOEP)
    return np.stack([np.clip(LEFT_CHUNKS * CHUNK - CQ * j - wp, -REL_CLIP, REL_CLIP) + REL_CLIP
                     for j in range(BAND_TILES)]).astype(np.int32)


def _skew_rows(xw, sign):
    row = lax.broadcasted_iota(jnp.int32, xw.shape, 0)
    for b in range(CQ.bit_length() - 1):
        amt = (1 << b) if sign > 0 else TOEP - (1 << b)
        xw = jnp.where(((row >> b) & 1) == 1, pltpu.roll(xw, amt, 1), xw)
    return xw


REL_PAD = 384
HP = 2


def _rel_onehot():
    return jnp.asarray(_rel_index_rows()[:, :, None] == np.arange(REL_PAD)[None, None, :], BF16)


def _fill_bias_tiles(rel_ref, oh_ref, bias_scr):
    parts = _split3(jnp.broadcast_to(rel_ref[...], (8, REL_PAD)))
    for j in range(BAND_TILES):
        row = sum(lax.dot_general(p, oh_ref[j], NT, preferred_element_type=F32) for p in parts)[0:1]
        bias_scr[j] = _skew_rows(jnp.broadcast_to(row, (CQ, TOEP)), +1)[:, :CQ]


FOX_Q = 512


def _fox_scores(s, cq, cr, diagonal):
    s = s + (cq - cr)
    if diagonal is None:
        return s
    bq, bk = s.shape
    key = lax.broadcasted_iota(jnp.int32, (bq, bk), 1) + diagonal
    return jnp.where(key <= lax.broadcasted_iota(jnp.int32, (bq, bk), 0), s, NEG)


def _chunk_scores(s, q0, k0, bias):
    bq, bk = s.shape
    qc = (q0 + lax.broadcasted_iota(jnp.int32, (bq, bk), 0)) >> 6
    kc = (k0 + lax.broadcasted_iota(jnp.int32, (bq, bk), 1)) >> 6
    return jnp.where((kc <= qc) & (kc >= qc - LEFT_CHUNKS), s + bias, NEG)


def _softmax_fwd(qkv, gate, aux, *, mode, width, name, ride=None):
    s = qkv.shape[0]
    nh = width // HEAD_DIM
    bk = CQ
    bq = FOX_Q if mode == "fox" else CQ
    nq, nkt, wide_q = s // bq, s // bk, bq // bk

    assert nh % HP == 0
    hcol = lambda hh: slice(hh * HEAD_DIM, (hh + 1) * HEAD_DIM)

    def body(q_ref, k_ref, v_ref, g_ref, *rest):
        if mode == "fox":
            cc_ref, cr_ref, mixed_ref, o_ref, lse_ref = rest
        else:
            rel_ref, oh_ref, mixed_ref, o_ref, lse_ref, bias_scr = rest
        qi = pl.program_id(1)
        q0 = qi * bq
        q = q_ref[...]

        if mode == "chunk":
            @pl.when(qi == 0)
            def _():
                for hh in range(HP):
                    _fill_bias_tiles(rel_ref.at[hh], oh_ref, bias_scr.at[hh])

            lo, hi = jnp.maximum(qi - (BAND_TILES - 1), 0), qi + 1

        def step(ki, carry, diagonal=None):
            k0 = pl.multiple_of(ki * bk, bk)
            kt = k_ref[pl.ds(k0, bk), :]
            vt = v_ref[pl.ds(k0, bk), :]
            out = []
            for hh in range(HP):
                m, l, acc = carry[hh]
                sc = lax.dot_general(q[:, hcol(hh)], kt[:, hcol(hh)], NT, preferred_element_type=F32) * SCALE
                if mode == "fox":
                    sc = _fox_scores(sc, cc_ref[hh], cr_ref[hh, ki], diagonal)
                else:
                    sc = _chunk_scores(sc, q0, k0, bias_scr[hh, ki - qi + (BAND_TILES - 1)])
                m_new = jnp.maximum(m, jnp.max(sc, axis=-1, keepdims=True))
                p = jnp.exp(sc - m_new)
                alpha = jnp.exp(m - m_new)
                l = alpha * l + jnp.sum(p, axis=-1, keepdims=True)
                acc = alpha * acc + jnp.dot(p.astype(BF16), vt[:, hcol(hh)], preferred_element_type=F32)
                out.append((m_new, l, acc))
            return tuple(out)

        init = ((jnp.full((bq, 1), NEG, F32), jnp.zeros((bq, 1), F32), jnp.zeros((bq, HEAD_DIM), F32)),) * HP
        if mode == "fox":
            done = lax.fori_loop(0, wide_q * qi, step, init)
            for u in range(wide_q):
                done = step(wide_q * qi + u, done, diagonal=u * bk)
        else:
            done = lax.fori_loop(lo, hi, step, init)
        o = jnp.concatenate([acc / l for _, l, acc in done], axis=1)
        g = g_ref[...]
        o_ref[...] = o
        mixed_ref[...] = (o * (g * _sigmoid(g))).astype(BF16)
        for hh, (m, l, _) in enumerate(done):
            lse_ref[hh] = m + jnp.log(l)

    wide = HP * HEAD_DIM
    head_col = lambda off: pl.BlockSpec((s, wide), lambda h, i: (0, off // HP + h))
    qblk = pl.BlockSpec((bq, wide), lambda h, i: (i, h))
    stat = pl.BlockSpec((HP, bq, 1), lambda h, i: (h, i, 0))
    in_specs = [qblk, head_col(nh), head_col(2 * nh), qblk]
    scratch = []
    if mode == "fox":
        in_specs += [stat, pl.BlockSpec((HP, nkt, 1, bk), lambda h, i: (h, 0, 0, 0))]
    else:
        in_specs += [pl.BlockSpec((HP, 1, REL_PAD), lambda h, i: (h, 0, 0)),
                     pl.BlockSpec((BAND_TILES, TOEP, REL_PAD), lambda h, i: (0, 0, 0))]
        scratch = [pltpu.VMEM((HP, BAND_TILES, CQ, CQ), F32)]
    outs, carried = _call(
        body, name=name, grid=(nh // HP, nq), in_specs=in_specs, out_specs=[qblk, qblk, stat],
        out_shape=[jax.ShapeDtypeStruct((s, width), BF16), jax.ShapeDtypeStruct((s, width), F32),
                   jax.ShapeDtypeStruct((nh, s, 1), F32)],
        scratch=scratch, sem=("parallel", "arbitrary"), args=(qkv, qkv, qkv, gate, *aux), ride=ride)
    return outs if ride is None else (*outs, carried)


def _softmax_bwd(qkv, gate, o, dmixed, dm_off, lse, aux, *, mode, width, name, ride=None):
    s = qkv.shape[0]
    nh = width // HEAD_DIM
    bk = CQ
    bq = FOX_Q if mode == "fox" else CQ
    nq, nkt, wide_q = s // bq, s // bk, bq // bk
    dmo = dm_off // HEAD_DIM
    rel_pad = REL_PAD
    assert nh % HP == 0 and dmo % HP == 0
    hcol = lambda hh: slice(hh * HEAD_DIM, (hh + 1) * HEAD_DIM)

    def body(q_ref, k_ref, v_ref, g_ref, o_ref, dm_ref, lse_ref, *rest):
        if mode == "fox":
            cc_ref, cr_ref, dq_ref, dk_ref, dv_ref, dg_ref, dc_ref, dcq_ref, dk_scr, dv_scr, dc_scr = rest
        else:
            rel_ref, oh_ref, dq_ref, dk_ref, dv_ref, dg_ref, drel_ref, dk_scr, dv_scr, bias_scr, db_scr = rest
        qi = pl.program_id(1)
        q0 = qi * bq

        @pl.when(qi == 0)
        def _():
            dk_scr[...] = jnp.zeros_like(dk_scr)
            dv_scr[...] = jnp.zeros_like(dv_scr)
            if mode == "fox":
                dc_scr[...] = jnp.zeros_like(dc_scr)
            else:
                db_scr[...] = jnp.zeros_like(db_scr)
                for hh in range(HP):
                    _fill_bias_tiles(rel_ref.at[hh], oh_ref, bias_scr.at[hh])

        g = g_ref[...]
        of = o_ref[...]
        dm = dm_ref[...]
        sig = _sigmoid(g)
        do = dm * (g * sig)
        dg_ref[...] = (dm * of * (sig * (1.0 + g * (1.0 - sig)))).astype(BF16)
        do_o = do * of
        delta = [jnp.sum(do_o[:, hcol(hh)], axis=-1, keepdims=True) for hh in range(HP)]
        do_b = do.astype(BF16)
        q = q_ref[...]
        if mode == "chunk":
            lo, hi = jnp.maximum(qi - (BAND_TILES - 1), 0), qi + 1

        def step(ki, carry, diagonal=None):
            k0 = pl.multiple_of(ki * bk, bk)
            kt = k_ref[pl.ds(k0, bk), :]
            vt = v_ref[pl.ds(k0, bk), :]
            out = []
            for hh in range(HP):
                dq, rsum = carry[hh]
                qh, kh, doh = q[:, hcol(hh)], kt[:, hcol(hh)], do_b[:, hcol(hh)]
                sc = lax.dot_general(qh, kh, NT, preferred_element_type=F32) * SCALE
                if mode == "fox":
                    sc = _fox_scores(sc, cc_ref[hh], cr_ref[hh, ki], diagonal)
                else:
                    sc = _chunk_scores(sc, q0, k0, bias_scr[hh, ki - qi + (BAND_TILES - 1)])
                p = jnp.exp(sc - lse_ref[hh])
                dp = lax.dot_general(doh, vt[:, hcol(hh)], NT, preferred_element_type=F32)
                ds = p * (dp - delta[hh])
                if mode == "fox":
                    dc_scr[hh, ki] += -jnp.sum(ds, axis=0, keepdims=True)
                    rsum = rsum + jnp.sum(ds, axis=-1, keepdims=True)
                else:
                    db_scr[hh, ki - qi + (BAND_TILES - 1)] += ds
                ds_b = (ds * SCALE).astype(BF16)
                dk_scr[pl.ds(k0, bk), hcol(hh)] += lax.dot_general(ds_b, qh, TN, preferred_element_type=F32)
                dv_scr[pl.ds(k0, bk), hcol(hh)] += lax.dot_general(p.astype(BF16), doh, TN, preferred_element_type=F32)
                out.append((dq + jnp.dot(ds_b, kh, preferred_element_type=F32), rsum))
            return tuple(out)

        init = ((jnp.zeros((bq, HEAD_DIM), F32), jnp.zeros((bq, 1), F32)),) * HP
        if mode == "fox":
            done = lax.fori_loop(0, wide_q * qi, step, init)
            for u in range(wide_q):
                done = step(wide_q * qi + u, done, diagonal=u * bk)
        else:
            done = lax.fori_loop(lo, hi, step, init)
        dq_ref[...] = jnp.concatenate([dq for dq, _ in done], axis=1).astype(BF16)
        if mode == "fox":
            for hh, (_, rsum) in enumerate(done):
                dcq_ref[hh] = rsum

        @pl.when(qi == nq - 1)
        def _():
            dk_ref[...] = dk_scr[...].astype(BF16)
            dv_ref[...] = dv_scr[...].astype(BF16)
            if mode == "fox":
                dc_ref[...] = dc_scr[...]
            else:
                for hh in range(HP):
                    tot = jnp.zeros((8, rel_pad), F32)
                    for j in range(BAND_TILES):
                        wide_ = jnp.concatenate([db_scr[hh, j], jnp.zeros((CQ, TOEP - CQ), F32)], axis=1)
                        diag = jnp.sum(_skew_rows(wide_, -1), axis=0, keepdims=True)
                        for part in _split3(jnp.broadcast_to(diag, (8, TOEP))):
                            tot = tot + jnp.dot(part, oh_ref[j], preferred_element_type=F32)
                    drel_ref[hh] = tot[0:1, :]

    wide = HP * HEAD_DIM
    head_col = lambda off: pl.BlockSpec((s, wide), lambda h, i: (0, off // HP + h))
    qblk = lambda off: pl.BlockSpec((bq, wide), lambda h, i: (i, off // HP + h))
    stat = pl.BlockSpec((HP, bq, 1), lambda h, i: (h, i, 0))
    in_specs = [qblk(0), head_col(nh), head_col(2 * nh), qblk(0), qblk(0), qblk(dmo), stat]
    out_specs = [qblk(0), head_col(0), head_col(0), qblk(0)]
    out_shape = [jax.ShapeDtypeStruct((s, width), BF16)] * 4
    scratch = [pltpu.VMEM((s, wide), F32), pltpu.VMEM((s, wide), F32)]
    if mode == "fox":
        rows = pl.BlockSpec((HP, nkt, 1, bk), lambda h, i: (h, 0, 0, 0))
        in_specs += [stat, rows]
        out_specs += [rows, stat]
        out_shape += [jax.ShapeDtypeStruct((nh, nkt, 1, bk), F32), jax.ShapeDtypeStruct((nh, s, 1), F32)]
        scratch += [pltpu.VMEM((HP, nkt, 1, bk), F32)]
    else:
        rel = pl.BlockSpec((HP, 1, rel_pad), lambda h, i: (h, 0, 0))
        in_specs += [rel, pl.BlockSpec((BAND_TILES, TOEP, rel_pad), lambda h, i: (0, 0, 0))]
        out_specs += [rel]
        out_shape += [jax.ShapeDtypeStruct((nh, 1, rel_pad), F32)]
        scratch += [pltpu.VMEM((HP, BAND_TILES, CQ, CQ), F32), pltpu.VMEM((HP, BAND_TILES, CQ, CQ), F32)]
    outs, carried = _call(
        body, name=name, grid=(nh // HP, nq), in_specs=in_specs, out_specs=out_specs, out_shape=out_shape,
        scratch=scratch, sem=("parallel", "arbitrary"), args=(qkv, qkv, qkv, gate, o, dmixed, lse, *aux), ride=ride)
    return outs if ride is None else (*outs, carried)


SBK = 256
SBQ = 2 * SBK
SB_TILES_PER_TRIP = 1
SB_DEAD = -110.0


def _suffix_excl(x, tri):
    r = x.shape[0]
    both = jnp.dot(jnp.concatenate(_split2(x), axis=0), tri, preferred_element_type=F32)
    return both[:r] + both[r:]


def _sb_logits(qh, kt, diag):
    z = lax.dot_general(qh, kt, NT, preferred_element_type=F32) * SCALE
    lom = jnp.minimum(-z, 0.0) - jnp.log(1.0 + jnp.exp(-jnp.abs(z)))
    if diag is not None:
        lom = jnp.where(diag, lom, 0.0)
    return z, lom


def _sb_fwd_tile(qh, kt, vt, tri, diag, run, acc):
    z, lom = _sb_logits(qh, kt, diag)
    a = jnp.exp(lom + z + (_suffix_excl(lom, tri) + run))
    if diag is not None:
        a = jnp.where(diag, a, 0.0)
    acc = acc + jnp.dot(a.astype(BF16), vt, preferred_element_type=F32)
    return run + jnp.sum(lom, axis=-1, keepdims=True), acc


def _sb_alive(run_a, run_b):
    return (jnp.max(jnp.maximum(run_a, run_b)) > SB_DEAD).astype(jnp.int32)


def _sb_fwd(qkv, gate, *, width, name):
    s = qkv.shape[0]
    nh = width // HEAD_DIM
    nq = s // SBQ

    def body(q_ref, k_ref, v_ref, g_ref, mixed_ref, o_ref):
        qi = pl.program_id(1)
        r_i = lax.broadcasted_iota(jnp.int32, (SBK, SBK), 0)
        c_i = lax.broadcasted_iota(jnp.int32, (SBK, SBK), 1)
        tri = (r_i > c_i).astype(BF16)
        diag = c_i < r_i
        q_a = q_ref[0:SBK, :]
        q_b = q_ref[SBK:SBQ, :]

        def kv(tile):
            k0 = pl.multiple_of(tile * SBK, SBK)
            return k_ref[pl.ds(k0, SBK), :], v_ref[pl.ds(k0, SBK), :]

        zero = (jnp.zeros((SBK, 1), F32), jnp.zeros((SBK, HEAD_DIM), F32))
        kt, vt = kv(2 * qi + 1)
        run_b, acc_b = _sb_fwd_tile(q_b, kt, vt, tri, diag, *zero)
        kt, vt = kv(2 * qi)
        run_b, acc_b = _sb_fwd_tile(q_b, kt, vt, tri, None, run_b, acc_b)
        run_a, acc_a = _sb_fwd_tile(q_a, kt, vt, tri, diag, *zero)

        trips = (2 // SB_TILES_PER_TRIP) * qi

        def step(carry):
            t, _, run_a, acc_a, run_b, acc_b = carry
            for u in range(SB_TILES_PER_TRIP):
                kt, vt = kv(2 * qi - 1 - (SB_TILES_PER_TRIP * t + u))
                run_a, acc_a = _sb_fwd_tile(q_a, kt, vt, tri, None, run_a, acc_a)
                run_b, acc_b = _sb_fwd_tile(q_b, kt, vt, tri, None, run_b, acc_b)
            return t + 1, _sb_alive(run_a, run_b), run_a, acc_a, run_b, acc_b

        _, _, _, acc_a, _, acc_b = lax.while_loop(
            lambda c: jnp.logical_and(c[0] < trips, c[1] > 0), step,
            (jnp.int32(0), _sb_alive(run_a, run_b), run_a, acc_a, run_b, acc_b))
        o = jnp.concatenate([acc_a, acc_b], axis=0)
        g = g_ref[...]
        o_ref[...] = o
        mixed_ref[...] = (o * (g * _sigmoid(g))).astype(BF16)

    head_col = lambda off: pl.BlockSpec((s, HEAD_DIM), lambda h, i: (0, off + h))
    qblk = pl.BlockSpec((SBQ, HEAD_DIM), lambda h, i: (i, h))
    return pl.pallas_call(
        body, name=name, grid=(nh, nq), in_specs=[qblk, head_col(nh), head_col(2 * nh), qblk],
        out_specs=[qblk, qblk],
        out_shape=[jax.ShapeDtypeStruct((s, width), BF16), jax.ShapeDtypeStruct((s, width), F32)],
        compiler_params=_params(("parallel", "arbitrary")),
    )(qkv, qkv, qkv, gate)


def _sb_bwd(qkv, gate, o, dmixed, *, width, name):
    s = qkv.shape[0]
    nh = width // HEAD_DIM
    nq = s // SBQ

    def body(q_ref, k_ref, v_ref, g_ref, o_ref, dm_ref, dq_ref, dk_ref, dv_ref, dg_ref, dk_scr, dv_scr):
        qi = pl.program_id(1)

        @pl.when(qi == 0)
        def _():
            dk_scr[...] = jnp.zeros_like(dk_scr)
            dv_scr[...] = jnp.zeros_like(dv_scr)

        g = g_ref[...]
        of = o_ref[...]
        dm = dm_ref[...]
        sig = _sigmoid(g)
        do = dm * (g * sig)
        dg_ref[...] = (dm * of * (sig * (1.0 + g * (1.0 - sig)))).astype(BF16)
        do_b = do.astype(BF16)
        q = q_ref[...]
        r_i = lax.broadcasted_iota(jnp.int32, (SBK, SBK), 0)
        c_i = lax.broadcasted_iota(jnp.int32, (SBK, SBK), 1)
        tri = (r_i > c_i).astype(BF16)
        tri_pre = (r_i < c_i).astype(BF16)
        diag = c_i < r_i
        q_a, q_b = q[0:SBK], q[SBK:SBQ]
        do_a, do_b2 = do_b[0:SBK], do_b[SBK:SBQ]

        def kv(t):
            k0 = pl.multiple_of(t * SBK, SBK)
            return k0, k_ref[pl.ds(k0, SBK), :], v_ref[pl.ds(k0, SBK), :]

        def mass(qh, tile_, dg_):
            return jnp.sum(_sb_logits(qh, kv(tile_)[1], dg_)[1], axis=-1, keepdims=True)

        run_b = mass(q_b, 2 * qi + 1, diag) + mass(q_b, 2 * qi, None)
        run_a = mass(q_a, 2 * qi, diag)
        trips = (2 // SB_TILES_PER_TRIP) * qi

        def scout(carry):
            t, _, run_a, run_b = carry
            for u in range(SB_TILES_PER_TRIP):
                tile_ = 2 * qi - 1 - (SB_TILES_PER_TRIP * t + u)
                run_a = run_a + mass(q_a, tile_, None)
                run_b = run_b + mass(q_b, tile_, None)
            return t + 1, _sb_alive(run_a, run_b), run_a, run_b

        walked, _, run_a, run_b = lax.while_loop(
            lambda c: jnp.logical_and(c[0] < trips, c[1] > 0), scout,
            (jnp.int32(0), _sb_alive(run_a, run_b), run_a, run_b))

        def tile(qh, doh, kt, vt, dg_, carry):
            rem, gpre, dq = carry
            z, lom = _sb_logits(qh, kt, dg_)
            rem = rem - jnp.sum(lom, axis=-1, keepdims=True)
            a = jnp.exp(lom + z + (_suffix_excl(lom, tri) + rem))
            if dg_ is not None:
                a = jnp.where(dg_, a, 0.0)
            gg = lax.dot_general(doh, vt, NT, preferred_element_type=F32) * a
            pre = _suffix_excl(gg, tri_pre) + gpre
            dz = gg * jnp.exp(lom) - pre * jnp.exp(lom + z)
            if dg_ is not None:
                dz = jnp.where(dg_, dz, 0.0)
            dz_b = (dz * SCALE).astype(BF16)
            dq = dq + jnp.dot(dz_b, kt, preferred_element_type=F32)
            return (rem, gpre + jnp.sum(gg, axis=-1, keepdims=True), dq), dz_b, a.astype(BF16)

        def both(t, ca, cb, dg_a):
            k0, kt, vt = kv(t)
            ca, dz_a, a_a = tile(q_a, do_a, kt, vt, dg_a, ca)
            cb, dz_b_, a_b = tile(q_b, do_b2, kt, vt, None, cb)
            dk_scr[pl.ds(k0, SBK), :] += lax.dot_general(jnp.concatenate([dz_a, dz_b_], axis=0), q, TN,
                                                         preferred_element_type=F32)
            dv_scr[pl.ds(k0, SBK), :] += lax.dot_general(jnp.concatenate([a_a, a_b], axis=0), do_b, TN,
                                                         preferred_element_type=F32)
            return ca, cb

        def step(t, carry):
            ca, cb = carry
            for u in range(SB_TILES_PER_TRIP):
                ca, cb = both(SB_TILES_PER_TRIP * t + u, ca, cb, None)
            return ca, cb

        zero = (jnp.zeros((SBK, 1), F32), jnp.zeros((SBK, HEAD_DIM), F32))
        ca, cb = lax.fori_loop(trips - walked, trips, step, ((run_a, *zero), (run_b, *zero)))
        ca, cb = both(2 * qi, ca, cb, diag)
        k0, kt, vt = kv(2 * qi + 1)
        cb, dz_b_, a_b = tile(q_b, do_b2, kt, vt, diag, cb)
        dk_scr[pl.ds(k0, SBK), :] += lax.dot_general(dz_b_, q_b, TN, preferred_element_type=F32)
        dv_scr[pl.ds(k0, SBK), :] += lax.dot_general(a_b, do_b2, TN, preferred_element_type=F32)
        dq_ref[...] = jnp.concatenate([ca[2], cb[2]], axis=0).astype(BF16)

        @pl.when(qi == nq - 1)
        def _():
            dk_ref[...] = dk_scr[...].astype(BF16)
            dv_ref[...] = dv_scr[...].astype(BF16)

    head_col = lambda off: pl.BlockSpec((s, HEAD_DIM), lambda h, i: (0, off + h))
    qblk = pl.BlockSpec((SBQ, HEAD_DIM), lambda h, i: (i, h))
    return pl.pallas_call(
        body, name=name, grid=(nh, nq),
        in_specs=[qblk, head_col(nh), head_col(2 * nh), qblk, qblk, qblk],
        out_specs=[qblk, head_col(0), head_col(0), qblk],
        out_shape=[jax.ShapeDtypeStruct((s, width), BF16)] * 4,
        scratch_shapes=[pltpu.VMEM((s, HEAD_DIM), F32), pltpu.VMEM((s, HEAD_DIM), F32)],
        compiler_params=_params(("parallel", "arbitrary")),
    )(qkv, qkv, qkv, gate, o, dmixed)


HBM = pl.BlockSpec(memory_space=pl.ANY)
MESH = pl.DeviceIdType.MESH


class _GatherRide:
    def __init__(self, shard):
        self.src = shard
        self.out_shape = jax.ShapeDtypeStruct((N_DEV, *shard.shape), shard.dtype)
        self.scratch = [pltpu.SemaphoreType.DMA((7,)), pltpu.SemaphoreType.DMA((7,)), pltpu.SemaphoreType.DMA]

    def _copies(self, x_ref, out_ref, send_sems, recv_sems, local_sem):
        x, y, c = lax.axis_index("x"), lax.axis_index("y"), lax.axis_index("c")
        me, sibling = (x, y, c), (x, y, 1 - c)
        chips = [(1 - x, y), (x, 1 - y), (1 - x, 1 - y)]

        def slot(px, py, pc):
            return out_ref.at[4 * px + 2 * py + pc]

        def copy(k, block, to, src=None):
            return pltpu.make_async_remote_copy(
                src_ref=slot(*block) if src is None else src, dst_ref=slot(*block),
                send_sem=send_sems.at[k], recv_sem=recv_sems.at[k], device_id=to, device_id_type=MESH)

        mine = pltpu.make_async_copy(x_ref, slot(*me), local_sem)
        first = [copy(0, me, sibling, src=x_ref)]
        first += [copy(1 + j, me, (*chip, c), src=x_ref) for j, chip in enumerate(chips)]
        landed = [copy(1 + j, (*chip, c), me) for j, chip in enumerate(chips)]
        passed = [copy(4 + j, (*chip, c), sibling) for j, chip in enumerate(chips)]
        from_sibling = [copy(0, sibling, me)] + [copy(4 + j, (*chip, 1 - c), me) for j, chip in enumerate(chips)]
        return mine, first, landed, passed, from_sibling

    def start(self, *refs):
        mine, first, _, _, _ = self._copies(*refs)
        mine.start()
        for cp in first:
            cp.start()

    def relay(self, *refs):
        _, _, landed, passed, _ = self._copies(*refs)
        for got, onward in zip(landed, passed):
            got.wait_recv()
            onward.start()

    def finish(self, *refs):
        mine, first, _, passed, from_sibling = self._copies(*refs)
        for cp in from_sibling:
            cp.wait_recv()
        for cp in first + passed:
            cp.wait_send()
        mine.wait()


def _all_gather(shard, *, name):
    ride = _GatherRide(shard)

    def body(*refs):
        ride.start(*refs)
        ride.relay(*refs)
        ride.finish(*refs)

    return pl.pallas_call(body, name=name, in_specs=[HBM], out_specs=HBM, out_shape=ride.out_shape,
                          scratch_shapes=ride.scratch)(shard)


class _Ride:
    def __init__(self, src, *, gather, chips=False):
        self.src, self.gather, self.chips = src, gather, chips
        n = N_DEV // 2 if chips else N_DEV
        self.out_shape = jax.ShapeDtypeStruct((n, *(src.shape if gather else src.shape[1:])), src.dtype)
        self.scratch = [pltpu.SemaphoreType.DMA((7,)), pltpu.SemaphoreType.DMA((7,)), pltpu.SemaphoreType.DMA]

    def _copies(self, src_ref, out_ref, send_sems, recv_sems, local_sem):
        x, y, c = lax.axis_index("x"), lax.axis_index("y"), lax.axis_index("c")
        slot = (lambda px, py, pc: 2 * px + py) if self.chips else (lambda px, py, pc: 4 * px + 2 * py + pc)
        me = slot(x, y, c)
        pick = (lambda j: src_ref) if self.gather else (lambda j: src_ref.at[j])
        mine = pltpu.make_async_copy(pick(me), out_ref.at[me], local_sem)
        copies = []
        for k in range(2 if self.chips else 1, N_DEV, 2 if self.chips else 1):
            px, py, pc = x ^ ((k >> 2) & 1), y ^ ((k >> 1) & 1), c ^ (k & 1)
            copies.append(pltpu.make_async_remote_copy(
                src_ref=pick(slot(px, py, pc)), dst_ref=out_ref.at[me],
                send_sem=send_sems.at[k - 1], recv_sem=recv_sems.at[k - 1],
                device_id=(px, py, pc), device_id_type=MESH))
        return mine, copies

    def start(self, *refs):
        mine, copies = self._copies(*refs)
        mine.start()
        for cp in copies:
            cp.start()

    def finish(self, *refs):
        mine, copies = self._copies(*refs)
        for cp in copies:
            cp.wait_recv()
        for cp in copies:
            cp.wait_send()
        mine.wait()


def _exchange(src, *, gather, name):
    ride = _Ride(src, gather=gather)

    def body(*refs):
        ride.start(*refs)
        ride.finish(*refs)

    return pl.pallas_call(body, name=name, in_specs=[HBM], out_specs=HBM, out_shape=ride.out_shape,
                          scratch_shapes=ride.scratch)(src)


class _SwapRide:
    def __init__(self, parts):
        _, r, c_ = parts.shape
        self.src = parts.reshape(N_DEV // 2, 2, r, c_)
        self.out_shape = jax.ShapeDtypeStruct((N_DEV // 2, r, c_), parts.dtype)
        self.scratch = [pltpu.SemaphoreType.DMA, pltpu.SemaphoreType.DMA, pltpu.SemaphoreType.DMA]

    def _copy(self, src_ref, out_ref, send_sem, recv_sem, unused_sem):
        x, y, c = lax.axis_index("x"), lax.axis_index("y"), lax.axis_index("c")
        return pltpu.make_async_remote_copy(
            src_ref=src_ref.at[:, 1 - c], dst_ref=out_ref, send_sem=send_sem, recv_sem=recv_sem,
            device_id=(x, y, 1 - c), device_id_type=MESH)

    def start(self, *refs):
        self._copy(*refs).start()

    def finish(self, *refs):
        self._copy(*refs).wait()


def _presum_on_chip(parts, *, name, rows=256, got=None):
    _, r, c_ = parts.shape
    rows = min(rows, r)
    assert r % rows == 0
    by_core = parts.reshape(N_DEV // 2, 2, r, c_)
    if got is None:
        swap_ride = _SwapRide(parts)

        def swap(*refs):
            swap_ride.start(*refs)
            swap_ride.finish(*refs)

        got = pl.pallas_call(swap, name=name + "_swap", in_specs=[HBM], out_specs=HBM, out_shape=swap_ride.out_shape,
                             scratch_shapes=swap_ride.scratch)(by_core)

    def add(core_ref, a_ref, b_ref, o_ref):
        del core_ref
        o_ref[...] = (a_ref[...].astype(F32) + b_ref[...].astype(F32)).astype(o_ref.dtype)

    blk = pl.BlockSpec((1, rows, c_), lambda j, i, core: (j, i, 0))
    mine = pl.BlockSpec((1, None, rows, c_), lambda j, i, core: (j, core[0], i, 0))
    core = jnp.reshape(lax.axis_index("c"), (1,)).astype(jnp.int32)
    return pl.pallas_call(
        add, name=name + "_sum",
        grid_spec=pltpu.PrefetchScalarGridSpec(num_scalar_prefetch=1, grid=(N_DEV // 2, r // rows),
                                               in_specs=[mine, blk], out_specs=blk),
        out_shape=jax.ShapeDtypeStruct((N_DEV // 2, r, c_), parts.dtype),
        compiler_params=_params(("parallel", "parallel")))(core, by_core, got)


def _call(body, *, name, grid, in_specs, out_specs, out_shape, scratch, sem, args, ride=None):
    if ride is None:
        outs = pl.pallas_call(body, name=name, grid=grid, in_specs=in_specs, out_specs=out_specs, out_shape=out_shape,
                              scratch_shapes=scratch, compiler_params=_params(sem))(*args)
        return list(outs), None
    n_in, n_out = len(in_specs), len(out_specs)

    def carrying(*refs):
        ins, src_ref = refs[:n_in], refs[n_in]
        outs, dst_ref = refs[n_in + 1:n_in + 1 + n_out], refs[n_in + 1 + n_out]
        rest = refs[n_in + 2 + n_out:]
        own, sems = rest[:len(rest) - 3], rest[len(rest) - 3:]
        ids = [pl.program_id(a) for a in range(len(grid))]
        first = functools.reduce(jnp.logical_and, [i == 0 for i in ids])
        last = functools.reduce(jnp.logical_and, [i == n - 1 for i, n in zip(ids, grid)])

        @pl.when(first)
        def _():
            ride.start(src_ref, dst_ref, *sems)

        if hasattr(ride, "relay"):
            at = functools.reduce(jnp.logical_and, [ids[0] == (3 * grid[0]) // 4] + [i == 0 for i in ids[1:]])

            @pl.when(at)
            def _():
                ride.relay(src_ref, dst_ref, *sems)

        body(*ins, *outs, *own)

        @pl.when(last)
        def _():
            ride.finish(src_ref, dst_ref, *sems)

    outs = pl.pallas_call(
        carrying, name=name, grid=grid, in_specs=[*in_specs, HBM], out_specs=[*out_specs, HBM],
        out_shape=[*out_shape, ride.out_shape], scratch_shapes=[*scratch, *ride.scratch],
        compiler_params=_params(("arbitrary",) * len(grid)))(*args, ride.src)
    return list(outs[:-1]), outs[-1]


def _adamw(parts, w, m, v, *, name, rows):
    r, c_ = w.shape
    rows = min(rows, r)
    assert r % rows == 0
    c1 = 1.0 / (1.0 - ADAM_B1 ** ADAM_STEP)
    c2 = 1.0 / (1.0 - ADAM_B2 ** ADAM_STEP)

    def body(p_ref, w_ref, m_ref, v_ref, g_ref, d_ref, nm_ref, nv_ref):
        g = p_ref[0].astype(F32)
        for i in range(1, parts.shape[0]):
            g = g + p_ref[i].astype(F32)
        nm = ADAM_B1 * m_ref[...] + (1.0 - ADAM_B1) * g
        nv = ADAM_B2 * v_ref[...] + (1.0 - ADAM_B2) * (g * g)
        g_ref[...] = g
        nm_ref[...] = nm
        nv_ref[...] = nv
        d_ref[...] = -ADAM_LR * ((nm * c1) / (jnp.sqrt(nv * c2) + ADAM_EPS) + ADAM_WD * w_ref[...])

    blk = pl.BlockSpec((rows, c_), lambda i: (i, 0))
    return pl.pallas_call(
        body, name=name, grid=(r // rows,),
        in_specs=[pl.BlockSpec((parts.shape[0], rows, c_), lambda i: (0, i, 0)), blk, blk, blk],
        out_specs=[blk] * 4, out_shape=[jax.ShapeDtypeStruct((r, c_), F32)] * 4,
        compiler_params=_params(("parallel",)),
    )(parts, w, m, v)


def _pad_cols(a, n):
    return jnp.pad(a, ((0, 0), (0, n - a.shape[1])))


def _fox_dc(shares, nh, s):
    key_side, query_side = shares
    return _pad_cols(jnp.transpose(key_side.reshape(nh, s) + query_side.reshape(nh, s)), 128)


def _local_step(x, target, norm_pre, norm_post, w_in_e, w_f, b_f, rel_bias, sh_out_e, sh_in_o, sh_out_o):
    s, d = x.shape
    wa = d // 2
    nha = wa // HEAD_DIM
    nq = s // CQ
    gather = lambda shard: _Ride(shard, gather=True)
    scatter = lambda parts: _Ride(parts, gather=False)

    h0 = _rms_fwd(x, norm_pre[0:1], name="rms_pre0")
    proj = lambda w, off, n, dt, nm, **kw: _matmul(h0, w, mode="nn", m=s, n=n, k=d, out_dtype=dt, name=nm,
                                                   b_off=(0, off), **kw)
    qkv_a, w_out_e = proj(w_in_e, 0, 3 * wa, BF16, "proj_qkv_a", ride=gather(sh_out_e))
    w_out_e = w_out_e.reshape(d, d)
    g_a = proj(w_in_e, 3 * wa, wa, F32, "proj_gate_a")
    qkv_b = proj(w_in_e, 4 * wa, 3 * wa, BF16, "proj_qkv_b")
    g_b = proj(w_in_e, 7 * wa, wa, F32, "proj_gate_b")
    af = _matmul(h0, w_f, mode="nn", m=s, n=128, k=d, out_dtype=F32, name="proj_forget")
    bias128 = _pad_cols(b_f, 128)
    cum = _gate_fwd(af, bias128, name="forget_cumsum")
    c_t = jnp.transpose(cum[:, :nha])
    c_col = c_t.reshape(nha, s, 1)
    c_row = c_t.reshape(nha, nq, 1, CQ)
    mixed_a, o_a, lse_a, w_in_o = _softmax_fwd(qkv_a, g_a, (c_col, c_row), mode="fox", width=wa, name="fox_fwd",
                                               ride=_GatherRide(sh_in_o))
    rel_aux = (_pad_cols(rel_bias, REL_PAD).reshape(nha, 1, REL_PAD), _rel_onehot())
    mixed_b, o_b, lse_b, w_out_o = _softmax_fwd(qkv_b, g_b, rel_aux, mode="chunk", width=wa, name="chunk_fwd",
                                                ride=gather(sh_out_o))
    w_out_o = w_out_o.reshape(d, d)
    mixed0 = (mixed_a, mixed_b)
    y0 = _matmul(mixed0, w_out_e, mode="nn", m=s, n=d, k=d, out_dtype=F32, name="out_proj0")
    x1 = _post_fwd(x, y0, norm_post[0:1], name="post0")

    h1 = _rms_fwd(x1, norm_pre[1:2], name="rms_pre1")
    qkv_c = _matmul(h1, w_in_o, mode="nn", m=s, n=3 * d, k=d, out_dtype=BF16, name="proj_qkv_c", b_groups=N_DEV)
    g_c = _matmul(h1, w_in_o, mode="nn", m=s, n=d, k=d, out_dtype=F32, name="proj_gate_c", b_off=(0, 3 * d),
                  b_groups=N_DEV)
    mixed1, o_c = _sb_fwd(qkv_c, g_c, width=d, name="sb_fwd")
    y1 = _matmul(mixed1, w_out_o, mode="nn", m=s, n=d, k=d, out_dtype=F32, name="out_proj1")
    x2 = _post_fwd(x1, y1, norm_post[1:2], name="post1")

    loss, dx2 = _loss_head(x2, target, name="loss_head")

    dy1, dgpost1 = _post_bwd(dx2, y1, norm_post[1:2], name="post_bwd1")
    dmixed1 = _matmul(dy1, w_out_o, mode="nt", m=s, n=d, k=d, out_dtype=F32, name="dmixed1")
    dw_out_o = _matmul(mixed1, dy1, mode="tn", m=d, n=d, k=s, out_dtype=BF16, name="dw_out1")
    dq_c, dk_c, dv_c, dg_c = _sb_bwd(qkv_c, g_c, o_c, dmixed1, width=d, name="sb_bwd")
    dproj1 = (dq_c, dk_c, dv_c, dg_c)
    dh1 = _matmul(dproj1, w_in_o, mode="nt", m=s, n=d, k=4 * d, out_dtype=F32, name="dh1", b_groups=N_DEV, tn=1024)
    parts_in_o = _matmul(h1, dproj1, mode="tn", m=d, n=4 * d, k=s, out_dtype=BF16, name="dw_in1", out_groups=N_DEV,
                         tn=1024)
    dx1, dgpre1 = _pre_bwd((dh1,), x1, norm_pre[1:2], dx2, name="pre_bwd1")

    dy0, dgpost0 = _post_bwd(dx1, y0, norm_post[0:1], name="post_bwd0")
    dmixed0, swapped_in_o = _matmul(dy0, w_out_e, mode="nt", m=s, n=d, k=d, out_dtype=F32, name="dmixed0",
                                    ride=_SwapRide(parts_in_o))
    dw_out_e = _matmul(mixed0, dy0, mode="tn", m=d, n=d, k=s, out_dtype=BF16, name="dw_out0")
    chip_in_o = _presum_on_chip(parts_in_o, name="rs_w_in_odd", got=swapped_in_o)
    dq_a, dk_a, dv_a, dg_a, *dc_shares, got_in_o = _softmax_bwd(
        qkv_a, g_a, o_a, dmixed0, 0, lse_a, (c_col, c_row), mode="fox", width=wa, name="fox_bwd",
        ride=_Ride(chip_in_o, gather=False, chips=True))
    dq_b, dk_b, dv_b, dg_b, drel, got_out_o = _softmax_bwd(
        qkv_b, g_b, o_b, dmixed0, wa, lse_b, rel_aux, mode="chunk", width=wa, name="chunk_bwd",
        ride=scatter(dw_out_o.reshape(N_DEV, d // N_DEV, d)))
    dc = _fox_dc(dc_shares, nha, s)
    daf, dbf = _gate_bwd(dc, af, bias128, name="forget_bwd")
    dproj0 = jnp.concatenate([dq_a, dk_a, dv_a, dg_a, dq_b, dk_b, dv_b, dg_b], axis=1)
    dw_in_e, got_out_e = _matmul(h0, dproj0, mode="tn", m=d, n=8 * wa, k=s, out_dtype=BF16, name="dw_in0",
                                 ride=scatter(dw_out_e.reshape(N_DEV, d // N_DEV, d)))
    dw_f = _matmul(h0, daf, mode="tn", m=d, n=128, k=s, out_dtype=BF16, name="dw_forget")
    dw_e = jnp.concatenate([dw_in_e, dw_f[:, :nha]], axis=1)
    parts_in_e = jnp.transpose(dw_e.reshape(d, N_DEV, dw_e.shape[1] // N_DEV), (1, 0, 2))
    chip_in_e = _presum_on_chip(parts_in_e, name="rs_w_in_even")
    dh0, got_in_e = _matmul(dproj0, w_in_e, mode="nt", m=s, n=d, k=8 * wa, out_dtype=F32, name="dh0_main",
                            ride=_Ride(chip_in_e, gather=False, chips=True))
    dh0f = _matmul(daf, w_f, mode="nt", m=s, n=d, k=128, out_dtype=F32, name="dh0_forget")
    dx0, dgpre0 = _pre_bwd((dh0, dh0f), x, norm_pre[0:1], dx1, name="pre_bwd0")

    small = dict(
        norm_pre=jnp.concatenate([dgpre0, dgpre1], axis=0),
        norm_post=jnp.concatenate([dgpost0, dgpost1], axis=0),
        b_f=dbf[:, :nha], rel_bias=drel[:, 0, :N_REL])
    got = dict(w_in_even=got_in_e, w_out_even=got_out_e, w_in_odd=got_in_o, w_out_odd=got_out_o)
    return loss, dx0, got, small


def _pack_small(norm_pre, norm_post, b_f, rel_bias):
    flat = jnp.concatenate([norm_pre.reshape(-1), norm_post.reshape(-1), b_f.reshape(-1), rel_bias.reshape(-1)])
    n = flat.shape[0]
    rows = -(-n // 128)
    rows = -(-rows // 8) * 8
    return jnp.pad(flat, (0, rows * 128 - n)).reshape(rows, 128)


def _unpack_small(slab, shapes):
    flat = slab.reshape(-1)
    out, off = [], 0
    for shp in shapes:
        n = int(np.prod(shp))
        out.append(flat[off:off + n].reshape(shp))
        off += n
    return out


def kernel(x, norm_pre, norm_post, w_in_even, b_f_even, rel_bias_even, w_out_even, w_in_odd, w_out_odd, loss_target, m_norm_pre, m_norm_post, m_w_in_even, m_b_f_even, m_rel_bias_even, m_w_out_even, m_w_in_odd, m_w_out_odd, v_norm_pre, v_norm_post, v_w_in_even, v_b_f_even, v_rel_bias_even, v_w_out_even, v_w_in_odd, v_w_out_odd):
    _, s, d = x.shape
    wa = d // 2
    nha = wa // HEAD_DIM
    in_e = w_in_even.shape[2] * N_DEV

    w_in_e_all = jnp.transpose(_all_gather(w_in_even[0].astype(BF16), name="ag_w_in_even"), (1, 0, 2)).reshape(d, in_e)
    w_main = w_in_e_all
    w_f = _pad_cols(w_in_e_all[:, 8 * wa:], 128)

    loss, dx, got, small = _local_step(
        x[0], loss_target[0], norm_pre, norm_post, w_main, w_f, b_f_even, rel_bias_even[0],
        w_out_even[0].astype(BF16), w_in_odd[0].astype(BF16), w_out_odd[0].astype(BF16))

    upd = {}
    upd["w_in_even"] = _adamw(got["w_in_even"], w_in_even[0], m_w_in_even[0], v_w_in_even[0],
                              name="adamw_w_in_even", rows=128)
    upd["w_out_even"] = _adamw(got["w_out_even"], w_out_even[0], m_w_out_even[0], v_w_out_even[0],
                               name="adamw_w_out_even", rows=64)
    upd["w_in_odd"] = _adamw(got["w_in_odd"], w_in_odd[0], m_w_in_odd[0], v_w_in_odd[0],
                             name="adamw_w_in_odd", rows=128)
    upd["w_out_odd"] = _adamw(got["w_out_odd"], w_out_odd[0], m_w_out_odd[0], v_w_out_odd[0],
                              name="adamw_w_out_odd", rows=64)

    shapes = [norm_pre.shape, norm_post.shape, b_f_even.shape, rel_bias_even.shape]
    g_slab = _pack_small(small["norm_pre"], small["norm_post"], small["b_f"], small["rel_bias"])
    parts_small = _exchange(g_slab, gather=True, name="ar_small")
    sm = _adamw(parts_small, _pack_small(norm_pre, norm_post, b_f_even, rel_bias_even),
                _pack_small(m_norm_pre, m_norm_post, m_b_f_even, m_rel_bias_even),
                _pack_small(v_norm_pre, v_norm_post, v_b_f_even, v_rel_bias_even), name="adamw_small", rows=g_slab.shape[0])
    sm = [_unpack_small(a, shapes) for a in sm]

    total = lax.psum(loss[0, 0], ("x", "y", "c"))

    def leaves(kind):
        return (sm[kind][0], sm[kind][1], upd["w_in_even"][kind][None], sm[kind][2], sm[kind][3],
                upd["w_out_even"][kind][None], upd["w_in_odd"][kind][None], upd["w_out_odd"][kind][None])

    return (total, dx[None], *leaves(0), *leaves(1), *leaves(2), *leaves(3))
```
